```python
import math
import jax
import jax.numpy as jnp
from jax import lax
import numpy as np

D_MODEL = 1024
BATCH = 8
SEQ = 4096
DEPTH = 4

HEAD_DIM = 64
ROPE_THETA = 10000.0
NORM_EPS = 1e-6
QBLK = 128

A_HEADS = 8
A_KV_HEADS = 2
A_GROUP = A_HEADS // A_KV_HEADS
A_WIDTH = A_HEADS * HEAD_DIM
NSA_BRANCHES = 3
CMP_LEN = 32
CMP_STRIDE = 16
CMP_HIDDEN = 2 * HEAD_DIM
SEL_LEN = 64
SEL_TOPK = 16
A_WINDOW = 512
FORCED_SCORE = 1e4

B_HEADS = 8
B_KV_HEADS = 1
B_GROUP = B_HEADS // B_KV_HEADS
B_WIDTH = B_HEADS * HEAD_DIM
B_WINDOW = 128

C_WIDTH = 512
C_GROUP_CH = 16
C_GROUPS = C_WIDTH // C_GROUP_CH
C_STATE = 64
DT_MIN = 1e-3
DT_MAX = 1e-1

D_FF = 4 * D_MODEL
N_BRANCH = 3

IN_WIDTHS = (A_WIDTH, NSA_BRANCHES * 2 * A_KV_HEADS * HEAD_DIM, NSA_BRANCHES * A_HEADS, B_WIDTH, 2 * B_KV_HEADS * HEAD_DIM, C_WIDTH, N_BRANCH * D_MODEL)
IN_COLS = sum(IN_WIDTHS)

kernel_name = 'hybrid_nsa_swa_s5_block'


def rmsnorm(x, gain):
    xf = x.astype(jnp.float32)
    y = xf * lax.rsqrt(jnp.mean(xf * xf, axis=-1, keepdims=True) + NORM_EPS)
    return (y * gain.astype(jnp.float32)).astype(x.dtype)


def rope(x, positions):
    half = x.shape[-1] // 2
    inv_freq = ROPE_THETA ** (-jnp.arange(half, dtype=jnp.float32) / half)
    ang = positions.astype(jnp.float32)[..., None] * inv_freq
    cos = jnp.cos(ang)[:, :, None, :]
    sin = jnp.sin(ang)[:, :, None, :]
    xf = x.astype(jnp.float32)
    x1, x2 = xf[..., :half], xf[..., half:]
    return jnp.concatenate([x1 * cos - x2 * sin, x2 * cos + x1 * sin], axis=-1).astype(x.dtype)


def masked_softmax(s, mask, axis=-1):
    s = jnp.where(mask, s.astype(jnp.float32), -jnp.inf)
    m = jnp.max(s, axis=axis, keepdims=True)
    m = jnp.where(jnp.isfinite(m), m, 0.0)
    e = jnp.exp(s - m)
    return e / jnp.maximum(jnp.sum(e, axis=axis, keepdims=True), 1e-30)


def banded_attention(q, k, v, window, sinks=None):
    bsz, seq, hkv, grp, dh = q.shape
    nblk = seq // QBLK
    pad = -(-window // QBLK) * QBLK
    span = pad + QBLK
    kp = jnp.pad(k, ((0, 0), (pad, 0), (0, 0), (0, 0)))
    vp = jnp.pad(v, ((0, 0), (pad, 0), (0, 0), (0, 0)))
    qb = jnp.moveaxis(q.reshape(bsz, nblk, QBLK, hkv, grp, dh), 1, 0)
    scale = dh ** -0.5

    def one_block(args):
        i, qi = args
        start = i * QBLK
        ki = lax.dynamic_slice_in_dim(kp, start, span, axis=1)
        vi = lax.dynamic_slice_in_dim(vp, start, span, axis=1)
        s = jnp.einsum('bqhgd,bkhd->bhgqk', qi, ki).astype(jnp.float32) * scale
        qpos = start + jnp.arange(QBLK)
        kpos = start - pad + jnp.arange(span)
        diff = qpos[:, None] - kpos[None, :]
        mask = (diff >= 0) & (diff < window) & (kpos[None, :] >= 0)
        s = jnp.where(mask, s, -jnp.inf)
        m = jnp.max(s, axis=-1, keepdims=True)
        if sinks is not None:
            sk = sinks.astype(jnp.float32).reshape(hkv, grp)[None, :, :, None, None]
            m = jnp.maximum(m, sk)
            e = jnp.exp(s - m)
            den = jnp.sum(e, axis=-1, keepdims=True) + jnp.exp(sk - m)
        else:
            e = jnp.exp(s - m)
            den = jnp.sum(e, axis=-1, keepdims=True)
        p = (e / den).astype(vi.dtype)
        return jnp.einsum('bhgqk,bkhd->bqhgd', p, vi)

    out = lax.map(one_block, (jnp.arange(nblk), qb))
    return jnp.moveaxis(out, 0, 1).reshape(bsz, seq, hkv, grp, dh)


def nsa_attention(q, k, v, gates, cmp_pos, cmp_w1, cmp_w2):
    bsz, seq, hkv, grp, dh = q.shape
    n_cmp = (seq - CMP_LEN) // CMP_STRIDE + 1
    n_sel = seq // SEL_LEN
    n_top = min(SEL_TOPK, n_sel)
    nblk = seq // QBLK
    scale = dh ** -0.5

    tok = jnp.arange(n_cmp)[:, None] * CMP_STRIDE + jnp.arange(CMP_LEN)[None, :]

    def compress(xb, j):
        blocks = xb[:, tok] + cmp_pos[j][None, None, :, None, :]
        blocks = jnp.moveaxis(blocks, 3, 2).reshape(bsz, n_cmp, hkv, CMP_LEN * dh)
        return jax.nn.gelu(blocks @ cmp_w1[j]) @ cmp_w2[j]

    kc = compress(k[:, :, 0], 0)
    vc = compress(v[:, :, 0], 1)
    cmp_end = jnp.arange(n_cmp) * CMP_STRIDE + CMP_LEN - 1

    c_start = jnp.arange(n_cmp) * CMP_STRIDE
    s_start = jnp.arange(n_sel) * SEL_LEN
    overlap = jnp.minimum(c_start[:, None] + CMP_LEN, s_start[None, :] + SEL_LEN) - jnp.maximum(c_start[:, None], s_start[None, :])
    cmp_to_sel = jnp.clip(overlap, 0, None).astype(jnp.float32) / CMP_LEN

    ks_blk = jnp.moveaxis(k[:, :, 1].reshape(bsz, n_sel, SEL_LEN, hkv, dh), 3, 1)
    vs_blk = jnp.moveaxis(v[:, :, 1].reshape(bsz, n_sel, SEL_LEN, hkv, dh), 3, 1)
    qc = q.reshape(bsz * nblk, QBLK, hkv, grp, dh)
    b_idx = jnp.repeat(jnp.arange(bsz), nblk)
    blk_idx = jnp.tile(jnp.arange(nblk), bsz)
    head_idx = jnp.arange(hkv)[None, :, None]
    sel_ids = jnp.arange(n_sel)

    def one_block(args):
        b, i, qi = args
        qpos = i * QBLK + jnp.arange(QBLK)
        s = jnp.einsum('qhgd,nhd->qhgn', qi, kc[b]).astype(jnp.float32) * scale
        p_c = masked_softmax(s, (cmp_end[None, :] <= qpos[:, None])[:, None, None, :])
        o_c = jnp.einsum('qhgn,nhd->qhgd', p_c.astype(qi.dtype), vc[b])
        imp = jnp.einsum('qhgn,nj->qhj', p_c, cmp_to_sel)
        cur = qpos // SEL_LEN
        valid = sel_ids[None, :] <= cur[:, None]
        forced = (sel_ids[None, :] == 0) | (sel_ids[None, :] == cur[:, None]) | (sel_ids[None, :] == cur[:, None] - 1)
        score = jnp.where(forced[:, None, :], FORCED_SCORE, imp)
        score = jnp.where(valid[:, None, :], score, -jnp.inf)
        _, top = lax.top_k(score, n_top)
        kg = ks_blk[b][head_idx, top]
        vg = vs_blk[b][head_idx, top]
        s = jnp.einsum('qhgd,qhnkd->qhgnk', qi, kg).astype(jnp.float32) * scale
        kpos = top[..., None] * SEL_LEN + jnp.arange(SEL_LEN)
        mask = (kpos <= qpos[:, None, None, None])[:, :, None]
        p_s = masked_softmax(s, mask, axis=(-2, -1))
        o_s = jnp.einsum('qhgnk,qhnkd->qhgd', p_s.astype(qi.dtype), vg)
        return o_c, o_s

    o_c, o_s = lax.map(one_block, (b_idx, blk_idx, qc))
    o_c = o_c.reshape(bsz, seq, hkv, grp, dh)
    o_s = o_s.reshape(bsz, seq, hkv, grp, dh)
    o_w = banded_attention(q, k[:, :, 2], v[:, :, 2], A_WINDOW)
    out = gates[:, :, 0][..., None] * o_c + gates[:, :, 1][..., None] * o_s + gates[:, :, 2][..., None] * o_w
    return out.reshape(bsz, seq, hkv * grp * dh)


def s5_layer(u, a_re, a_im, log_dt, b_re, b_im, c_re, c_im, d, glu_w, glu_b):
    bsz, seq, _ = u.shape
    uf = u.astype(jnp.float32)
    ug = uf.reshape(bsz, seq, C_GROUPS, C_GROUP_CH)
    dt = jnp.exp(log_dt.astype(jnp.float32))[:, None]
    ar, ai = a_re.astype(jnp.float32), a_im.astype(jnp.float32)
    mag = jnp.exp(dt * ar)
    abar_r, abar_i = mag * jnp.cos(dt * ai), mag * jnp.sin(dt * ai)
    den = ar * ar + ai * ai
    nr, ni = abar_r - 1.0, abar_i
    coef_r = (nr * ar + ni * ai) / den
    coef_i = (ni * ar - nr * ai) / den
    br, bi = b_re.astype(jnp.float32), b_im.astype(jnp.float32)
    bbar_r = coef_r[..., None] * br - coef_i[..., None] * bi
    bbar_i = coef_r[..., None] * bi + coef_i[..., None] * br
    bu_r = jnp.einsum('bsgh,gph->bsgp', ug, bbar_r)
    bu_i = jnp.einsum('bsgh,gph->bsgp', ug, bbar_i)
    el_r = jnp.broadcast_to(abar_r, bu_r.shape)
    el_i = jnp.broadcast_to(abar_i, bu_i.shape)

    def combine(e1, e2):
        a1r, a1i, b1r, b1i = e1
        a2r, a2i, b2r, b2i = e2
        return (a2r * a1r - a2i * a1i, a2r * a1i + a2i * a1r,
                a2r * b1r - a2i * b1i + b2r, a2r * b1i + a2i * b1r + b2i)

    _, _, xr, xi = lax.associative_scan(combine, (el_r, el_i, bu_r, bu_i), axis=1)
    y = jnp.einsum('bsgp,ghp->bsgh', xr, c_re.astype(jnp.float32)) - jnp.einsum('bsgp,ghp->bsgh', xi, c_im.astype(jnp.float32))
    y = y.reshape(bsz, seq, C_WIDTH) + d.astype(jnp.float32) * uf
    z = jax.nn.gelu(y)
    out = z * jax.nn.sigmoid(z @ glu_w.astype(jnp.float32) + glu_b.astype(jnp.float32))
    return out.astype(u.dtype)


def setup_inputs(seed: int = 0) -> dict:
    key = jax.random.key(seed)
    ks = jax.random.split(key, 32)
    f32 = jnp.float32
    L = DEPTH

    def nrm(k, shape, scale):
        return scale * jax.random.normal(k, shape, f32)

    x = jax.random.normal(ks[0], (BATCH, SEQ, D_MODEL), f32)
    offset = jax.random.randint(ks[1], (BATCH, 1), 0, 4096, jnp.int32)
    positions = offset + jnp.arange(SEQ, dtype=jnp.int32)[None, :]
    state_idx = jnp.arange(C_STATE, dtype=f32)
    return {
        'x': x,
        'positions': positions,
        'norm_mix': 1.0 + nrm(ks[2], (L, D_MODEL), 0.02),
        'w_in': nrm(ks[3], (L, D_MODEL, IN_COLS), D_MODEL ** -0.5),
        'nsa_cmp_pos': nrm(ks[4], (L, 2, CMP_LEN, HEAD_DIM), 0.02),
        'nsa_cmp_w1': nrm(ks[5], (L, 2, CMP_LEN * HEAD_DIM, CMP_HIDDEN), (CMP_LEN * HEAD_DIM) ** -0.5),
        'nsa_cmp_w2': nrm(ks[6], (L, 2, CMP_HIDDEN, HEAD_DIM), CMP_HIDDEN ** -0.5),
        'swa_sinks': nrm(ks[7], (L, B_HEADS), 0.5),
        's5_a_re': -0.5 + nrm(ks[8], (L, C_GROUPS, C_STATE), 0.01),
        's5_a_im': math.pi * state_idx + nrm(ks[9], (L, C_GROUPS, C_STATE), 0.01),
        's5_log_dt': jax.random.uniform(ks[10], (L, C_GROUPS), f32, math.log(DT_MIN), math.log(DT_MAX)),
        's5_b_re': nrm(ks[11], (L, C_GROUPS, C_STATE, C_GROUP_CH), (2 * C_GROUP_CH) ** -0.5),
        's5_b_im': nrm(ks[12], (L, C_GROUPS, C_STATE, C_GROUP_CH), (2 * C_GROUP_CH) ** -0.5),
        's5_c_re': nrm(ks[13], (L, C_GROUPS, C_GROUP_CH, C_STATE), 0.5 ** 0.5),
        's5_c_im': nrm(ks[14], (L, C_GROUPS, C_GROUP_CH, C_STATE), 0.5 ** 0.5),
        's5_d': nrm(ks[15], (L, C_WIDTH), 0.5),
        's5_glu_w': nrm(ks[16], (L, C_WIDTH, C_WIDTH), C_WIDTH ** -0.5),
        's5_glu_b': nrm(ks[17], (L, C_WIDTH), 0.01),
        'w_branch_a': nrm(ks[18], (L, A_WIDTH, D_MODEL), A_WIDTH ** -0.5),
        'w_branch_b': nrm(ks[19], (L, B_WIDTH, D_MODEL), B_WIDTH ** -0.5),
        'w_branch_c': nrm(ks[20], (L, C_WIDTH, D_MODEL), C_WIDTH ** -0.5),
        'w_out': nrm(ks[21], (L, D_MODEL, D_MODEL), D_MODEL ** -0.5),
        'norm_mlp': 1.0 + nrm(ks[22], (L, D_MODEL), 0.02),
        'w_mlp_up': nrm(ks[23], (L, D_MODEL, D_FF), D_MODEL ** -0.5),
        'w_mlp_down': nrm(ks[24], (L, D_FF, D_MODEL), D_FF ** -0.5),
        'norm_final': 1.0 + nrm(ks[25], (D_MODEL,), 0.02),
    }


def reference(x, positions, norm_mix, w_in, nsa_cmp_pos, nsa_cmp_w1, nsa_cmp_w2, swa_sinks,
              s5_a_re, s5_a_im, s5_log_dt, s5_b_re, s5_b_im, s5_c_re, s5_c_im, s5_d, s5_glu_w, s5_glu_b,
              w_branch_a, w_branch_b, w_branch_c, w_out, norm_mlp, w_mlp_up, w_mlp_down, norm_final):
    bsz, seq, _ = x.shape
    splits = []
    acc = 0
    for w in IN_WIDTHS[:-1]:
        acc += w
        splits.append(acc)

    for l in range(DEPTH):
        h = rmsnorm(x, norm_mix[l])
        proj = h @ w_in[l]
        q_a, kv_a, g_a, q_b, kv_b, u_c, g_m = jnp.split(proj, splits, axis=-1)

        q_a = rope(q_a.reshape(bsz, seq, A_HEADS, HEAD_DIM), positions).reshape(bsz, seq, A_KV_HEADS, A_GROUP, HEAD_DIM)
        kv_a = kv_a.reshape(bsz, seq, NSA_BRANCHES, 2, A_KV_HEADS, HEAD_DIM)
        k_a = rope(kv_a[:, :, :, 0].reshape(bsz, seq, NSA_BRANCHES * A_KV_HEADS, HEAD_DIM), positions).reshape(bsz, seq, NSA_BRANCHES, A_KV_HEADS, HEAD_DIM)
        v_a = kv_a[:, :, :, 1]
        gates_a = jax.nn.sigmoid(g_a.reshape(bsz, seq, NSA_BRANCHES, A_KV_HEADS, A_GROUP))
        y_a = nsa_attention(q_a, k_a, v_a, gates_a, nsa_cmp_pos[l], nsa_cmp_w1[l], nsa_cmp_w2[l])

        q_b = rope(q_b.reshape(bsz, seq, B_HEADS, HEAD_DIM), positions).reshape(bsz, seq, B_KV_HEADS, B_GROUP, HEAD_DIM)
        kv_b = kv_b.reshape(bsz, seq, 2, B_KV_HEADS, HEAD_DIM)
        k_b = rope(kv_b[:, :, 0], positions)
        y_b = banded_attention(q_b, k_b, kv_b[:, :, 1], B_WINDOW, swa_sinks[l]).reshape(bsz, seq, B_WIDTH)

        y_c = s5_layer(u_c, s5_a_re[l], s5_a_im[l], s5_log_dt[l], s5_b_re[l], s5_b_im[l],
                       s5_c_re[l], s5_c_im[l], s5_d[l], s5_glu_w[l], s5_glu_b[l])

        g_m = jax.nn.sigmoid(g_m.reshape(bsz, seq, N_BRANCH, D_MODEL))
        merged = (g_m[:, :, 0] * (y_a @ w_branch_a[l])
                  + g_m[:, :, 1] * (y_b @ w_branch_b[l])
                  + g_m[:, :, 2] * (y_c @ w_branch_c[l]))
        x = x + merged @ w_out[l]

        h = rmsnorm(x, norm_mlp[l])
        x = x + jnp.square(jax.nn.relu(h @ w_mlp_up[l])) @ w_mlp_down[l]

    return rmsnorm(x, norm_final)
```

```python
import functools
import math

import jax
import jax.numpy as jnp
from jax import lax
from jax.experimental import pallas as pl
from jax.experimental.pallas import tpu as pltpu

D_MODEL = 1024
HEAD_DIM = 64
ROPE_THETA = 10000.0
NORM_EPS = 1e-6
QBLK = 128

A_HEADS = 8
A_KV_HEADS = 2
A_GROUP = A_HEADS // A_KV_HEADS
A_WIDTH = A_HEADS * HEAD_DIM
NSA_BRANCHES = 3
CMP_LEN = 32
CMP_STRIDE = 16
CMP_HIDDEN = 2 * HEAD_DIM
SEL_LEN = 64
SEL_TOPK = 16
A_WINDOW = 512
FORCED_SCORE = 1e4

B_HEADS = 8
B_WIDTH = B_HEADS * HEAD_DIM
B_WINDOW = 128

C_WIDTH = 512
C_GROUP_CH = 16
C_GROUPS = C_WIDTH // C_GROUP_CH
C_STATE = 64
C_STATES = C_GROUPS * C_STATE

D_FF = 4 * D_MODEL
N_BRANCH = 3

IN_WIDTHS = (A_WIDTH, NSA_BRANCHES * 2 * A_KV_HEADS * HEAD_DIM, NSA_BRANCHES * A_HEADS,
             B_WIDTH, 2 * HEAD_DIM, C_WIDTH, N_BRANCH * D_MODEL)

LANES = 128
VMEM_LIMIT = 56 * 1024 * 1024

MXU_DTYPE = jnp.bfloat16
ROW_TILE = 256
S5_CHUNK = 64
S5_LANES = 512
SEL_CHUNK = 512
FF_CHUNK = 1024

_ROPE_COLS = 12 * LANES
_PROJ_COLS = 20 * LANES

_NEG_INF = float("-inf")
_M_INIT = -1e30


def _f32(x):
    return x.astype(jnp.float32)


def _mm(a, b):
    return jnp.dot(a.astype(MXU_DTYPE), b.astype(MXU_DTYPE), preferred_element_type=jnp.float32)


def _mm_nt(a, b):
    return lax.dot_general(a.astype(MXU_DTYPE), b.astype(MXU_DTYPE), (((1,), (1,)), ((), ())),
                           preferred_element_type=jnp.float32)


def _rmsnorm(x, gain):
    return x * lax.rsqrt(jnp.mean(x * x, axis=-1, keepdims=True) + NORM_EPS) * gain


def _params(sem):
    return pltpu.CompilerParams(dimension_semantics=sem, vmem_limit_bytes=VMEM_LIMIT)


def _rope_table_kernel(pos_ref, invf_ref, sign_ref, cos_ref, sin_ref):
    ang = _f32(pos_ref[...]) * invf_ref[...]
    cos_ref[...] = jnp.cos(ang)
    sin_ref[...] = jnp.sin(ang) * sign_ref[...]


def _rope_tables(positions):
    n = positions.size
    half = HEAD_DIM // 2
    inv_freq = ROPE_THETA ** (-jnp.arange(half, dtype=jnp.float32) / half)
    invf = jnp.tile(inv_freq, LANES // half).reshape(1, LANES)
    sign = jnp.tile(jnp.concatenate([-jnp.ones(half, jnp.float32), jnp.ones(half, jnp.float32)]),
                    LANES // HEAD_DIM).reshape(1, LANES)
    tm = min(1024, n)
    row = pl.BlockSpec((tm, LANES), lambda i: (i, 0))
    const = pl.BlockSpec((1, LANES), lambda i: (0, 0))
    return pl.pallas_call(
        _rope_table_kernel,
        grid=(n // tm,),
        in_specs=[pl.BlockSpec((tm, 1), lambda i: (i, 0)), const, const],
        out_specs=[row, row],
        out_shape=[jax.ShapeDtypeStruct((n, LANES), jnp.float32)] * 2,
        compiler_params=_params(("parallel",)),
        name="rope_tables",
    )(positions.reshape(n, 1), invf, sign)


def _swap_halves(x):
    lane = lax.broadcasted_iota(jnp.int32, x.shape, 1)
    first = (lane & (HEAD_DIM - 1)) < HEAD_DIM // 2
    return jnp.where(first, pltpu.roll(x, LANES - HEAD_DIM // 2, 1), pltpu.roll(x, HEAD_DIM // 2, 1))


def _proj_kernel(x_ref, gain_ref, w_ref, cos_ref, sin_ref,
                 qa_ref, qb_ref, ka_ref, kvb_ref, va_ref, ga_ref, uc_ref):
    h = _rmsnorm(x_ref[...], gain_ref[...]).astype(MXU_DTYPE)
    cos = cos_ref[...]
    sin = sin_ref[...]

    def roped(j):
        xj = jnp.dot(h, w_ref[:, j * LANES:(j + 1) * LANES], preferred_element_type=jnp.float32)
        return xj, xj * cos + _swap_halves(xj) * sin

    for j in range(4):
        qa_ref[:, j * LANES:(j + 1) * LANES] = roped(j)[1].astype(qa_ref.dtype)
    for j in range(4):
        qb_ref[:, j * LANES:(j + 1) * LANES] = roped(4 + j)[1].astype(qb_ref.dtype)
    for j in range(3):
        ka_ref[:, j * LANES:(j + 1) * LANES] = roped(8 + j)[1].astype(ka_ref.dtype)
    raw, rot = roped(11)
    lane = lax.broadcasted_iota(jnp.int32, raw.shape, 1)
    kvb_ref[...] = jnp.where(lane < HEAD_DIM, rot, raw).astype(kvb_ref.dtype)
    va_ref[...] = jnp.dot(h, w_ref[:, 12 * LANES:15 * LANES],
                          preferred_element_type=jnp.float32).astype(va_ref.dtype)
    ga_ref[...] = jax.nn.sigmoid(jnp.dot(h, w_ref[:, 15 * LANES:16 * LANES],
                                         preferred_element_type=jnp.float32))
    uc_ref[...] = jnp.dot(h, w_ref[:, 16 * LANES:20 * LANES], preferred_element_type=jnp.float32)


def _pack_w_in(w_in):
    o = [0]
    for w in IN_WIDTHS:
        o.append(o[-1] + w)
    scale = HEAD_DIM ** -0.5
    d = w_in.shape[0]
    q_a = w_in[:, o[0]:o[1]] * scale
    kv_a = w_in[:, o[1]:o[2]].reshape(d, NSA_BRANCHES, 2, A_KV_HEADS * HEAD_DIM)
    k_a = kv_a[:, :, 0].reshape(d, NSA_BRANCHES * LANES)
    v_a = kv_a[:, :, 1].reshape(d, NSA_BRANCHES * LANES)
    g_a = jnp.pad(w_in[:, o[2]:o[3]], ((0, 0), (0, LANES - NSA_BRANCHES * A_HEADS)))
    q_b = w_in[:, o[3]:o[4]] * scale
    kv_b = w_in[:, o[4]:o[5]]
    u_c = w_in[:, o[5]:o[6]]
    return jnp.concatenate([q_a, q_b, k_a, kv_b, v_a, g_a, u_c], axis=1).astype(MXU_DTYPE)


def _in_projection(x2, gain, w_packed, cos, sin):
    n = x2.shape[0]
    tm = min(ROW_TILE, n)

    def rows(width):
        return pl.BlockSpec((tm, width), lambda i: (i, 0))

    def const(shape):
        return pl.BlockSpec(shape, lambda i: (0, 0))

    widths = (A_WIDTH, B_WIDTH, 3 * LANES, LANES, 3 * LANES, LANES, C_WIDTH)
    dtypes = (MXU_DTYPE, MXU_DTYPE, MXU_DTYPE, MXU_DTYPE, MXU_DTYPE, jnp.float32, jnp.float32)
    return pl.pallas_call(
        _proj_kernel,
        grid=(n // tm,),
        in_specs=[rows(D_MODEL), const((1, D_MODEL)), const((D_MODEL, _PROJ_COLS)), rows(LANES), rows(LANES)],
        out_specs=[rows(w) for w in widths],
        out_shape=[jax.ShapeDtypeStruct((n, w), dt) for w, dt in zip(widths, dtypes)],
        compiler_params=_params(("parallel",)),
        name="in_projection",
    )(x2, gain.reshape(1, D_MODEL), w_packed, cos, sin)


def _compress_kernel(seg_ref, pos_ref, w1_ref, w2_ref, o_ref):
    seg = seg_ref[...]
    half = seg.shape[1]
    first = _mm(seg, w1_ref[:half, :])
    second = _mm(seg, w1_ref[half:, :])
    nseg = seg.shape[0]
    hidden = first + pltpu.roll(second, nseg - 1, 0) + _mm(pos_ref[...], w1_ref[...])
    o_ref[...] = _mm(jax.nn.gelu(hidden), w2_ref[...]).astype(o_ref.dtype)


def _compress(segs, pos_flat, w1, w2):
    _, bsz, hkv, nseg, segw = segs.shape
    return pl.pallas_call(
        _compress_kernel,
        grid=(2, bsz, hkv),
        in_specs=[
            pl.BlockSpec((None, None, None, nseg, segw), lambda j, b, h: (j, b, h, 0, 0)),
            pl.BlockSpec((None, 1, CMP_LEN * HEAD_DIM), lambda j, b, h: (j, 0, 0)),
            pl.BlockSpec((None, CMP_LEN * HEAD_DIM, CMP_HIDDEN), lambda j, b, h: (j, 0, 0)),
            pl.BlockSpec((None, CMP_HIDDEN, HEAD_DIM), lambda j, b, h: (j, 0, 0)),
        ],
        out_specs=pl.BlockSpec((None, None, None, nseg, HEAD_DIM), lambda j, b, h: (j, b, h, 0, 0)),
        out_shape=jax.ShapeDtypeStruct((2, bsz, hkv, nseg, HEAD_DIM), MXU_DTYPE),
        compiler_params=_params(("parallel", "parallel", "parallel")),
        name="nsa_compress",
    )(segs, pos_flat, w1, w2)


def _split3(x):
    hi = x.astype(MXU_DTYPE)
    r1 = x - _f32(hi)
    mid = r1.astype(MXU_DTYPE)
    lo = (r1 - _f32(mid)).astype(MXU_DTYPE)
    return hi, mid, lo


def _nsa_kernel(q_ref, kt_ref, v_ref, kc_ref, vc_ref, g_ref, c2s_ref, expand_ref, eye_ref, o_ref, *, seq):
    i = pl.program_id(1)
    q0 = i * QBLK
    n_cmp = kc_ref.shape[1]
    n_sel = seq // SEL_LEN
    kc_sz = min(SEL_CHUNK, seq)
    span = min(A_WINDOW + QBLK, seq)
    rows = A_GROUP * QBLK

    q = q_ref[...]
    gates = g_ref[...]
    qp_col = q0 + lax.broadcasted_iota(jnp.int32, (QBLK, 1), 0)
    qp_lane = q0 + lax.broadcasted_iota(jnp.int32, (1, QBLK), 1)

    heads_out = []
    for h in range(A_KV_HEADS):
        qs = jnp.concatenate([q[:, (h * A_GROUP + g) * HEAD_DIM:(h * A_GROUP + g + 1) * HEAD_DIM]
                              for g in range(A_GROUP)], axis=0)

        s = _mm_nt(qs, kc_ref[h]).reshape(A_GROUP, QBLK, n_cmp)
        cmp_end = lax.broadcasted_iota(jnp.int32, (1, n_cmp), 1) * CMP_STRIDE + (CMP_LEN - 1)
        vis = (cmp_end <= qp_col)[None]
        s = jnp.where(vis, s, _NEG_INF)
        m = jnp.max(s, axis=-1, keepdims=True)
        m = jnp.where(m == _NEG_INF, 0.0, m)
        e = jnp.exp(s - m)
        p_c = e / jnp.maximum(jnp.sum(e, axis=-1, keepdims=True), 1e-30)
        o_c = _mm(p_c.reshape(rows, n_cmp), vc_ref[h])

        p_sum = p_c[0] + p_c[1] + p_c[2] + p_c[3]
        imp_t = sum(_mm_nt(c2s_ref[...], part) for part in _split3(p_sum))
        blk = lax.broadcasted_iota(jnp.int32, (n_sel, QBLK), 0)
        cur = jnp.right_shift(qp_lane, 6)
        forced = (blk == 0) | (blk == cur) | (blk == cur - 1)
        score = jnp.where(forced, FORCED_SCORE, imp_t)
        score = jnp.where(blk <= cur, score, _NEG_INF)
        rank = jnp.zeros((n_sel, QBLK), jnp.float32)
        for j in range(n_sel):
            row = score[j:j + 1, :]
            rank = rank + jnp.where(blk > j, jnp.where(row >= score, 1.0, 0.0),
                                    jnp.where(row > score, 1.0, 0.0))
        sel_t = jnp.where(rank < float(min(SEL_TOPK, n_sel)), 1.0, 0.0).astype(MXU_DTYPE)
        sel = _mm_nt(eye_ref[...], sel_t).astype(MXU_DTYPE)

        k_row = (1 * A_KV_HEADS + h) * HEAD_DIM
        v_col = 1 * LANES

        def sel_step(c, carry):
            m_run, l_run, acc = carry
            k0 = pl.multiple_of(c * kc_sz, kc_sz)
            sc = _mm(qs, kt_ref[k_row:k_row + HEAD_DIM, pl.ds(k0, kc_sz)]).reshape(A_GROUP, QBLK, kc_sz)
            picked = _mm(sel, expand_ref[:, pl.ds(k0, kc_sz)])
            kpos = k0 + lax.broadcasted_iota(jnp.int32, (1, kc_sz), 1)
            allowed = jnp.where(kpos <= qp_col, picked, 0.0) > 0.5
            sc = jnp.where(allowed[None], sc, _NEG_INF).reshape(rows, kc_sz)
            m_new = jnp.maximum(m_run, jnp.max(sc, axis=-1, keepdims=True))
            alpha = jnp.exp(m_run - m_new)
            p = jnp.exp(sc - m_new)
            l_new = alpha * l_run + jnp.sum(p, axis=-1, keepdims=True)
            acc_new = alpha * acc + _mm(p, v_ref[pl.ds(k0, kc_sz), v_col:v_col + LANES])
            return m_new, l_new, acc_new

        n_chunks = (q0 + QBLK + kc_sz - 1) // kc_sz
        init = (jnp.full((rows, 1), _M_INIT, jnp.float32), jnp.zeros((rows, 1), jnp.float32),
                jnp.zeros((rows, LANES), jnp.float32))
        _, l_s, acc_s = lax.fori_loop(0, n_chunks, sel_step, init)
        o_s = acc_s[:, h * HEAD_DIM:(h + 1) * HEAD_DIM] / jnp.maximum(l_s, 1e-30)

        w0 = pl.multiple_of(jnp.maximum(q0 + QBLK - span, 0), QBLK)
        k_row = (2 * A_KV_HEADS + h) * HEAD_DIM
        sw = _mm(qs, kt_ref[k_row:k_row + HEAD_DIM, pl.ds(w0, span)]).reshape(A_GROUP, QBLK, span)
        diff = qp_col - (w0 + lax.broadcasted_iota(jnp.int32, (1, span), 1))
        inwin = jnp.where(diff >= 0, diff, A_WINDOW) < A_WINDOW
        sw = jnp.where(inwin[None], sw, _NEG_INF).reshape(rows, span)
        ew = jnp.exp(sw - jnp.max(sw, axis=-1, keepdims=True))
        o_w = _mm(ew, v_ref[pl.ds(w0, span), 2 * LANES:3 * LANES])[:, h * HEAD_DIM:(h + 1) * HEAD_DIM]
        o_w = o_w / jnp.sum(ew, axis=-1, keepdims=True)

        for g in range(A_GROUP):
            hd = h * A_GROUP + g
            r = slice(g * QBLK, (g + 1) * QBLK)
            heads_out.append(gates[:, hd:hd + 1] * o_c[r]
                             + gates[:, A_HEADS + hd:A_HEADS + hd + 1] * o_s[r]
                             + gates[:, 2 * A_HEADS + hd:2 * A_HEADS + hd + 1] * o_w[r])
    o_ref[...] = jnp.concatenate(heads_out, axis=1).astype(o_ref.dtype)


def _nsa_attention(qa, kt, va, kcvc, ga, bsz, seq):
    nblk = seq // QBLK
    n_seg = seq // CMP_STRIDE
    n_sel = seq // SEL_LEN
    assert seq >= A_WINDOW + QBLK and seq % SEL_CHUNK == 0
    c_start = jnp.arange(n_seg) * CMP_STRIDE
    s_start = jnp.arange(n_sel) * SEL_LEN
    overlap = (jnp.minimum(c_start[None, :] + CMP_LEN, s_start[:, None] + SEL_LEN)
               - jnp.maximum(c_start[None, :], s_start[:, None]))
    c2s_t = (jnp.clip(overlap, 0, None).astype(jnp.float32) / CMP_LEN).astype(MXU_DTYPE)
    expand = (jnp.arange(seq)[None, :] // SEL_LEN == jnp.arange(n_sel)[:, None]).astype(MXU_DTYPE)
    eye = jnp.eye(QBLK, dtype=MXU_DTYPE)

    def const(shape):
        return pl.BlockSpec(shape, lambda b, i: (0, 0))

    return pl.pallas_call(
        functools.partial(_nsa_kernel, seq=seq),
        grid=(bsz, nblk),
        in_specs=[
            pl.BlockSpec((QBLK, A_WIDTH), lambda b, i: (b * nblk + i, 0)),
            pl.BlockSpec((None, 3 * LANES, seq), lambda b, i: (b, 0, 0)),
            pl.BlockSpec((seq, 3 * LANES), lambda b, i: (b, 0)),
            pl.BlockSpec((None, None, A_KV_HEADS, n_seg, HEAD_DIM), lambda b, i: (0, b, 0, 0, 0)),
            pl.BlockSpec((None, None, A_KV_HEADS, n_seg, HEAD_DIM), lambda b, i: (1, b, 0, 0, 0)),
            pl.BlockSpec((QBLK, LANES), lambda b, i: (b * nblk + i, 0)),
            const((n_sel, n_seg)), const((n_sel, seq)), const((QBLK, QBLK)),
        ],
        out_specs=pl.BlockSpec((QBLK, A_WIDTH), lambda b, i: (b * nblk + i, 0)),
        out_shape=jax.ShapeDtypeStruct((bsz * seq, A_WIDTH), MXU_DTYPE),
        compiler_params=_params(("parallel", "arbitrary")),
        name="nsa_attention",
    )(qa, kt, va, kcvc, kcvc, ga, c2s_t, expand, eye)


def _swa_kernel(sink_ref, q_ref, kt_ref, kv_ref, o_ref, *, seq):
    i = pl.program_id(1)
    q0 = i * QBLK
    span = min(B_WINDOW + QBLK, seq)
    q = q_ref[...]
    qs = jnp.concatenate([q[:, hd * HEAD_DIM:(hd + 1) * HEAD_DIM] for hd in range(B_HEADS)], axis=0)
    w0 = pl.multiple_of(jnp.maximum(q0 + QBLK - span, 0), QBLK)
    s = _mm(qs, kt_ref[:, pl.ds(w0, span)]).reshape(B_HEADS, QBLK, span)
    qp = q0 + lax.broadcasted_iota(jnp.int32, (QBLK, 1), 0)
    diff = qp - (w0 + lax.broadcasted_iota(jnp.int32, (1, span), 1))
    inwin = jnp.where(diff >= 0, diff, B_WINDOW) < B_WINDOW
    s = jnp.where(inwin[None], s, _NEG_INF).reshape(B_HEADS * QBLK, span)
    sink = jnp.concatenate([jnp.full((QBLK, 1), sink_ref[hd], jnp.float32) for hd in range(B_HEADS)], axis=0)
    m = jnp.maximum(jnp.max(s, axis=-1, keepdims=True), sink)
    e = jnp.exp(s - m)
    den = jnp.sum(e, axis=-1, keepdims=True) + jnp.exp(sink - m)
    o = _mm(e, kv_ref[pl.ds(w0, span), :])[:, HEAD_DIM:] / den
    o_ref[...] = jnp.concatenate([o[hd * QBLK:(hd + 1) * QBLK] for hd in range(B_HEADS)],
                                 axis=1).astype(o_ref.dtype)


def _swa_attention(qb, ktb, kvb, sinks, bsz, seq):
    nblk = seq // QBLK
    assert seq >= B_WINDOW + QBLK
    return pl.pallas_call(
        functools.partial(_swa_kernel, seq=seq),
        grid=(bsz, nblk),
        in_specs=[
            pl.BlockSpec(memory_space=pltpu.SMEM),
            pl.BlockSpec((QBLK, B_WIDTH), lambda b, i: (b * nblk + i, 0)),
            pl.BlockSpec((None, HEAD_DIM, seq), lambda b, i: (b, 0, 0)),
            pl.BlockSpec((seq, LANES), lambda b, i: (b, 0)),
        ],
        out_specs=pl.BlockSpec((QBLK, B_WIDTH), lambda b, i: (b * nblk + i, 0)),
        out_shape=jax.ShapeDtypeStruct((bsz * seq, B_WIDTH), MXU_DTYPE),
        compiler_params=_params(("parallel", "arbitrary")),
        name="swa_attention",
    )(sinks, qb, ktb, kvb)


def _s5_discretize_kernel(are_ref, aim_ref, logdt_ref, bre_ref, bim_ref,
                          abr_ref, abi_ref, bbr_ref, bbi_ref):
    ar = are_ref[...]
    ai = aim_ref[...]
    dt = jnp.exp(logdt_ref[...])
    mag = jnp.exp(dt * ar)
    abr = mag * jnp.cos(dt * ai)
    abi = mag * jnp.sin(dt * ai)
    den = ar * ar + ai * ai
    nr = abr - 1.0
    coef_r = (nr * ar + abi * ai) / den
    coef_i = (abi * ar - nr * ai) / den
    abr_ref[...] = abr
    abi_ref[...] = abi
    br = bre_ref[...]
    bi = bim_ref[...]
    bbr_ref[...] = coef_r[:, None, :] * br - coef_i[:, None, :] * bi
    bbi_ref[...] = coef_r[:, None, :] * bi + coef_i[:, None, :] * br


def _s5_discretize(a_re, a_im, log_dt, b_re, b_im):
    lg = a_re.shape[0] * a_re.shape[1]
    a2 = lambda a: a.reshape(lg, C_STATE)
    b3 = lambda b: jnp.swapaxes(b.reshape(lg, C_STATE, C_GROUP_CH), 1, 2)
    sa = jax.ShapeDtypeStruct((lg, C_STATE), jnp.float32)
    sb = jax.ShapeDtypeStruct((lg, C_GROUP_CH, C_STATE), jnp.float32)
    return pl.pallas_call(_s5_discretize_kernel, out_shape=[sa, sa, sb, sb], name="s5_discretize")(
        a2(a_re), a2(a_im), log_dt.reshape(lg, 1), b3(b_re), b3(b_im))


def _s5_kernel(u_ref, bmat_ref, abar_ref, cmat_ref, d_ref, gw_ref, gb_ref, y_ref, state_ref, xs_ref):
    t_steps, bsz, _ = u_ref.shape

    @pl.when(pl.program_id(0) == 0)
    def _():
        state_ref[...] = jnp.zeros_like(state_ref)

    u = u_ref[...].reshape(t_steps * bsz, C_WIDTH)
    xs_ref[...] = _mm(u, bmat_ref[...])

    for cg in range(C_STATES // S5_LANES):
        re = slice(cg * S5_LANES, (cg + 1) * S5_LANES)
        im = slice(C_STATES + cg * S5_LANES, C_STATES + (cg + 1) * S5_LANES)
        ar = jnp.broadcast_to(abar_ref[0:1, re], (bsz, S5_LANES))
        ai = jnp.broadcast_to(abar_ref[0:1, im], (bsz, S5_LANES))

        def step(t, carry):
            xr, xi = carry
            r0 = pl.multiple_of(t * bsz, bsz)
            nr = ar * xr - ai * xi + xs_ref[pl.ds(r0, bsz), re]
            ni = ar * xi + ai * xr + xs_ref[pl.ds(r0, bsz), im]
            xs_ref[pl.ds(r0, bsz), re] = nr
            xs_ref[pl.ds(r0, bsz), im] = ni
            return nr, ni

        xr, xi = lax.fori_loop(0, t_steps, step, (state_ref[:, re], state_ref[:, im]), unroll=8)
        state_ref[:, re] = xr
        state_ref[:, im] = xi

    y = _mm(xs_ref[...], cmat_ref[...]) + d_ref[...] * u
    z = jax.nn.gelu(y)
    out = z * jax.nn.sigmoid(_mm(z, gw_ref[...]) + gb_ref[...])
    y_ref[...] = out.reshape(t_steps, bsz, C_WIDTH).astype(y_ref.dtype)


def _s5(u_tm, bmat, abar, cmat, d, glu_w, glu_b):
    seq, bsz, _ = u_tm.shape
    t_steps = min(S5_CHUNK, seq)

    def const(shape):
        return pl.BlockSpec(shape, lambda i: (0, 0))

    return pl.pallas_call(
        _s5_kernel,
        grid=(seq // t_steps,),
        in_specs=[
            pl.BlockSpec((t_steps, bsz, C_WIDTH), lambda i: (i, 0, 0)),
            const((C_WIDTH, 2 * C_STATES)), const((1, 2 * C_STATES)), const((2 * C_STATES, C_WIDTH)),
            const((1, C_WIDTH)), const((C_WIDTH, C_WIDTH)), const((1, C_WIDTH)),
        ],
        out_specs=pl.BlockSpec((t_steps, bsz, C_WIDTH), lambda i: (i, 0, 0)),
        out_shape=jax.ShapeDtypeStruct((seq, bsz, C_WIDTH), MXU_DTYPE),
        scratch_shapes=[pltpu.VMEM((bsz, 2 * C_STATES), jnp.float32),
                        pltpu.VMEM((t_steps * bsz, 2 * C_STATES), jnp.float32)],
        compiler_params=_params(("arbitrary",)),
        name="s5_scan",
    )(u_tm, bmat, abar, cmat, d.reshape(1, C_WIDTH), glu_w.astype(MXU_DTYPE), glu_b.reshape(1, C_WIDTH))


def _block_diag(per_group):
    g, r, c = per_group.shape
    eye = jnp.eye(g, dtype=per_group.dtype)
    return jnp.einsum("grc,gk->grkc", per_group, eye).reshape(g * r, g * c)


def _merge_kernel(x_ref, ya_ref, yb_ref, yc_ref, gain_ref, wg_ref, pa_ref, pb_ref, pc_ref, wo_ref, o_ref):
    x = x_ref[...]
    h = _rmsnorm(x, gain_ref[...]).astype(MXU_DTYPE)
    merged = None
    for j, (y_ref, p_ref) in enumerate(((ya_ref, pa_ref), (yb_ref, pb_ref), (yc_ref, pc_ref))):
        gate = jax.nn.sigmoid(jnp.dot(h, wg_ref[:, j * D_MODEL:(j + 1) * D_MODEL],
                                      preferred_element_type=jnp.float32))
        term = gate * jnp.dot(y_ref[...], p_ref[...], preferred_element_type=jnp.float32)
        merged = term if merged is None else merged + term
    o_ref[...] = x + _mm(merged, wo_ref[...])


def _merge(x2, ya, yb, yc, gain, wg, pa, pb, pc, wo):
    n = x2.shape[0]
    tm = min(ROW_TILE, n)

    def rows(width):
        return pl.BlockSpec((tm, width), lambda i: (i, 0))

    def const(shape):
        return pl.BlockSpec(shape, lambda i: (0, 0))

    return pl.pallas_call(
        _merge_kernel,
        grid=(n // tm,),
        in_specs=[rows(D_MODEL), rows(A_WIDTH), rows(B_WIDTH), rows(C_WIDTH), const((1, D_MODEL)),
                  const((D_MODEL, N_BRANCH * D_MODEL)), const((A_WIDTH, D_MODEL)), const((B_WIDTH, D_MODEL)),
                  const((C_WIDTH, D_MODEL)), const((D_MODEL, D_MODEL))],
        out_specs=rows(D_MODEL),
        out_shape=jax.ShapeDtypeStruct((n, D_MODEL), jnp.float32),
        compiler_params=_params(("parallel",)),
        name="merge",
    )(x2, ya, yb, yc, gain.reshape(1, D_MODEL), wg, pa, pb, pc, wo)


def _mlp_kernel(x_ref, gain_ref, wu_ref, wd_ref, fgain_ref, o_ref, *, final_norm):
    x = x_ref[...]
    h = _rmsnorm(x, gain_ref[...]).astype(MXU_DTYPE)
    acc = x
    for c in range(D_FF // FF_CHUNK):
        cols = slice(c * FF_CHUNK, (c + 1) * FF_CHUNK)
        up = jnp.dot(h, wu_ref[:, cols], preferred_element_type=jnp.float32)
        acc = acc + _mm(jnp.square(jnp.maximum(up, 0.0)), wd_ref[cols, :])
    o_ref[...] = _rmsnorm(acc, fgain_ref[...]) if final_norm else acc


def _mlp(x2, gain, wu, wd, fgain, final_norm):
    n = x2.shape[0]
    tm = min(ROW_TILE, n)
    rows = pl.BlockSpec((tm, D_MODEL), lambda i: (i, 0))

    def const(shape):
        return pl.BlockSpec(shape, lambda i: (0, 0))

    return pl.pallas_call(
        functools.partial(_mlp_kernel, final_norm=final_norm),
        grid=(n // tm,),
        in_specs=[rows, const((1, D_MODEL)), const((D_MODEL, D_FF)), const((D_FF, D_MODEL)), const((1, D_MODEL))],
        out_specs=rows,
        out_shape=jax.ShapeDtypeStruct((n, D_MODEL), jnp.float32),
        compiler_params=_params(("parallel",)),
        name="mlp",
    )(x2, gain.reshape(1, D_MODEL), wu, wd, fgain.reshape(1, D_MODEL))


def kernel(x, positions, norm_mix, w_in, nsa_cmp_pos, nsa_cmp_w1, nsa_cmp_w2, swa_sinks, s5_a_re, s5_a_im, s5_log_dt, s5_b_re, s5_b_im, s5_c_re, s5_c_im, s5_d, s5_glu_w, s5_glu_b, w_branch_a, w_branch_b, w_branch_c, w_out, norm_mlp, w_mlp_up, w_mlp_down, norm_final):
    bsz, seq, _ = x.shape
    depth = w_in.shape[0]
    n = bsz * seq
    n_seg = seq // CMP_STRIDE
    gate_col = sum(IN_WIDTHS[:-1])
    bf = lambda w: w.astype(MXU_DTYPE)

    cos, sin = _rope_tables(positions)
    abr, abi, bbr, bbi = _s5_discretize(s5_a_re, s5_a_im, s5_log_dt, s5_b_re, s5_b_im)
    abr = abr.reshape(depth, 1, C_STATES)
    abi = abi.reshape(depth, 1, C_STATES)
    bbr = bbr.reshape(depth, C_GROUPS, C_GROUP_CH, C_STATE)
    bbi = bbi.reshape(depth, C_GROUPS, C_GROUP_CH, C_STATE)

    x2 = x.reshape(n, D_MODEL)
    for l in range(depth):
        qa, qb, ka, kvb, va, ga, uc = _in_projection(x2, norm_mix[l], _pack_w_in(w_in[l]), cos, sin)

        def segments(t):
            t = t[:, :LANES].reshape(bsz, n_seg, CMP_STRIDE, A_KV_HEADS, HEAD_DIM)
            return jnp.transpose(t, (0, 3, 1, 2, 4)).reshape(bsz, A_KV_HEADS, n_seg, CMP_STRIDE * HEAD_DIM)

        kcvc = _compress(jnp.stack([segments(ka), segments(va)]),
                         nsa_cmp_pos[l].reshape(2, 1, CMP_LEN * HEAD_DIM), nsa_cmp_w1[l], nsa_cmp_w2[l])
        kt = jnp.swapaxes(ka.reshape(bsz, seq, 3 * LANES), 1, 2)
        ya = _nsa_attention(qa, kt, va, kcvc, ga, bsz, seq)

        ktb = jnp.swapaxes(kvb[:, :HEAD_DIM].reshape(bsz, seq, HEAD_DIM), 1, 2)
        yb = _swa_attention(qb, ktb, kvb, swa_sinks[l], bsz, seq)

        bmat = jnp.concatenate([_block_diag(bbr[l]), _block_diag(bbi[l])], axis=1)
        cmat = jnp.concatenate([_block_diag(jnp.swapaxes(s5_c_re[l], 1, 2)),
                                -_block_diag(jnp.swapaxes(s5_c_im[l], 1, 2))], axis=0)
        abar = jnp.concatenate([abr[l], abi[l]], axis=1)
        u_tm = jnp.swapaxes(uc.reshape(bsz, seq, C_WIDTH), 0, 1)
        yc_tm = _s5(u_tm, bf(bmat), abar, bf(cmat), s5_d[l], s5_glu_w[l], s5_glu_b[l])
        yc = jnp.swapaxes(yc_tm, 0, 1).reshape(n, C_WIDTH)

        x2 = _merge(x2, ya, yb, yc, norm_mix[l], bf(w_in[l][:, gate_col:]), bf(w_branch_a[l]),
                    bf(w_branch_b[l]), bf(w_branch_c[l]), bf(w_out[l]))
        x2 = _mlp(x2, norm_mlp[l], bf(w_mlp_up[l]), bf(w_mlp_down[l]), norm_final, l == depth - 1)
    return x2.reshape(bsz, seq, D_MODEL)
```

```python
import functools

import jax
import jax.numpy as jnp
from jax import lax
from jax.experimental import pallas as pl
from jax.experimental.pallas import tpu as pltpu

D_MODEL = 1024
HEAD_DIM = 64
ROPE_THETA = 10000.0
NORM_EPS = 1e-6
QBLK = 128

A_HEADS = 8
A_KV_HEADS = 2
A_GROUP = A_HEADS // A_KV_HEADS
A_WIDTH = A_HEADS * HEAD_DIM
NSA_BRANCHES = 3
CMP_LEN = 32
CMP_STRIDE = 16
CMP_HIDDEN = 2 * HEAD_DIM
SEL_LEN = 64
SEL_TOPK = 16
A_WINDOW = 512
FORCED_SCORE = 1e4

B_HEADS = 8
B_WIDTH = B_HEADS * HEAD_DIM
B_WINDOW = 128

C_WIDTH = 512
C_GROUP_CH = 16
C_GROUPS = C_WIDTH // C_GROUP_CH
C_STATE = 64
C_STATES = C_GROUPS * C_STATE

D_FF = 4 * D_MODEL
N_BRANCH = 3

IN_WIDTHS = (A_WIDTH, NSA_BRANCHES * 2 * A_KV_HEADS * HEAD_DIM, NSA_BRANCHES * A_HEADS,
             B_WIDTH, 2 * HEAD_DIM, C_WIDTH, N_BRANCH * D_MODEL)

LANES = 128
SUBLANES = 8
VMEM_LIMIT = 56 * 1024 * 1024

MXU_DTYPE = jnp.bfloat16
ROW_TILE = 256
S5_CHUNK = 64
S5_LANES = 512
SEL_CHUNK = 512
FF_CHUNK = 1024

_PROJ_COLS = 20 * LANES

LOG2E = 1.4426950408889634
_NEG_INF = float("-inf")
_M_INIT = -1e30


def _f32(x):
    return x.astype(jnp.float32)


def _mm(a, b):
    return jnp.dot(a.astype(MXU_DTYPE), b.astype(MXU_DTYPE), preferred_element_type=jnp.float32)


def _mm_nt(a, b):
    return lax.dot_general(a.astype(MXU_DTYPE), b.astype(MXU_DTYPE), (((1,), (1,)), ((), ())),
                           preferred_element_type=jnp.float32)


def _rmsnorm(x, gain):
    return x * lax.rsqrt(jnp.mean(x * x, axis=-1, keepdims=True) + NORM_EPS) * gain


def _resident(shape):
    zeros = (0,) * len(shape)
    return pl.BlockSpec(shape, lambda *_: zeros, pipeline_mode=pl.Buffered(1))


def _params(sem):
    return pltpu.CompilerParams(dimension_semantics=sem, vmem_limit_bytes=VMEM_LIMIT)


def _rope_table_kernel(pos_ref, invf_ref, sign_ref, cos_ref, sin_ref):
    ang = _f32(pos_ref[...]) * invf_ref[...]
    cos_ref[...] = jnp.cos(ang)
    sin_ref[...] = jnp.sin(ang) * sign_ref[...]


def _rope_tables(positions):
    n = positions.size
    half = HEAD_DIM // 2
    inv_freq = ROPE_THETA ** (-jnp.arange(half, dtype=jnp.float32) / half)
    invf = jnp.tile(inv_freq, LANES // half).reshape(1, LANES)
    sign = jnp.tile(jnp.concatenate([-jnp.ones(half, jnp.float32), jnp.ones(half, jnp.float32)]),
                    LANES // HEAD_DIM).reshape(1, LANES)
    tm = min(1024, n)
    row = pl.BlockSpec((tm, LANES), lambda i: (i, 0))
    return pl.pallas_call(
        _rope_table_kernel,
        grid=(n // tm,),
        in_specs=[pl.BlockSpec((tm, 1), lambda i: (i, 0)), _resident((1, LANES)), _resident((1, LANES))],
        out_specs=[row, row],
        out_shape=[jax.ShapeDtypeStruct((n, LANES), jnp.float32)] * 2,
        compiler_params=_params(("parallel",)),
        name="rope_tables",
    )(positions.reshape(n, 1), invf, sign)


def _swap_halves(x):
    lane = lax.broadcasted_iota(jnp.int32, x.shape, 1)
    first = (lane & (HEAD_DIM - 1)) < HEAD_DIM // 2
    return jnp.where(first, pltpu.roll(x, LANES - HEAD_DIM // 2, 1), pltpu.roll(x, HEAD_DIM // 2, 1))


def _proj_kernel(x_ref, gain_ref, w_ref, cos_ref, sin_ref,
                 qa_ref, qb_ref, ka_ref, kvb_ref, va_ref, ga_ref, uc_ref):
    h = _rmsnorm(x_ref[...], gain_ref[...]).astype(MXU_DTYPE)
    cos = cos_ref[...]
    sin = sin_ref[...]

    def roped(j):
        xj = jnp.dot(h, w_ref[:, j * LANES:(j + 1) * LANES], preferred_element_type=jnp.float32)
        return xj, xj * cos + _swap_halves(xj) * sin

    for j in range(4):
        qa_ref[:, j * LANES:(j + 1) * LANES] = roped(j)[1].astype(qa_ref.dtype)
    for j in range(4):
        qb_ref[:, j * LANES:(j + 1) * LANES] = roped(4 + j)[1].astype(qb_ref.dtype)
    for j in range(3):
        ka_ref[:, j * LANES:(j + 1) * LANES] = roped(8 + j)[1].astype(ka_ref.dtype)
    raw, rot = roped(11)
    lane = lax.broadcasted_iota(jnp.int32, raw.shape, 1)
    kvb_ref[...] = jnp.where(lane < HEAD_DIM, rot, raw).astype(kvb_ref.dtype)
    va_ref[...] = jnp.dot(h, w_ref[:, 12 * LANES:15 * LANES],
                          preferred_element_type=jnp.float32).astype(va_ref.dtype)
    ga_ref[...] = jax.nn.sigmoid(jnp.dot(h, w_ref[:, 15 * LANES:16 * LANES],
                                         preferred_element_type=jnp.float32))
    uc_ref[...] = jnp.dot(h, w_ref[:, 16 * LANES:20 * LANES], preferred_element_type=jnp.float32)


def _pack_w_in(w_in):
    o = [0]
    for w in IN_WIDTHS:
        o.append(o[-1] + w)
    scale = HEAD_DIM ** -0.5 * LOG2E
    d = w_in.shape[0]
    q_a = w_in[:, o[0]:o[1]] * scale
    kv_a = w_in[:, o[1]:o[2]].reshape(d, NSA_BRANCHES, 2, A_KV_HEADS * HEAD_DIM)
    k_a = kv_a[:, :, 0].reshape(d, NSA_BRANCHES * LANES)
    v_a = kv_a[:, :, 1].reshape(d, NSA_BRANCHES * LANES)
    g_a = jnp.pad(w_in[:, o[2]:o[3]], ((0, 0), (0, LANES - NSA_BRANCHES * A_HEADS)))
    q_b = w_in[:, o[3]:o[4]] * scale
    kv_b = w_in[:, o[4]:o[5]]
    u_c = w_in[:, o[5]:o[6]]
    return jnp.concatenate([q_a, q_b, k_a, kv_b, v_a, g_a, u_c], axis=1).astype(MXU_DTYPE)


def _in_projection(x2, gain, w_packed, cos, sin):
    n = x2.shape[0]
    tm = min(ROW_TILE, n)

    def rows(width):
        return pl.BlockSpec((tm, width), lambda i: (i, 0))

    widths = (A_WIDTH, B_WIDTH, 3 * LANES, LANES, 3 * LANES, LANES, C_WIDTH)
    dtypes = (MXU_DTYPE, MXU_DTYPE, MXU_DTYPE, MXU_DTYPE, MXU_DTYPE, jnp.float32, jnp.float32)
    return pl.pallas_call(
        _proj_kernel,
        grid=(n // tm,),
        in_specs=[rows(D_MODEL), _resident((1, D_MODEL)), _resident((D_MODEL, _PROJ_COLS)),
                  rows(LANES), rows(LANES)],
        out_specs=[rows(w) for w in widths],
        out_shape=[jax.ShapeDtypeStruct((n, w), dt) for w, dt in zip(widths, dtypes)],
        compiler_params=_params(("parallel",)),
        name="in_projection",
    )(x2, gain.reshape(1, D_MODEL), w_packed, cos, sin)


def _compress_kernel(seg_ref, pos_ref, w1_ref, w2_ref, o_ref):
    seg = seg_ref[...]
    half = seg.shape[1]
    first = _mm(seg, w1_ref[:half, :])
    second = _mm(seg, w1_ref[half:, :])
    nseg = seg.shape[0]
    hidden = first + pltpu.roll(second, nseg - 1, 0) + _mm(pos_ref[...], w1_ref[...])
    o_ref[...] = _mm(jax.nn.gelu(hidden), w2_ref[...]).astype(o_ref.dtype)


def _compress(segs, pos_flat, w1, w2):
    _, bsz, hkv, nseg, segw = segs.shape
    return pl.pallas_call(
        _compress_kernel,
        grid=(2, bsz, hkv),
        in_specs=[
            pl.BlockSpec((None, None, None, nseg, segw), lambda j, b, h: (j, b, h, 0, 0)),
            pl.BlockSpec((None, 1, CMP_LEN * HEAD_DIM), lambda j, b, h: (j, 0, 0)),
            pl.BlockSpec((None, CMP_LEN * HEAD_DIM, CMP_HIDDEN), lambda j, b, h: (j, 0, 0)),
            pl.BlockSpec((None, CMP_HIDDEN, HEAD_DIM), lambda j, b, h: (j, 0, 0)),
        ],
        out_specs=pl.BlockSpec((None, None, None, nseg, HEAD_DIM), lambda j, b, h: (j, b, h, 0, 0)),
        out_shape=jax.ShapeDtypeStruct((2, bsz, hkv, nseg, HEAD_DIM), MXU_DTYPE),
        compiler_params=_params(("parallel", "parallel", "parallel")),
        name="nsa_compress",
    )(segs, pos_flat, w1, w2)


def _split3(x):
    hi = x.astype(MXU_DTYPE)
    r1 = x - _f32(hi)
    mid = r1.astype(MXU_DTYPE)
    lo = (r1 - _f32(mid)).astype(MXU_DTYPE)
    return hi, mid, lo


def _col_max(s):
    return jnp.max(s, axis=0, keepdims=True)


def _tile_lanes(x, n):
    return jnp.concatenate([x] * n, axis=1)


def _pad_head_rows(x, h, fill):
    other = jnp.full(x.shape, fill, x.dtype)
    return jnp.concatenate([x, other] if h == 0 else [other, x], axis=0)


def _store_transposed_pairs(o_ref, heads_t):
    for p in range(len(heads_t) // 2):
        pair = jnp.concatenate([heads_t[2 * p], heads_t[2 * p + 1]], axis=0)
        o_ref[:, p * LANES:(p + 1) * LANES] = pair.T.astype(o_ref.dtype)


def _nsa_kernel(qt_ref, k_ref, vt_ref, kc_ref, vct_ref, g_ref, c2s_ref, o_ref, selb_ref, *, seq):
    i = pl.program_id(1)
    q0 = i * QBLK
    n_cmp = kc_ref.shape[0]
    n_sel = seq // SEL_LEN
    kc_sz = min(SEL_CHUNK, seq)
    span = min(A_WINDOW + QBLK, seq)
    blocks_per_chunk = kc_sz // SEL_LEN

    gates_t = g_ref[...].T
    qp_lane = q0 + lax.broadcasted_iota(jnp.int32, (1, QBLK), 1)

    cmp_end = lax.broadcasted_iota(jnp.int32, (n_cmp, 1), 0) * CMP_STRIDE + (CMP_LEN - 1)
    bias_c = _tile_lanes(jnp.where(cmp_end <= qp_lane, 0.0, _NEG_INF), A_GROUP)

    w0 = pl.multiple_of(jnp.maximum(q0 + QBLK - span, 0), QBLK)
    diff = qp_lane - (w0 + lax.broadcasted_iota(jnp.int32, (span, 1), 0))
    bias_w = _tile_lanes(jnp.where(diff >= 0, jnp.where(diff < A_WINDOW, 0.0, _NEG_INF), _NEG_INF), A_GROUP)

    def head_rows(h):
        return slice(h * HEAD_DIM, (h + 1) * HEAD_DIM), slice((1 - h) * HEAD_DIM, (1 - h) * HEAD_DIM + 1)

    w_qs, o_cs, o_ws = [], [], []
    for h in range(A_KV_HEADS):
        own, ones_row = head_rows(h)
        q_t = jnp.concatenate([qt_ref[(h * A_GROUP + g) * HEAD_DIM:(h * A_GROUP + g + 1) * HEAD_DIM, :]
                               for g in range(A_GROUP)], axis=1)
        w_q = _pad_head_rows(q_t, h, 0.0)
        w_qs.append(w_q)

        s = _mm(kc_ref[...], w_q) + bias_c
        m = _col_max(s)
        m = jnp.where(m == _NEG_INF, 0.0, m)
        e = jnp.exp2(s - m)
        p = e * (1.0 / jnp.maximum(jnp.sum(e, axis=0, keepdims=True), 1e-30))
        o_cs.append(_mm(vct_ref[...], p)[own])

        p_sum = p[:, 0:QBLK]
        for g in range(1, A_GROUP):
            p_sum = p_sum + p[:, g * QBLK:(g + 1) * QBLK]
        imp_t = sum(_mm(c2s_ref[...], part) for part in _split3(p_sum))
        blk = lax.broadcasted_iota(jnp.int32, (n_sel, QBLK), 0)
        cur = jnp.right_shift(qp_lane, 6)
        forced = (blk == 0) | (blk == cur) | (blk == cur - 1)
        score = jnp.where(forced, FORCED_SCORE, imp_t)
        score = jnp.where(blk <= cur, score, _NEG_INF)
        n_grp = n_sel // SUBLANES
        sub = lax.broadcasted_iota(jnp.int32, (SUBLANES, QBLK), 0)
        score_grp = [score[SUBLANES * v:SUBLANES * (v + 1), :] for v in range(n_grp)]
        rank_grp = [jnp.zeros((SUBLANES, QBLK), jnp.float32) for _ in range(n_grp)]
        for j in range(n_sel):
            row = jnp.broadcast_to(score[j:j + 1, :], (SUBLANES, QBLK))
            vj, rj = divmod(j, SUBLANES)
            for v in range(n_grp):
                if v > vj:
                    inc = jnp.where(row >= score_grp[v], 1.0, 0.0)
                elif v < vj:
                    inc = jnp.where(row > score_grp[v], 1.0, 0.0)
                else:
                    inc = jnp.where(sub > rj, jnp.where(row >= score_grp[v], 1.0, 0.0),
                                    jnp.where(row > score_grp[v], 1.0, 0.0))
                rank_grp[v] = rank_grp[v] + inc
        rank = jnp.concatenate(rank_grp, axis=0)
        picked = jnp.where(rank < float(min(SEL_TOPK, n_sel)), 0.0, _NEG_INF)
        selb_ref[h] = jnp.where(blk <= cur, picked, _NEG_INF)

        sw = _mm(k_ref[pl.ds(w0, span), 2 * LANES:3 * LANES], w_q) + bias_w
        pw = jnp.exp2(sw - _col_max(sw))
        vt_w = _pad_head_rows(vt_ref[2 * LANES + h * HEAD_DIM:2 * LANES + (h + 1) * HEAD_DIM, pl.ds(w0, span)],
                              h, 1.0)
        acc_w = _mm(vt_w, pw)
        o_ws.append(acc_w[own] / acc_w[ones_row])

    def sel_step(c, carry):
        k0 = pl.multiple_of(c * kc_sz, kc_sz)
        causal = (k0 + lax.broadcasted_iota(jnp.int32, (kc_sz, 1), 0)) <= qp_lane
        out = []
        for h in range(A_KV_HEADS):
            m_run, acc = carry[h]
            sc = _mm(k_ref[pl.ds(k0, kc_sz), LANES:2 * LANES], w_qs[h])
            slabs = [jnp.broadcast_to(selb_ref[h, pl.ds(c * blocks_per_chunk + jj, 1), :], (SEL_LEN, QBLK))
                     for jj in range(blocks_per_chunk)]
            bias = jnp.where(causal, jnp.concatenate(slabs, axis=0), _NEG_INF)
            sc = sc + _tile_lanes(bias, A_GROUP)
            m_new = jnp.maximum(m_run, _col_max(sc))
            pr = jnp.exp2(sc - m_new)
            vt = _pad_head_rows(vt_ref[LANES + h * HEAD_DIM:LANES + (h + 1) * HEAD_DIM, pl.ds(k0, kc_sz)], h, 1.0)
            out.append((m_new, jnp.exp2(m_run - m_new) * acc + _mm(vt, pr)))
        return tuple(out)

    n_chunks = (q0 + QBLK + kc_sz - 1) // kc_sz
    init = tuple((jnp.full((1, A_GROUP * QBLK), _M_INIT, jnp.float32),
                  jnp.zeros((LANES, A_GROUP * QBLK), jnp.float32)) for _ in range(A_KV_HEADS))
    sel_out = lax.fori_loop(0, n_chunks, sel_step, init)

    heads_t = []
    for h in range(A_KV_HEADS):
        own, ones_row = head_rows(h)
        acc_s = sel_out[h][1]
        o_s = acc_s[own] / jnp.maximum(acc_s[ones_row], 1e-30)
        for g in range(A_GROUP):
            hd = h * A_GROUP + g
            cols = slice(g * QBLK, (g + 1) * QBLK)
            heads_t.append(gates_t[hd:hd + 1, :] * o_cs[h][:, cols]
                           + gates_t[A_HEADS + hd:A_HEADS + hd + 1, :] * o_s[:, cols]
                           + gates_t[2 * A_HEADS + hd:2 * A_HEADS + hd + 1, :] * o_ws[h][:, cols])
    _store_transposed_pairs(o_ref, heads_t)


def _nsa_attention(qat, ka, vat, kc, vct, ga, bsz, seq):
    nblk = seq // QBLK
    n_seg = seq // CMP_STRIDE
    n_sel = seq // SEL_LEN
    assert seq >= A_WINDOW + QBLK and seq % SEL_CHUNK == 0
    c_start = jnp.arange(n_seg) * CMP_STRIDE
    s_start = jnp.arange(n_sel) * SEL_LEN
    overlap = (jnp.minimum(c_start[None, :] + CMP_LEN, s_start[:, None] + SEL_LEN)
               - jnp.maximum(c_start[None, :], s_start[:, None]))
    c2s = (jnp.clip(overlap, 0, None).astype(jnp.float32) / CMP_LEN).astype(MXU_DTYPE)
    return pl.pallas_call(
        functools.partial(_nsa_kernel, seq=seq),
        grid=(bsz, nblk),
        in_specs=[
            pl.BlockSpec((None, A_WIDTH, QBLK), lambda b, i: (b, 0, i)),
            pl.BlockSpec((seq, 3 * LANES), lambda b, i: (b, 0)),
            pl.BlockSpec((None, 3 * LANES, seq), lambda b, i: (b, 0, 0)),
            pl.BlockSpec((None, n_seg, LANES), lambda b, i: (b, 0, 0)),
            pl.BlockSpec((None, LANES, n_seg), lambda b, i: (b, 0, 0)),
            pl.BlockSpec((QBLK, LANES), lambda b, i: (b * nblk + i, 0)),
            _resident((n_sel, n_seg)),
        ],
        out_specs=pl.BlockSpec((QBLK, A_WIDTH), lambda b, i: (b * nblk + i, 0)),
        out_shape=jax.ShapeDtypeStruct((bsz * seq, A_WIDTH), MXU_DTYPE),
        scratch_shapes=[pltpu.VMEM((A_KV_HEADS, n_sel, QBLK), jnp.float32)],
        compiler_params=_params(("parallel", "arbitrary")),
        name="nsa_attention",
    )(qat, ka, vat, kc, vct, ga, c2s)


def _swa_kernel(sink_ref, qt_ref, kv_ref, vt_ref, o_ref, *, seq):
    i = pl.program_id(1)
    q0 = i * QBLK
    span = min(B_WINDOW + QBLK, seq)
    w0 = pl.multiple_of(jnp.maximum(q0 + QBLK - span, 0), QBLK)
    qp_lane = q0 + lax.broadcasted_iota(jnp.int32, (1, QBLK), 1)
    diff = qp_lane - (w0 + lax.broadcasted_iota(jnp.int32, (span, 1), 0))
    bias = _tile_lanes(jnp.where(diff >= 0, jnp.where(diff < B_WINDOW, 0.0, _NEG_INF), _NEG_INF), B_HEADS)
    q_t = jnp.concatenate([qt_ref[hd * HEAD_DIM:(hd + 1) * HEAD_DIM, :] for hd in range(B_HEADS)], axis=1)
    s = _mm(kv_ref[pl.ds(w0, span), :], _pad_head_rows(q_t, 0, 0.0)) + bias
    sink = jnp.concatenate([jnp.full((1, QBLK), sink_ref[hd] * LOG2E, jnp.float32) for hd in range(B_HEADS)],
                           axis=1)
    m = jnp.maximum(_col_max(s), sink)
    acc = _mm(_pad_head_rows(vt_ref[:, pl.ds(w0, span)], 1, 1.0), jnp.exp2(s - m))
    o_t = acc[HEAD_DIM:] / (acc[0:1] + jnp.exp2(sink - m))
    _store_transposed_pairs(o_ref, [o_t[:, hd * QBLK:(hd + 1) * QBLK] for hd in range(B_HEADS)])


def _swa_attention(qbt, kvb, vbt, sinks, bsz, seq):
    nblk = seq // QBLK
    assert seq >= B_WINDOW + QBLK
    return pl.pallas_call(
        functools.partial(_swa_kernel, seq=seq),
        grid=(bsz, nblk),
        in_specs=[
            pl.BlockSpec(memory_space=pltpu.SMEM),
            pl.BlockSpec((None, B_WIDTH, QBLK), lambda b, i: (b, 0, i)),
            pl.BlockSpec((seq, LANES), lambda b, i: (b, 0)),
            pl.BlockSpec((None, HEAD_DIM, seq), lambda b, i: (b, 0, 0)),
        ],
        out_specs=pl.BlockSpec((QBLK, B_WIDTH), lambda b, i: (b * nblk + i, 0)),
        out_shape=jax.ShapeDtypeStruct((bsz * seq, B_WIDTH), MXU_DTYPE),
        compiler_params=_params(("parallel", "arbitrary")),
        name="swa_attention",
    )(sinks, qbt, kvb, vbt)


def _s5_discretize_kernel(are_ref, aim_ref, logdt_ref, bre_ref, bim_ref,
                          abr_ref, abi_ref, bbr_ref, bbi_ref):
    ar = are_ref[...]
    ai = aim_ref[...]
    dt = jnp.exp(logdt_ref[...])
    mag = jnp.exp(dt * ar)
    abr = mag * jnp.cos(dt * ai)
    abi = mag * jnp.sin(dt * ai)
    den = ar * ar + ai * ai
    nr = abr - 1.0
    coef_r = (nr * ar + abi * ai) / den
    coef_i = (abi * ar - nr * ai) / den
    abr_ref[...] = abr
    abi_ref[...] = abi
    br = bre_ref[...]
    bi = bim_ref[...]
    bbr_ref[...] = coef_r[:, None, :] * br - coef_i[:, None, :] * bi
    bbi_ref[...] = coef_r[:, None, :] * bi + coef_i[:, None, :] * br


def _s5_discretize(a_re, a_im, log_dt, b_re, b_im):
    lg = a_re.shape[0] * a_re.shape[1]
    a2 = lambda a: a.reshape(lg, C_STATE)
    b3 = lambda b: jnp.swapaxes(b.reshape(lg, C_STATE, C_GROUP_CH), 1, 2)
    sa = jax.ShapeDtypeStruct((lg, C_STATE), jnp.float32)
    sb = jax.ShapeDtypeStruct((lg, C_GROUP_CH, C_STATE), jnp.float32)
    return pl.pallas_call(_s5_discretize_kernel, out_shape=[sa, sa, sb, sb], name="s5_discretize")(
        a2(a_re), a2(a_im), log_dt.reshape(lg, 1), b3(b_re), b3(b_im))


def _s5_kernel(u_ref, bmat_ref, abar_ref, cmat_ref, d_ref, gw_ref, gb_ref, y_ref, state_ref, xs_ref):
    t_steps, bsz, _ = u_ref.shape

    @pl.when(pl.program_id(0) == 0)
    def _():
        state_ref[...] = jnp.zeros_like(state_ref)

    u = u_ref[...].reshape(t_steps * bsz, C_WIDTH)
    xs_ref[...] = _mm(u, bmat_ref[...])

    for cg in range(C_STATES // S5_LANES):
        re = slice(cg * S5_LANES, (cg + 1) * S5_LANES)
        im = slice(C_STATES + cg * S5_LANES, C_STATES + (cg + 1) * S5_LANES)
        ar = jnp.broadcast_to(abar_ref[0:1, re], (bsz, S5_LANES))
        ai = jnp.broadcast_to(abar_ref[0:1, im], (bsz, S5_LANES))

        def step(t, carry):
            xr, xi = carry
            r0 = pl.multiple_of(t * bsz, bsz)
            nr = ar * xr - ai * xi + xs_ref[pl.ds(r0, bsz), re]
            ni = ar * xi + ai * xr + xs_ref[pl.ds(r0, bsz), im]
            xs_ref[pl.ds(r0, bsz), re] = nr
            xs_ref[pl.ds(r0, bsz), im] = ni
            return nr, ni

        xr, xi = lax.fori_loop(0, t_steps, step, (state_ref[:, re], state_ref[:, im]), unroll=8)
        state_ref[:, re] = xr
        state_ref[:, im] = xi

    y = _mm(xs_ref[...], cmat_ref[...]) + d_ref[...] * u
    z = jax.nn.gelu(y)
    out = z * jax.nn.sigmoid(_mm(z, gw_ref[...]) + gb_ref[...])
    y_ref[...] = out.reshape(t_steps, bsz, C_WIDTH).astype(y_ref.dtype)


def _s5(u_tm, bmat, abar, cmat, d, glu_w, glu_b):
    seq, bsz, _ = u_tm.shape
    t_steps = min(S5_CHUNK, seq)
    return pl.pallas_call(
        _s5_kernel,
        grid=(seq // t_steps,),
        in_specs=[
            pl.BlockSpec((t_steps, bsz, C_WIDTH), lambda i: (i, 0, 0)),
            _resident((C_WIDTH, 2 * C_STATES)), _resident((1, 2 * C_STATES)), _resident((2 * C_STATES, C_WIDTH)),
            _resident((1, C_WIDTH)), _resident((C_WIDTH, C_WIDTH)), _resident((1, C_WIDTH)),
        ],
        out_specs=pl.BlockSpec((t_steps, bsz, C_WIDTH), lambda i: (i, 0, 0)),
        out_shape=jax.ShapeDtypeStruct((seq, bsz, C_WIDTH), MXU_DTYPE),
        scratch_shapes=[pltpu.VMEM((bsz, 2 * C_STATES), jnp.float32),
                        pltpu.VMEM((t_steps * bsz, 2 * C_STATES), jnp.float32)],
        compiler_params=_params(("arbitrary",)),
        name="s5_scan",
    )(u_tm, bmat, abar, cmat, d.reshape(1, C_WIDTH), glu_w.astype(MXU_DTYPE), glu_b.reshape(1, C_WIDTH))


def _block_diag(per_group):
    g, r, c = per_group.shape
    eye = jnp.eye(g, dtype=per_group.dtype)
    return jnp.einsum("grc,gk->grkc", per_group, eye).reshape(g * r, g * c)


def _merge_kernel(x_ref, ya_ref, yb_ref, yc_ref, gain_ref, wg_ref, pa_ref, pb_ref, pc_ref, wo_ref, o_ref):
    x = x_ref[...]
    h = _rmsnorm(x, gain_ref[...]).astype(MXU_DTYPE)
    merged = None
    for j, (y_ref, p_ref) in enumerate(((ya_ref, pa_ref), (yb_ref, pb_ref), (yc_ref, pc_ref))):
        gate = jax.nn.sigmoid(jnp.dot(h, wg_ref[:, j * D_MODEL:(j + 1) * D_MODEL],
                                      preferred_element_type=jnp.float32))
        term = gate * jnp.dot(y_ref[...], p_ref[...], preferred_element_type=jnp.float32)
        merged = term if merged is None else merged + term
    o_ref[...] = x + _mm(merged, wo_ref[...])


def _merge(x2, ya, yb, yc, gain, wg, pa, pb, pc, wo):
    n = x2.shape[0]
    tm = min(ROW_TILE, n)

    def rows(width):
        return pl.BlockSpec((tm, width), lambda i: (i, 0))

    return pl.pallas_call(
        _merge_kernel,
        grid=(n // tm,),
        in_specs=[rows(D_MODEL), rows(A_WIDTH), rows(B_WIDTH), rows(C_WIDTH), _resident((1, D_MODEL)),
                  _resident((D_MODEL, N_BRANCH * D_MODEL)), _resident((A_WIDTH, D_MODEL)),
                  _resident((B_WIDTH, D_MODEL)), _resident((C_WIDTH, D_MODEL)), _resident((D_MODEL, D_MODEL))],
        out_specs=rows(D_MODEL),
        out_shape=jax.ShapeDtypeStruct((n, D_MODEL), jnp.float32),
        compiler_params=_params(("parallel",)),
        name="merge",
    )(x2, ya, yb, yc, gain.reshape(1, D_MODEL), wg, pa, pb, pc, wo)


def _mlp_kernel(x_ref, gain_ref, wu_ref, wd_ref, fgain_ref, o_ref, *, final_norm):
    x = x_ref[...]
    h = _rmsnorm(x, gain_ref[...]).astype(MXU_DTYPE)
    acc = x
    for c in range(D_FF // FF_CHUNK):
        cols = slice(c * FF_CHUNK, (c + 1) * FF_CHUNK)
        up = jnp.dot(h, wu_ref[:, cols], preferred_element_type=jnp.float32)
        acc = acc + _mm(jnp.square(jnp.maximum(up, 0.0)), wd_ref[cols, :])
    o_ref[...] = _rmsnorm(acc, fgain_ref[...]) if final_norm else acc


def _mlp(x2, gain, wu, wd, fgain, final_norm):
    n = x2.shape[0]
    tm = min(ROW_TILE, n)
    rows = pl.BlockSpec((tm, D_MODEL), lambda i: (i, 0))
    return pl.pallas_call(
        functools.partial(_mlp_kernel, final_norm=final_norm),
        grid=(n // tm,),
        in_specs=[rows, _resident((1, D_MODEL)), _resident((D_MODEL, D_FF)), _resident((D_FF, D_MODEL)),
                  _resident((1, D_MODEL))],
        out_specs=rows,
        out_shape=jax.ShapeDtypeStruct((n, D_MODEL), jnp.float32),
        compiler_params=_params(("parallel",)),
        name="mlp",
    )(x2, gain.reshape(1, D_MODEL), wu, wd, fgain.reshape(1, D_MODEL))


def kernel(x, positions, norm_mix, w_in, nsa_cmp_pos, nsa_cmp_w1, nsa_cmp_w2, swa_sinks, s5_a_re, s5_a_im, s5_log_dt, s5_b_re, s5_b_im, s5_c_re, s5_c_im, s5_d, s5_glu_w, s5_glu_b, w_branch_a, w_branch_b, w_branch_c, w_out, norm_mlp, w_mlp_up, w_mlp_down, norm_final):
    bsz, seq, _ = x.shape
    depth = w_in.shape[0]
    n = bsz * seq
    n_seg = seq // CMP_STRIDE
    gate_col = sum(IN_WIDTHS[:-1])
    bf = lambda w: w.astype(MXU_DTYPE)

    cos, sin = _rope_tables(positions)
    abr, abi, bbr, bbi = _s5_discretize(s5_a_re, s5_a_im, s5_log_dt, s5_b_re, s5_b_im)
    abr = abr.reshape(depth, 1, C_STATES)
    abi = abi.reshape(depth, 1, C_STATES)
    bbr = bbr.reshape(depth, C_GROUPS, C_GROUP_CH, C_STATE)
    bbi = bbi.reshape(depth, C_GROUPS, C_GROUP_CH, C_STATE)

    def to_lanes(t):
        return jnp.swapaxes(t.reshape(bsz, seq, t.shape[-1]), 1, 2)

    x2 = x.reshape(n, D_MODEL)
    for l in range(depth):
        qa, qb, ka, kvb, va, ga, uc = _in_projection(x2, norm_mix[l], _pack_w_in(w_in[l]), cos, sin)

        def segments(t):
            t = t[:, :LANES].reshape(bsz, n_seg, CMP_STRIDE, A_KV_HEADS, HEAD_DIM)
            return jnp.transpose(t, (0, 3, 1, 2, 4)).reshape(bsz, A_KV_HEADS, n_seg, CMP_STRIDE * HEAD_DIM)

        kcvc = _compress(jnp.stack([segments(ka), segments(va)]),
                         nsa_cmp_pos[l].reshape(2, 1, CMP_LEN * HEAD_DIM), nsa_cmp_w1[l], nsa_cmp_w2[l])
        kc = jnp.transpose(kcvc[0], (0, 2, 1, 3)).reshape(bsz, n_seg, LANES)
        vct = jnp.swapaxes(kcvc[1], 2, 3).reshape(bsz, LANES, n_seg)
        ya = _nsa_attention(to_lanes(qa), ka, to_lanes(va), kc, vct, ga, bsz, seq)

        yb = _swa_attention(to_lanes(qb), kvb, to_lanes(kvb[:, HEAD_DIM:]), swa_sinks[l], bsz, seq)

        bmat = jnp.concatenate([_block_diag(bbr[l]), _block_diag(bbi[l])], axis=1)
        cmat = jnp.concatenate([_block_diag(jnp.swapaxes(s5_c_re[l], 1, 2)),
                                -_block_diag(jnp.swapaxes(s5_c_im[l], 1, 2))], axis=0)
        abar = jnp.concatenate([abr[l], abi[l]], axis=1)
        u_tm = jnp.swapaxes(uc.reshape(bsz, seq, C_WIDTH), 0, 1)
        yc_tm = _s5(u_tm, bf(bmat), abar, bf(cmat), s5_d[l], s5_glu_w[l], s5_glu_b[l])
        yc = jnp.swapaxes(yc_tm, 0, 1).reshape(n, C_WIDTH)

        x2 = _merge(x2, ya, yb, yc, norm_mix[l], bf(w_in[l][:, gate_col:]), bf(w_branch_a[l]),
                    bf(w_branch_b[l]), bf(w_branch_c[l]), bf(w_out[l]))
        x2 = _mlp(x2, norm_mlp[l], bf(w_mlp_up[l]), bf(w_mlp_down[l]), norm_final, l == depth - 1)
    return x2.reshape(bsz, seq, D_MODEL)
```

```python
import functools

import jax
import jax.numpy as jnp
from jax import lax
from jax.experimental import pallas as pl
from jax.experimental.pallas import tpu as pltpu

D_MODEL = 1024
HEAD_DIM = 64
ROPE_THETA = 10000.0
NORM_EPS = 1e-6
QBLK = 128

A_HEADS = 8
A_KV_HEADS = 2
A_GROUP = A_HEADS // A_KV_HEADS
A_WIDTH = A_HEADS * HEAD_DIM
NSA_BRANCHES = 3
CMP_LEN = 32
CMP_STRIDE = 16
CMP_HIDDEN = 2 * HEAD_DIM
SEL_LEN = 64
SEL_TOPK = 16
A_WINDOW = 512
FORCED_SCORE = 1e4

B_HEADS = 8
B_WIDTH = B_HEADS * HEAD_DIM
B_WINDOW = 128

C_WIDTH = 512
C_GROUP_CH = 16
C_GROUPS = C_WIDTH // C_GROUP_CH
C_STATE = 64
C_STATES = C_GROUPS * C_STATE
_HALF_CH = C_WIDTH // 2
_HALF_ST = C_STATES // 2

D_FF = 4 * D_MODEL
N_BRANCH = 3

IN_WIDTHS = (A_WIDTH, NSA_BRANCHES * 2 * A_KV_HEADS * HEAD_DIM, NSA_BRANCHES * A_HEADS,
             B_WIDTH, 2 * HEAD_DIM, C_WIDTH, N_BRANCH * D_MODEL)

LANES = 128
SUBLANES = 8
VMEM_LIMIT = 56 * 1024 * 1024

MXU_DTYPE = jnp.bfloat16
ROW_TILE = 256
S5_CHUNK = 64
S5_LANES = 512
SEL_CHUNK = 256
FF_CHUNK = 1024

_ROPE_COLS = 12 * LANES
_PROJ_COLS = 20 * LANES

LOG2E = 1.4426950408889634
_NEG_INF = float("-inf")
_M_INIT = -1e30


def _f32(x):
    return x.astype(jnp.float32)


def _mm(a, b):
    return jnp.dot(a.astype(MXU_DTYPE), b.astype(MXU_DTYPE), preferred_element_type=jnp.float32)


def _mm_nt(a, b):
    return lax.dot_general(a.astype(MXU_DTYPE), b.astype(MXU_DTYPE), (((1,), (1,)), ((), ())),
                           preferred_element_type=jnp.float32)


def _rmsnorm(x, gain):
    return x * lax.rsqrt(jnp.mean(x * x, axis=-1, keepdims=True) + NORM_EPS) * gain


def _resident(shape):
    zeros = (0,) * len(shape)
    return pl.BlockSpec(shape, lambda *_: zeros, pipeline_mode=pl.Buffered(1))


def _params(sem):
    return pltpu.CompilerParams(dimension_semantics=sem, vmem_limit_bytes=VMEM_LIMIT)


def _rope_table_kernel(pos_ref, invf_ref, sign_ref, cos_ref, sin_ref):
    ang = _f32(pos_ref[...]) * invf_ref[...]
    cos_ref[...] = jnp.cos(ang)
    sin_ref[...] = jnp.sin(ang) * sign_ref[...]


def _rope_tables(positions):
    n = positions.size
    half = HEAD_DIM // 2
    inv_freq = ROPE_THETA ** (-jnp.arange(half, dtype=jnp.float32) / half)
    invf = jnp.tile(inv_freq, LANES // half).reshape(1, LANES)
    sign = jnp.tile(jnp.concatenate([-jnp.ones(half, jnp.float32), jnp.ones(half, jnp.float32)]),
                    LANES // HEAD_DIM).reshape(1, LANES)
    tm = min(1024, n)
    row = pl.BlockSpec((tm, LANES), lambda i: (i, 0))
    return pl.pallas_call(
        _rope_table_kernel,
        grid=(n // tm,),
        in_specs=[pl.BlockSpec((tm, 1), lambda i: (i, 0)), _resident((1, LANES)), _resident((1, LANES))],
        out_specs=[row, row],
        out_shape=[jax.ShapeDtypeStruct((n, LANES), jnp.float32)] * 2,
        compiler_params=_params(("parallel",)),
        name="rope_tables",
    )(positions.reshape(n, 1), invf, sign)


def _swap_halves(x):
    lane = lax.broadcasted_iota(jnp.int32, x.shape, 1)
    first = (lane & (HEAD_DIM - 1)) < HEAD_DIM // 2
    return jnp.where(first, pltpu.roll(x, LANES - HEAD_DIM // 2, 1), pltpu.roll(x, HEAD_DIM // 2, 1))


def _proj_kernel(x_ref, gain_ref, w_ref, cos_ref, sin_ref,
                 qa_ref, qb_ref, ka_ref, kvb_ref, va_ref, ga_ref, uc_ref):
    h = _rmsnorm(x_ref[...], gain_ref[...]).astype(MXU_DTYPE)
    cos = cos_ref[...]
    sin = sin_ref[...]

    rope_in = jnp.dot(h, w_ref[:, :_ROPE_COLS], preferred_element_type=jnp.float32)
    rest = jnp.dot(h, w_ref[:, _ROPE_COLS:], preferred_element_type=jnp.float32)

    def roped(j):
        xj = rope_in[:, j * LANES:(j + 1) * LANES]
        return xj, xj * cos + _swap_halves(xj) * sin

    for j in range(4):
        qa_ref[:, j * LANES:(j + 1) * LANES] = roped(j)[1].astype(qa_ref.dtype)
    for j in range(4):
        qb_ref[:, j * LANES:(j + 1) * LANES] = roped(4 + j)[1].astype(qb_ref.dtype)
    for j in range(3):
        ka_ref[:, j * LANES:(j + 1) * LANES] = roped(8 + j)[1].astype(ka_ref.dtype)
    raw, rot = roped(11)
    lane = lax.broadcasted_iota(jnp.int32, raw.shape, 1)
    kvb_ref[...] = jnp.where(lane < HEAD_DIM, rot, raw).astype(kvb_ref.dtype)
    va_ref[...] = rest[:, :3 * LANES].astype(va_ref.dtype)
    ga_ref[...] = jax.nn.sigmoid(rest[:, 3 * LANES:4 * LANES])
    uc_ref[...] = rest[:, 4 * LANES:]


def _pack_w_in(w_in):
    o = [0]
    for w in IN_WIDTHS:
        o.append(o[-1] + w)
    scale = HEAD_DIM ** -0.5 * LOG2E
    d = w_in.shape[0]
    q_a = w_in[:, o[0]:o[1]] * scale
    kv_a = w_in[:, o[1]:o[2]].reshape(d, NSA_BRANCHES, 2, A_KV_HEADS * HEAD_DIM)
    k_a = kv_a[:, :, 0].reshape(d, NSA_BRANCHES * LANES)
    v_a = kv_a[:, :, 1].reshape(d, NSA_BRANCHES * LANES)
    g_a = jnp.pad(w_in[:, o[2]:o[3]], ((0, 0), (0, LANES - NSA_BRANCHES * A_HEADS)))
    q_b = w_in[:, o[3]:o[4]] * scale
    kv_b = w_in[:, o[4]:o[5]]
    u_c = w_in[:, o[5]:o[6]]
    return jnp.concatenate([q_a, q_b, k_a, kv_b, v_a, g_a, u_c], axis=1).astype(MXU_DTYPE)


def _in_projection(x2, gain, w_packed, cos, sin):
    n = x2.shape[0]
    tm = min(ROW_TILE, n)

    def rows(width):
        return pl.BlockSpec((tm, width), lambda i: (i, 0))

    widths = (A_WIDTH, B_WIDTH, 3 * LANES, LANES, 3 * LANES, LANES, C_WIDTH)
    dtypes = (MXU_DTYPE, MXU_DTYPE, MXU_DTYPE, MXU_DTYPE, MXU_DTYPE, jnp.float32, jnp.float32)
    return pl.pallas_call(
        _proj_kernel,
        grid=(n // tm,),
        in_specs=[rows(D_MODEL), _resident((1, D_MODEL)), _resident((D_MODEL, _PROJ_COLS)),
                  rows(LANES), rows(LANES)],
        out_specs=[rows(w) for w in widths],
        out_shape=[jax.ShapeDtypeStruct((n, w), dt) for w, dt in zip(widths, dtypes)],
        compiler_params=_params(("parallel",)),
        name="in_projection",
    )(x2, gain.reshape(1, D_MODEL), w_packed, cos, sin)


def _compress_kernel(seg_ref, pos_ref, w1_ref, w2_ref, o_ref):
    seg = seg_ref[...]
    half = seg.shape[1]
    first = _mm(seg, w1_ref[:half, :])
    second = _mm(seg, w1_ref[half:, :])
    nseg = seg.shape[0]
    hidden = first + pltpu.roll(second, nseg - 1, 0) + _mm(pos_ref[...], w1_ref[...])
    o_ref[...] = _mm(jax.nn.gelu(hidden), w2_ref[...]).astype(o_ref.dtype)


def _compress(segs, pos_flat, w1, w2):
    _, bsz, hkv, nseg, segw = segs.shape
    return pl.pallas_call(
        _compress_kernel,
        grid=(2, bsz, hkv),
        in_specs=[
            pl.BlockSpec((None, None, None, nseg, segw), lambda j, b, h: (j, b, h, 0, 0)),
            pl.BlockSpec((None, 1, CMP_LEN * HEAD_DIM), lambda j, b, h: (j, 0, 0)),
            pl.BlockSpec((None, CMP_LEN * HEAD_DIM, CMP_HIDDEN), lambda j, b, h: (j, 0, 0)),
            pl.BlockSpec((None, CMP_HIDDEN, HEAD_DIM), lambda j, b, h: (j, 0, 0)),
        ],
        out_specs=pl.BlockSpec((None, None, None, nseg, HEAD_DIM), lambda j, b, h: (j, b, h, 0, 0)),
        out_shape=jax.ShapeDtypeStruct((2, bsz, hkv, nseg, HEAD_DIM), MXU_DTYPE),
        compiler_params=_params(("parallel", "parallel", "parallel")),
        name="nsa_compress",
    )(segs, pos_flat, w1, w2)


def _split3(x):
    hi = x.astype(MXU_DTYPE)
    r1 = x - _f32(hi)
    mid = r1.astype(MXU_DTYPE)
    lo = (r1 - _f32(mid)).astype(MXU_DTYPE)
    return hi, mid, lo


def _col_max(s):
    return jnp.max(s, axis=0, keepdims=True)


def _tile_lanes(x, n):
    return jnp.concatenate([x] * n, axis=1)


def _pad_head_rows(x, h, fill):
    other = jnp.full(x.shape, fill, x.dtype)
    return jnp.concatenate([x, other] if h == 0 else [other, x], axis=0)


def _store_transposed_pairs(o_ref, heads_t):
    for p in range(len(heads_t) // 2):
        pair = jnp.concatenate([heads_t[2 * p], heads_t[2 * p + 1]], axis=0)
        o_ref[:, p * LANES:(p + 1) * LANES] = pair.T.astype(o_ref.dtype)


def _nsa_kernel(qt_ref, k_ref, vt_ref, kc_ref, vct_ref, g_ref, c2s_ref, o_ref, selb_ref, sca_ref, scb_ref,
                *, seq):
    i = pl.program_id(1)
    q0 = i * QBLK
    n_cmp = kc_ref.shape[0]
    n_sel = seq // SEL_LEN
    kc_sz = min(SEL_CHUNK, seq)
    span = min(A_WINDOW + QBLK, seq)
    blocks_per_chunk = kc_sz // SEL_LEN

    gates_t = g_ref[...].T
    qp_lane = q0 + lax.broadcasted_iota(jnp.int32, (1, QBLK), 1)

    cmp_end = lax.broadcasted_iota(jnp.int32, (n_cmp, 1), 0) * CMP_STRIDE + (CMP_LEN - 1)
    bias_c = _tile_lanes(jnp.where(cmp_end <= qp_lane, 0.0, _NEG_INF), A_GROUP)

    w0 = pl.multiple_of(jnp.maximum(q0 + QBLK - span, 0), QBLK)
    diff = qp_lane - (w0 + lax.broadcasted_iota(jnp.int32, (span, 1), 0))
    bias_w = _tile_lanes(jnp.where(diff >= 0, jnp.where(diff < A_WINDOW, 0.0, _NEG_INF), _NEG_INF), A_GROUP)

    def head_rows(h):
        return slice(h * HEAD_DIM, (h + 1) * HEAD_DIM), slice((1 - h) * HEAD_DIM, (1 - h) * HEAD_DIM + 1)

    w_qs, o_cs, o_ws = [], [], []
    for h in range(A_KV_HEADS):
        own, ones_row = head_rows(h)
        q_t = jnp.concatenate([qt_ref[(h * A_GROUP + g) * HEAD_DIM:(h * A_GROUP + g + 1) * HEAD_DIM, :]
                               for g in range(A_GROUP)], axis=1)
        w_q = _pad_head_rows(q_t, h, 0.0)
        w_qs.append(w_q)

        s = _mm(kc_ref[...], w_q) + bias_c
        m = _col_max(s)
        m = jnp.where(m == _NEG_INF, 0.0, m)
        e = jnp.exp2(s - m)
        p = e * (1.0 / jnp.maximum(jnp.sum(e, axis=0, keepdims=True), 1e-30))
        o_cs.append(_mm(vct_ref[...], p)[own])

        p_sum = p[:, 0:QBLK]
        for g in range(1, A_GROUP):
            p_sum = p_sum + p[:, g * QBLK:(g + 1) * QBLK]
        imp_t = sum(_mm(c2s_ref[...], part) for part in _split3(p_sum))
        blk = lax.broadcasted_iota(jnp.int32, (n_sel, QBLK), 0)
        cur = jnp.right_shift(qp_lane, 6)
        forced = (blk == 0) | (blk == cur) | (blk == cur - 1)
        score = jnp.where(forced, FORCED_SCORE, imp_t)
        score = jnp.where(blk <= cur, score, _NEG_INF)
        n_grp = n_sel // SUBLANES
        sub = lax.broadcasted_iota(jnp.int32, (SUBLANES, QBLK), 0)
        score_grp = [score[SUBLANES * v:SUBLANES * (v + 1), :] for v in range(n_grp)]
        rank_grp = [jnp.zeros((SUBLANES, QBLK), jnp.float32) for _ in range(n_grp)]
        for j in range(n_sel):
            row = jnp.broadcast_to(score[j:j + 1, :], (SUBLANES, QBLK))
            vj, rj = divmod(j, SUBLANES)
            for v in range(n_grp):
                if v > vj:
                    inc = jnp.where(row >= score_grp[v], 1.0, 0.0)
                elif v < vj:
                    inc = jnp.where(row > score_grp[v], 1.0, 0.0)
                else:
                    inc = jnp.where(sub > rj, jnp.where(row >= score_grp[v], 1.0, 0.0),
                                    jnp.where(row > score_grp[v], 1.0, 0.0))
                rank_grp[v] = rank_grp[v] + inc
        rank = jnp.concatenate(rank_grp, axis=0)
        picked = jnp.where(rank < float(min(SEL_TOPK, n_sel)), 0.0, _NEG_INF)
        selb_ref[h] = jnp.where(blk <= cur, picked, _NEG_INF)

        sw = _mm(k_ref[pl.ds(w0, span), 2 * LANES:3 * LANES], w_q) + bias_w
        pw = jnp.exp2(sw - _col_max(sw))
        vt_w = _pad_head_rows(vt_ref[2 * LANES + h * HEAD_DIM:2 * LANES + (h + 1) * HEAD_DIM, pl.ds(w0, span)],
                              h, 1.0)
        acc_w = _mm(vt_w, pw)
        o_ws.append(acc_w[own] / acc_w[ones_row])

    n_chunks = (q0 + QBLK + kc_sz - 1) // kc_sz
    last = n_chunks - 1

    def issue_scores(c, dst_ref):
        k0 = pl.multiple_of(jnp.minimum(c, last) * kc_sz, kc_sz)
        keys = k_ref[pl.ds(k0, kc_sz), LANES:2 * LANES]
        for h in range(A_KV_HEADS):
            dst_ref[h] = _mm(keys, w_qs[h])

    def consume(c, src_ref, states):
        cc = jnp.minimum(c, last)
        k0 = pl.multiple_of(cc * kc_sz, kc_sz)
        causal = (c * kc_sz + lax.broadcasted_iota(jnp.int32, (kc_sz, 1), 0)) <= qp_lane
        out = []
        for h in range(A_KV_HEADS):
            m_run, acc = states[h]
            slabs = [jnp.broadcast_to(selb_ref[h, pl.ds(cc * blocks_per_chunk + jj, 1), :], (SEL_LEN, QBLK))
                     for jj in range(blocks_per_chunk)]
            bias = jnp.where(causal, jnp.concatenate(slabs, axis=0), _NEG_INF)
            sc = src_ref[h] + _tile_lanes(bias, A_GROUP)
            m_new = jnp.maximum(m_run, _col_max(sc))
            pr = jnp.exp2(sc - m_new)
            vt = _pad_head_rows(vt_ref[LANES + h * HEAD_DIM:LANES + (h + 1) * HEAD_DIM, pl.ds(k0, kc_sz)], h, 1.0)
            out.append((m_new, jnp.exp2(m_run - m_new) * acc + _mm(vt, pr)))
        return tuple(out)

    def sel_step(t, states):
        issue_scores(2 * t + 1, scb_ref)
        states = consume(2 * t, sca_ref, states)
        issue_scores(2 * t + 2, sca_ref)
        return consume(2 * t + 1, scb_ref, states)

    init = tuple((jnp.full((1, A_GROUP * QBLK), _M_INIT, jnp.float32),
                  jnp.zeros((LANES, A_GROUP * QBLK), jnp.float32)) for _ in range(A_KV_HEADS))
    issue_scores(0, sca_ref)
    sel_out = lax.fori_loop(0, (n_chunks + 1) // 2, sel_step, init)

    heads_t = []
    for h in range(A_KV_HEADS):
        own, ones_row = head_rows(h)
        acc_s = sel_out[h][1]
        o_s = acc_s[own] / jnp.maximum(acc_s[ones_row], 1e-30)
        for g in range(A_GROUP):
            hd = h * A_GROUP + g
            cols = slice(g * QBLK, (g + 1) * QBLK)
            heads_t.append(gates_t[hd:hd + 1, :] * o_cs[h][:, cols]
                           + gates_t[A_HEADS + hd:A_HEADS + hd + 1, :] * o_s[:, cols]
                           + gates_t[2 * A_HEADS + hd:2 * A_HEADS + hd + 1, :] * o_ws[h][:, cols])
    _store_transposed_pairs(o_ref, heads_t)


def _nsa_attention(qat, ka, vat, kc, vct, ga, bsz, seq):
    nblk = seq // QBLK
    n_seg = seq // CMP_STRIDE
    n_sel = seq // SEL_LEN
    assert seq >= A_WINDOW + QBLK and seq % SEL_CHUNK == 0
    c_start = jnp.arange(n_seg) * CMP_STRIDE
    s_start = jnp.arange(n_sel) * SEL_LEN
    overlap = (jnp.minimum(c_start[None, :] + CMP_LEN, s_start[:, None] + SEL_LEN)
               - jnp.maximum(c_start[None, :], s_start[:, None]))
    c2s = (jnp.clip(overlap, 0, None).astype(jnp.float32) / CMP_LEN).astype(MXU_DTYPE)
    return pl.pallas_call(
        functools.partial(_nsa_kernel, seq=seq),
        grid=(bsz, nblk),
        in_specs=[
            pl.BlockSpec((None, A_WIDTH, QBLK), lambda b, i: (b, 0, i)),
            pl.BlockSpec((seq, 3 * LANES), lambda b, i: (b, 0)),
            pl.BlockSpec((None, 3 * LANES, seq), lambda b, i: (b, 0, 0)),
            pl.BlockSpec((None, n_seg, LANES), lambda b, i: (b, 0, 0)),
            pl.BlockSpec((None, LANES, n_seg), lambda b, i: (b, 0, 0)),
            pl.BlockSpec((QBLK, LANES), lambda b, i: (b * nblk + i, 0)),
            _resident((n_sel, n_seg)),
        ],
        out_specs=pl.BlockSpec((QBLK, A_WIDTH), lambda b, i: (b * nblk + i, 0)),
        out_shape=jax.ShapeDtypeStruct((bsz * seq, A_WIDTH), MXU_DTYPE),
        scratch_shapes=[pltpu.VMEM((A_KV_HEADS, n_sel, QBLK), jnp.float32)]
        + [pltpu.VMEM((A_KV_HEADS, min(SEL_CHUNK, seq), A_GROUP * QBLK), jnp.float32)] * 2,
        compiler_params=_params(("parallel", "arbitrary")),
        name="nsa_attention",
    )(qat, ka, vat, kc, vct, ga, c2s)


def _swa_kernel(sink_ref, qt_ref, kv_ref, vt_ref, o_ref, *, seq):
    i = pl.program_id(1)
    q0 = i * QBLK
    span = min(B_WINDOW + QBLK, seq)
    w0 = pl.multiple_of(jnp.maximum(q0 + QBLK - span, 0), QBLK)
    qp_lane = q0 + lax.broadcasted_iota(jnp.int32, (1, QBLK), 1)
    diff = qp_lane - (w0 + lax.broadcasted_iota(jnp.int32, (span, 1), 0))
    bias = _tile_lanes(jnp.where(diff >= 0, jnp.where(diff < B_WINDOW, 0.0, _NEG_INF), _NEG_INF), B_HEADS)
    q_t = jnp.concatenate([qt_ref[hd * HEAD_DIM:(hd + 1) * HEAD_DIM, :] for hd in range(B_HEADS)], axis=1)
    s = _mm(kv_ref[pl.ds(w0, span), :], _pad_head_rows(q_t, 0, 0.0)) + bias
    sink = jnp.concatenate([jnp.full((1, QBLK), sink_ref[hd] * LOG2E, jnp.float32) for hd in range(B_HEADS)],
                           axis=1)
    m = jnp.maximum(_col_max(s), sink)
    acc = _mm(_pad_head_rows(vt_ref[:, pl.ds(w0, span)], 1, 1.0), jnp.exp2(s - m))
    o_t = acc[HEAD_DIM:] / (acc[0:1] + jnp.exp2(sink - m))
    _store_transposed_pairs(o_ref, [o_t[:, hd * QBLK:(hd + 1) * QBLK] for hd in range(B_HEADS)])


def _swa_attention(qbt, kvb, vbt, sinks, bsz, seq):
    nblk = seq // QBLK
    assert seq >= B_WINDOW + QBLK
    return pl.pallas_call(
        functools.partial(_swa_kernel, seq=seq),
        grid=(bsz, nblk),
        in_specs=[
            pl.BlockSpec(memory_space=pltpu.SMEM),
            pl.BlockSpec((None, B_WIDTH, QBLK), lambda b, i: (b, 0, i)),
            pl.BlockSpec((seq, LANES), lambda b, i: (b, 0)),
            pl.BlockSpec((None, HEAD_DIM, seq), lambda b, i: (b, 0, 0)),
        ],
        out_specs=pl.BlockSpec((QBLK, B_WIDTH), lambda b, i: (b * nblk + i, 0)),
        out_shape=jax.ShapeDtypeStruct((bsz * seq, B_WIDTH), MXU_DTYPE),
        compiler_params=_params(("parallel", "arbitrary")),
        name="swa_attention",
    )(sinks, qbt, kvb, vbt)


def _s5_discretize_kernel(are_ref, aim_ref, logdt_ref, bre_ref, bim_ref,
                          abr_ref, abi_ref, bbr_ref, bbi_ref):
    ar = are_ref[...]
    ai = aim_ref[...]
    dt = jnp.exp(logdt_ref[...])
    mag = jnp.exp(dt * ar)
    abr = mag * jnp.cos(dt * ai)
    abi = mag * jnp.sin(dt * ai)
    den = ar * ar + ai * ai
    nr = abr - 1.0
    coef_r = (nr * ar + abi * ai) / den
    coef_i = (abi * ar - nr * ai) / den
    abr_ref[...] = abr
    abi_ref[...] = abi
    br = bre_ref[...]
    bi = bim_ref[...]
    bbr_ref[...] = coef_r[:, None, :] * br - coef_i[:, None, :] * bi
    bbi_ref[...] = coef_r[:, None, :] * bi + coef_i[:, None, :] * br


def _s5_discretize(a_re, a_im, log_dt, b_re, b_im):
    lg = a_re.shape[0] * a_re.shape[1]
    a2 = lambda a: a.reshape(lg, C_STATE)
    b3 = lambda b: jnp.swapaxes(b.reshape(lg, C_STATE, C_GROUP_CH), 1, 2)
    sa = jax.ShapeDtypeStruct((lg, C_STATE), jnp.float32)
    sb = jax.ShapeDtypeStruct((lg, C_GROUP_CH, C_STATE), jnp.float32)
    return pl.pallas_call(_s5_discretize_kernel, out_shape=[sa, sa, sb, sb], name="s5_discretize")(
        a2(a_re), a2(a_im), log_dt.reshape(lg, 1), b3(b_re), b3(b_im))


def _s5_kernel(u_ref, bmat_ref, abar_ref, cmat_ref, d_ref, gw_ref, gb_ref, y_ref, state_ref, ut_ref, xs_ref):
    bsz, t_steps, _ = u_ref.shape

    @pl.when(pl.program_id(0) == 0)
    def _():
        state_ref[...] = jnp.zeros_like(state_ref)

    def gather(t, carry):
        ut_ref[pl.ds(pl.multiple_of(t * bsz, bsz), bsz), :] = u_ref[:, t, :]
        return carry

    lax.fori_loop(0, t_steps, gather, 0, unroll=8)
    u = ut_ref[...]
    for hf in range(2):
        bu = _mm(u[:, hf * _HALF_CH:(hf + 1) * _HALF_CH], bmat_ref[hf])
        xs_ref[:, hf * _HALF_ST:(hf + 1) * _HALF_ST] = bu[:, :_HALF_ST]
        xs_ref[:, C_STATES + hf * _HALF_ST:C_STATES + (hf + 1) * _HALF_ST] = bu[:, _HALF_ST:]

    for cg in range(C_STATES // S5_LANES):
        re = slice(cg * S5_LANES, (cg + 1) * S5_LANES)
        im = slice(C_STATES + cg * S5_LANES, C_STATES + (cg + 1) * S5_LANES)
        ar = jnp.broadcast_to(abar_ref[0:1, re], (bsz, S5_LANES))
        ai = jnp.broadcast_to(abar_ref[0:1, im], (bsz, S5_LANES))

        def step(t, carry):
            xr, xi = carry
            r0 = pl.multiple_of(t * bsz, bsz)
            nr = ar * xr - ai * xi + xs_ref[pl.ds(r0, bsz), re]
            ni = ar * xi + ai * xr + xs_ref[pl.ds(r0, bsz), im]
            xs_ref[pl.ds(r0, bsz), re] = nr
            xs_ref[pl.ds(r0, bsz), im] = ni
            return nr, ni

        xr, xi = lax.fori_loop(0, t_steps, step, (state_ref[:, re], state_ref[:, im]), unroll=8)
        state_ref[:, re] = xr
        state_ref[:, im] = xi

    ys = []
    for hf in range(2):
        xr = xs_ref[:, hf * _HALF_ST:(hf + 1) * _HALF_ST]
        xi = xs_ref[:, C_STATES + hf * _HALF_ST:C_STATES + (hf + 1) * _HALF_ST]
        ys.append(_mm(xr, cmat_ref[hf, :_HALF_ST, :]) + _mm(xi, cmat_ref[hf, _HALF_ST:, :]))
    y = jnp.concatenate(ys, axis=1) + d_ref[...] * u
    z = jax.nn.gelu(y)
    out = z * jax.nn.sigmoid(_mm(z, gw_ref[...]) + gb_ref[...])
    y_ref[...] = out.reshape(t_steps, bsz, C_WIDTH).astype(y_ref.dtype)


def _s5(u, bmat, abar, cmat, d, glu_w, glu_b):
    bsz, seq, _ = u.shape
    t_steps = min(S5_CHUNK, seq)
    return pl.pallas_call(
        _s5_kernel,
        grid=(seq // t_steps,),
        in_specs=[
            pl.BlockSpec((bsz, t_steps, C_WIDTH), lambda i: (0, i, 0)),
            _resident((2, _HALF_CH, 2 * _HALF_ST)), _resident((1, 2 * C_STATES)), _resident((2, 2 * _HALF_ST, _HALF_CH)),
            _resident((1, C_WIDTH)), _resident((C_WIDTH, C_WIDTH)), _resident((1, C_WIDTH)),
        ],
        out_specs=pl.BlockSpec((t_steps, bsz, C_WIDTH), lambda i: (i, 0, 0)),
        out_shape=jax.ShapeDtypeStruct((seq, bsz, C_WIDTH), MXU_DTYPE),
        scratch_shapes=[pltpu.VMEM((bsz, 2 * C_STATES), jnp.float32),
                        pltpu.VMEM((t_steps * bsz, C_WIDTH), jnp.float32),
                        pltpu.VMEM((t_steps * bsz, 2 * C_STATES), jnp.float32)],
        compiler_params=_params(("arbitrary",)),
        name="s5_scan",
    )(u, bmat, abar, cmat, d.reshape(1, C_WIDTH), glu_w.astype(MXU_DTYPE), glu_b.reshape(1, C_WIDTH))


def _block_diag(per_group):
    g, r, c = per_group.shape
    eye = jnp.eye(g, dtype=per_group.dtype)
    return jnp.einsum("grc,gk->grkc", per_group, eye).reshape(g * r, g * c)


def _merge_kernel(x_ref, ya_ref, yb_ref, yc_ref, gain_ref, wg_ref, pa_ref, pb_ref, pc_ref, wo_ref, o_ref):
    x = x_ref[...]
    h = _rmsnorm(x, gain_ref[...]).astype(MXU_DTYPE)
    merged = None
    for j, (y_ref, p_ref) in enumerate(((ya_ref, pa_ref), (yb_ref, pb_ref), (yc_ref, pc_ref))):
        gate = jax.nn.sigmoid(jnp.dot(h, wg_ref[:, j * D_MODEL:(j + 1) * D_MODEL],
                                      preferred_element_type=jnp.float32))
        term = gate * jnp.dot(y_ref[...], p_ref[...], preferred_element_type=jnp.float32)
        merged = term if merged is None else merged + term
    o_ref[...] = x + _mm(merged, wo_ref[...])


def _merge(x2, ya, yb, yc, gain, wg, pa, pb, pc, wo):
    n = x2.shape[0]
    tm = min(ROW_TILE, n)

    def rows(width):
        return pl.BlockSpec((tm, width), lambda i: (i, 0))

    return pl.pallas_call(
        _merge_kernel,
        grid=(n // tm,),
        in_specs=[rows(D_MODEL), rows(A_WIDTH), rows(B_WIDTH), rows(C_WIDTH), _resident((1, D_MODEL)),
                  _resident((D_MODEL, N_BRANCH * D_MODEL)), _resident((A_WIDTH, D_MODEL)),
                  _resident((B_WIDTH, D_MODEL)), _resident((C_WIDTH, D_MODEL)), _resident((D_MODEL, D_MODEL))],
        out_specs=rows(D_MODEL),
        out_shape=jax.ShapeDtypeStruct((n, D_MODEL), jnp.float32),
        compiler_params=_params(("parallel",)),
        name="merge",
    )(x2, ya, yb, yc, gain.reshape(1, D_MODEL), wg, pa, pb, pc, wo)


def _mlp_kernel(x_ref, gain_ref, wu_ref, wd_ref, fgain_ref, o_ref, *, final_norm):
    x = x_ref[...]
    h = _rmsnorm(x, gain_ref[...]).astype(MXU_DTYPE)
    acc = x
    for c in range(D_FF // FF_CHUNK):
        cols = slice(c * FF_CHUNK, (c + 1) * FF_CHUNK)
        up = jnp.dot(h, wu_ref[:, cols], preferred_element_type=jnp.float32)
        acc = acc + _mm(jnp.square(jnp.maximum(up, 0.0)), wd_ref[cols, :])
    o_ref[...] = _rmsnorm(acc, fgain_ref[...]) if final_norm else acc


def _mlp(x2, gain, wu, wd, fgain, final_norm):
    n = x2.shape[0]
    tm = min(ROW_TILE, n)
    rows = pl.BlockSpec((tm, D_MODEL), lambda i: (i, 0))
    return pl.pallas_call(
        functools.partial(_mlp_kernel, final_norm=final_norm),
        grid=(n // tm,),
        in_specs=[rows, _resident((1, D_MODEL)), _resident((D_MODEL, D_FF)), _resident((D_FF, D_MODEL)),
                  _resident((1, D_MODEL))],
        out_specs=rows,
        out_shape=jax.ShapeDtypeStruct((n, D_MODEL), jnp.float32),
        compiler_params=_params(("parallel",)),
        name="mlp",
    )(x2, gain.reshape(1, D_MODEL), wu, wd, fgain.reshape(1, D_MODEL))


def kernel(x, positions, norm_mix, w_in, nsa_cmp_pos, nsa_cmp_w1, nsa_cmp_w2, swa_sinks, s5_a_re, s5_a_im, s5_log_dt, s5_b_re, s5_b_im, s5_c_re, s5_c_im, s5_d, s5_glu_w, s5_glu_b, w_branch_a, w_branch_b, w_branch_c, w_out, norm_mlp, w_mlp_up, w_mlp_down, norm_final):
    bsz, seq, _ = x.shape
    depth = w_in.shape[0]
    n = bsz * seq
    n_seg = seq // CMP_STRIDE
    gate_col = sum(IN_WIDTHS[:-1])
    bf = lambda w: w.astype(MXU_DTYPE)

    cos, sin = _rope_tables(positions)
    abr, abi, bbr, bbi = _s5_discretize(s5_a_re, s5_a_im, s5_log_dt, s5_b_re, s5_b_im)
    abr = abr.reshape(depth, 1, C_STATES)
    abi = abi.reshape(depth, 1, C_STATES)
    bbr = bbr.reshape(depth, C_GROUPS, C_GROUP_CH, C_STATE)
    bbi = bbi.reshape(depth, C_GROUPS, C_GROUP_CH, C_STATE)

    def to_lanes(t):
        return jnp.swapaxes(t.reshape(bsz, seq, t.shape[-1]), 1, 2)

    x2 = x.reshape(n, D_MODEL)
    for l in range(depth):
        qa, qb, ka, kvb, va, ga, uc = _in_projection(x2, norm_mix[l], _pack_w_in(w_in[l]), cos, sin)

        def segments(t):
            t = t[:, :LANES].reshape(bsz, n_seg, CMP_STRIDE, A_KV_HEADS, HEAD_DIM)
            return jnp.transpose(t, (0, 3, 1, 2, 4)).reshape(bsz, A_KV_HEADS, n_seg, CMP_STRIDE * HEAD_DIM)

        kcvc = _compress(jnp.stack([segments(ka), segments(va)]),
                         nsa_cmp_pos[l].reshape(2, 1, CMP_LEN * HEAD_DIM), nsa_cmp_w1[l], nsa_cmp_w2[l])
        kc = jnp.transpose(kcvc[0], (0, 2, 1, 3)).reshape(bsz, n_seg, LANES)
        vct = jnp.swapaxes(kcvc[1], 2, 3).reshape(bsz, LANES, n_seg)
        ya = _nsa_attention(to_lanes(qa), ka, to_lanes(va), kc, vct, ga, bsz, seq)

        yb = _swa_attention(to_lanes(qb), kvb, to_lanes(kvb[:, HEAD_DIM:]), swa_sinks[l], bsz, seq)

        cre_t = jnp.swapaxes(s5_c_re[l], 1, 2)
        cim_t = jnp.swapaxes(s5_c_im[l], 1, 2)
        halves = (slice(0, C_GROUPS // 2), slice(C_GROUPS // 2, C_GROUPS))
        bmat = jnp.stack([jnp.concatenate([_block_diag(bbr[l][g]), _block_diag(bbi[l][g])], axis=1)
                          for g in halves])
        cmat = jnp.stack([jnp.concatenate([_block_diag(cre_t[g]), -_block_diag(cim_t[g])], axis=0)
                          for g in halves])
        abar = jnp.concatenate([abr[l], abi[l]], axis=1)
        yc_tm = _s5(uc.reshape(bsz, seq, C_WIDTH), bf(bmat), abar, bf(cmat), s5_d[l], s5_glu_w[l], s5_glu_b[l])
        yc = jnp.swapaxes(yc_tm, 0, 1).reshape(n, C_WIDTH)

        x2 = _merge(x2, ya, yb, yc, norm_mix[l], bf(w_in[l][:, gate_col:]), bf(w_branch_a[l]),
                    bf(w_branch_b[l]), bf(w_branch_c[l]), bf(w_out[l]))
        x2 = _mlp(x2, norm_mlp[l], bf(w_mlp_up[l]), bf(w_mlp_down[l]), norm_final, l == depth - 1)
    return x2.reshape(bsz, seq, D_MODEL)
```

```python
import functools

import jax
import jax.numpy as jnp
from jax import lax
from jax.experimental import pallas as pl
from jax.experimental.pallas import tpu as pltpu

D_MODEL = 1024
HEAD_DIM = 64
ROPE_THETA = 10000.0
NORM_EPS = 1e-6
QBLK = 128

A_HEADS = 8
A_KV_HEADS = 2
A_GROUP = A_HEADS // A_KV_HEADS
A_WIDTH = A_HEADS * HEAD_DIM
NSA_BRANCHES = 3
CMP_LEN = 32
CMP_STRIDE = 16
CMP_HIDDEN = 2 * HEAD_DIM
SEL_LEN = 64
SEL_TOPK = 16
A_WINDOW = 512
FORCED_SCORE = 1e4

B_HEADS = 8
B_WIDTH = B_HEADS * HEAD_DIM
B_WINDOW = 128

C_WIDTH = 512
C_GROUP_CH = 16
C_GROUPS = C_WIDTH // C_GROUP_CH
C_STATE = 64
C_STATES = C_GROUPS * C_STATE
_HALF_CH = C_WIDTH // 2
_HALF_ST = C_STATES // 2

D_FF = 4 * D_MODEL
N_BRANCH = 3

IN_WIDTHS = (A_WIDTH, NSA_BRANCHES * 2 * A_KV_HEADS * HEAD_DIM, NSA_BRANCHES * A_HEADS,
             B_WIDTH, 2 * HEAD_DIM, C_WIDTH, N_BRANCH * D_MODEL)

LANES = 128
SUBLANES = 8
VMEM_LIMIT = 56 * 1024 * 1024

MXU_DTYPE = jnp.bfloat16
ROW_TILE = 256
S5_CHUNK = 64
S5_LANES = 512
SEL_CHUNK = 256
SWA_QSUB = 2
FF_CHUNK = 1024

_ROPE_COLS = 12 * LANES
_PROJ_COLS = 20 * LANES

LOG2E = 1.4426950408889634
_NEG_INF = float("-inf")
_M_INIT = -1e30


def _f32(x):
    return x.astype(jnp.float32)


def _mm(a, b):
    return jnp.dot(a.astype(MXU_DTYPE), b.astype(MXU_DTYPE), preferred_element_type=jnp.float32)


def _mm_nt(a, b):
    return lax.dot_general(a.astype(MXU_DTYPE), b.astype(MXU_DTYPE), (((1,), (1,)), ((), ())),
                           preferred_element_type=jnp.float32)


def _rmsnorm(x, gain):
    return x * lax.rsqrt(jnp.mean(x * x, axis=-1, keepdims=True) + NORM_EPS) * gain


def _resident(shape):
    zeros = (0,) * len(shape)
    return pl.BlockSpec(shape, lambda *_: zeros, pipeline_mode=pl.Buffered(1))


def _params(sem):
    return pltpu.CompilerParams(dimension_semantics=sem, vmem_limit_bytes=VMEM_LIMIT)


def _rope_table_kernel(pos_ref, invf_ref, sign_ref, cos_ref, sin_ref):
    ang = _f32(pos_ref[...]) * invf_ref[...]
    cos_ref[...] = jnp.cos(ang)
    sin_ref[...] = jnp.sin(ang) * sign_ref[...]


def _rope_tables(positions):
    n = positions.size
    half = HEAD_DIM // 2
    inv_freq = ROPE_THETA ** (-jnp.arange(half, dtype=jnp.float32) / half)
    invf = jnp.tile(inv_freq, LANES // half).reshape(1, LANES)
    sign = jnp.tile(jnp.concatenate([-jnp.ones(half, jnp.float32), jnp.ones(half, jnp.float32)]),
                    LANES // HEAD_DIM).reshape(1, LANES)
    tm = min(1024, n)
    row = pl.BlockSpec((tm, LANES), lambda i: (i, 0))
    return pl.pallas_call(
        _rope_table_kernel,
        grid=(n // tm,),
        in_specs=[pl.BlockSpec((tm, 1), lambda i: (i, 0)), _resident((1, LANES)), _resident((1, LANES))],
        out_specs=[row, row],
        out_shape=[jax.ShapeDtypeStruct((n, LANES), jnp.float32)] * 2,
        compiler_params=_params(("parallel",)),
        name="rope_tables",
    )(positions.reshape(n, 1), invf, sign)


def _swap_halves(x):
    lane = lax.broadcasted_iota(jnp.int32, x.shape, 1)
    first = (lane & (HEAD_DIM - 1)) < HEAD_DIM // 2
    return jnp.where(first, pltpu.roll(x, LANES - HEAD_DIM // 2, 1), pltpu.roll(x, HEAD_DIM // 2, 1))


def _proj_kernel(x_ref, gain_ref, w_ref, cos_ref, sin_ref,
                 qa_ref, qb_ref, ka_ref, kvb_ref, va_ref, ga_ref, uc_ref):
    h = _rmsnorm(x_ref[...], gain_ref[...]).astype(MXU_DTYPE)
    cos = cos_ref[...]
    sin = sin_ref[...]

    rope_in = jnp.dot(h, w_ref[:, :_ROPE_COLS], preferred_element_type=jnp.float32)
    rest = jnp.dot(h, w_ref[:, _ROPE_COLS:], preferred_element_type=jnp.float32)

    def roped(j):
        xj = rope_in[:, j * LANES:(j + 1) * LANES]
        return xj, xj * cos + _swap_halves(xj) * sin

    for j in range(4):
        qa_ref[:, j * LANES:(j + 1) * LANES] = roped(j)[1].astype(qa_ref.dtype)
    for j in range(4):
        qb_ref[:, j * LANES:(j + 1) * LANES] = roped(4 + j)[1].astype(qb_ref.dtype)
    for j in range(3):
        ka_ref[:, j * LANES:(j + 1) * LANES] = roped(8 + j)[1].astype(ka_ref.dtype)
    raw, rot = roped(11)
    lane = lax.broadcasted_iota(jnp.int32, raw.shape, 1)
    kvb_ref[...] = jnp.where(lane < HEAD_DIM, rot, raw).astype(kvb_ref.dtype)
    va_ref[...] = rest[:, :3 * LANES].astype(va_ref.dtype)
    ga_ref[...] = jax.nn.sigmoid(rest[:, 3 * LANES:4 * LANES])
    uc_ref[...] = rest[:, 4 * LANES:]


def _pack_w_in(w_in):
    o = [0]
    for w in IN_WIDTHS:
        o.append(o[-1] + w)
    scale = HEAD_DIM ** -0.5 * LOG2E
    d = w_in.shape[0]
    q_a = w_in[:, o[0]:o[1]] * scale
    kv_a = w_in[:, o[1]:o[2]].reshape(d, NSA_BRANCHES, 2, A_KV_HEADS * HEAD_DIM)
    k_a = kv_a[:, :, 0].reshape(d, NSA_BRANCHES * LANES)
    v_a = kv_a[:, :, 1].reshape(d, NSA_BRANCHES * LANES)
    g_a = jnp.pad(w_in[:, o[2]:o[3]], ((0, 0), (0, LANES - NSA_BRANCHES * A_HEADS)))
    q_b = w_in[:, o[3]:o[4]] * scale
    kv_b = w_in[:, o[4]:o[5]]
    u_c = w_in[:, o[5]:o[6]]
    return jnp.concatenate([q_a, q_b, k_a, kv_b, v_a, g_a, u_c], axis=1).astype(MXU_DTYPE)


def _in_projection(x2, gain, w_packed, cos, sin):
    n = x2.shape[0]
    tm = min(ROW_TILE, n)

    def rows(width):
        return pl.BlockSpec((tm, width), lambda i: (i, 0))

    widths = (A_WIDTH, B_WIDTH, 3 * LANES, LANES, 3 * LANES, LANES, C_WIDTH)
    dtypes = (MXU_DTYPE, MXU_DTYPE, MXU_DTYPE, MXU_DTYPE, MXU_DTYPE, jnp.float32, jnp.float32)
    return pl.pallas_call(
        _proj_kernel,
        grid=(n // tm,),
        in_specs=[rows(D_MODEL), _resident((1, D_MODEL)), _resident((D_MODEL, _PROJ_COLS)),
                  rows(LANES), rows(LANES)],
        out_specs=[rows(w) for w in widths],
        out_shape=[jax.ShapeDtypeStruct((n, w), dt) for w, dt in zip(widths, dtypes)],
        compiler_params=_params(("parallel",)),
        name="in_projection",
    )(x2, gain.reshape(1, D_MODEL), w_packed, cos, sin)


def _compress_kernel(seg_ref, pos_ref, w1_ref, w2_ref, o_ref):
    seg = seg_ref[...]
    half = seg.shape[1]
    first = _mm(seg, w1_ref[:half, :])
    second = _mm(seg, w1_ref[half:, :])
    nseg = seg.shape[0]
    hidden = first + pltpu.roll(second, nseg - 1, 0) + _mm(pos_ref[...], w1_ref[...])
    o_ref[...] = _mm(jax.nn.gelu(hidden), w2_ref[...]).astype(o_ref.dtype)


def _compress(segs, pos_flat, w1, w2):
    _, bsz, hkv, nseg, segw = segs.shape
    return pl.pallas_call(
        _compress_kernel,
        grid=(2, bsz, hkv),
        in_specs=[
            pl.BlockSpec((None, None, None, nseg, segw), lambda j, b, h: (j, b, h, 0, 0)),
            pl.BlockSpec((None, 1, CMP_LEN * HEAD_DIM), lambda j, b, h: (j, 0, 0)),
            pl.BlockSpec((None, CMP_LEN * HEAD_DIM, CMP_HIDDEN), lambda j, b, h: (j, 0, 0)),
            pl.BlockSpec((None, CMP_HIDDEN, HEAD_DIM), lambda j, b, h: (j, 0, 0)),
        ],
        out_specs=pl.BlockSpec((None, None, None, nseg, HEAD_DIM), lambda j, b, h: (j, b, h, 0, 0)),
        out_shape=jax.ShapeDtypeStruct((2, bsz, hkv, nseg, HEAD_DIM), MXU_DTYPE),
        compiler_params=_params(("parallel", "parallel", "parallel")),
        name="nsa_compress",
    )(segs, pos_flat, w1, w2)


def _split3(x):
    hi = x.astype(MXU_DTYPE)
    r1 = x - _f32(hi)
    mid = r1.astype(MXU_DTYPE)
    lo = (r1 - _f32(mid)).astype(MXU_DTYPE)
    return hi, mid, lo


def _col_max(s):
    return jnp.max(s, axis=0, keepdims=True)


def _tile_lanes(x, n):
    return jnp.concatenate([x] * n, axis=1)


def _pad_head_rows(x, h, fill):
    other = jnp.full(x.shape, fill, x.dtype)
    return jnp.concatenate([x, other] if h == 0 else [other, x], axis=0)


def _store_transposed_pairs(o_ref, heads_t, row0=0):
    for p in range(len(heads_t) // 2):
        pair = jnp.concatenate([heads_t[2 * p], heads_t[2 * p + 1]], axis=0)
        o_ref[row0:row0 + QBLK, p * LANES:(p + 1) * LANES] = pair.T.astype(o_ref.dtype)


def _nsa_kernel(qt_ref, k_ref, vt_ref, kc_ref, vct_ref, g_ref, c2s_ref, o_ref,
                selb_ref, sca_ref, scb_ref, cmp_ref, win_ref, *, seq):
    i = pl.program_id(1)
    q0 = i * QBLK
    n_cmp = kc_ref.shape[0]
    n_sel = seq // SEL_LEN
    kc_sz = min(SEL_CHUNK, seq)
    span = min(A_WINDOW + QBLK, seq)
    blocks_per_chunk = kc_sz // SEL_LEN

    gates_t = g_ref[...].T
    qp_lane = q0 + lax.broadcasted_iota(jnp.int32, (1, QBLK), 1)

    cmp_end = lax.broadcasted_iota(jnp.int32, (n_cmp, 1), 0) * CMP_STRIDE + (CMP_LEN - 1)
    bias_c = _tile_lanes(jnp.where(cmp_end <= qp_lane, 0.0, _NEG_INF), A_GROUP)

    w0 = pl.multiple_of(jnp.maximum(q0 + QBLK - span, 0), QBLK)
    diff = qp_lane - (w0 + lax.broadcasted_iota(jnp.int32, (span, 1), 0))
    bias_w = _tile_lanes(jnp.where(diff >= 0, jnp.where(diff < A_WINDOW, 0.0, _NEG_INF), _NEG_INF), A_GROUP)

    def head_rows(h):
        return slice(h * HEAD_DIM, (h + 1) * HEAD_DIM), slice((1 - h) * HEAD_DIM, (1 - h) * HEAD_DIM + 1)

    w_qs = []
    for h in range(A_KV_HEADS):
        q_t = jnp.concatenate([qt_ref[(h * A_GROUP + g) * HEAD_DIM:(h * A_GROUP + g + 1) * HEAD_DIM, :]
                               for g in range(A_GROUP)], axis=1)
        w_qs.append(_pad_head_rows(q_t, h, 0.0))

    n_chunks = (q0 + QBLK + kc_sz - 1) // kc_sz
    last = n_chunks - 1

    def issue_scores(c, dst_ref):
        k0 = pl.multiple_of(jnp.minimum(c, last) * kc_sz, kc_sz)
        keys = k_ref[pl.ds(k0, kc_sz), LANES:2 * LANES]
        for h in range(A_KV_HEADS):
            dst_ref[h] = _mm(keys, w_qs[h])

    for h in range(A_KV_HEADS):
        cmp_ref[h] = _mm(kc_ref[...], w_qs[h])
    for h in range(A_KV_HEADS):
        win_ref[h] = _mm(k_ref[pl.ds(w0, span), 2 * LANES:3 * LANES], w_qs[h])
    issue_scores(0, sca_ref)

    o_cs, o_ws = [], []
    for h in range(A_KV_HEADS):
        own, ones_row = head_rows(h)

        s = cmp_ref[h] + bias_c
        m = _col_max(s)
        m = jnp.where(m == _NEG_INF, 0.0, m)
        e = jnp.exp2(s - m)
        p = e * (1.0 / jnp.maximum(jnp.sum(e, axis=0, keepdims=True), 1e-30))
        o_cs.append(_mm(vct_ref[...], p)[own])

        p_sum = p[:, 0:QBLK]
        for g in range(1, A_GROUP):
            p_sum = p_sum + p[:, g * QBLK:(g + 1) * QBLK]
        imp_t = sum(_mm(c2s_ref[...], part) for part in _split3(p_sum))
        blk = lax.broadcasted_iota(jnp.int32, (n_sel, QBLK), 0)
        cur = jnp.right_shift(qp_lane, 6)
        forced = (blk == 0) | (blk == cur) | (blk == cur - 1)
        score = jnp.where(forced, FORCED_SCORE, imp_t)
        score = jnp.where(blk <= cur, score, _NEG_INF)
        n_grp = n_sel // SUBLANES
        sub = lax.broadcasted_iota(jnp.int32, (SUBLANES, QBLK), 0)
        score_grp = [score[SUBLANES * v:SUBLANES * (v + 1), :] for v in range(n_grp)]
        rank_grp = [jnp.zeros((SUBLANES, QBLK), jnp.float32) for _ in range(n_grp)]
        for j in range(n_sel):
            row = jnp.broadcast_to(score[j:j + 1, :], (SUBLANES, QBLK))
            vj, rj = divmod(j, SUBLANES)
            for v in range(n_grp):
                if v > vj:
                    inc = jnp.where(row >= score_grp[v], 1.0, 0.0)
                elif v < vj:
                    inc = jnp.where(row > score_grp[v], 1.0, 0.0)
                else:
                    inc = jnp.where(sub > rj, jnp.where(row >= score_grp[v], 1.0, 0.0),
                                    jnp.where(row > score_grp[v], 1.0, 0.0))
                rank_grp[v] = rank_grp[v] + inc
        rank = jnp.concatenate(rank_grp, axis=0)
        picked = jnp.where(rank < float(min(SEL_TOPK, n_sel)), 0.0, _NEG_INF)
        selb_ref[h] = jnp.where(blk <= cur, picked, _NEG_INF)

    for h in range(A_KV_HEADS):
        own, ones_row = head_rows(h)
        sw = win_ref[h] + bias_w
        pw = jnp.exp2(sw - _col_max(sw))
        vt_w = _pad_head_rows(vt_ref[2 * LANES + h * HEAD_DIM:2 * LANES + (h + 1) * HEAD_DIM, pl.ds(w0, span)],
                              h, 1.0)
        acc_w = _mm(vt_w, pw)
        o_ws.append(acc_w[own] / acc_w[ones_row])

    def consume(c, src_ref, states):
        cc = jnp.minimum(c, last)
        k0 = pl.multiple_of(cc * kc_sz, kc_sz)
        causal = (c * kc_sz + lax.broadcasted_iota(jnp.int32, (kc_sz, 1), 0)) <= qp_lane
        out = []
        for h in range(A_KV_HEADS):
            m_run, acc = states[h]
            slabs = [jnp.broadcast_to(selb_ref[h, pl.ds(cc * blocks_per_chunk + jj, 1), :], (SEL_LEN, QBLK))
                     for jj in range(blocks_per_chunk)]
            bias = jnp.where(causal, jnp.concatenate(slabs, axis=0), _NEG_INF)
            sc = src_ref[h] + _tile_lanes(bias, A_GROUP)
            m_new = jnp.maximum(m_run, _col_max(sc))
            pr = jnp.exp2(sc - m_new)
            vt = _pad_head_rows(vt_ref[LANES + h * HEAD_DIM:LANES + (h + 1) * HEAD_DIM, pl.ds(k0, kc_sz)], h, 1.0)
            out.append((m_new, jnp.exp2(m_run - m_new) * acc + _mm(vt, pr)))
        return tuple(out)

    def sel_step(t, states):
        issue_scores(2 * t + 1, scb_ref)
        states = consume(2 * t, sca_ref, states)
        issue_scores(2 * t + 2, sca_ref)
        return consume(2 * t + 1, scb_ref, states)

    init = tuple((jnp.full((1, A_GROUP * QBLK), _M_INIT, jnp.float32),
                  jnp.zeros((LANES, A_GROUP * QBLK), jnp.float32)) for _ in range(A_KV_HEADS))
    sel_out = lax.fori_loop(0, (n_chunks + 1) // 2, sel_step, init)

    heads_t = []
    for h in range(A_KV_HEADS):
        own, ones_row = head_rows(h)
        acc_s = sel_out[h][1]
        o_s = acc_s[own] / jnp.maximum(acc_s[ones_row], 1e-30)
        for g in range(A_GROUP):
            hd = h * A_GROUP + g
            cols = slice(g * QBLK, (g + 1) * QBLK)
            heads_t.append(gates_t[hd:hd + 1, :] * o_cs[h][:, cols]
                           + gates_t[A_HEADS + hd:A_HEADS + hd + 1, :] * o_s[:, cols]
                           + gates_t[2 * A_HEADS + hd:2 * A_HEADS + hd + 1, :] * o_ws[h][:, cols])
    _store_transposed_pairs(o_ref, heads_t)


def _nsa_attention(qat, ka, vat, kc, vct, ga, bsz, seq):
    nblk = seq // QBLK
    n_seg = seq // CMP_STRIDE
    n_sel = seq // SEL_LEN
    assert seq >= A_WINDOW + QBLK and seq % SEL_CHUNK == 0
    c_start = jnp.arange(n_seg) * CMP_STRIDE
    s_start = jnp.arange(n_sel) * SEL_LEN
    overlap = (jnp.minimum(c_start[None, :] + CMP_LEN, s_start[:, None] + SEL_LEN)
               - jnp.maximum(c_start[None, :], s_start[:, None]))
    c2s = (jnp.clip(overlap, 0, None).astype(jnp.float32) / CMP_LEN).astype(MXU_DTYPE)
    return pl.pallas_call(
        functools.partial(_nsa_kernel, seq=seq),
        grid=(bsz, nblk),
        in_specs=[
            pl.BlockSpec((None, A_WIDTH, QBLK), lambda b, i: (b, 0, i)),
            pl.BlockSpec((seq, 3 * LANES), lambda b, i: (b, 0)),
            pl.BlockSpec((None, 3 * LANES, seq), lambda b, i: (b, 0, 0)),
            pl.BlockSpec((None, n_seg, LANES), lambda b, i: (b, 0, 0)),
            pl.BlockSpec((None, LANES, n_seg), lambda b, i: (b, 0, 0)),
            pl.BlockSpec((QBLK, LANES), lambda b, i: (b * nblk + i, 0)),
            _resident((n_sel, n_seg)),
        ],
        out_specs=pl.BlockSpec((QBLK, A_WIDTH), lambda b, i: (b * nblk + i, 0)),
        out_shape=jax.ShapeDtypeStruct((bsz * seq, A_WIDTH), MXU_DTYPE),
        scratch_shapes=[pltpu.VMEM((A_KV_HEADS, n_sel, QBLK), jnp.float32)]
        + [pltpu.VMEM((A_KV_HEADS, min(SEL_CHUNK, seq), A_GROUP * QBLK), jnp.float32)] * 2
        + [pltpu.VMEM((A_KV_HEADS, n_seg, A_GROUP * QBLK), jnp.float32),
           pltpu.VMEM((A_KV_HEADS, min(A_WINDOW + QBLK, seq), A_GROUP * QBLK), jnp.float32)],
        compiler_params=_params(("parallel", "arbitrary")),
        name="nsa_attention",
    )(qat, ka, vat, kc, vct, ga, c2s)


def _swa_kernel(sink_ref, qt_ref, kv_ref, vt_ref, o_ref, sc_ref, *, seq):
    i = pl.program_id(1)
    span = min(B_WINDOW + QBLK, seq)
    sink = jnp.concatenate([jnp.full((1, QBLK), sink_ref[hd] * LOG2E, jnp.float32) for hd in range(B_HEADS)],
                           axis=1)
    starts = []
    for sb in range(SWA_QSUB):
        q0 = (i * SWA_QSUB + sb) * QBLK
        w0 = pl.multiple_of(jnp.maximum(q0 + QBLK - span, 0), QBLK)
        starts.append((q0, w0))
        q_t = jnp.concatenate([qt_ref[hd * HEAD_DIM:(hd + 1) * HEAD_DIM, sb * QBLK:(sb + 1) * QBLK]
                               for hd in range(B_HEADS)], axis=1)
        sc_ref[sb] = _mm(kv_ref[pl.ds(w0, span), :], _pad_head_rows(q_t, 0, 0.0))
    for sb, (q0, w0) in enumerate(starts):
        qp_lane = q0 + lax.broadcasted_iota(jnp.int32, (1, QBLK), 1)
        diff = qp_lane - (w0 + lax.broadcasted_iota(jnp.int32, (span, 1), 0))
        bias = _tile_lanes(jnp.where(diff >= 0, jnp.where(diff < B_WINDOW, 0.0, _NEG_INF), _NEG_INF), B_HEADS)
        s = sc_ref[sb] + bias
        m = jnp.maximum(_col_max(s), sink)
        acc = _mm(_pad_head_rows(vt_ref[:, pl.ds(w0, span)], 1, 1.0), jnp.exp2(s - m))
        o_t = acc[HEAD_DIM:] / (acc[0:1] + jnp.exp2(sink - m))
        _store_transposed_pairs(o_ref, [o_t[:, hd * QBLK:(hd + 1) * QBLK] for hd in range(B_HEADS)], sb * QBLK)


def _swa_attention(qbt, kvb, vbt, sinks, bsz, seq):
    rows = SWA_QSUB * QBLK
    nstep = seq // rows
    span = min(B_WINDOW + QBLK, seq)
    assert seq >= B_WINDOW + QBLK and seq % rows == 0
    return pl.pallas_call(
        functools.partial(_swa_kernel, seq=seq),
        grid=(bsz, nstep),
        in_specs=[
            pl.BlockSpec(memory_space=pltpu.SMEM),
            pl.BlockSpec((None, B_WIDTH, rows), lambda b, i: (b, 0, i)),
            pl.BlockSpec((seq, LANES), lambda b, i: (b, 0)),
            pl.BlockSpec((None, HEAD_DIM, seq), lambda b, i: (b, 0, 0)),
        ],
        out_specs=pl.BlockSpec((rows, B_WIDTH), lambda b, i: (b * nstep + i, 0)),
        out_shape=jax.ShapeDtypeStruct((bsz * seq, B_WIDTH), MXU_DTYPE),
        scratch_shapes=[pltpu.VMEM((SWA_QSUB, span, B_HEADS * QBLK), jnp.float32)],
        compiler_params=_params(("parallel", "arbitrary")),
        name="swa_attention",
    )(sinks, qbt, kvb, vbt)


def _s5_discretize_kernel(are_ref, aim_ref, logdt_ref, bre_ref, bim_ref,
                          abr_ref, abi_ref, bbr_ref, bbi_ref):
    ar = are_ref[...]
    ai = aim_ref[...]
    dt = jnp.exp(logdt_ref[...])
    mag = jnp.exp(dt * ar)
    abr = mag * jnp.cos(dt * ai)
    abi = mag * jnp.sin(dt * ai)
    den = ar * ar + ai * ai
    nr = abr - 1.0
    coef_r = (nr * ar + abi * ai) / den
    coef_i = (abi * ar - nr * ai) / den
    abr_ref[...] = abr
    abi_ref[...] = abi
    br = bre_ref[...]
    bi = bim_ref[...]
    bbr_ref[...] = coef_r[:, None, :] * br - coef_i[:, None, :] * bi
    bbi_ref[...] = coef_r[:, None, :] * bi + coef_i[:, None, :] * br


def _s5_discretize(a_re, a_im, log_dt, b_re, b_im):
    lg = a_re.shape[0] * a_re.shape[1]
    a2 = lambda a: a.reshape(lg, C_STATE)
    b3 = lambda b: jnp.swapaxes(b.reshape(lg, C_STATE, C_GROUP_CH), 1, 2)
    sa = jax.ShapeDtypeStruct((lg, C_STATE), jnp.float32)
    sb = jax.ShapeDtypeStruct((lg, C_GROUP_CH, C_STATE), jnp.float32)
    return pl.pallas_call(_s5_discretize_kernel, out_shape=[sa, sa, sb, sb], name="s5_discretize")(
        a2(a_re), a2(a_im), log_dt.reshape(lg, 1), b3(b_re), b3(b_im))


def _s5_kernel(u_ref, bmat_ref, abar_ref, cmat_ref, d_ref, gw_ref, gb_ref, y_ref, state_ref, ut_ref, xs_ref):
    bsz, t_steps, _ = u_ref.shape

    @pl.when(pl.program_id(0) == 0)
    def _():
        state_ref[...] = jnp.zeros_like(state_ref)

    def gather(t, carry):
        ut_ref[pl.ds(pl.multiple_of(t * bsz, bsz), bsz), :] = u_ref[:, t, :]
        return carry

    lax.fori_loop(0, t_steps, gather, 0, unroll=8)
    u = ut_ref[...]
    for hf in range(2):
        bu = _mm(u[:, hf * _HALF_CH:(hf + 1) * _HALF_CH], bmat_ref[hf])
        xs_ref[:, hf * _HALF_ST:(hf + 1) * _HALF_ST] = bu[:, :_HALF_ST]
        xs_ref[:, C_STATES + hf * _HALF_ST:C_STATES + (hf + 1) * _HALF_ST] = bu[:, _HALF_ST:]

    for cg in range(C_STATES // S5_LANES):
        re = slice(cg * S5_LANES, (cg + 1) * S5_LANES)
        im = slice(C_STATES + cg * S5_LANES, C_STATES + (cg + 1) * S5_LANES)
        ar = jnp.broadcast_to(abar_ref[0:1, re], (bsz, S5_LANES))
        ai = jnp.broadcast_to(abar_ref[0:1, im], (bsz, S5_LANES))

        def step(t, carry):
            xr, xi = carry
            r0 = pl.multiple_of(t * bsz, bsz)
            nr = ar * xr - ai * xi + xs_ref[pl.ds(r0, bsz), re]
            ni = ar * xi + ai * xr + xs_ref[pl.ds(r0, bsz), im]
            xs_ref[pl.ds(r0, bsz), re] = nr
            xs_ref[pl.ds(r0, bsz), im] = ni
            return nr, ni

        xr, xi = lax.fori_loop(0, t_steps, step, (state_ref[:, re], state_ref[:, im]), unroll=8)
        state_ref[:, re] = xr
        state_ref[:, im] = xi

    ys = []
    for hf in range(2):
        xr = xs_ref[:, hf * _HALF_ST:(hf + 1) * _HALF_ST]
        xi = xs_ref[:, C_STATES + hf * _HALF_ST:C_STATES + (hf + 1) * _HALF_ST]
        ys.append(_mm(xr, cmat_ref[hf, :_HALF_ST, :]) + _mm(xi, cmat_ref[hf, _HALF_ST:, :]))
    y = jnp.concatenate(ys, axis=1) + d_ref[...] * u
    z = jax.nn.gelu(y)
    out = z * jax.nn.sigmoid(_mm(z, gw_ref[...]) + gb_ref[...])
    y_ref[...] = out.reshape(t_steps, bsz, C_WIDTH).astype(y_ref.dtype)


def _s5(u, bmat, abar, cmat, d, glu_w, glu_b):
    bsz, seq, _ = u.shape
    t_steps = min(S5_CHUNK, seq)
    return pl.pallas_call(
        _s5_kernel,
        grid=(seq // t_steps,),
        in_specs=[
            pl.BlockSpec((bsz, t_steps, C_WIDTH), lambda i: (0, i, 0)),
            _resident((2, _HALF_CH, 2 * _HALF_ST)), _resident((1, 2 * C_STATES)), _resident((2, 2 * _HALF_ST, _HALF_CH)),
            _resident((1, C_WIDTH)), _resident((C_WIDTH, C_WIDTH)), _resident((1, C_WIDTH)),
        ],
        out_specs=pl.BlockSpec((t_steps, bsz, C_WIDTH), lambda i: (i, 0, 0)),
        out_shape=jax.ShapeDtypeStruct((seq, bsz, C_WIDTH), MXU_DTYPE),
        scratch_shapes=[pltpu.VMEM((bsz, 2 * C_STATES), jnp.float32),
                        pltpu.VMEM((t_steps * bsz, C_WIDTH), jnp.float32),
                        pltpu.VMEM((t_steps * bsz, 2 * C_STATES), jnp.float32)],
        compiler_params=_params(("arbitrary",)),
        name="s5_scan",
    )(u, bmat, abar, cmat, d.reshape(1, C_WIDTH), glu_w.astype(MXU_DTYPE), glu_b.reshape(1, C_WIDTH))


def _block_diag(per_group):
    g, r, c = per_group.shape
    eye = jnp.eye(g, dtype=per_group.dtype)
    return jnp.einsum("grc,gk->grkc", per_group, eye).reshape(g * r, g * c)


def _merge_kernel(x_ref, ya_ref, yb_ref, yc_ref, gain_ref, wg_ref, pa_ref, pb_ref, pc_ref, wo_ref, o_ref):
    x = x_ref[...]
    h = _rmsnorm(x, gain_ref[...]).astype(MXU_DTYPE)
    merged = None
    for j, (y_ref, p_ref) in enumerate(((ya_ref, pa_ref), (yb_ref, pb_ref), (yc_ref, pc_ref))):
        gate = jax.nn.sigmoid(jnp.dot(h, wg_ref[:, j * D_MODEL:(j + 1) * D_MODEL],
                                      preferred_element_type=jnp.float32))
        term = gate * jnp.dot(y_ref[...], p_ref[...], preferred_element_type=jnp.float32)
        merged = term if merged is None else merged + term
    o_ref[...] = x + _mm(merged, wo_ref[...])


def _merge(x2, ya, yb, yc, gain, wg, pa, pb, pc, wo):
    n = x2.shape[0]
    tm = min(ROW_TILE, n)

    def rows(width):
        return pl.BlockSpec((tm, width), lambda i: (i, 0))

    return pl.pallas_call(
        _merge_kernel,
        grid=(n // tm,),
        in_specs=[rows(D_MODEL), rows(A_WIDTH), rows(B_WIDTH), rows(C_WIDTH), _resident((1, D_MODEL)),
                  _resident((D_MODEL, N_BRANCH * D_MODEL)), _resident((A_WIDTH, D_MODEL)),
                  _resident((B_WIDTH, D_MODEL)), _resident((C_WIDTH, D_MODEL)), _resident((D_MODEL, D_MODEL))],
        out_specs=rows(D_MODEL),
        out_shape=jax.ShapeDtypeStruct((n, D_MODEL), jnp.float32),
        compiler_params=_params(("parallel",)),
        name="merge",
    )(x2, ya, yb, yc, gain.reshape(1, D_MODEL), wg, pa, pb, pc, wo)


def _mlp_kernel(x_ref, gain_ref, wu_ref, wd_ref, fgain_ref, o_ref, *, final_norm):
    x = x_ref[...]
    h = _rmsnorm(x, gain_ref[...]).astype(MXU_DTYPE)
    acc = x
    for c in range(D_FF // FF_CHUNK):
        cols = slice(c * FF_CHUNK, (c + 1) * FF_CHUNK)
        up = jnp.dot(h, wu_ref[:, cols], preferred_element_type=jnp.float32)
        acc = acc + _mm(jnp.square(jnp.maximum(up, 0.0)), wd_ref[cols, :])
    o_ref[...] = _rmsnorm(acc, fgain_ref[...]) if final_norm else acc


def _mlp(x2, gain, wu, wd, fgain, final_norm):
    n = x2.shape[0]
    tm = min(ROW_TILE, n)
    rows = pl.BlockSpec((tm, D_MODEL), lambda i: (i, 0))
    return pl.pallas_call(
        functools.partial(_mlp_kernel, final_norm=final_norm),
        grid=(n // tm,),
        in_specs=[rows, _resident((1, D_MODEL)), _resident((D_MODEL, D_FF)), _resident((D_FF, D_MODEL)),
                  _resident((1, D_MODEL))],
        out_specs=rows,
        out_shape=jax.ShapeDtypeStruct((n, D_MODEL), jnp.float32),
        compiler_params=_params(("parallel",)),
        name="mlp",
    )(x2, gain.reshape(1, D_MODEL), wu, wd, fgain.reshape(1, D_MODEL))


def kernel(x, positions, norm_mix, w_in, nsa_cmp_pos, nsa_cmp_w1, nsa_cmp_w2, swa_sinks, s5_a_re, s5_a_im, s5_log_dt, s5_b_re, s5_b_im, s5_c_re, s5_c_im, s5_d, s5_glu_w, s5_glu_b, w_branch_a, w_branch_b, w_branch_c, w_out, norm_mlp, w_mlp_up, w_mlp_down, norm_final):
    bsz, seq, _ = x.shape
    depth = w_in.shape[0]
    n = bsz * seq
    n_seg = seq // CMP_STRIDE
    gate_col = sum(IN_WIDTHS[:-1])
    bf = lambda w: w.astype(MXU_DTYPE)

    cos, sin = _rope_tables(positions)
    abr, abi, bbr, bbi = _s5_discretize(s5_a_re, s5_a_im, s5_log_dt, s5_b_re, s5_b_im)
    abr = abr.reshape(depth, 1, C_STATES)
    abi = abi.reshape(depth, 1, C_STATES)
    bbr = bbr.reshape(depth, C_GROUPS, C_GROUP_CH, C_STATE)
    bbi = bbi.reshape(depth, C_GROUPS, C_GROUP_CH, C_STATE)

    def to_lanes(t):
        return jnp.swapaxes(t.reshape(bsz, seq, t.shape[-1]), 1, 2)

    x2 = x.reshape(n, D_MODEL)
    for l in range(depth):
        qa, qb, ka, kvb, va, ga, uc = _in_projection(x2, norm_mix[l], _pack_w_in(w_in[l]), cos, sin)

        def segments(t):
            t = t[:, :LANES].reshape(bsz, n_seg, CMP_STRIDE, A_KV_HEADS, HEAD_DIM)
            return jnp.transpose(t, (0, 3, 1, 2, 4)).reshape(bsz, A_KV_HEADS, n_seg, CMP_STRIDE * HEAD_DIM)

        kcvc = _compress(jnp.stack([segments(ka), segments(va)]),
                         nsa_cmp_pos[l].reshape(2, 1, CMP_LEN * HEAD_DIM), nsa_cmp_w1[l], nsa_cmp_w2[l])
        kc = jnp.transpose(kcvc[0], (0, 2, 1, 3)).reshape(bsz, n_seg, LANES)
        vct = jnp.swapaxes(kcvc[1], 2, 3).reshape(bsz, LANES, n_seg)
        ya = _nsa_attention(to_lanes(qa), ka, to_lanes(va), kc, vct, ga, bsz, seq)

        yb = _swa_attention(to_lanes(qb), kvb, to_lanes(kvb[:, HEAD_DIM:]), swa_sinks[l], bsz, seq)

        cre_t = jnp.swapaxes(s5_c_re[l], 1, 2)
        cim_t = jnp.swapaxes(s5_c_im[l], 1, 2)
        halves = (slice(0, C_GROUPS // 2), slice(C_GROUPS // 2, C_GROUPS))
        bmat = jnp.stack([jnp.concatenate([_block_diag(bbr[l][g]), _block_diag(bbi[l][g])], axis=1)
                          for g in halves])
        cmat = jnp.stack([jnp.concatenate([_block_diag(cre_t[g]), -_block_diag(cim_t[g])], axis=0)
                          for g in halves])
        abar = jnp.concatenate([abr[l], abi[l]], axis=1)
        yc_tm = _s5(uc.reshape(bsz, seq, C_WIDTH), bf(bmat), abar, bf(cmat), s5_d[l], s5_glu_w[l], s5_glu_b[l])
        yc = jnp.swapaxes(yc_tm, 0, 1).reshape(n, C_WIDTH)

        x2 = _merge(x2, ya, yb, yc, norm_mix[l], bf(w_in[l][:, gate_col:]), bf(w_branch_a[l]),
                    bf(w_branch_b[l]), bf(w_branch_c[l]), bf(w_out[l]))
        x2 = _mlp(x2, norm_mlp[l], bf(w_mlp_up[l]), bf(w_mlp_down[l]), norm_final, l == depth - 1)
    return x2.reshape(bsz, seq, D_MODEL)
```

```python
import functools

import jax
import jax.numpy as jnp
from jax import lax
from jax.experimental import pallas as pl
from jax.experimental.pallas import tpu as pltpu

D_MODEL = 1024
HEAD_DIM = 64
ROPE_THETA = 10000.0
NORM_EPS = 1e-6
QBLK = 128

A_HEADS = 8
A_KV_HEADS = 2
A_GROUP = A_HEADS // A_KV_HEADS
A_WIDTH = A_HEADS * HEAD_DIM
NSA_BRANCHES = 3
CMP_LEN = 32
CMP_STRIDE = 16
CMP_HIDDEN = 2 * HEAD_DIM
SEL_LEN = 64
SEL_TOPK = 16
A_WINDOW = 512
FORCED_SCORE = 1e4

B_HEADS = 8
B_WIDTH = B_HEADS * HEAD_DIM
B_WINDOW = 128

C_WIDTH = 512
C_GROUP_CH = 16
C_GROUPS = C_WIDTH // C_GROUP_CH
C_STATE = 64
C_STATES = C_GROUPS * C_STATE
_HALF_CH = C_WIDTH // 2
_HALF_ST = C_STATES // 2

D_FF = 4 * D_MODEL
N_BRANCH = 3

IN_WIDTHS = (A_WIDTH, NSA_BRANCHES * 2 * A_KV_HEADS * HEAD_DIM, NSA_BRANCHES * A_HEADS,
             B_WIDTH, 2 * HEAD_DIM, C_WIDTH, N_BRANCH * D_MODEL)

LANES = 128
SUBLANES = 8
VMEM_LIMIT = 56 * 1024 * 1024

MXU_DTYPE = jnp.bfloat16
ROW_TILE = 256
S5_CHUNK = 64
S5_LANES = 512
SEL_CHUNK = 256
SWA_QSUB = 2
FF_CHUNK = 1024

LOG2E = 1.4426950408889634
_NEG_INF = float("-inf")
_M_INIT = -1e30


def _f32(x):
    return x.astype(jnp.float32)


def _mm(a, b):
    return jnp.dot(a.astype(MXU_DTYPE), b.astype(MXU_DTYPE), preferred_element_type=jnp.float32)


def _mm_nt(a, b):
    return lax.dot_general(a.astype(MXU_DTYPE), b.astype(MXU_DTYPE), (((1,), (1,)), ((), ())),
                           preferred_element_type=jnp.float32)


def _rmsnorm(x, gain):
    return x * lax.rsqrt(jnp.mean(x * x, axis=-1, keepdims=True) + NORM_EPS) * gain


def _resident(shape):
    zeros = (0,) * len(shape)
    return pl.BlockSpec(shape, lambda *_: zeros, pipeline_mode=pl.Buffered(1))


def _params(sem):
    return pltpu.CompilerParams(dimension_semantics=sem, vmem_limit_bytes=VMEM_LIMIT)


def _rope_table_kernel(pos_ref, invf_ref, sign_ref, cos_ref, sin_ref):
    ang = _f32(pos_ref[...]) * invf_ref[...]
    cos_ref[...] = jnp.cos(ang)
    sin_ref[...] = jnp.sin(ang) * sign_ref[...]


def _rope_tables(positions):
    n = positions.size
    half = HEAD_DIM // 2
    inv_freq = ROPE_THETA ** (-jnp.arange(half, dtype=jnp.float32) / half)
    invf = jnp.tile(inv_freq, LANES // half)
    sign = jnp.tile(jnp.concatenate([-jnp.ones(half, jnp.float32), jnp.ones(half, jnp.float32)]),
                    LANES // HEAD_DIM)
    tm = min(1024, n)
    natural = pl.pallas_call(
        _rope_table_kernel,
        grid=(n // tm,),
        in_specs=[pl.BlockSpec((tm, 1), lambda i: (i, 0)), _resident((1, LANES)), _resident((1, LANES))],
        out_specs=[pl.BlockSpec((tm, LANES), lambda i: (i, 0))] * 2,
        out_shape=[jax.ShapeDtypeStruct((n, LANES), jnp.float32)] * 2,
        compiler_params=_params(("parallel",)),
        name="rope_tables",
    )(positions.reshape(n, 1), invf.reshape(1, LANES), sign.reshape(1, LANES))
    transposed = pl.pallas_call(
        _rope_table_kernel,
        grid=(n // tm,),
        in_specs=[pl.BlockSpec((1, tm), lambda i: (0, i)), _resident((HEAD_DIM, 1)), _resident((HEAD_DIM, 1))],
        out_specs=[pl.BlockSpec((HEAD_DIM, tm), lambda i: (0, i))] * 2,
        out_shape=[jax.ShapeDtypeStruct((HEAD_DIM, n), jnp.float32)] * 2,
        compiler_params=_params(("parallel",)),
        name="rope_tables_t",
    )(positions.reshape(1, n), invf[:HEAD_DIM].reshape(HEAD_DIM, 1), sign[:HEAD_DIM].reshape(HEAD_DIM, 1))
    return natural + transposed


def _swap_halves(x):
    lane = lax.broadcasted_iota(jnp.int32, x.shape, 1)
    first = (lane & (HEAD_DIM - 1)) < HEAD_DIM // 2
    return jnp.where(first, pltpu.roll(x, LANES - HEAD_DIM // 2, 1), pltpu.roll(x, HEAD_DIM // 2, 1))


def _proj_kernel(x_ref, gain_ref, wn_ref, wt_ref, cos_ref, sin_ref, cost_ref, sint_ref,
                 qat_ref, qbt_ref, vat_ref, vbt_ref, ka_ref, kvb_ref, vc_ref, ga_ref, uc_ref):
    h = _rmsnorm(x_ref[...], gain_ref[...]).astype(MXU_DTYPE)
    nat = jnp.dot(h, wn_ref[...], preferred_element_type=jnp.float32)
    tr = _mm_nt(wt_ref[...], h)

    cos = cos_ref[...]
    sin = sin_ref[...]

    def roped(j):
        xj = nat[:, j * LANES:(j + 1) * LANES]
        return xj, xj * cos + _swap_halves(xj) * sin

    for j in range(3):
        ka_ref[:, j * LANES:(j + 1) * LANES] = roped(j)[1].astype(ka_ref.dtype)
    raw, rot = roped(3)
    lane = lax.broadcasted_iota(jnp.int32, raw.shape, 1)
    kvb_ref[...] = jnp.where(lane < HEAD_DIM, rot, raw).astype(kvb_ref.dtype)
    vc_ref[...] = nat[:, 4 * LANES:5 * LANES].astype(vc_ref.dtype)
    ga_ref[...] = jax.nn.sigmoid(nat[:, 5 * LANES:6 * LANES])
    uc_ref[...] = nat[:, 6 * LANES:]

    cos_t = cost_ref[...]
    sin_t = sint_ref[...]
    half = HEAD_DIM // 2

    def roped_t(r0):
        xh = tr[r0:r0 + HEAD_DIM]
        return xh * cos_t + jnp.concatenate([xh[half:], xh[:half]], axis=0) * sin_t

    for hd in range(A_HEADS):
        qat_ref[hd * HEAD_DIM:(hd + 1) * HEAD_DIM, :] = roped_t(hd * HEAD_DIM).astype(qat_ref.dtype)
    for hd in range(B_HEADS):
        qbt_ref[hd * HEAD_DIM:(hd + 1) * HEAD_DIM, :] = roped_t(A_WIDTH + hd * HEAD_DIM).astype(qbt_ref.dtype)
    v0 = A_WIDTH + B_WIDTH
    vat_ref[...] = tr[v0:v0 + 2 * LANES].astype(vat_ref.dtype)
    vbt_ref[...] = tr[v0 + 2 * LANES:].astype(vbt_ref.dtype)


_NAT_COLS = 10 * LANES
_TR_ROWS = A_WIDTH + B_WIDTH + 2 * LANES + HEAD_DIM


def _pack_w_in(w_in):
    o = [0]
    for w in IN_WIDTHS:
        o.append(o[-1] + w)
    scale = HEAD_DIM ** -0.5 * LOG2E
    d = w_in.shape[0]
    q_a = w_in[:, o[0]:o[1]] * scale
    kv_a = w_in[:, o[1]:o[2]].reshape(d, NSA_BRANCHES, 2, A_KV_HEADS * HEAD_DIM)
    k_a = kv_a[:, :, 0].reshape(d, NSA_BRANCHES * LANES)
    v_a = kv_a[:, :, 1].reshape(d, NSA_BRANCHES * LANES)
    g_a = jnp.pad(w_in[:, o[2]:o[3]], ((0, 0), (0, LANES - NSA_BRANCHES * A_HEADS)))
    q_b = w_in[:, o[3]:o[4]] * scale
    kv_b = w_in[:, o[4]:o[5]]
    u_c = w_in[:, o[5]:o[6]]
    w_nat = jnp.concatenate([k_a, kv_b, v_a[:, :LANES], g_a, u_c], axis=1)
    w_tr = jnp.concatenate([q_a, q_b, v_a[:, LANES:], kv_b[:, HEAD_DIM:]], axis=1).T
    return w_nat.astype(MXU_DTYPE), w_tr.astype(MXU_DTYPE)


def _in_projection(x2, gain, w_nat, w_tr, tables, bsz, seq):
    n = x2.shape[0]
    tm = min(ROW_TILE, seq)
    tiles = seq // tm

    def rows(width):
        return pl.BlockSpec((tm, width), lambda i: (i, 0))

    def lanes(height):
        return pl.BlockSpec((None, height, tm), lambda i: (i // tiles, 0, i % tiles))

    table_t = pl.BlockSpec((HEAD_DIM, tm), lambda i: (0, i))
    t_heights = (A_WIDTH, B_WIDTH, 2 * LANES, HEAD_DIM)
    widths = (3 * LANES, LANES, LANES, LANES, C_WIDTH)
    dtypes = (MXU_DTYPE, MXU_DTYPE, MXU_DTYPE, jnp.float32, jnp.float32)
    return pl.pallas_call(
        _proj_kernel,
        grid=(n // tm,),
        in_specs=[rows(D_MODEL), _resident((1, D_MODEL)), _resident((D_MODEL, _NAT_COLS)),
                  _resident((_TR_ROWS, D_MODEL)), rows(LANES), rows(LANES), table_t, table_t],
        out_specs=[lanes(hh) for hh in t_heights] + [rows(w) for w in widths],
        out_shape=[jax.ShapeDtypeStruct((bsz, hh, seq), MXU_DTYPE) for hh in t_heights]
        + [jax.ShapeDtypeStruct((n, w), dt) for w, dt in zip(widths, dtypes)],
        compiler_params=_params(("parallel",)),
        name="in_projection",
    )(x2, gain.reshape(1, D_MODEL), w_nat, w_tr, *tables)


def _compress_kernel(seg_ref, pos_ref, w1_ref, w2_ref, o_ref):
    seg = seg_ref[...]
    half = seg.shape[1]
    first = _mm(seg, w1_ref[:half, :])
    second = _mm(seg, w1_ref[half:, :])
    nseg = seg.shape[0]
    hidden = first + pltpu.roll(second, nseg - 1, 0) + _mm(pos_ref[...], w1_ref[...])
    o_ref[...] = _mm(jax.nn.gelu(hidden), w2_ref[...]).astype(o_ref.dtype)


def _compress(segs, pos_flat, w1, w2):
    _, bsz, hkv, nseg, segw = segs.shape
    return pl.pallas_call(
        _compress_kernel,
        grid=(2, bsz, hkv),
        in_specs=[
            pl.BlockSpec((None, None, None, nseg, segw), lambda j, b, h: (j, b, h, 0, 0)),
            pl.BlockSpec((None, 1, CMP_LEN * HEAD_DIM), lambda j, b, h: (j, 0, 0)),
            pl.BlockSpec((None, CMP_LEN * HEAD_DIM, CMP_HIDDEN), lambda j, b, h: (j, 0, 0)),
            pl.BlockSpec((None, CMP_HIDDEN, HEAD_DIM), lambda j, b, h: (j, 0, 0)),
        ],
        out_specs=pl.BlockSpec((None, None, None, nseg, HEAD_DIM), lambda j, b, h: (j, b, h, 0, 0)),
        out_shape=jax.ShapeDtypeStruct((2, bsz, hkv, nseg, HEAD_DIM), MXU_DTYPE),
        compiler_params=_params(("parallel", "parallel", "parallel")),
        name="nsa_compress",
    )(segs, pos_flat, w1, w2)


def _split3(x):
    hi = x.astype(MXU_DTYPE)
    r1 = x - _f32(hi)
    mid = r1.astype(MXU_DTYPE)
    lo = (r1 - _f32(mid)).astype(MXU_DTYPE)
    return hi, mid, lo


def _col_max(s):
    return jnp.max(s, axis=0, keepdims=True)


def _tile_lanes(x, n):
    return jnp.concatenate([x] * n, axis=1)


def _pad_head_rows(x, h, fill):
    other = jnp.full(x.shape, fill, x.dtype)
    return jnp.concatenate([x, other] if h == 0 else [other, x], axis=0)


def _store_transposed_pairs(o_ref, heads_t, row0=0):
    for p in range(len(heads_t) // 2):
        pair = jnp.concatenate([heads_t[2 * p], heads_t[2 * p + 1]], axis=0)
        o_ref[row0:row0 + QBLK, p * LANES:(p + 1) * LANES] = pair.T.astype(o_ref.dtype)


def _nsa_kernel(qt_ref, k_ref, vt_ref, kc_ref, vct_ref, g_ref, c2s_ref, o_ref,
                selb_ref, sca_ref, scb_ref, cmp_ref, win_ref, *, seq):
    i = pl.program_id(1)
    q0 = i * QBLK
    n_cmp = kc_ref.shape[0]
    n_sel = seq // SEL_LEN
    kc_sz = min(SEL_CHUNK, seq)
    span = min(A_WINDOW + QBLK, seq)
    blocks_per_chunk = kc_sz // SEL_LEN

    gates_t = g_ref[...].T
    qp_lane = q0 + lax.broadcasted_iota(jnp.int32, (1, QBLK), 1)

    cmp_end = lax.broadcasted_iota(jnp.int32, (n_cmp, 1), 0) * CMP_STRIDE + (CMP_LEN - 1)
    bias_c = _tile_lanes(jnp.where(cmp_end <= qp_lane, 0.0, _NEG_INF), A_GROUP)

    w0 = pl.multiple_of(jnp.maximum(q0 + QBLK - span, 0), QBLK)
    diff = qp_lane - (w0 + lax.broadcasted_iota(jnp.int32, (span, 1), 0))
    bias_w = _tile_lanes(jnp.where(diff >= 0, jnp.where(diff < A_WINDOW, 0.0, _NEG_INF), _NEG_INF), A_GROUP)

    def head_rows(h):
        return slice(h * HEAD_DIM, (h + 1) * HEAD_DIM), slice((1 - h) * HEAD_DIM, (1 - h) * HEAD_DIM + 1)

    w_qs = []
    for h in range(A_KV_HEADS):
        q_t = jnp.concatenate([qt_ref[(h * A_GROUP + g) * HEAD_DIM:(h * A_GROUP + g + 1) * HEAD_DIM, :]
                               for g in range(A_GROUP)], axis=1)
        w_qs.append(_pad_head_rows(q_t, h, 0.0))

    n_chunks = (q0 + QBLK + kc_sz - 1) // kc_sz
    last = n_chunks - 1

    def issue_scores(c, dst_ref):
        k0 = pl.multiple_of(jnp.minimum(c, last) * kc_sz, kc_sz)
        keys = k_ref[pl.ds(k0, kc_sz), LANES:2 * LANES]
        for h in range(A_KV_HEADS):
            dst_ref[h] = _mm(keys, w_qs[h])

    for h in range(A_KV_HEADS):
        cmp_ref[h] = _mm(kc_ref[...], w_qs[h])
    for h in range(A_KV_HEADS):
        win_ref[h] = _mm(k_ref[pl.ds(w0, span), 2 * LANES:3 * LANES], w_qs[h])
    issue_scores(0, sca_ref)

    o_cs, o_ws = [], []
    for h in range(A_KV_HEADS):
        own, ones_row = head_rows(h)

        s = cmp_ref[h] + bias_c
        m = _col_max(s)
        m = jnp.where(m == _NEG_INF, 0.0, m)
        e = jnp.exp2(s - m)
        p = e * (1.0 / jnp.maximum(jnp.sum(e, axis=0, keepdims=True), 1e-30))
        o_cs.append(_mm(vct_ref[...], p)[own])

        p_sum = p[:, 0:QBLK]
        for g in range(1, A_GROUP):
            p_sum = p_sum + p[:, g * QBLK:(g + 1) * QBLK]
        imp_t = sum(_mm(c2s_ref[...], part) for part in _split3(p_sum))
        blk = lax.broadcasted_iota(jnp.int32, (n_sel, QBLK), 0)
        cur = jnp.right_shift(qp_lane, 6)
        forced = (blk == 0) | (blk == cur) | (blk == cur - 1)
        score = jnp.where(forced, FORCED_SCORE, imp_t)
        score = jnp.where(blk <= cur, score, _NEG_INF)
        n_grp = n_sel // SUBLANES
        sub = lax.broadcasted_iota(jnp.int32, (SUBLANES, QBLK), 0)
        score_grp = [score[SUBLANES * v:SUBLANES * (v + 1), :] for v in range(n_grp)]
        rank_grp = [jnp.zeros((SUBLANES, QBLK), jnp.float32) for _ in range(n_grp)]
        for j in range(n_sel):
            row = jnp.broadcast_to(score[j:j + 1, :], (SUBLANES, QBLK))
            vj, rj = divmod(j, SUBLANES)
            for v in range(n_grp):
                if v > vj:
                    inc = jnp.where(row >= score_grp[v], 1.0, 0.0)
                elif v < vj:
                    inc = jnp.where(row > score_grp[v], 1.0, 0.0)
                else:
                    inc = jnp.where(sub > rj, jnp.where(row >= score_grp[v], 1.0, 0.0),
                                    jnp.where(row > score_grp[v], 1.0, 0.0))
                rank_grp[v] = rank_grp[v] + inc
        rank = jnp.concatenate(rank_grp, axis=0)
        picked = jnp.where(rank < float(min(SEL_TOPK, n_sel)), 0.0, _NEG_INF)
        selb_ref[h] = jnp.where(blk <= cur, picked, _NEG_INF)

    for h in range(A_KV_HEADS):
        own, ones_row = head_rows(h)
        sw = win_ref[h] + bias_w
        pw = jnp.exp2(sw - _col_max(sw))
        vt_w = _pad_head_rows(vt_ref[LANES + h * HEAD_DIM:LANES + (h + 1) * HEAD_DIM, pl.ds(w0, span)],
                              h, 1.0)
        acc_w = _mm(vt_w, pw)
        o_ws.append(acc_w[own] / acc_w[ones_row])

    def consume(c, src_ref, states):
        cc = jnp.minimum(c, last)
        k0 = pl.multiple_of(cc * kc_sz, kc_sz)
        causal = (c * kc_sz + lax.broadcasted_iota(jnp.int32, (kc_sz, 1), 0)) <= qp_lane
        out = []
        for h in range(A_KV_HEADS):
            m_run, acc = states[h]
            slabs = [jnp.broadcast_to(selb_ref[h, pl.ds(cc * blocks_per_chunk + jj, 1), :], (SEL_LEN, QBLK))
                     for jj in range(blocks_per_chunk)]
            bias = jnp.where(causal, jnp.concatenate(slabs, axis=0), _NEG_INF)
            sc = src_ref[h] + _tile_lanes(bias, A_GROUP)
            m_new = jnp.maximum(m_run, _col_max(sc))
            pr = jnp.exp2(sc - m_new)
            vt = _pad_head_rows(vt_ref[h * HEAD_DIM:(h + 1) * HEAD_DIM, pl.ds(k0, kc_sz)], h, 1.0)
            out.append((m_new, jnp.exp2(m_run - m_new) * acc + _mm(vt, pr)))
        return tuple(out)

    def sel_step(t, states):
        issue_scores(2 * t + 1, scb_ref)
        states = consume(2 * t, sca_ref, states)
        issue_scores(2 * t + 2, sca_ref)
        return consume(2 * t + 1, scb_ref, states)

    init = tuple((jnp.full((1, A_GROUP * QBLK), _M_INIT, jnp.float32),
                  jnp.zeros((LANES, A_GROUP * QBLK), jnp.float32)) for _ in range(A_KV_HEADS))
    sel_out = lax.fori_loop(0, (n_chunks + 1) // 2, sel_step, init)

    heads_t = []
    for h in range(A_KV_HEADS):
        own, ones_row = head_rows(h)
        acc_s = sel_out[h][1]
        o_s = acc_s[own] / jnp.maximum(acc_s[ones_row], 1e-30)
        for g in range(A_GROUP):
            hd = h * A_GROUP + g
            cols = slice(g * QBLK, (g + 1) * QBLK)
            heads_t.append(gates_t[hd:hd + 1, :] * o_cs[h][:, cols]
                           + gates_t[A_HEADS + hd:A_HEADS + hd + 1, :] * o_s[:, cols]
                           + gates_t[2 * A_HEADS + hd:2 * A_HEADS + hd + 1, :] * o_ws[h][:, cols])
    _store_transposed_pairs(o_ref, heads_t)


def _nsa_attention(qat, ka, vat, kc, vct, ga, bsz, seq):
    nblk = seq // QBLK
    n_seg = seq // CMP_STRIDE
    n_sel = seq // SEL_LEN
    assert seq >= A_WINDOW + QBLK and seq % SEL_CHUNK == 0
    c_start = jnp.arange(n_seg) * CMP_STRIDE
    s_start = jnp.arange(n_sel) * SEL_LEN
    overlap = (jnp.minimum(c_start[None, :] + CMP_LEN, s_start[:, None] + SEL_LEN)
               - jnp.maximum(c_start[None, :], s_start[:, None]))
    c2s = (jnp.clip(overlap, 0, None).astype(jnp.float32) / CMP_LEN).astype(MXU_DTYPE)
    return pl.pallas_call(
        functools.partial(_nsa_kernel, seq=seq),
        grid=(bsz, nblk),
        in_specs=[
            pl.BlockSpec((None, A_WIDTH, QBLK), lambda b, i: (b, 0, i)),
            pl.BlockSpec((seq, 3 * LANES), lambda b, i: (b, 0)),
            pl.BlockSpec((None, 2 * LANES, seq), lambda b, i: (b, 0, 0)),
            pl.BlockSpec((None, n_seg, LANES), lambda b, i: (b, 0, 0)),
            pl.BlockSpec((None, LANES, n_seg), lambda b, i: (b, 0, 0)),
            pl.BlockSpec((QBLK, LANES), lambda b, i: (b * nblk + i, 0)),
            _resident((n_sel, n_seg)),
        ],
        out_specs=pl.BlockSpec((QBLK, A_WIDTH), lambda b, i: (b * nblk + i, 0)),
        out_shape=jax.ShapeDtypeStruct((bsz * seq, A_WIDTH), MXU_DTYPE),
        scratch_shapes=[pltpu.VMEM((A_KV_HEADS, n_sel, QBLK), jnp.float32)]
        + [pltpu.VMEM((A_KV_HEADS, min(SEL_CHUNK, seq), A_GROUP * QBLK), jnp.float32)] * 2
        + [pltpu.VMEM((A_KV_HEADS, n_seg, A_GROUP * QBLK), jnp.float32),
           pltpu.VMEM((A_KV_HEADS, min(A_WINDOW + QBLK, seq), A_GROUP * QBLK), jnp.float32)],
        compiler_params=_params(("parallel", "arbitrary")),
        name="nsa_attention",
    )(qat, ka, vat, kc, vct, ga, c2s)


def _swa_kernel(sink_ref, qt_ref, kv_ref, vt_ref, o_ref, sc_ref, *, seq):
    i = pl.program_id(1)
    span = min(B_WINDOW + QBLK, seq)
    sink = jnp.concatenate([jnp.full((1, QBLK), sink_ref[hd] * LOG2E, jnp.float32) for hd in range(B_HEADS)],
                           axis=1)
    starts = []
    for sb in range(SWA_QSUB):
        q0 = (i * SWA_QSUB + sb) * QBLK
        w0 = pl.multiple_of(jnp.maximum(q0 + QBLK - span, 0), QBLK)
        starts.append((q0, w0))
        q_t = jnp.concatenate([qt_ref[hd * HEAD_DIM:(hd + 1) * HEAD_DIM, sb * QBLK:(sb + 1) * QBLK]
                               for hd in range(B_HEADS)], axis=1)
        sc_ref[sb] = _mm(kv_ref[pl.ds(w0, span), :], _pad_head_rows(q_t, 0, 0.0))
    for sb, (q0, w0) in enumerate(starts):
        qp_lane = q0 + lax.broadcasted_iota(jnp.int32, (1, QBLK), 1)
        diff = qp_lane - (w0 + lax.broadcasted_iota(jnp.int32, (span, 1), 0))
        bias = _tile_lanes(jnp.where(diff >= 0, jnp.where(diff < B_WINDOW, 0.0, _NEG_INF), _NEG_INF), B_HEADS)
        s = sc_ref[sb] + bias
        m = jnp.maximum(_col_max(s), sink)
        acc = _mm(_pad_head_rows(vt_ref[:, pl.ds(w0, span)], 1, 1.0), jnp.exp2(s - m))
        o_t = acc[HEAD_DIM:] / (acc[0:1] + jnp.exp2(sink - m))
        _store_transposed_pairs(o_ref, [o_t[:, hd * QBLK:(hd + 1) * QBLK] for hd in range(B_HEADS)], sb * QBLK)


def _swa_attention(qbt, kvb, vbt, sinks, bsz, seq):
    rows = SWA_QSUB * QBLK
    nstep = seq // rows
    span = min(B_WINDOW + QBLK, seq)
    assert seq >= B_WINDOW + QBLK and seq % rows == 0
    return pl.pallas_call(
        functools.partial(_swa_kernel, seq=seq),
        grid=(bsz, nstep),
        in_specs=[
            pl.BlockSpec(memory_space=pltpu.SMEM),
            pl.BlockSpec((None, B_WIDTH, rows), lambda b, i: (b, 0, i)),
            pl.BlockSpec((seq, LANES), lambda b, i: (b, 0)),
            pl.BlockSpec((None, HEAD_DIM, seq), lambda b, i: (b, 0, 0)),
        ],
        out_specs=pl.BlockSpec((rows, B_WIDTH), lambda b, i: (b * nstep + i, 0)),
        out_shape=jax.ShapeDtypeStruct((bsz * seq, B_WIDTH), MXU_DTYPE),
        scratch_shapes=[pltpu.VMEM((SWA_QSUB, span, B_HEADS * QBLK), jnp.float32)],
        compiler_params=_params(("parallel", "arbitrary")),
        name="swa_attention",
    )(sinks, qbt, kvb, vbt)


def _s5_discretize_kernel(are_ref, aim_ref, logdt_ref, bre_ref, bim_ref,
                          abr_ref, abi_ref, bbr_ref, bbi_ref):
    ar = are_ref[...]
    ai = aim_ref[...]
    dt = jnp.exp(logdt_ref[...])
    mag = jnp.exp(dt * ar)
    abr = mag * jnp.cos(dt * ai)
    abi = mag * jnp.sin(dt * ai)
    den = ar * ar + ai * ai
    nr = abr - 1.0
    coef_r = (nr * ar + abi * ai) / den
    coef_i = (abi * ar - nr * ai) / den
    abr_ref[...] = abr
    abi_ref[...] = abi
    br = bre_ref[...]
    bi = bim_ref[...]
    bbr_ref[...] = coef_r[:, None, :] * br - coef_i[:, None, :] * bi
    bbi_ref[...] = coef_r[:, None, :] * bi + coef_i[:, None, :] * br


def _s5_discretize(a_re, a_im, log_dt, b_re, b_im):
    lg = a_re.shape[0] * a_re.shape[1]
    a2 = lambda a: a.reshape(lg, C_STATE)
    b3 = lambda b: jnp.swapaxes(b.reshape(lg, C_STATE, C_GROUP_CH), 1, 2)
    sa = jax.ShapeDtypeStruct((lg, C_STATE), jnp.float32)
    sb = jax.ShapeDtypeStruct((lg, C_GROUP_CH, C_STATE), jnp.float32)
    return pl.pallas_call(_s5_discretize_kernel, out_shape=[sa, sa, sb, sb], name="s5_discretize")(
        a2(a_re), a2(a_im), log_dt.reshape(lg, 1), b3(b_re), b3(b_im))


def _s5_kernel(u_ref, bmat_ref, abar_ref, cmat_ref, d_ref, gw_ref, gb_ref, y_ref, state_ref, ut_ref, xs_ref):
    bsz, t_steps, _ = u_ref.shape

    @pl.when(pl.program_id(0) == 0)
    def _():
        state_ref[...] = jnp.zeros_like(state_ref)

    def gather(t, carry):
        ut_ref[pl.ds(pl.multiple_of(t * bsz, bsz), bsz), :] = u_ref[:, t, :]
        return carry

    lax.fori_loop(0, t_steps, gather, 0, unroll=8)
    u = ut_ref[...]
    for hf in range(2):
        bu = _mm(u[:, hf * _HALF_CH:(hf + 1) * _HALF_CH], bmat_ref[hf])
        xs_ref[:, hf * _HALF_ST:(hf + 1) * _HALF_ST] = bu[:, :_HALF_ST]
        xs_ref[:, C_STATES + hf * _HALF_ST:C_STATES + (hf + 1) * _HALF_ST] = bu[:, _HALF_ST:]

    for cg in range(C_STATES // S5_LANES):
        re = slice(cg * S5_LANES, (cg + 1) * S5_LANES)
        im = slice(C_STATES + cg * S5_LANES, C_STATES + (cg + 1) * S5_LANES)
        ar = jnp.broadcast_to(abar_ref[0:1, re], (bsz, S5_LANES))
        ai = jnp.broadcast_to(abar_ref[0:1, im], (bsz, S5_LANES))

        def step(t, carry):
            xr, xi = carry
            r0 = pl.multiple_of(t * bsz, bsz)
            nr = ar * xr - ai * xi + xs_ref[pl.ds(r0, bsz), re]
            ni = ar * xi + ai * xr + xs_ref[pl.ds(r0, bsz), im]
            xs_ref[pl.ds(r0, bsz), re] = nr
            xs_ref[pl.ds(r0, bsz), im] = ni
            return nr, ni

        xr, xi = lax.fori_loop(0, t_steps, step, (state_ref[:, re], state_ref[:, im]), unroll=8)
        state_ref[:, re] = xr
        state_ref[:, im] = xi

    ys = []
    for hf in range(2):
        xr = xs_ref[:, hf * _HALF_ST:(hf + 1) * _HALF_ST]
        xi = xs_ref[:, C_STATES + hf * _HALF_ST:C_STATES + (hf + 1) * _HALF_ST]
        ys.append(_mm(xr, cmat_ref[hf, :_HALF_ST, :]) + _mm(xi, cmat_ref[hf, _HALF_ST:, :]))
    y = jnp.concatenate(ys, axis=1) + d_ref[...] * u
    z = jax.nn.gelu(y)
    out = z * jax.nn.sigmoid(_mm(z, gw_ref[...]) + gb_ref[...])
    y_ref[...] = out.reshape(t_steps, bsz, C_WIDTH).astype(y_ref.dtype)


def _s5(u, bmat, abar, cmat, d, glu_w, glu_b):
    bsz, seq, _ = u.shape
    t_steps = min(S5_CHUNK, seq)
    return pl.pallas_call(
        _s5_kernel,
        grid=(seq // t_steps,),
        in_specs=[
            pl.BlockSpec((bsz, t_steps, C_WIDTH), lambda i: (0, i, 0)),
            _resident((2, _HALF_CH, 2 * _HALF_ST)), _resident((1, 2 * C_STATES)), _resident((2, 2 * _HALF_ST, _HALF_CH)),
            _resident((1, C_WIDTH)), _resident((C_WIDTH, C_WIDTH)), _resident((1, C_WIDTH)),
        ],
        out_specs=pl.BlockSpec((t_steps, bsz, C_WIDTH), lambda i: (i, 0, 0)),
        out_shape=jax.ShapeDtypeStruct((seq, bsz, C_WIDTH), MXU_DTYPE),
        scratch_shapes=[pltpu.VMEM((bsz, 2 * C_STATES), jnp.float32),
                        pltpu.VMEM((t_steps * bsz, C_WIDTH), jnp.float32),
                        pltpu.VMEM((t_steps * bsz, 2 * C_STATES), jnp.float32)],
        compiler_params=_params(("arbitrary",)),
        name="s5_scan",
    )(u, bmat, abar, cmat, d.reshape(1, C_WIDTH), glu_w.astype(MXU_DTYPE), glu_b.reshape(1, C_WIDTH))


def _block_diag(per_group):
    g, r, c = per_group.shape
    eye = jnp.eye(g, dtype=per_group.dtype)
    return jnp.einsum("grc,gk->grkc", per_group, eye).reshape(g * r, g * c)


def _merge_kernel(x_ref, ya_ref, yb_ref, yc_ref, gain_ref, wg_ref, pa_ref, pb_ref, pc_ref, wo_ref, o_ref):
    x = x_ref[...]
    h = _rmsnorm(x, gain_ref[...]).astype(MXU_DTYPE)
    merged = None
    for j, (y_ref, p_ref) in enumerate(((ya_ref, pa_ref), (yb_ref, pb_ref), (yc_ref, pc_ref))):
        gate = jax.nn.sigmoid(jnp.dot(h, wg_ref[:, j * D_MODEL:(j + 1) * D_MODEL],
                                      preferred_element_type=jnp.float32))
        term = gate * jnp.dot(y_ref[...], p_ref[...], preferred_element_type=jnp.float32)
        merged = term if merged is None else merged + term
    o_ref[...] = x + _mm(merged, wo_ref[...])


def _merge(x2, ya, yb, yc, gain, wg, pa, pb, pc, wo):
    n = x2.shape[0]
    tm = min(ROW_TILE, n)

    def rows(width):
        return pl.BlockSpec((tm, width), lambda i: (i, 0))

    return pl.pallas_call(
        _merge_kernel,
        grid=(n // tm,),
        in_specs=[rows(D_MODEL), rows(A_WIDTH), rows(B_WIDTH), rows(C_WIDTH), _resident((1, D_MODEL)),
                  _resident((D_MODEL, N_BRANCH * D_MODEL)), _resident((A_WIDTH, D_MODEL)),
                  _resident((B_WIDTH, D_MODEL)), _resident((C_WIDTH, D_MODEL)), _resident((D_MODEL, D_MODEL))],
        out_specs=rows(D_MODEL),
        out_shape=jax.ShapeDtypeStruct((n, D_MODEL), jnp.float32),
        compiler_params=_params(("parallel",)),
        name="merge",
    )(x2, ya, yb, yc, gain.reshape(1, D_MODEL), wg, pa, pb, pc, wo)


def _mlp_kernel(x_ref, gain_ref, wu_ref, wd_ref, fgain_ref, o_ref, *, final_norm):
    x = x_ref[...]
    h = _rmsnorm(x, gain_ref[...]).astype(MXU_DTYPE)
    acc = x
    for c in range(D_FF // FF_CHUNK):
        cols = slice(c * FF_CHUNK, (c + 1) * FF_CHUNK)
        up = jnp.dot(h, wu_ref[:, cols], preferred_element_type=jnp.float32)
        acc = acc + _mm(jnp.square(jnp.maximum(up, 0.0)), wd_ref[cols, :])
    o_ref[...] = _rmsnorm(acc, fgain_ref[...]) if final_norm else acc


def _mlp(x2, gain, wu, wd, fgain, final_norm):
    n = x2.shape[0]
    tm = min(ROW_TILE, n)
    rows = pl.BlockSpec((tm, D_MODEL), lambda i: (i, 0))
    return pl.pallas_call(
        functools.partial(_mlp_kernel, final_norm=final_norm),
        grid=(n // tm,),
        in_specs=[rows, _resident((1, D_MODEL)), _resident((D_MODEL, D_FF)), _resident((D_FF, D_MODEL)),
                  _resident((1, D_MODEL))],
        out_specs=rows,
        out_shape=jax.ShapeDtypeStruct((n, D_MODEL), jnp.float32),
        compiler_params=_params(("parallel",)),
        name="mlp",
    )(x2, gain.reshape(1, D_MODEL), wu, wd, fgain.reshape(1, D_MODEL))


def kernel(x, positions, norm_mix, w_in, nsa_cmp_pos, nsa_cmp_w1, nsa_cmp_w2, swa_sinks, s5_a_re, s5_a_im, s5_log_dt, s5_b_re, s5_b_im, s5_c_re, s5_c_im, s5_d, s5_glu_w, s5_glu_b, w_branch_a, w_branch_b, w_branch_c, w_out, norm_mlp, w_mlp_up, w_mlp_down, norm_final):
    bsz, seq, _ = x.shape
    depth = w_in.shape[0]
    n = bsz * seq
    n_seg = seq // CMP_STRIDE
    gate_col = sum(IN_WIDTHS[:-1])
    bf = lambda w: w.astype(MXU_DTYPE)

    tables = _rope_tables(positions)
    abr, abi, bbr, bbi = _s5_discretize(s5_a_re, s5_a_im, s5_log_dt, s5_b_re, s5_b_im)
    abr = abr.reshape(depth, 1, C_STATES)
    abi = abi.reshape(depth, 1, C_STATES)
    bbr = bbr.reshape(depth, C_GROUPS, C_GROUP_CH, C_STATE)
    bbi = bbi.reshape(depth, C_GROUPS, C_GROUP_CH, C_STATE)

    x2 = x.reshape(n, D_MODEL)
    for l in range(depth):
        qat, qbt, vat, vbt, ka, kvb, vc, ga, uc = _in_projection(x2, norm_mix[l], *_pack_w_in(w_in[l]), tables,
                                                                  bsz, seq)

        def segments(t):
            t = t[:, :LANES].reshape(bsz, n_seg, CMP_STRIDE, A_KV_HEADS, HEAD_DIM)
            return jnp.transpose(t, (0, 3, 1, 2, 4)).reshape(bsz, A_KV_HEADS, n_seg, CMP_STRIDE * HEAD_DIM)

        kcvc = _compress(jnp.stack([segments(ka), segments(vc)]),
                         nsa_cmp_pos[l].reshape(2, 1, CMP_LEN * HEAD_DIM), nsa_cmp_w1[l], nsa_cmp_w2[l])
        kc = jnp.transpose(kcvc[0], (0, 2, 1, 3)).reshape(bsz, n_seg, LANES)
        vct = jnp.swapaxes(kcvc[1], 2, 3).reshape(bsz, LANES, n_seg)
        ya = _nsa_attention(qat, ka, vat, kc, vct, ga, bsz, seq)

        yb = _swa_attention(qbt, kvb, vbt, swa_sinks[l], bsz, seq)

        cre_t = jnp.swapaxes(s5_c_re[l], 1, 2)
        cim_t = jnp.swapaxes(s5_c_im[l], 1, 2)
        halves = (slice(0, C_GROUPS // 2), slice(C_GROUPS // 2, C_GROUPS))
        bmat = jnp.stack([jnp.concatenate([_block_diag(bbr[l][g]), _block_diag(bbi[l][g])], axis=1)
                          for g in halves])
        cmat = jnp.stack([jnp.concatenate([_block_diag(cre_t[g]), -_block_diag(cim_t[g])], axis=0)
                          for g in halves])
        abar = jnp.concatenate([abr[l], abi[l]], axis=1)
        yc_tm = _s5(uc.reshape(bsz, seq, C_WIDTH), bf(bmat), abar, bf(cmat), s5_d[l], s5_glu_w[l], s5_glu_b[l])
        yc = jnp.swapaxes(yc_tm, 0, 1).reshape(n, C_WIDTH)

        x2 = _merge(x2, ya, yb, yc, norm_mix[l], bf(w_in[l][:, gate_col:]), bf(w_branch_a[l]),
                    bf(w_branch_b[l]), bf(w_branch_c[l]), bf(w_out[l]))
        x2 = _mlp(x2, norm_mlp[l], bf(w_mlp_up[l]), bf(w_mlp_down[l]), norm_final, l == depth - 1)
    return x2.reshape(bsz, seq, D_MODEL)
```

```python
import functools

import jax
import jax.numpy as jnp
from jax import lax
from jax.experimental import pallas as pl
from jax.experimental.pallas import tpu as pltpu

D_MODEL = 1024
HEAD_DIM = 64
ROPE_THETA = 10000.0
NORM_EPS = 1e-6
QBLK = 128

A_HEADS = 8
A_KV_HEADS = 2
A_GROUP = A_HEADS // A_KV_HEADS
A_WIDTH = A_HEADS * HEAD_DIM
NSA_BRANCHES = 3
CMP_LEN = 32
CMP_STRIDE = 16
CMP_HIDDEN = 2 * HEAD_DIM
SEL_LEN = 64
SEL_TOPK = 16
A_WINDOW = 512
FORCED_SCORE = 1e4

B_HEADS = 8
B_WIDTH = B_HEADS * HEAD_DIM
B_WINDOW = 128

C_WIDTH = 512
C_GROUP_CH = 16
C_GROUPS = C_WIDTH // C_GROUP_CH
C_STATE = 64
C_STATES = C_GROUPS * C_STATE
_HALF_CH = C_WIDTH // 2
_HALF_ST = C_STATES // 2

D_FF = 4 * D_MODEL
N_BRANCH = 3

IN_WIDTHS = (A_WIDTH, NSA_BRANCHES * 2 * A_KV_HEADS * HEAD_DIM, NSA_BRANCHES * A_HEADS,
             B_WIDTH, 2 * HEAD_DIM, C_WIDTH, N_BRANCH * D_MODEL)

LANES = 128
SUBLANES = 8
VMEM_LIMIT = 56 * 1024 * 1024

MXU_DTYPE = jnp.bfloat16
ROW_TILE = 256
S5_CHUNK = 64
S5_LANES = 512
SEL_CHUNK = 256
SWA_QSUB = 2
FF_CHUNK = 1024

LOG2E = 1.4426950408889634
_NEG_INF = float("-inf")
_M_INIT = -1e30


def _f32(x):
    return x.astype(jnp.float32)


def _mm(a, b):
    return jnp.dot(a.astype(MXU_DTYPE), b.astype(MXU_DTYPE), preferred_element_type=jnp.float32)


def _mm_nt(a, b):
    return lax.dot_general(a.astype(MXU_DTYPE), b.astype(MXU_DTYPE), (((1,), (1,)), ((), ())),
                           preferred_element_type=jnp.float32)


def _rmsnorm(x, gain):
    return x * lax.rsqrt(jnp.mean(x * x, axis=-1, keepdims=True) + NORM_EPS) * gain


def _resident(shape):
    zeros = (0,) * len(shape)
    return pl.BlockSpec(shape, lambda *_: zeros, pipeline_mode=pl.Buffered(1))


def _params(sem):
    return pltpu.CompilerParams(dimension_semantics=sem, vmem_limit_bytes=VMEM_LIMIT)


def _rope_table_kernel(pos_ref, invf_ref, sign_ref, cos_ref, sin_ref):
    ang = _f32(pos_ref[...]) * invf_ref[...]
    cos_ref[...] = jnp.cos(ang)
    sin_ref[...] = jnp.sin(ang) * sign_ref[...]


def _rope_tables(positions):
    n = positions.size
    half = HEAD_DIM // 2
    inv_freq = ROPE_THETA ** (-jnp.arange(half, dtype=jnp.float32) / half)
    invf = jnp.tile(inv_freq, LANES // half)
    sign = jnp.tile(jnp.concatenate([-jnp.ones(half, jnp.float32), jnp.ones(half, jnp.float32)]),
                    LANES // HEAD_DIM)
    tm = min(1024, n)
    natural = pl.pallas_call(
        _rope_table_kernel,
        grid=(n // tm,),
        in_specs=[pl.BlockSpec((tm, 1), lambda i: (i, 0)), _resident((1, LANES)), _resident((1, LANES))],
        out_specs=[pl.BlockSpec((tm, LANES), lambda i: (i, 0))] * 2,
        out_shape=[jax.ShapeDtypeStruct((n, LANES), jnp.float32)] * 2,
        compiler_params=_params(("parallel",)),
        name="rope_tables",
    )(positions.reshape(n, 1), invf.reshape(1, LANES), sign.reshape(1, LANES))
    transposed = pl.pallas_call(
        _rope_table_kernel,
        grid=(n // tm,),
        in_specs=[pl.BlockSpec((1, tm), lambda i: (0, i)), _resident((HEAD_DIM, 1)), _resident((HEAD_DIM, 1))],
        out_specs=[pl.BlockSpec((HEAD_DIM, tm), lambda i: (0, i))] * 2,
        out_shape=[jax.ShapeDtypeStruct((HEAD_DIM, n), jnp.float32)] * 2,
        compiler_params=_params(("parallel",)),
        name="rope_tables_t",
    )(positions.reshape(1, n), invf[:HEAD_DIM].reshape(HEAD_DIM, 1), sign[:HEAD_DIM].reshape(HEAD_DIM, 1))
    return natural + transposed


def _swap_halves(x):
    lane = lax.broadcasted_iota(jnp.int32, x.shape, 1)
    first = (lane & (HEAD_DIM - 1)) < HEAD_DIM // 2
    return jnp.where(first, pltpu.roll(x, LANES - HEAD_DIM // 2, 1), pltpu.roll(x, HEAD_DIM // 2, 1))


def _proj_kernel(x_ref, gain_ref, wn_ref, wt_ref, cos_ref, sin_ref, cost_ref, sint_ref,
                 qat_ref, qbt_ref, vat_ref, vbt_ref, ka_ref, kvb_ref, vc_ref, ga_ref, uc_ref):
    h = _rmsnorm(x_ref[...], gain_ref[...]).astype(MXU_DTYPE)
    nat = jnp.dot(h, wn_ref[...], preferred_element_type=jnp.float32)
    tr = _mm_nt(wt_ref[...], h)

    cos = cos_ref[...]
    sin = sin_ref[...]

    def roped(j):
        xj = nat[:, j * LANES:(j + 1) * LANES]
        return xj, xj * cos + _swap_halves(xj) * sin

    for j in range(3):
        ka_ref[:, j * LANES:(j + 1) * LANES] = roped(j)[1].astype(ka_ref.dtype)
    raw, rot = roped(3)
    lane = lax.broadcasted_iota(jnp.int32, raw.shape, 1)
    kvb_ref[...] = jnp.where(lane < HEAD_DIM, rot, raw).astype(kvb_ref.dtype)
    vc_ref[...] = nat[:, 4 * LANES:5 * LANES].astype(vc_ref.dtype)
    ga_ref[...] = jax.nn.sigmoid(nat[:, 5 * LANES:6 * LANES])
    uc_ref[...] = nat[:, 6 * LANES:]

    cos_t = cost_ref[...]
    sin_t = sint_ref[...]
    half = HEAD_DIM // 2

    def roped_t(r0):
        xh = tr[r0:r0 + HEAD_DIM]
        return xh * cos_t + jnp.concatenate([xh[half:], xh[:half]], axis=0) * sin_t

    for hd in range(A_HEADS):
        qat_ref[hd * HEAD_DIM:(hd + 1) * HEAD_DIM, :] = roped_t(hd * HEAD_DIM).astype(qat_ref.dtype)
    for hd in range(B_HEADS):
        qbt_ref[hd * HEAD_DIM:(hd + 1) * HEAD_DIM, :] = roped_t(A_WIDTH + hd * HEAD_DIM).astype(qbt_ref.dtype)
    v0 = A_WIDTH + B_WIDTH
    vat_ref[...] = tr[v0:v0 + 2 * LANES].astype(vat_ref.dtype)
    vbt_ref[...] = tr[v0 + 2 * LANES:].astype(vbt_ref.dtype)


_NAT_COLS = 10 * LANES
_TR_ROWS = A_WIDTH + B_WIDTH + 2 * LANES + HEAD_DIM


def _pack_w_in(w_in):
    o = [0]
    for w in IN_WIDTHS:
        o.append(o[-1] + w)
    scale = HEAD_DIM ** -0.5 * LOG2E
    q_a = w_in[:, o[0]:o[1]] * scale
    kv_cols = [w_in[:, o[1] + j * LANES:o[1] + (j + 1) * LANES] for j in range(2 * NSA_BRANCHES)]
    k_a = jnp.concatenate(kv_cols[0::2], axis=1)
    v_a = jnp.concatenate(kv_cols[1::2], axis=1)
    g_a = jnp.pad(w_in[:, o[2]:o[3]], ((0, 0), (0, LANES - NSA_BRANCHES * A_HEADS)))
    q_b = w_in[:, o[3]:o[4]] * scale
    kv_b = w_in[:, o[4]:o[5]]
    u_c = w_in[:, o[5]:o[6]]
    w_nat = jnp.concatenate([k_a, kv_b, v_a[:, :LANES], g_a, u_c], axis=1)
    w_tr = jnp.concatenate([q_a, q_b, v_a[:, LANES:], kv_b[:, HEAD_DIM:]], axis=1).T
    return w_nat.astype(MXU_DTYPE), w_tr.astype(MXU_DTYPE)


def _in_projection(x2, gain, w_nat, w_tr, tables, bsz, seq):
    n = x2.shape[0]
    tm = min(ROW_TILE, seq)
    tiles = seq // tm

    def rows(width):
        return pl.BlockSpec((tm, width), lambda i: (i, 0))

    def lanes(height):
        return pl.BlockSpec((None, height, tm), lambda i: (i // tiles, 0, i % tiles))

    table_t = pl.BlockSpec((HEAD_DIM, tm), lambda i: (0, i))
    t_heights = (A_WIDTH, B_WIDTH, 2 * LANES, HEAD_DIM)
    widths = (3 * LANES, LANES, LANES, LANES, C_WIDTH)
    dtypes = (MXU_DTYPE, MXU_DTYPE, MXU_DTYPE, jnp.float32, jnp.float32)
    return pl.pallas_call(
        _proj_kernel,
        grid=(n // tm,),
        in_specs=[rows(D_MODEL), _resident((1, D_MODEL)), _resident((D_MODEL, _NAT_COLS)),
                  _resident((_TR_ROWS, D_MODEL)), rows(LANES), rows(LANES), table_t, table_t],
        out_specs=[lanes(hh) for hh in t_heights] + [rows(w) for w in widths],
        out_shape=[jax.ShapeDtypeStruct((bsz, hh, seq), MXU_DTYPE) for hh in t_heights]
        + [jax.ShapeDtypeStruct((n, w), dt) for w, dt in zip(widths, dtypes)],
        compiler_params=_params(("parallel",)),
        name="in_projection",
    )(x2, gain.reshape(1, D_MODEL), w_nat, w_tr, *tables)


def _compress_kernel(seg_ref, pos_ref, w1_ref, w2_ref, o_ref):
    seg = seg_ref[...]
    half = seg.shape[1]
    first = _mm(seg, w1_ref[:half, :])
    second = _mm(seg, w1_ref[half:, :])
    nseg = seg.shape[0]
    hidden = first + pltpu.roll(second, nseg - 1, 0) + _mm(pos_ref[...], w1_ref[...])
    o_ref[...] = _mm(jax.nn.gelu(hidden), w2_ref[...]).astype(o_ref.dtype)


def _compress(segs, pos_flat, w1, w2):
    _, bsz, hkv, nseg, segw = segs.shape
    return pl.pallas_call(
        _compress_kernel,
        grid=(2, bsz, hkv),
        in_specs=[
            pl.BlockSpec((None, None, None, nseg, segw), lambda j, b, h: (j, b, h, 0, 0)),
            pl.BlockSpec((None, 1, CMP_LEN * HEAD_DIM), lambda j, b, h: (j, 0, 0)),
            pl.BlockSpec((None, CMP_LEN * HEAD_DIM, CMP_HIDDEN), lambda j, b, h: (j, 0, 0)),
            pl.BlockSpec((None, CMP_HIDDEN, HEAD_DIM), lambda j, b, h: (j, 0, 0)),
        ],
        out_specs=pl.BlockSpec((None, None, None, nseg, HEAD_DIM), lambda j, b, h: (j, b, h, 0, 0)),
        out_shape=jax.ShapeDtypeStruct((2, bsz, hkv, nseg, HEAD_DIM), MXU_DTYPE),
        compiler_params=_params(("parallel", "parallel", "parallel")),
        name="nsa_compress",
    )(segs, pos_flat, w1, w2)


def _split3(x):
    hi = x.astype(MXU_DTYPE)
    r1 = x - _f32(hi)
    mid = r1.astype(MXU_DTYPE)
    lo = (r1 - _f32(mid)).astype(MXU_DTYPE)
    return hi, mid, lo


def _col_max(s):
    return jnp.max(s, axis=0, keepdims=True)


def _tile_lanes(x, n):
    return jnp.concatenate([x] * n, axis=1)


def _pad_head_rows(x, h, fill):
    other = jnp.full(x.shape, fill, x.dtype)
    return jnp.concatenate([x, other] if h == 0 else [other, x], axis=0)


def _store_transposed_pairs(o_ref, heads_t, row0=0):
    for p in range(len(heads_t) // 2):
        pair = jnp.concatenate([heads_t[2 * p], heads_t[2 * p + 1]], axis=0)
        o_ref[row0:row0 + QBLK, p * LANES:(p + 1) * LANES] = pair.T.astype(o_ref.dtype)


def _nsa_kernel(qt_ref, k_ref, vt_ref, kc_ref, vct_ref, g_ref, c2s_ref, o_ref,
                selb_ref, sca_ref, scb_ref, cmp_ref, win_ref, *, seq):
    i = pl.program_id(1)
    q0 = i * QBLK
    n_cmp = kc_ref.shape[0]
    n_sel = seq // SEL_LEN
    kc_sz = min(SEL_CHUNK, seq)
    span = min(A_WINDOW + QBLK, seq)
    blocks_per_chunk = kc_sz // SEL_LEN

    gates_t = g_ref[...].T
    qp_lane = q0 + lax.broadcasted_iota(jnp.int32, (1, QBLK), 1)

    cmp_end = lax.broadcasted_iota(jnp.int32, (n_cmp, 1), 0) * CMP_STRIDE + (CMP_LEN - 1)
    bias_c = _tile_lanes(jnp.where(cmp_end <= qp_lane, 0.0, _NEG_INF), A_GROUP)

    w0 = pl.multiple_of(jnp.maximum(q0 + QBLK - span, 0), QBLK)
    diff = qp_lane - (w0 + lax.broadcasted_iota(jnp.int32, (span, 1), 0))
    bias_w = _tile_lanes(jnp.where(diff >= 0, jnp.where(diff < A_WINDOW, 0.0, _NEG_INF), _NEG_INF), A_GROUP)

    def head_rows(h):
        return slice(h * HEAD_DIM, (h + 1) * HEAD_DIM), slice((1 - h) * HEAD_DIM, (1 - h) * HEAD_DIM + 1)

    w_qs = []
    for h in range(A_KV_HEADS):
        q_t = jnp.concatenate([qt_ref[(h * A_GROUP + g) * HEAD_DIM:(h * A_GROUP + g + 1) * HEAD_DIM, :]
                               for g in range(A_GROUP)], axis=1)
        w_qs.append(_pad_head_rows(q_t, h, 0.0))

    n_chunks = (q0 + QBLK + kc_sz - 1) // kc_sz
    last = n_chunks - 1

    def issue_scores(c, dst_ref):
        k0 = pl.multiple_of(jnp.minimum(c, last) * kc_sz, kc_sz)
        keys = k_ref[pl.ds(k0, kc_sz), LANES:2 * LANES]
        for h in range(A_KV_HEADS):
            dst_ref[h] = _mm(keys, w_qs[h])

    for h in range(A_KV_HEADS):
        cmp_ref[h] = _mm(kc_ref[...], w_qs[h])
    for h in range(A_KV_HEADS):
        win_ref[h] = _mm(k_ref[pl.ds(w0, span), 2 * LANES:3 * LANES], w_qs[h])
    issue_scores(0, sca_ref)

    o_cs, o_ws = [], []
    for h in range(A_KV_HEADS):
        own, ones_row = head_rows(h)

        s = cmp_ref[h] + bias_c
        m = _col_max(s)
        m = jnp.where(m == _NEG_INF, 0.0, m)
        e = jnp.exp2(s - m)
        p = e * (1.0 / jnp.maximum(jnp.sum(e, axis=0, keepdims=True), 1e-30))
        o_cs.append(_mm(vct_ref[...], p)[own])

        p_sum = p[:, 0:QBLK]
        for g in range(1, A_GROUP):
            p_sum = p_sum + p[:, g * QBLK:(g + 1) * QBLK]
        imp_t = sum(_mm(c2s_ref[...], part) for part in _split3(p_sum))
        blk = lax.broadcasted_iota(jnp.int32, (n_sel, QBLK), 0)
        cur = jnp.right_shift(qp_lane, 6)
        forced = (blk == 0) | (blk == cur) | (blk == cur - 1)
        score = jnp.where(forced, FORCED_SCORE, imp_t)
        score = jnp.where(blk <= cur, score, _NEG_INF)
        n_grp = n_sel // SUBLANES
        sub = lax.broadcasted_iota(jnp.int32, (SUBLANES, QBLK), 0)
        score_grp = [score[SUBLANES * v:SUBLANES * (v + 1), :] for v in range(n_grp)]
        rank_grp = [jnp.zeros((SUBLANES, QBLK), jnp.float32) for _ in range(n_grp)]
        for j in range(n_sel):
            row = jnp.broadcast_to(score[j:j + 1, :], (SUBLANES, QBLK))
            vj, rj = divmod(j, SUBLANES)
            for v in range(n_grp):
                if v > vj:
                    inc = jnp.where(row >= score_grp[v], 1.0, 0.0)
                elif v < vj:
                    inc = jnp.where(row > score_grp[v], 1.0, 0.0)
                else:
                    inc = jnp.where(sub > rj, jnp.where(row >= score_grp[v], 1.0, 0.0),
                                    jnp.where(row > score_grp[v], 1.0, 0.0))
                rank_grp[v] = rank_grp[v] + inc
        rank = jnp.concatenate(rank_grp, axis=0)
        picked = jnp.where(rank < float(min(SEL_TOPK, n_sel)), 0.0, _NEG_INF)
        selb_ref[h] = jnp.where(blk <= cur, picked, _NEG_INF)

    for h in range(A_KV_HEADS):
        own, ones_row = head_rows(h)
        sw = win_ref[h] + bias_w
        pw = jnp.exp2(sw - _col_max(sw))
        vt_w = _pad_head_rows(vt_ref[LANES + h * HEAD_DIM:LANES + (h + 1) * HEAD_DIM, pl.ds(w0, span)],
                              h, 1.0)
        acc_w = _mm(vt_w, pw)
        o_ws.append(acc_w[own] / acc_w[ones_row])

    def consume(c, src_ref, states):
        cc = jnp.minimum(c, last)
        k0 = pl.multiple_of(cc * kc_sz, kc_sz)
        causal = (c * kc_sz + lax.broadcasted_iota(jnp.int32, (kc_sz, 1), 0)) <= qp_lane
        out = []
        for h in range(A_KV_HEADS):
            m_run, acc = states[h]
            slabs = [jnp.broadcast_to(selb_ref[h, pl.ds(cc * blocks_per_chunk + jj, 1), :], (SEL_LEN, QBLK))
                     for jj in range(blocks_per_chunk)]
            bias = jnp.where(causal, jnp.concatenate(slabs, axis=0), _NEG_INF)
            sc = src_ref[h] + _tile_lanes(bias, A_GROUP)
            m_new = jnp.maximum(m_run, _col_max(sc))
            pr = jnp.exp2(sc - m_new)
            vt = _pad_head_rows(vt_ref[h * HEAD_DIM:(h + 1) * HEAD_DIM, pl.ds(k0, kc_sz)], h, 1.0)
            out.append((m_new, jnp.exp2(m_run - m_new) * acc + _mm(vt, pr)))
        return tuple(out)

    def sel_step(t, states):
        issue_scores(2 * t + 1, scb_ref)
        states = consume(2 * t, sca_ref, states)
        issue_scores(2 * t + 2, sca_ref)
        return consume(2 * t + 1, scb_ref, states)

    init = tuple((jnp.full((1, A_GROUP * QBLK), _M_INIT, jnp.float32),
                  jnp.zeros((LANES, A_GROUP * QBLK), jnp.float32)) for _ in range(A_KV_HEADS))
    sel_out = lax.fori_loop(0, (n_chunks + 1) // 2, sel_step, init)

    heads_t = []
    for h in range(A_KV_HEADS):
        own, ones_row = head_rows(h)
        acc_s = sel_out[h][1]
        o_s = acc_s[own] / jnp.maximum(acc_s[ones_row], 1e-30)
        for g in range(A_GROUP):
            hd = h * A_GROUP + g
            cols = slice(g * QBLK, (g + 1) * QBLK)
            heads_t.append(gates_t[hd:hd + 1, :] * o_cs[h][:, cols]
                           + gates_t[A_HEADS + hd:A_HEADS + hd + 1, :] * o_s[:, cols]
                           + gates_t[2 * A_HEADS + hd:2 * A_HEADS + hd + 1, :] * o_ws[h][:, cols])
    _store_transposed_pairs(o_ref, heads_t)


def _nsa_attention(qat, ka, vat, kc, vct, ga, bsz, seq):
    nblk = seq // QBLK
    n_seg = seq // CMP_STRIDE
    n_sel = seq // SEL_LEN
    assert seq >= A_WINDOW + QBLK and seq % SEL_CHUNK == 0
    c_start = jnp.arange(n_seg) * CMP_STRIDE
    s_start = jnp.arange(n_sel) * SEL_LEN
    overlap = (jnp.minimum(c_start[None, :] + CMP_LEN, s_start[:, None] + SEL_LEN)
               - jnp.maximum(c_start[None, :], s_start[:, None]))
    c2s = (jnp.clip(overlap, 0, None).astype(jnp.float32) / CMP_LEN).astype(MXU_DTYPE)
    return pl.pallas_call(
        functools.partial(_nsa_kernel, seq=seq),
        grid=(bsz, nblk),
        in_specs=[
            pl.BlockSpec((None, A_WIDTH, QBLK), lambda b, i: (b, 0, i)),
            pl.BlockSpec((seq, 3 * LANES), lambda b, i: (b, 0)),
            pl.BlockSpec((None, 2 * LANES, seq), lambda b, i: (b, 0, 0)),
            pl.BlockSpec((None, n_seg, LANES), lambda b, i: (b, 0, 0)),
            pl.BlockSpec((None, LANES, n_seg), lambda b, i: (b, 0, 0)),
            pl.BlockSpec((QBLK, LANES), lambda b, i: (b * nblk + i, 0)),
            _resident((n_sel, n_seg)),
        ],
        out_specs=pl.BlockSpec((QBLK, A_WIDTH), lambda b, i: (b * nblk + i, 0)),
        out_shape=jax.ShapeDtypeStruct((bsz * seq, A_WIDTH), MXU_DTYPE),
        scratch_shapes=[pltpu.VMEM((A_KV_HEADS, n_sel, QBLK), jnp.float32)]
        + [pltpu.VMEM((A_KV_HEADS, min(SEL_CHUNK, seq), A_GROUP * QBLK), jnp.float32)] * 2
        + [pltpu.VMEM((A_KV_HEADS, n_seg, A_GROUP * QBLK), jnp.float32),
           pltpu.VMEM((A_KV_HEADS, min(A_WINDOW + QBLK, seq), A_GROUP * QBLK), jnp.float32)],
        compiler_params=_params(("parallel", "arbitrary")),
        name="nsa_attention",
    )(qat, ka, vat, kc, vct, ga, c2s)


def _swa_kernel(sink_ref, qt_ref, kv_ref, vt_ref, o_ref, sc_ref, *, seq):
    i = pl.program_id(1)
    span = min(B_WINDOW + QBLK, seq)
    sink = jnp.concatenate([jnp.full((1, QBLK), sink_ref[hd] * LOG2E, jnp.float32) for hd in range(B_HEADS)],
                           axis=1)
    starts = []
    for sb in range(SWA_QSUB):
        q0 = (i * SWA_QSUB + sb) * QBLK
        w0 = pl.multiple_of(jnp.maximum(q0 + QBLK - span, 0), QBLK)
        starts.append((q0, w0))
        q_t = jnp.concatenate([qt_ref[hd * HEAD_DIM:(hd + 1) * HEAD_DIM, sb * QBLK:(sb + 1) * QBLK]
                               for hd in range(B_HEADS)], axis=1)
        sc_ref[sb] = _mm(kv_ref[pl.ds(w0, span), :], _pad_head_rows(q_t, 0, 0.0))
    for sb, (q0, w0) in enumerate(starts):
        qp_lane = q0 + lax.broadcasted_iota(jnp.int32, (1, QBLK), 1)
        diff = qp_lane - (w0 + lax.broadcasted_iota(jnp.int32, (span, 1), 0))
        bias = _tile_lanes(jnp.where(diff >= 0, jnp.where(diff < B_WINDOW, 0.0, _NEG_INF), _NEG_INF), B_HEADS)
        s = sc_ref[sb] + bias
        m = jnp.maximum(_col_max(s), sink)
        acc = _mm(_pad_head_rows(vt_ref[:, pl.ds(w0, span)], 1, 1.0), jnp.exp2(s - m))
        o_t = acc[HEAD_DIM:] / (acc[0:1] + jnp.exp2(sink - m))
        _store_transposed_pairs(o_ref, [o_t[:, hd * QBLK:(hd + 1) * QBLK] for hd in range(B_HEADS)], sb * QBLK)


def _swa_attention(qbt, kvb, vbt, sinks, bsz, seq):
    rows = SWA_QSUB * QBLK
    nstep = seq // rows
    span = min(B_WINDOW + QBLK, seq)
    assert seq >= B_WINDOW + QBLK and seq % rows == 0
    return pl.pallas_call(
        functools.partial(_swa_kernel, seq=seq),
        grid=(bsz, nstep),
        in_specs=[
            pl.BlockSpec(memory_space=pltpu.SMEM),
            pl.BlockSpec((None, B_WIDTH, rows), lambda b, i: (b, 0, i)),
            pl.BlockSpec((seq, LANES), lambda b, i: (b, 0)),
            pl.BlockSpec((None, HEAD_DIM, seq), lambda b, i: (b, 0, 0)),
        ],
        out_specs=pl.BlockSpec((rows, B_WIDTH), lambda b, i: (b * nstep + i, 0)),
        out_shape=jax.ShapeDtypeStruct((bsz * seq, B_WIDTH), MXU_DTYPE),
        scratch_shapes=[pltpu.VMEM((SWA_QSUB, span, B_HEADS * QBLK), jnp.float32)],
        compiler_params=_params(("parallel", "arbitrary")),
        name="swa_attention",
    )(sinks, qbt, kvb, vbt)


def _s5_discretize_kernel(are_ref, aim_ref, logdt_ref, bre_ref, bim_ref,
                          abr_ref, abi_ref, bbr_ref, bbi_ref):
    ar = are_ref[...]
    ai = aim_ref[...]
    dt = jnp.exp(logdt_ref[...])
    mag = jnp.exp(dt * ar)
    abr = mag * jnp.cos(dt * ai)
    abi = mag * jnp.sin(dt * ai)
    den = ar * ar + ai * ai
    nr = abr - 1.0
    coef_r = (nr * ar + abi * ai) / den
    coef_i = (abi * ar - nr * ai) / den
    abr_ref[...] = abr
    abi_ref[...] = abi
    br = bre_ref[...]
    bi = bim_ref[...]
    bbr_ref[...] = coef_r[:, None, :] * br - coef_i[:, None, :] * bi
    bbi_ref[...] = coef_r[:, None, :] * bi + coef_i[:, None, :] * br


def _s5_discretize(a_re, a_im, log_dt, b_re, b_im):
    lg = a_re.shape[0] * a_re.shape[1]
    a2 = lambda a: a.reshape(lg, C_STATE)
    b3 = lambda b: jnp.swapaxes(b.reshape(lg, C_STATE, C_GROUP_CH), 1, 2)
    sa = jax.ShapeDtypeStruct((lg, C_STATE), jnp.float32)
    sb = jax.ShapeDtypeStruct((lg, C_GROUP_CH, C_STATE), jnp.float32)
    return pl.pallas_call(_s5_discretize_kernel, out_shape=[sa, sa, sb, sb], name="s5_discretize")(
        a2(a_re), a2(a_im), log_dt.reshape(lg, 1), b3(b_re), b3(b_im))


def _s5_kernel(u_ref, bmat_ref, abar_ref, cmat_ref, d_ref, gw_ref, gb_ref, y_ref, state_ref, ut_ref, xs_ref):
    bsz, t_steps, _ = u_ref.shape

    @pl.when(pl.program_id(0) == 0)
    def _():
        state_ref[...] = jnp.zeros_like(state_ref)

    def gather(t, carry):
        ut_ref[pl.ds(pl.multiple_of(t * bsz, bsz), bsz), :] = u_ref[:, t, :]
        return carry

    lax.fori_loop(0, t_steps, gather, 0, unroll=8)
    u = ut_ref[...]
    for hf in range(2):
        bu = _mm(u[:, hf * _HALF_CH:(hf + 1) * _HALF_CH], bmat_ref[hf])
        xs_ref[:, hf * _HALF_ST:(hf + 1) * _HALF_ST] = bu[:, :_HALF_ST]
        xs_ref[:, C_STATES + hf * _HALF_ST:C_STATES + (hf + 1) * _HALF_ST] = bu[:, _HALF_ST:]

    for cg in range(C_STATES // S5_LANES):
        re = slice(cg * S5_LANES, (cg + 1) * S5_LANES)
        im = slice(C_STATES + cg * S5_LANES, C_STATES + (cg + 1) * S5_LANES)
        ar = jnp.broadcast_to(abar_ref[0:1, re], (bsz, S5_LANES))
        ai = jnp.broadcast_to(abar_ref[0:1, im], (bsz, S5_LANES))

        def step(t, carry):
            xr, xi = carry
            r0 = pl.multiple_of(t * bsz, bsz)
            nr = ar * xr - ai * xi + xs_ref[pl.ds(r0, bsz), re]
            ni = ar * xi + ai * xr + xs_ref[pl.ds(r0, bsz), im]
            xs_ref[pl.ds(r0, bsz), re] = nr
            xs_ref[pl.ds(r0, bsz), im] = ni
            return nr, ni

        xr, xi = lax.fori_loop(0, t_steps, step, (state_ref[:, re], state_ref[:, im]), unroll=8)
        state_ref[:, re] = xr
        state_ref[:, im] = xi

    ys = []
    for hf in range(2):
        xr = xs_ref[:, hf * _HALF_ST:(hf + 1) * _HALF_ST]
        xi = xs_ref[:, C_STATES + hf * _HALF_ST:C_STATES + (hf + 1) * _HALF_ST]
        ys.append(_mm(xr, cmat_ref[hf, :_HALF_ST, :]) + _mm(xi, cmat_ref[hf, _HALF_ST:, :]))
    y = jnp.concatenate(ys, axis=1) + d_ref[...] * u
    z = jax.nn.gelu(y)
    out = z * jax.nn.sigmoid(_mm(z, gw_ref[...]) + gb_ref[...])
    y_ref[...] = out.reshape(t_steps, bsz, C_WIDTH).astype(y_ref.dtype)


def _s5(u, bmat, abar, cmat, d, glu_w, glu_b):
    bsz, seq, _ = u.shape
    t_steps = min(S5_CHUNK, seq)
    return pl.pallas_call(
        _s5_kernel,
        grid=(seq // t_steps,),
        in_specs=[
            pl.BlockSpec((bsz, t_steps, C_WIDTH), lambda i: (0, i, 0)),
            _resident((2, _HALF_CH, 2 * _HALF_ST)), _resident((1, 2 * C_STATES)), _resident((2, 2 * _HALF_ST, _HALF_CH)),
            _resident((1, C_WIDTH)), _resident((C_WIDTH, C_WIDTH)), _resident((1, C_WIDTH)),
        ],
        out_specs=pl.BlockSpec((t_steps, bsz, C_WIDTH), lambda i: (i, 0, 0)),
        out_shape=jax.ShapeDtypeStruct((seq, bsz, C_WIDTH), MXU_DTYPE),
        scratch_shapes=[pltpu.VMEM((bsz, 2 * C_STATES), jnp.float32),
                        pltpu.VMEM((t_steps * bsz, C_WIDTH), jnp.float32),
                        pltpu.VMEM((t_steps * bsz, 2 * C_STATES), jnp.float32)],
        compiler_params=_params(("arbitrary",)),
        name="s5_scan",
    )(u, bmat, abar, cmat, d.reshape(1, C_WIDTH), glu_w.astype(MXU_DTYPE), glu_b.reshape(1, C_WIDTH))


def _block_diag(per_group):
    g, r, c = per_group.shape
    eye = jnp.eye(g, dtype=per_group.dtype)
    return jnp.einsum("grc,gk->grkc", per_group, eye).reshape(g * r, g * c)


def _merge_kernel(x_ref, ya_ref, yb_ref, yc_ref, gain_ref, wg_ref, pa_ref, pb_ref, pc_ref, wo_ref, o_ref):
    x = x_ref[...]
    h = _rmsnorm(x, gain_ref[...]).astype(MXU_DTYPE)
    merged = None
    for j, (y_ref, p_ref) in enumerate(((ya_ref, pa_ref), (yb_ref, pb_ref), (yc_ref, pc_ref))):
        gate = jax.nn.sigmoid(jnp.dot(h, wg_ref[:, j * D_MODEL:(j + 1) * D_MODEL],
                                      preferred_element_type=jnp.float32))
        term = gate * jnp.dot(y_ref[...], p_ref[...], preferred_element_type=jnp.float32)
        merged = term if merged is None else merged + term
    o_ref[...] = x + _mm(merged, wo_ref[...])


def _merge(x2, ya, yb, yc, gain, wg, pa, pb, pc, wo):
    n = x2.shape[0]
    tm = min(ROW_TILE, n)

    def rows(width):
        return pl.BlockSpec((tm, width), lambda i: (i, 0))

    return pl.pallas_call(
        _merge_kernel,
        grid=(n // tm,),
        in_specs=[rows(D_MODEL), rows(A_WIDTH), rows(B_WIDTH), rows(C_WIDTH), _resident((1, D_MODEL)),
                  _resident((D_MODEL, N_BRANCH * D_MODEL)), _resident((A_WIDTH, D_MODEL)),
                  _resident((B_WIDTH, D_MODEL)), _resident((C_WIDTH, D_MODEL)), _resident((D_MODEL, D_MODEL))],
        out_specs=rows(D_MODEL),
        out_shape=jax.ShapeDtypeStruct((n, D_MODEL), jnp.float32),
        compiler_params=_params(("parallel",)),
        name="merge",
    )(x2, ya, yb, yc, gain.reshape(1, D_MODEL), wg, pa, pb, pc, wo)


def _mlp_kernel(x_ref, gain_ref, wu_ref, wd_ref, fgain_ref, o_ref, *, final_norm):
    x = x_ref[...]
    h = _rmsnorm(x, gain_ref[...]).astype(MXU_DTYPE)
    acc = x
    for c in range(D_FF // FF_CHUNK):
        cols = slice(c * FF_CHUNK, (c + 1) * FF_CHUNK)
        up = jnp.dot(h, wu_ref[:, cols], preferred_element_type=jnp.float32)
        acc = acc + _mm(jnp.square(jnp.maximum(up, 0.0)), wd_ref[cols, :])
    o_ref[...] = _rmsnorm(acc, fgain_ref[...]) if final_norm else acc


def _mlp(x2, gain, wu, wd, fgain, final_norm):
    n = x2.shape[0]
    tm = min(ROW_TILE, n)
    rows = pl.BlockSpec((tm, D_MODEL), lambda i: (i, 0))
    return pl.pallas_call(
        functools.partial(_mlp_kernel, final_norm=final_norm),
        grid=(n // tm,),
        in_specs=[rows, _resident((1, D_MODEL)), _resident((D_MODEL, D_FF)), _resident((D_FF, D_MODEL)),
                  _resident((1, D_MODEL))],
        out_specs=rows,
        out_shape=jax.ShapeDtypeStruct((n, D_MODEL), jnp.float32),
        compiler_params=_params(("parallel",)),
        name="mlp",
    )(x2, gain.reshape(1, D_MODEL), wu, wd, fgain.reshape(1, D_MODEL))


def kernel(x, positions, norm_mix, w_in, nsa_cmp_pos, nsa_cmp_w1, nsa_cmp_w2, swa_sinks, s5_a_re, s5_a_im, s5_log_dt, s5_b_re, s5_b_im, s5_c_re, s5_c_im, s5_d, s5_glu_w, s5_glu_b, w_branch_a, w_branch_b, w_branch_c, w_out, norm_mlp, w_mlp_up, w_mlp_down, norm_final):
    bsz, seq, _ = x.shape
    depth = w_in.shape[0]
    n = bsz * seq
    n_seg = seq // CMP_STRIDE
    gate_col = sum(IN_WIDTHS[:-1])
    bf = lambda w: w.astype(MXU_DTYPE)

    tables = _rope_tables(positions)
    abr, abi, bbr, bbi = _s5_discretize(s5_a_re, s5_a_im, s5_log_dt, s5_b_re, s5_b_im)
    abr = abr.reshape(depth, 1, C_STATES)
    abi = abi.reshape(depth, 1, C_STATES)
    bbr = bbr.reshape(depth, C_GROUPS, C_GROUP_CH, C_STATE)
    bbi = bbi.reshape(depth, C_GROUPS, C_GROUP_CH, C_STATE)

    x2 = x.reshape(n, D_MODEL)
    for l in range(depth):
        qat, qbt, vat, vbt, ka, kvb, vc, ga, uc = _in_projection(x2, norm_mix[l], *_pack_w_in(w_in[l]), tables,
                                                                  bsz, seq)

        def segments(t):
            t = t[:, :LANES].reshape(bsz, n_seg, CMP_STRIDE, A_KV_HEADS, HEAD_DIM)
            return jnp.transpose(t, (0, 3, 1, 2, 4)).reshape(bsz, A_KV_HEADS, n_seg, CMP_STRIDE * HEAD_DIM)

        kcvc = _compress(jnp.stack([segments(ka), segments(vc)]),
                         nsa_cmp_pos[l].reshape(2, 1, CMP_LEN * HEAD_DIM), nsa_cmp_w1[l], nsa_cmp_w2[l])
        kc = jnp.transpose(kcvc[0], (0, 2, 1, 3)).reshape(bsz, n_seg, LANES)
        vct = jnp.swapaxes(kcvc[1], 2, 3).reshape(bsz, LANES, n_seg)
        ya = _nsa_attention(qat, ka, vat, kc, vct, ga, bsz, seq)

        yb = _swa_attention(qbt, kvb, vbt, swa_sinks[l], bsz, seq)

        cre_t = jnp.swapaxes(s5_c_re[l], 1, 2)
        cim_t = jnp.swapaxes(s5_c_im[l], 1, 2)
        halves = (slice(0, C_GROUPS // 2), slice(C_GROUPS // 2, C_GROUPS))
        bmat = jnp.stack([jnp.concatenate([_block_diag(bbr[l][g]), _block_diag(bbi[l][g])], axis=1)
                          for g in halves])
        cmat = jnp.stack([jnp.concatenate([_block_diag(cre_t[g]), -_block_diag(cim_t[g])], axis=0)
                          for g in halves])
        abar = jnp.concatenate([abr[l], abi[l]], axis=1)
        yc_tm = _s5(uc.reshape(bsz, seq, C_WIDTH), bf(bmat), abar, bf(cmat), s5_d[l], s5_glu_w[l], s5_glu_b[l])
        yc = jnp.swapaxes(yc_tm, 0, 1).reshape(n, C_WIDTH)

        x2 = _merge(x2, ya, yb, yc, norm_mix[l], bf(w_in[l][:, gate_col:]), bf(w_branch_a[l]),
                    bf(w_branch_b[l]), bf(w_branch_c[l]), bf(w_out[l]))
        x2 = _mlp(x2, norm_mlp[l], bf(w_mlp_up[l]), bf(w_mlp_down[l]), norm_final, l == depth - 1)
    return x2.reshape(bsz, seq, D_MODEL)
```

```python
import functools

import jax
import jax.numpy as jnp
from jax import lax
from jax.experimental import pallas as pl
from jax.experimental.pallas import tpu as pltpu

D_MODEL = 1024
HEAD_DIM = 64
ROPE_THETA = 10000.0
NORM_EPS = 1e-6
QBLK = 128

A_HEADS = 8
A_KV_HEADS = 2
A_GROUP = A_HEADS // A_KV_HEADS
A_WIDTH = A_HEADS * HEAD_DIM
NSA_BRANCHES = 3
CMP_LEN = 32
CMP_STRIDE = 16
CMP_HIDDEN = 2 * HEAD_DIM
SEL_LEN = 64
SEL_TOPK = 16
A_WINDOW = 512
FORCED_SCORE = 1e4

B_HEADS = 8
B_WIDTH = B_HEADS * HEAD_DIM
B_WINDOW = 128

C_WIDTH = 512
C_GROUP_CH = 16
C_GROUPS = C_WIDTH // C_GROUP_CH
C_STATE = 64
C_STATES = C_GROUPS * C_STATE
_HALF_CH = C_WIDTH // 2
_HALF_ST = C_STATES // 2

D_FF = 4 * D_MODEL
N_BRANCH = 3

IN_WIDTHS = (A_WIDTH, NSA_BRANCHES * 2 * A_KV_HEADS * HEAD_DIM, NSA_BRANCHES * A_HEADS,
             B_WIDTH, 2 * HEAD_DIM, C_WIDTH, N_BRANCH * D_MODEL)

LANES = 128
SUBLANES = 8
VMEM_LIMIT = 56 * 1024 * 1024

MXU_DTYPE = jnp.bfloat16
ROW_TILE = 256
S5_CHUNK = 64
S5_LANES = 512
SEL_CHUNK = 256
SWA_QSUB = 2
FF_CHUNK = 1024

LOG2E = 1.4426950408889634
_NEG_INF = float("-inf")
_M_INIT = -1e30


def _f32(x):
    return x.astype(jnp.float32)


def _mm(a, b):
    return jnp.dot(a.astype(MXU_DTYPE), b.astype(MXU_DTYPE), preferred_element_type=jnp.float32)


def _mm_nt(a, b):
    return lax.dot_general(a.astype(MXU_DTYPE), b.astype(MXU_DTYPE), (((1,), (1,)), ((), ())),
                           preferred_element_type=jnp.float32)


def _rmsnorm(x, gain):
    return x * lax.rsqrt(jnp.mean(x * x, axis=-1, keepdims=True) + NORM_EPS) * gain


def _resident(shape):
    zeros = (0,) * len(shape)
    return pl.BlockSpec(shape, lambda *_: zeros, pipeline_mode=pl.Buffered(1))


def _params(sem):
    return pltpu.CompilerParams(dimension_semantics=sem, vmem_limit_bytes=VMEM_LIMIT)


def _rope_table_kernel(pos_ref, invf_ref, sign_ref, cos_ref, sin_ref):
    ang = _f32(pos_ref[...]) * invf_ref[...]
    cos_ref[...] = jnp.cos(ang)
    sin_ref[...] = jnp.sin(ang) * sign_ref[...]


def _rope_tables(positions):
    n = positions.size
    half = HEAD_DIM // 2
    inv_freq = ROPE_THETA ** (-jnp.arange(half, dtype=jnp.float32) / half)
    invf = jnp.tile(inv_freq, LANES // half)
    sign = jnp.tile(jnp.concatenate([-jnp.ones(half, jnp.float32), jnp.ones(half, jnp.float32)]),
                    LANES // HEAD_DIM)
    tm = min(1024, n)
    natural = pl.pallas_call(
        _rope_table_kernel,
        grid=(n // tm,),
        in_specs=[pl.BlockSpec((tm, 1), lambda i: (i, 0)), _resident((1, LANES)), _resident((1, LANES))],
        out_specs=[pl.BlockSpec((tm, LANES), lambda i: (i, 0))] * 2,
        out_shape=[jax.ShapeDtypeStruct((n, LANES), jnp.float32)] * 2,
        compiler_params=_params(("parallel",)),
        name="rope_tables",
    )(positions.reshape(n, 1), invf.reshape(1, LANES), sign.reshape(1, LANES))
    transposed = pl.pallas_call(
        _rope_table_kernel,
        grid=(n // tm,),
        in_specs=[pl.BlockSpec((1, tm), lambda i: (0, i)), _resident((HEAD_DIM, 1)), _resident((HEAD_DIM, 1))],
        out_specs=[pl.BlockSpec((HEAD_DIM, tm), lambda i: (0, i))] * 2,
        out_shape=[jax.ShapeDtypeStruct((HEAD_DIM, n), jnp.float32)] * 2,
        compiler_params=_params(("parallel",)),
        name="rope_tables_t",
    )(positions.reshape(1, n), invf[:HEAD_DIM].reshape(HEAD_DIM, 1), sign[:HEAD_DIM].reshape(HEAD_DIM, 1))
    return natural + transposed


def _swap_halves(x):
    lane = lax.broadcasted_iota(jnp.int32, x.shape, 1)
    first = (lane & (HEAD_DIM - 1)) < HEAD_DIM // 2
    return jnp.where(first, pltpu.roll(x, LANES - HEAD_DIM // 2, 1), pltpu.roll(x, HEAD_DIM // 2, 1))


def _proj_kernel(x_ref, gain_ref, wn_ref, wt_ref, cos_ref, sin_ref, cost_ref, sint_ref,
                 qat_ref, qbt_ref, vat_ref, vbt_ref, ka_ref, kvb_ref, kseg_ref, vseg_ref, ga_ref, uc_ref,
                 seg_ref):
    h = _rmsnorm(x_ref[...], gain_ref[...]).astype(MXU_DTYPE)
    nat = jnp.dot(h, wn_ref[...], preferred_element_type=jnp.float32)
    tr = _mm_nt(wt_ref[...], h)

    cos = cos_ref[...]
    sin = sin_ref[...]

    def roped(j):
        xj = nat[:, j * LANES:(j + 1) * LANES]
        return xj, xj * cos + _swap_halves(xj) * sin

    for j in range(1, NSA_BRANCHES):
        ka_ref[:, (j - 1) * LANES:j * LANES] = roped(j)[1].astype(ka_ref.dtype)
    raw, rot = roped(3)
    lane = lax.broadcasted_iota(jnp.int32, raw.shape, 1)
    kvb_ref[...] = jnp.where(lane < HEAD_DIM, rot, raw).astype(kvb_ref.dtype)
    n_seg_tile = nat.shape[0] // CMP_STRIDE
    for j, (src, dst_ref) in enumerate(((roped(0)[1], kseg_ref), (nat[:, 4 * LANES:5 * LANES], vseg_ref))):
        seg_ref[j] = src
        for tok in range(CMP_STRIDE):
            dst_ref[:, tok * LANES:(tok + 1) * LANES] = (
                seg_ref[j, pl.ds(tok, n_seg_tile, stride=CMP_STRIDE), :].astype(dst_ref.dtype))
    ga_ref[...] = jax.nn.sigmoid(nat[:, 5 * LANES:6 * LANES])
    uc_ref[...] = nat[:, 6 * LANES:]

    cos_t = cost_ref[...]
    sin_t = sint_ref[...]
    half = HEAD_DIM // 2

    def roped_t(r0):
        xh = tr[r0:r0 + HEAD_DIM]
        return xh * cos_t + jnp.concatenate([xh[half:], xh[:half]], axis=0) * sin_t

    for hd in range(A_HEADS):
        qat_ref[hd * HEAD_DIM:(hd + 1) * HEAD_DIM, :] = roped_t(hd * HEAD_DIM).astype(qat_ref.dtype)
    for hd in range(B_HEADS):
        qbt_ref[hd * HEAD_DIM:(hd + 1) * HEAD_DIM, :] = roped_t(A_WIDTH + hd * HEAD_DIM).astype(qbt_ref.dtype)
    v0 = A_WIDTH + B_WIDTH
    vat_ref[...] = tr[v0:v0 + 2 * LANES].astype(vat_ref.dtype)
    vbt_ref[...] = tr[v0 + 2 * LANES:].astype(vbt_ref.dtype)


_NAT_COLS = 10 * LANES
_TR_ROWS = A_WIDTH + B_WIDTH + 2 * LANES + HEAD_DIM


def _pack_w_in(w_in):
    o = [0]
    for w in IN_WIDTHS:
        o.append(o[-1] + w)
    scale = HEAD_DIM ** -0.5 * LOG2E
    q_a = w_in[:, o[0]:o[1]] * scale
    kv_cols = [w_in[:, o[1] + j * LANES:o[1] + (j + 1) * LANES] for j in range(2 * NSA_BRANCHES)]
    k_a = jnp.concatenate(kv_cols[0::2], axis=1)
    v_a = jnp.concatenate(kv_cols[1::2], axis=1)
    g_a = jnp.pad(w_in[:, o[2]:o[3]], ((0, 0), (0, LANES - NSA_BRANCHES * A_HEADS)))
    q_b = w_in[:, o[3]:o[4]] * scale
    kv_b = w_in[:, o[4]:o[5]]
    u_c = w_in[:, o[5]:o[6]]
    w_nat = jnp.concatenate([k_a, kv_b, v_a[:, :LANES], g_a, u_c], axis=1)
    w_tr = jnp.concatenate([q_a, q_b, v_a[:, LANES:], kv_b[:, HEAD_DIM:]], axis=1).T
    return w_nat.astype(MXU_DTYPE), w_tr.astype(MXU_DTYPE)


def _in_projection(x2, gain, w_nat, w_tr, tables, bsz, seq):
    n = x2.shape[0]
    tm = min(ROW_TILE, seq)
    tiles = seq // tm

    def rows(width):
        return pl.BlockSpec((tm, width), lambda i: (i, 0))

    def lanes(height):
        return pl.BlockSpec((None, height, tm), lambda i: (i // tiles, 0, i % tiles))

    table_t = pl.BlockSpec((HEAD_DIM, tm), lambda i: (0, i))
    t_heights = (A_WIDTH, B_WIDTH, 2 * LANES, HEAD_DIM)
    widths = (2 * LANES, LANES, LANES, C_WIDTH)
    dtypes = (MXU_DTYPE, MXU_DTYPE, jnp.float32, jnp.float32)
    seg_rows = tm // CMP_STRIDE
    seg_spec = pl.BlockSpec((seg_rows, CMP_STRIDE * LANES), lambda i: (i, 0))
    seg_shape = jax.ShapeDtypeStruct((n // CMP_STRIDE, CMP_STRIDE * LANES), MXU_DTYPE)
    return pl.pallas_call(
        _proj_kernel,
        grid=(n // tm,),
        in_specs=[rows(D_MODEL), _resident((1, D_MODEL)), _resident((D_MODEL, _NAT_COLS)),
                  _resident((_TR_ROWS, D_MODEL)), rows(LANES), rows(LANES), table_t, table_t],
        out_specs=[lanes(hh) for hh in t_heights] + [rows(w) for w in widths[:2]] + [seg_spec, seg_spec]
        + [rows(w) for w in widths[2:]],
        out_shape=[jax.ShapeDtypeStruct((bsz, hh, seq), MXU_DTYPE) for hh in t_heights]
        + [jax.ShapeDtypeStruct((n, w), dt) for w, dt in zip(widths[:2], dtypes[:2])] + [seg_shape, seg_shape]
        + [jax.ShapeDtypeStruct((n, w), dt) for w, dt in zip(widths[2:], dtypes[2:])],
        scratch_shapes=[pltpu.VMEM((2, tm, LANES), jnp.float32)],
        compiler_params=_params(("parallel",)),
        name="in_projection",
    )(x2, gain.reshape(1, D_MODEL), w_nat, w_tr, *tables)


def _compress_kernel(kseg_ref, vseg_ref, pos_ref, w1_ref, w1x_ref, w2_ref, kc_ref, vct_ref):
    nseg = kseg_ref.shape[0]
    both = []
    for j, seg_ref in enumerate((kseg_ref, vseg_ref)):
        seg = seg_ref[...]
        pos_term = _mm(pos_ref[j], w1_ref[j])
        heads = []
        for h in range(A_KV_HEADS):
            hidden = (_mm(seg, w1x_ref[j, 0, h]) + pltpu.roll(_mm(seg, w1x_ref[j, 1, h]), nseg - 1, 0)
                      + pos_term)
            heads.append(_mm(jax.nn.gelu(hidden), w2_ref[j]))
        both.append(jnp.concatenate(heads, axis=1))
    kc_ref[...] = both[0].astype(kc_ref.dtype)
    vct_ref[...] = both[1].T.astype(vct_ref.dtype)


def _expand_w1(w1):
    w = w1.reshape(2, 2, CMP_STRIDE, HEAD_DIM, CMP_HIDDEN)
    eye = jnp.eye(A_KV_HEADS, dtype=w1.dtype)
    return jnp.einsum("jatdc,hg->jahtgdc", w, eye).reshape(2, 2, A_KV_HEADS, CMP_STRIDE * LANES, CMP_HIDDEN)


def _compress(kseg, vseg, pos_flat, w1, w2, bsz):
    nseg = kseg.shape[0] // bsz
    segw = kseg.shape[1]
    seg_spec = pl.BlockSpec((nseg, segw), lambda b: (b, 0))
    return pl.pallas_call(
        _compress_kernel,
        grid=(bsz,),
        in_specs=[seg_spec, seg_spec, _resident((2, 1, CMP_LEN * HEAD_DIM)),
                  _resident((2, CMP_LEN * HEAD_DIM, CMP_HIDDEN)),
                  _resident((2, 2, A_KV_HEADS, segw, CMP_HIDDEN)), _resident((2, CMP_HIDDEN, HEAD_DIM))],
        out_specs=[pl.BlockSpec((None, nseg, LANES), lambda b: (b, 0, 0)),
                   pl.BlockSpec((None, LANES, nseg), lambda b: (b, 0, 0))],
        out_shape=[jax.ShapeDtypeStruct((bsz, nseg, LANES), MXU_DTYPE),
                   jax.ShapeDtypeStruct((bsz, LANES, nseg), MXU_DTYPE)],
        compiler_params=_params(("parallel",)),
        name="nsa_compress",
    )(kseg, vseg, pos_flat.astype(MXU_DTYPE), w1.astype(MXU_DTYPE), _expand_w1(w1).astype(MXU_DTYPE),
      w2.astype(MXU_DTYPE))


def _split3(x):
    hi = x.astype(MXU_DTYPE)
    r1 = x - _f32(hi)
    mid = r1.astype(MXU_DTYPE)
    lo = (r1 - _f32(mid)).astype(MXU_DTYPE)
    return hi, mid, lo


def _col_max(s):
    return jnp.max(s, axis=0, keepdims=True)


def _tile_lanes(x, n):
    return jnp.concatenate([x] * n, axis=1)


def _pad_head_rows(x, h, fill):
    other = jnp.full(x.shape, fill, x.dtype)
    return jnp.concatenate([x, other] if h == 0 else [other, x], axis=0)


def _store_transposed_pairs(o_ref, heads_t, row0=0):
    for p in range(len(heads_t) // 2):
        pair = jnp.concatenate([heads_t[2 * p], heads_t[2 * p + 1]], axis=0)
        o_ref[row0:row0 + QBLK, p * LANES:(p + 1) * LANES] = pair.T.astype(o_ref.dtype)


def _nsa_kernel(qt_ref, k_ref, vt_ref, kc_ref, vct_ref, g_ref, c2s_ref, o_ref,
                selb_ref, sca_ref, scb_ref, cmp_ref, win_ref, *, seq):
    i = pl.program_id(1)
    q0 = i * QBLK
    n_cmp = kc_ref.shape[0]
    n_sel = seq // SEL_LEN
    kc_sz = min(SEL_CHUNK, seq)
    span = min(A_WINDOW + QBLK, seq)
    blocks_per_chunk = kc_sz // SEL_LEN

    gates_t = g_ref[...].T
    qp_lane = q0 + lax.broadcasted_iota(jnp.int32, (1, QBLK), 1)

    cmp_end = lax.broadcasted_iota(jnp.int32, (n_cmp, 1), 0) * CMP_STRIDE + (CMP_LEN - 1)
    bias_c = _tile_lanes(jnp.where(cmp_end <= qp_lane, 0.0, _NEG_INF), A_GROUP)

    w0 = pl.multiple_of(jnp.maximum(q0 + QBLK - span, 0), QBLK)
    diff = qp_lane - (w0 + lax.broadcasted_iota(jnp.int32, (span, 1), 0))
    bias_w = _tile_lanes(jnp.where(diff >= 0, jnp.where(diff < A_WINDOW, 0.0, _NEG_INF), _NEG_INF), A_GROUP)

    def head_rows(h):
        return slice(h * HEAD_DIM, (h + 1) * HEAD_DIM), slice((1 - h) * HEAD_DIM, (1 - h) * HEAD_DIM + 1)

    w_qs = []
    for h in range(A_KV_HEADS):
        q_t = jnp.concatenate([qt_ref[(h * A_GROUP + g) * HEAD_DIM:(h * A_GROUP + g + 1) * HEAD_DIM, :]
                               for g in range(A_GROUP)], axis=1)
        w_qs.append(_pad_head_rows(q_t, h, 0.0))

    n_chunks = (q0 + QBLK + kc_sz - 1) // kc_sz
    last = n_chunks - 1

    def issue_scores(c, dst_ref):
        k0 = pl.multiple_of(jnp.minimum(c, last) * kc_sz, kc_sz)
        keys = k_ref[pl.ds(k0, kc_sz), :LANES]
        for h in range(A_KV_HEADS):
            dst_ref[h] = _mm(keys, w_qs[h])

    for h in range(A_KV_HEADS):
        cmp_ref[h] = _mm(kc_ref[...], w_qs[h])
    for h in range(A_KV_HEADS):
        win_ref[h] = _mm(k_ref[pl.ds(w0, span), LANES:], w_qs[h])
    issue_scores(0, sca_ref)

    o_cs, o_ws = [], []
    for h in range(A_KV_HEADS):
        own, ones_row = head_rows(h)

        s = cmp_ref[h] + bias_c
        m = _col_max(s)
        m = jnp.where(m == _NEG_INF, 0.0, m)
        e = jnp.exp2(s - m)
        p = e * (1.0 / jnp.maximum(jnp.sum(e, axis=0, keepdims=True), 1e-30))
        o_cs.append(_mm(vct_ref[...], p)[own])

        p_sum = p[:, 0:QBLK]
        for g in range(1, A_GROUP):
            p_sum = p_sum + p[:, g * QBLK:(g + 1) * QBLK]
        imp_t = sum(_mm(c2s_ref[...], part) for part in _split3(p_sum))
        blk = lax.broadcasted_iota(jnp.int32, (n_sel, QBLK), 0)
        cur = jnp.right_shift(qp_lane, 6)
        forced = (blk == 0) | (blk == cur) | (blk == cur - 1)
        score = jnp.where(forced, FORCED_SCORE, imp_t)
        score = jnp.where(blk <= cur, score, _NEG_INF)
        n_grp = n_sel // SUBLANES
        sub = lax.broadcasted_iota(jnp.int32, (SUBLANES, QBLK), 0)
        score_grp = [score[SUBLANES * v:SUBLANES * (v + 1), :] for v in range(n_grp)]
        rank_grp = [jnp.zeros((SUBLANES, QBLK), jnp.float32) for _ in range(n_grp)]
        for j in range(n_sel):
            row = jnp.broadcast_to(score[j:j + 1, :], (SUBLANES, QBLK))
            vj, rj = divmod(j, SUBLANES)
            for v in range(n_grp):
                if v > vj:
                    inc = jnp.where(row >= score_grp[v], 1.0, 0.0)
                elif v < vj:
                    inc = jnp.where(row > score_grp[v], 1.0, 0.0)
                else:
                    inc = jnp.where(sub > rj, jnp.where(row >= score_grp[v], 1.0, 0.0),
                                    jnp.where(row > score_grp[v], 1.0, 0.0))
                rank_grp[v] = rank_grp[v] + inc
        rank = jnp.concatenate(rank_grp, axis=0)
        picked = jnp.where(rank < float(min(SEL_TOPK, n_sel)), 0.0, _NEG_INF)
        selb_ref[h] = jnp.where(blk <= cur, picked, _NEG_INF)

    for h in range(A_KV_HEADS):
        own, ones_row = head_rows(h)
        sw = win_ref[h] + bias_w
        pw = jnp.exp2(sw - _col_max(sw))
        vt_w = _pad_head_rows(vt_ref[LANES + h * HEAD_DIM:LANES + (h + 1) * HEAD_DIM, pl.ds(w0, span)],
                              h, 1.0)
        acc_w = _mm(vt_w, pw)
        o_ws.append(acc_w[own] / acc_w[ones_row])

    def consume(c, src_ref, states):
        cc = jnp.minimum(c, last)
        k0 = pl.multiple_of(cc * kc_sz, kc_sz)
        causal = (c * kc_sz + lax.broadcasted_iota(jnp.int32, (kc_sz, 1), 0)) <= qp_lane
        out = []
        for h in range(A_KV_HEADS):
            m_run, acc = states[h]
            slabs = [jnp.broadcast_to(selb_ref[h, pl.ds(cc * blocks_per_chunk + jj, 1), :], (SEL_LEN, QBLK))
                     for jj in range(blocks_per_chunk)]
            bias = jnp.where(causal, jnp.concatenate(slabs, axis=0), _NEG_INF)
            sc = src_ref[h] + _tile_lanes(bias, A_GROUP)
            m_new = jnp.maximum(m_run, _col_max(sc))
            pr = jnp.exp2(sc - m_new)
            vt = _pad_head_rows(vt_ref[h * HEAD_DIM:(h + 1) * HEAD_DIM, pl.ds(k0, kc_sz)], h, 1.0)
            out.append((m_new, jnp.exp2(m_run - m_new) * acc + _mm(vt, pr)))
        return tuple(out)

    def sel_step(t, states):
        issue_scores(2 * t + 1, scb_ref)
        states = consume(2 * t, sca_ref, states)
        issue_scores(2 * t + 2, sca_ref)
        return consume(2 * t + 1, scb_ref, states)

    init = tuple((jnp.full((1, A_GROUP * QBLK), _M_INIT, jnp.float32),
                  jnp.zeros((LANES, A_GROUP * QBLK), jnp.float32)) for _ in range(A_KV_HEADS))
    sel_out = lax.fori_loop(0, (n_chunks + 1) // 2, sel_step, init)

    heads_t = []
    for h in range(A_KV_HEADS):
        own, ones_row = head_rows(h)
        acc_s = sel_out[h][1]
        o_s = acc_s[own] / jnp.maximum(acc_s[ones_row], 1e-30)
        for g in range(A_GROUP):
            hd = h * A_GROUP + g
            cols = slice(g * QBLK, (g + 1) * QBLK)
            heads_t.append(gates_t[hd:hd + 1, :] * o_cs[h][:, cols]
                           + gates_t[A_HEADS + hd:A_HEADS + hd + 1, :] * o_s[:, cols]
                           + gates_t[2 * A_HEADS + hd:2 * A_HEADS + hd + 1, :] * o_ws[h][:, cols])
    _store_transposed_pairs(o_ref, heads_t)


def _nsa_attention(qat, ka, vat, kc, vct, ga, bsz, seq):
    nblk = seq // QBLK
    n_seg = seq // CMP_STRIDE
    n_sel = seq // SEL_LEN
    assert seq >= A_WINDOW + QBLK and seq % SEL_CHUNK == 0
    c_start = jnp.arange(n_seg) * CMP_STRIDE
    s_start = jnp.arange(n_sel) * SEL_LEN
    overlap = (jnp.minimum(c_start[None, :] + CMP_LEN, s_start[:, None] + SEL_LEN)
               - jnp.maximum(c_start[None, :], s_start[:, None]))
    c2s = (jnp.clip(overlap, 0, None).astype(jnp.float32) / CMP_LEN).astype(MXU_DTYPE)
    return pl.pallas_call(
        functools.partial(_nsa_kernel, seq=seq),
        grid=(bsz, nblk),
        in_specs=[
            pl.BlockSpec((None, A_WIDTH, QBLK), lambda b, i: (b, 0, i)),
            pl.BlockSpec((seq, 2 * LANES), lambda b, i: (b, 0)),
            pl.BlockSpec((None, 2 * LANES, seq), lambda b, i: (b, 0, 0)),
            pl.BlockSpec((None, n_seg, LANES), lambda b, i: (b, 0, 0)),
            pl.BlockSpec((None, LANES, n_seg), lambda b, i: (b, 0, 0)),
            pl.BlockSpec((QBLK, LANES), lambda b, i: (b * nblk + i, 0)),
            _resident((n_sel, n_seg)),
        ],
        out_specs=pl.BlockSpec((QBLK, A_WIDTH), lambda b, i: (b * nblk + i, 0)),
        out_shape=jax.ShapeDtypeStruct((bsz * seq, A_WIDTH), MXU_DTYPE),
        scratch_shapes=[pltpu.VMEM((A_KV_HEADS, n_sel, QBLK), jnp.float32)]
        + [pltpu.VMEM((A_KV_HEADS, min(SEL_CHUNK, seq), A_GROUP * QBLK), jnp.float32)] * 2
        + [pltpu.VMEM((A_KV_HEADS, n_seg, A_GROUP * QBLK), jnp.float32),
           pltpu.VMEM((A_KV_HEADS, min(A_WINDOW + QBLK, seq), A_GROUP * QBLK), jnp.float32)],
        compiler_params=_params(("parallel", "arbitrary")),
        name="nsa_attention",
    )(qat, ka, vat, kc, vct, ga, c2s)


def _swa_kernel(sink_ref, qt_ref, kv_ref, vt_ref, o_ref, sc_ref, *, seq):
    i = pl.program_id(1)
    span = min(B_WINDOW + QBLK, seq)
    sink = jnp.concatenate([jnp.full((1, QBLK), sink_ref[hd] * LOG2E, jnp.float32) for hd in range(B_HEADS)],
                           axis=1)
    starts = []
    for sb in range(SWA_QSUB):
        q0 = (i * SWA_QSUB + sb) * QBLK
        w0 = pl.multiple_of(jnp.maximum(q0 + QBLK - span, 0), QBLK)
        starts.append((q0, w0))
        q_t = jnp.concatenate([qt_ref[hd * HEAD_DIM:(hd + 1) * HEAD_DIM, sb * QBLK:(sb + 1) * QBLK]
                               for hd in range(B_HEADS)], axis=1)
        sc_ref[sb] = _mm(kv_ref[pl.ds(w0, span), :], _pad_head_rows(q_t, 0, 0.0))
    for sb, (q0, w0) in enumerate(starts):
        qp_lane = q0 + lax.broadcasted_iota(jnp.int32, (1, QBLK), 1)
        diff = qp_lane - (w0 + lax.broadcasted_iota(jnp.int32, (span, 1), 0))
        bias = _tile_lanes(jnp.where(diff >= 0, jnp.where(diff < B_WINDOW, 0.0, _NEG_INF), _NEG_INF), B_HEADS)
        s = sc_ref[sb] + bias
        m = jnp.maximum(_col_max(s), sink)
        acc = _mm(_pad_head_rows(vt_ref[:, pl.ds(w0, span)], 1, 1.0), jnp.exp2(s - m))
        o_t = acc[HEAD_DIM:] / (acc[0:1] + jnp.exp2(sink - m))
        _store_transposed_pairs(o_ref, [o_t[:, hd * QBLK:(hd + 1) * QBLK] for hd in range(B_HEADS)], sb * QBLK)


def _swa_attention(qbt, kvb, vbt, sinks, bsz, seq):
    rows = SWA_QSUB * QBLK
    nstep = seq // rows
    span = min(B_WINDOW + QBLK, seq)
    assert seq >= B_WINDOW + QBLK and seq % rows == 0
    return pl.pallas_call(
        functools.partial(_swa_kernel, seq=seq),
        grid=(bsz, nstep),
        in_specs=[
            pl.BlockSpec(memory_space=pltpu.SMEM),
            pl.BlockSpec((None, B_WIDTH, rows), lambda b, i: (b, 0, i)),
            pl.BlockSpec((seq, LANES), lambda b, i: (b, 0)),
            pl.BlockSpec((None, HEAD_DIM, seq), lambda b, i: (b, 0, 0)),
        ],
        out_specs=pl.BlockSpec((rows, B_WIDTH), lambda b, i: (b * nstep + i, 0)),
        out_shape=jax.ShapeDtypeStruct((bsz * seq, B_WIDTH), MXU_DTYPE),
        scratch_shapes=[pltpu.VMEM((SWA_QSUB, span, B_HEADS * QBLK), jnp.float32)],
        compiler_params=_params(("parallel", "arbitrary")),
        name="swa_attention",
    )(sinks, qbt, kvb, vbt)


def _s5_discretize_kernel(are_ref, aim_ref, logdt_ref, bre_ref, bim_ref,
                          abr_ref, abi_ref, bbr_ref, bbi_ref):
    ar = are_ref[...]
    ai = aim_ref[...]
    dt = jnp.exp(logdt_ref[...])
    mag = jnp.exp(dt * ar)
    abr = mag * jnp.cos(dt * ai)
    abi = mag * jnp.sin(dt * ai)
    den = ar * ar + ai * ai
    nr = abr - 1.0
    coef_r = (nr * ar + abi * ai) / den
    coef_i = (abi * ar - nr * ai) / den
    abr_ref[...] = abr
    abi_ref[...] = abi
    br = bre_ref[...]
    bi = bim_ref[...]
    bbr_ref[...] = coef_r[:, None, :] * br - coef_i[:, None, :] * bi
    bbi_ref[...] = coef_r[:, None, :] * bi + coef_i[:, None, :] * br


def _s5_discretize(a_re, a_im, log_dt, b_re, b_im):
    lg = a_re.shape[0] * a_re.shape[1]
    a2 = lambda a: a.reshape(lg, C_STATE)
    b3 = lambda b: jnp.swapaxes(b.reshape(lg, C_STATE, C_GROUP_CH), 1, 2)
    sa = jax.ShapeDtypeStruct((lg, C_STATE), jnp.float32)
    sb = jax.ShapeDtypeStruct((lg, C_GROUP_CH, C_STATE), jnp.float32)
    return pl.pallas_call(_s5_discretize_kernel, out_shape=[sa, sa, sb, sb], name="s5_discretize")(
        a2(a_re), a2(a_im), log_dt.reshape(lg, 1), b3(b_re), b3(b_im))


def _s5_kernel(u_ref, bmat_ref, abar_ref, cmat_ref, d_ref, gw_ref, gb_ref, y_ref, state_ref, ut_ref, xs_ref):
    bsz, t_steps, _ = u_ref.shape

    @pl.when(pl.program_id(0) == 0)
    def _():
        state_ref[...] = jnp.zeros_like(state_ref)

    def gather(t, carry):
        ut_ref[pl.ds(pl.multiple_of(t * bsz, bsz), bsz), :] = u_ref[:, t, :]
        return carry

    lax.fori_loop(0, t_steps, gather, 0, unroll=8)
    u = ut_ref[...]
    for hf in range(2):
        bu = _mm(u[:, hf * _HALF_CH:(hf + 1) * _HALF_CH], bmat_ref[hf])
        xs_ref[:, hf * _HALF_ST:(hf + 1) * _HALF_ST] = bu[:, :_HALF_ST]
        xs_ref[:, C_STATES + hf * _HALF_ST:C_STATES + (hf + 1) * _HALF_ST] = bu[:, _HALF_ST:]

    for cg in range(C_STATES // S5_LANES):
        re = slice(cg * S5_LANES, (cg + 1) * S5_LANES)
        im = slice(C_STATES + cg * S5_LANES, C_STATES + (cg + 1) * S5_LANES)
        ar = jnp.broadcast_to(abar_ref[0:1, re], (bsz, S5_LANES))
        ai = jnp.broadcast_to(abar_ref[0:1, im], (bsz, S5_LANES))

        def step(t, carry):
            xr, xi = carry
            r0 = pl.multiple_of(t * bsz, bsz)
            nr = ar * xr - ai * xi + xs_ref[pl.ds(r0, bsz), re]
            ni = ar * xi + ai * xr + xs_ref[pl.ds(r0, bsz), im]
            xs_ref[pl.ds(r0, bsz), re] = nr
            xs_ref[pl.ds(r0, bsz), im] = ni
            return nr, ni

        xr, xi = lax.fori_loop(0, t_steps, step, (state_ref[:, re], state_ref[:, im]), unroll=8)
        state_ref[:, re] = xr
        state_ref[:, im] = xi

    ys = []
    for hf in range(2):
        xr = xs_ref[:, hf * _HALF_ST:(hf + 1) * _HALF_ST]
        xi = xs_ref[:, C_STATES + hf * _HALF_ST:C_STATES + (hf + 1) * _HALF_ST]
        ys.append(_mm(xr, cmat_ref[hf, :_HALF_ST, :]) + _mm(xi, cmat_ref[hf, _HALF_ST:, :]))
    y = jnp.concatenate(ys, axis=1) + d_ref[...] * u
    z = jax.nn.gelu(y)
    out = z * jax.nn.sigmoid(_mm(z, gw_ref[...]) + gb_ref[...])
    y_ref[...] = out.reshape(t_steps, bsz, C_WIDTH).astype(y_ref.dtype)


def _s5(u, bmat, abar, cmat, d, glu_w, glu_b):
    bsz, seq, _ = u.shape
    t_steps = min(S5_CHUNK, seq)
    return pl.pallas_call(
        _s5_kernel,
        grid=(seq // t_steps,),
        in_specs=[
            pl.BlockSpec((bsz, t_steps, C_WIDTH), lambda i: (0, i, 0)),
            _resident((2, _HALF_CH, 2 * _HALF_ST)), _resident((1, 2 * C_STATES)), _resident((2, 2 * _HALF_ST, _HALF_CH)),
            _resident((1, C_WIDTH)), _resident((C_WIDTH, C_WIDTH)), _resident((1, C_WIDTH)),
        ],
        out_specs=pl.BlockSpec((t_steps, bsz, C_WIDTH), lambda i: (i, 0, 0)),
        out_shape=jax.ShapeDtypeStruct((seq, bsz, C_WIDTH), MXU_DTYPE),
        scratch_shapes=[pltpu.VMEM((bsz, 2 * C_STATES), jnp.float32),
                        pltpu.VMEM((t_steps * bsz, C_WIDTH), jnp.float32),
                        pltpu.VMEM((t_steps * bsz, 2 * C_STATES), jnp.float32)],
        compiler_params=_params(("arbitrary",)),
        name="s5_scan",
    )(u, bmat, abar, cmat, d.reshape(1, C_WIDTH), glu_w.astype(MXU_DTYPE), glu_b.reshape(1, C_WIDTH))


def _block_diag(per_group):
    g, r, c = per_group.shape
    eye = jnp.eye(g, dtype=per_group.dtype)
    return jnp.einsum("grc,gk->grkc", per_group, eye).reshape(g * r, g * c)


def _merge_kernel(x_ref, ya_ref, yb_ref, yc_ref, gain_ref, wg_ref, pa_ref, pb_ref, pc_ref, wo_ref, o_ref):
    x = x_ref[...]
    h = _rmsnorm(x, gain_ref[...]).astype(MXU_DTYPE)
    merged = None
    for j, (y_ref, p_ref) in enumerate(((ya_ref, pa_ref), (yb_ref, pb_ref), (yc_ref, pc_ref))):
        gate = jax.nn.sigmoid(jnp.dot(h, wg_ref[:, j * D_MODEL:(j + 1) * D_MODEL],
                                      preferred_element_type=jnp.float32))
        term = gate * jnp.dot(y_ref[...], p_ref[...], preferred_element_type=jnp.float32)
        merged = term if merged is None else merged + term
    o_ref[...] = x + _mm(merged, wo_ref[...])


def _merge(x2, ya, yb, yc, gain, wg, pa, pb, pc, wo):
    n = x2.shape[0]
    tm = min(ROW_TILE, n)

    def rows(width):
        return pl.BlockSpec((tm, width), lambda i: (i, 0))

    return pl.pallas_call(
        _merge_kernel,
        grid=(n // tm,),
        in_specs=[rows(D_MODEL), rows(A_WIDTH), rows(B_WIDTH), rows(C_WIDTH), _resident((1, D_MODEL)),
                  _resident((D_MODEL, N_BRANCH * D_MODEL)), _resident((A_WIDTH, D_MODEL)),
                  _resident((B_WIDTH, D_MODEL)), _resident((C_WIDTH, D_MODEL)), _resident((D_MODEL, D_MODEL))],
        out_specs=rows(D_MODEL),
        out_shape=jax.ShapeDtypeStruct((n, D_MODEL), jnp.float32),
        compiler_params=_params(("parallel",)),
        name="merge",
    )(x2, ya, yb, yc, gain.reshape(1, D_MODEL), wg, pa, pb, pc, wo)


def _mlp_kernel(x_ref, gain_ref, wu_ref, wd_ref, fgain_ref, o_ref, *, final_norm):
    x = x_ref[...]
    h = _rmsnorm(x, gain_ref[...]).astype(MXU_DTYPE)
    acc = x
    for c in range(D_FF // FF_CHUNK):
        cols = slice(c * FF_CHUNK, (c + 1) * FF_CHUNK)
        up = jnp.dot(h, wu_ref[:, cols], preferred_element_type=jnp.float32)
        acc = acc + _mm(jnp.square(jnp.maximum(up, 0.0)), wd_ref[cols, :])
    o_ref[...] = _rmsnorm(acc, fgain_ref[...]) if final_norm else acc


def _mlp(x2, gain, wu, wd, fgain, final_norm):
    n = x2.shape[0]
    tm = min(ROW_TILE, n)
    rows = pl.BlockSpec((tm, D_MODEL), lambda i: (i, 0))
    return pl.pallas_call(
        functools.partial(_mlp_kernel, final_norm=final_norm),
        grid=(n // tm,),
        in_specs=[rows, _resident((1, D_MODEL)), _resident((D_MODEL, D_FF)), _resident((D_FF, D_MODEL)),
                  _resident((1, D_MODEL))],
        out_specs=rows,
        out_shape=jax.ShapeDtypeStruct((n, D_MODEL), jnp.float32),
        compiler_params=_params(("parallel",)),
        name="mlp",
    )(x2, gain.reshape(1, D_MODEL), wu, wd, fgain.reshape(1, D_MODEL))


def kernel(x, positions, norm_mix, w_in, nsa_cmp_pos, nsa_cmp_w1, nsa_cmp_w2, swa_sinks, s5_a_re, s5_a_im, s5_log_dt, s5_b_re, s5_b_im, s5_c_re, s5_c_im, s5_d, s5_glu_w, s5_glu_b, w_branch_a, w_branch_b, w_branch_c, w_out, norm_mlp, w_mlp_up, w_mlp_down, norm_final):
    bsz, seq, _ = x.shape
    depth = w_in.shape[0]
    n = bsz * seq
    gate_col = sum(IN_WIDTHS[:-1])
    bf = lambda w: w.astype(MXU_DTYPE)

    tables = _rope_tables(positions)
    abr, abi, bbr, bbi = _s5_discretize(s5_a_re, s5_a_im, s5_log_dt, s5_b_re, s5_b_im)
    abr = abr.reshape(depth, 1, C_STATES)
    abi = abi.reshape(depth, 1, C_STATES)
    bbr = bbr.reshape(depth, C_GROUPS, C_GROUP_CH, C_STATE)
    bbi = bbi.reshape(depth, C_GROUPS, C_GROUP_CH, C_STATE)

    x2 = x.reshape(n, D_MODEL)
    for l in range(depth):
        qat, qbt, vat, vbt, ka, kvb, kseg, vseg, ga, uc = _in_projection(
            x2, norm_mix[l], *_pack_w_in(w_in[l]), tables, bsz, seq)

        kc, vct = _compress(kseg, vseg, nsa_cmp_pos[l].reshape(2, 1, CMP_LEN * HEAD_DIM), nsa_cmp_w1[l],
                            nsa_cmp_w2[l], bsz)
        ya = _nsa_attention(qat, ka, vat, kc, vct, ga, bsz, seq)

        yb = _swa_attention(qbt, kvb, vbt, swa_sinks[l], bsz, seq)

        cre_t = jnp.swapaxes(s5_c_re[l], 1, 2)
        cim_t = jnp.swapaxes(s5_c_im[l], 1, 2)
        halves = (slice(0, C_GROUPS // 2), slice(C_GROUPS // 2, C_GROUPS))
        bmat = jnp.stack([jnp.concatenate([_block_diag(bbr[l][g]), _block_diag(bbi[l][g])], axis=1)
                          for g in halves])
        cmat = jnp.stack([jnp.concatenate([_block_diag(cre_t[g]), -_block_diag(cim_t[g])], axis=0)
                          for g in halves])
        abar = jnp.concatenate([abr[l], abi[l]], axis=1)
        yc_tm = _s5(uc.reshape(bsz, seq, C_WIDTH), bf(bmat), abar, bf(cmat), s5_d[l], s5_glu_w[l], s5_glu_b[l])
        yc = jnp.swapaxes(yc_tm, 0, 1).reshape(n, C_WIDTH)

        x2 = _merge(x2, ya, yb, yc, norm_mix[l], bf(w_in[l][:, gate_col:]), bf(w_branch_a[l]),
                    bf(w_branch_b[l]), bf(w_branch_c[l]), bf(w_out[l]))
        x2 = _mlp(x2, norm_mlp[l], bf(w_mlp_up[l]), bf(w_mlp_down[l]), norm_final, l == depth - 1)
    return x2.reshape(bsz, seq, D_MODEL)
```

```python
import functools

import jax
import jax.numpy as jnp
from jax import lax
from jax.experimental import pallas as pl
from jax.experimental.pallas import tpu as pltpu

D_MODEL = 1024
HEAD_DIM = 64
ROPE_THETA = 10000.0
NORM_EPS = 1e-6
QBLK = 128

A_HEADS = 8
A_KV_HEADS = 2
A_GROUP = A_HEADS // A_KV_HEADS
A_WIDTH = A_HEADS * HEAD_DIM
NSA_BRANCHES = 3
CMP_LEN = 32
CMP_STRIDE = 16
CMP_HIDDEN = 2 * HEAD_DIM
SEL_LEN = 64
SEL_TOPK = 16
A_WINDOW = 512
FORCED_SCORE = 1e4

B_HEADS = 8
B_WIDTH = B_HEADS * HEAD_DIM
B_WINDOW = 128

C_WIDTH = 512
C_GROUP_CH = 16
C_GROUPS = C_WIDTH // C_GROUP_CH
C_STATE = 64
C_STATES = C_GROUPS * C_STATE
_HALF_CH = C_WIDTH // 2
_HALF_ST = C_STATES // 2

D_FF = 4 * D_MODEL
N_BRANCH = 3

IN_WIDTHS = (A_WIDTH, NSA_BRANCHES * 2 * A_KV_HEADS * HEAD_DIM, NSA_BRANCHES * A_HEADS,
             B_WIDTH, 2 * HEAD_DIM, C_WIDTH, N_BRANCH * D_MODEL)

LANES = 128
SUBLANES = 8
VMEM_LIMIT = 56 * 1024 * 1024

MXU_DTYPE = jnp.bfloat16
ROW_TILE = 512
S5_CHUNK = 64
S5_LANES = 1024
SEL_CHUNK = 256
SWA_QSUB = 2
FF_CHUNK = 1024

LOG2E = 1.4426950408889634
_NEG_INF = float("-inf")
_M_INIT = -1e30


def _f32(x):
    return x.astype(jnp.float32)


def _mm(a, b):
    return jnp.dot(a.astype(MXU_DTYPE), b.astype(MXU_DTYPE), preferred_element_type=jnp.float32)


def _mm_nt(a, b):
    return lax.dot_general(a.astype(MXU_DTYPE), b.astype(MXU_DTYPE), (((1,), (1,)), ((), ())),
                           preferred_element_type=jnp.float32)


def _rmsnorm(x, gain):
    return x * lax.rsqrt(jnp.mean(x * x, axis=-1, keepdims=True) + NORM_EPS) * gain


def _resident(shape):
    zeros = (0,) * len(shape)
    return pl.BlockSpec(shape, lambda *_: zeros, pipeline_mode=pl.Buffered(1))


def _params(sem):
    return pltpu.CompilerParams(dimension_semantics=sem, vmem_limit_bytes=VMEM_LIMIT)


def _rope_table_kernel(pos_ref, invf_ref, sign_ref, cos_ref, sin_ref):
    ang = _f32(pos_ref[...]) * invf_ref[...]
    cos_ref[...] = jnp.cos(ang)
    sin_ref[...] = jnp.sin(ang) * sign_ref[...]


def _rope_tables(positions):
    n = positions.size
    half = HEAD_DIM // 2
    inv_freq = ROPE_THETA ** (-jnp.arange(half, dtype=jnp.float32) / half)
    invf = jnp.tile(inv_freq, LANES // half)
    sign = jnp.tile(jnp.concatenate([-jnp.ones(half, jnp.float32), jnp.ones(half, jnp.float32)]),
                    LANES // HEAD_DIM)
    tm = min(1024, n)
    natural = pl.pallas_call(
        _rope_table_kernel,
        grid=(n // tm,),
        in_specs=[pl.BlockSpec((tm, 1), lambda i: (i, 0)), _resident((1, LANES)), _resident((1, LANES))],
        out_specs=[pl.BlockSpec((tm, LANES), lambda i: (i, 0))] * 2,
        out_shape=[jax.ShapeDtypeStruct((n, LANES), jnp.float32)] * 2,
        compiler_params=_params(("parallel",)),
        name="rope_tables",
    )(positions.reshape(n, 1), invf.reshape(1, LANES), sign.reshape(1, LANES))
    transposed = pl.pallas_call(
        _rope_table_kernel,
        grid=(n // tm,),
        in_specs=[pl.BlockSpec((1, tm), lambda i: (0, i)), _resident((HEAD_DIM, 1)), _resident((HEAD_DIM, 1))],
        out_specs=[pl.BlockSpec((HEAD_DIM, tm), lambda i: (0, i))] * 2,
        out_shape=[jax.ShapeDtypeStruct((HEAD_DIM, n), jnp.float32)] * 2,
        compiler_params=_params(("parallel",)),
        name="rope_tables_t",
    )(positions.reshape(1, n), invf[:HEAD_DIM].reshape(HEAD_DIM, 1), sign[:HEAD_DIM].reshape(HEAD_DIM, 1))
    return natural + transposed


def _swap_halves(x):
    lane = lax.broadcasted_iota(jnp.int32, x.shape, 1)
    first = (lane & (HEAD_DIM - 1)) < HEAD_DIM // 2
    return jnp.where(first, pltpu.roll(x, LANES - HEAD_DIM // 2, 1), pltpu.roll(x, HEAD_DIM // 2, 1))


def _proj_kernel(x_ref, gain_ref, wn_ref, wt_ref, cos_ref, sin_ref, cost_ref, sint_ref,
                 qat_ref, qbt_ref, vat_ref, vbt_ref, ka_ref, kvb_ref, kseg_ref, vseg_ref, ga_ref, uc_ref,
                 seg_ref):
    h = _rmsnorm(x_ref[...], gain_ref[...]).astype(MXU_DTYPE)
    nat = jnp.dot(h, wn_ref[...], preferred_element_type=jnp.float32)
    tr = _mm_nt(wt_ref[...], h)

    cos = cos_ref[...]
    sin = sin_ref[...]

    def roped(j):
        xj = nat[:, j * LANES:(j + 1) * LANES]
        return xj, xj * cos + _swap_halves(xj) * sin

    for j in range(1, NSA_BRANCHES):
        ka_ref[:, (j - 1) * LANES:j * LANES] = roped(j)[1].astype(ka_ref.dtype)
    raw, rot = roped(3)
    lane = lax.broadcasted_iota(jnp.int32, raw.shape, 1)
    kvb_ref[...] = jnp.where(lane < HEAD_DIM, rot, raw).astype(kvb_ref.dtype)
    n_seg_tile = nat.shape[0] // CMP_STRIDE
    for j, (src, dst_ref) in enumerate(((roped(0)[1], kseg_ref), (nat[:, 4 * LANES:5 * LANES], vseg_ref))):
        seg_ref[j] = src
        for tok in range(CMP_STRIDE):
            dst_ref[:, tok * LANES:(tok + 1) * LANES] = (
                seg_ref[j, pl.ds(tok, n_seg_tile, stride=CMP_STRIDE), :].astype(dst_ref.dtype))
    ga_ref[...] = jax.nn.sigmoid(nat[:, 5 * LANES:6 * LANES])
    uc_ref[...] = nat[:, 6 * LANES:]

    cos_t = cost_ref[...]
    sin_t = sint_ref[...]
    half = HEAD_DIM // 2

    def roped_t(r0):
        xh = tr[r0:r0 + HEAD_DIM]
        return xh * cos_t + jnp.concatenate([xh[half:], xh[:half]], axis=0) * sin_t

    for hd in range(A_HEADS):
        qat_ref[hd * HEAD_DIM:(hd + 1) * HEAD_DIM, :] = roped_t(hd * HEAD_DIM).astype(qat_ref.dtype)
    for hd in range(B_HEADS):
        qbt_ref[hd * HEAD_DIM:(hd + 1) * HEAD_DIM, :] = roped_t(A_WIDTH + hd * HEAD_DIM).astype(qbt_ref.dtype)
    v0 = A_WIDTH + B_WIDTH
    vat_ref[...] = tr[v0:v0 + 2 * LANES].astype(vat_ref.dtype)
    vbt_ref[...] = tr[v0 + 2 * LANES:].astype(vbt_ref.dtype)


_NAT_COLS = 10 * LANES
_TR_ROWS = A_WIDTH + B_WIDTH + 2 * LANES + HEAD_DIM


def _pack_w_in(w_in):
    o = [0]
    for w in IN_WIDTHS:
        o.append(o[-1] + w)
    scale = HEAD_DIM ** -0.5 * LOG2E
    q_a = w_in[:, o[0]:o[1]] * scale
    kv_cols = [w_in[:, o[1] + j * LANES:o[1] + (j + 1) * LANES] for j in range(2 * NSA_BRANCHES)]
    k_a = jnp.concatenate(kv_cols[0::2], axis=1)
    v_a = jnp.concatenate(kv_cols[1::2], axis=1)
    g_a = jnp.pad(w_in[:, o[2]:o[3]], ((0, 0), (0, LANES - NSA_BRANCHES * A_HEADS)))
    q_b = w_in[:, o[3]:o[4]] * scale
    kv_b = w_in[:, o[4]:o[5]]
    u_c = w_in[:, o[5]:o[6]]
    w_nat = jnp.concatenate([k_a, kv_b, v_a[:, :LANES], g_a, u_c], axis=1)
    w_tr = jnp.concatenate([q_a, q_b, v_a[:, LANES:], kv_b[:, HEAD_DIM:]], axis=1).T
    return w_nat.astype(MXU_DTYPE), w_tr.astype(MXU_DTYPE)


def _in_projection(x2, gain, w_nat, w_tr, tables, bsz, seq):
    n = x2.shape[0]
    tm = min(ROW_TILE, seq)
    tiles = seq // tm

    def rows(width):
        return pl.BlockSpec((tm, width), lambda i: (i, 0))

    def lanes(height):
        return pl.BlockSpec((None, height, tm), lambda i: (i // tiles, 0, i % tiles))

    table_t = pl.BlockSpec((HEAD_DIM, tm), lambda i: (0, i))
    t_heights = (A_WIDTH, B_WIDTH, 2 * LANES, HEAD_DIM)
    widths = (2 * LANES, LANES, LANES, C_WIDTH)
    dtypes = (MXU_DTYPE, MXU_DTYPE, jnp.float32, jnp.float32)
    seg_rows = tm // CMP_STRIDE
    seg_spec = pl.BlockSpec((seg_rows, CMP_STRIDE * LANES), lambda i: (i, 0))
    seg_shape = jax.ShapeDtypeStruct((n // CMP_STRIDE, CMP_STRIDE * LANES), MXU_DTYPE)
    return pl.pallas_call(
        _proj_kernel,
        grid=(n // tm,),
        in_specs=[rows(D_MODEL), _resident((1, D_MODEL)), _resident((D_MODEL, _NAT_COLS)),
                  _resident((_TR_ROWS, D_MODEL)), rows(LANES), rows(LANES), table_t, table_t],
        out_specs=[lanes(hh) for hh in t_heights] + [rows(w) for w in widths[:2]] + [seg_spec, seg_spec]
        + [rows(w) for w in widths[2:]],
        out_shape=[jax.ShapeDtypeStruct((bsz, hh, seq), MXU_DTYPE) for hh in t_heights]
        + [jax.ShapeDtypeStruct((n, w), dt) for w, dt in zip(widths[:2], dtypes[:2])] + [seg_shape, seg_shape]
        + [jax.ShapeDtypeStruct((n, w), dt) for w, dt in zip(widths[2:], dtypes[2:])],
        scratch_shapes=[pltpu.VMEM((2, tm, LANES), jnp.float32)],
        compiler_params=_params(("parallel",)),
        name="in_projection",
    )(x2, gain.reshape(1, D_MODEL), w_nat, w_tr, *tables)


def _compress_kernel(kseg_ref, vseg_ref, pos_ref, w1_ref, w1x_ref, w2_ref, kc_ref, vct_ref):
    nseg = kseg_ref.shape[0]
    both = []
    for j, seg_ref in enumerate((kseg_ref, vseg_ref)):
        seg = seg_ref[...]
        pos_term = _mm(pos_ref[j], w1_ref[j])
        heads = []
        for h in range(A_KV_HEADS):
            hidden = (_mm(seg, w1x_ref[j, 0, h]) + pltpu.roll(_mm(seg, w1x_ref[j, 1, h]), nseg - 1, 0)
                      + pos_term)
            heads.append(_mm(jax.nn.gelu(hidden), w2_ref[j]))
        both.append(jnp.concatenate(heads, axis=1))
    kc_ref[...] = both[0].astype(kc_ref.dtype)
    vct_ref[...] = both[1].T.astype(vct_ref.dtype)


def _expand_w1(w1):
    w = w1.reshape(2, 2, CMP_STRIDE, HEAD_DIM, CMP_HIDDEN)
    eye = jnp.eye(A_KV_HEADS, dtype=w1.dtype)
    return jnp.einsum("jatdc,hg->jahtgdc", w, eye).reshape(2, 2, A_KV_HEADS, CMP_STRIDE * LANES, CMP_HIDDEN)


def _compress(kseg, vseg, pos_flat, w1, w2, bsz):
    nseg = kseg.shape[0] // bsz
    segw = kseg.shape[1]
    seg_spec = pl.BlockSpec((nseg, segw), lambda b: (b, 0))
    return pl.pallas_call(
        _compress_kernel,
        grid=(bsz,),
        in_specs=[seg_spec, seg_spec, _resident((2, 1, CMP_LEN * HEAD_DIM)),
                  _resident((2, CMP_LEN * HEAD_DIM, CMP_HIDDEN)),
                  _resident((2, 2, A_KV_HEADS, segw, CMP_HIDDEN)), _resident((2, CMP_HIDDEN, HEAD_DIM))],
        out_specs=[pl.BlockSpec((None, nseg, LANES), lambda b: (b, 0, 0)),
                   pl.BlockSpec((None, LANES, nseg), lambda b: (b, 0, 0))],
        out_shape=[jax.ShapeDtypeStruct((bsz, nseg, LANES), MXU_DTYPE),
                   jax.ShapeDtypeStruct((bsz, LANES, nseg), MXU_DTYPE)],
        compiler_params=_params(("parallel",)),
        name="nsa_compress",
    )(kseg, vseg, pos_flat.astype(MXU_DTYPE), w1.astype(MXU_DTYPE), _expand_w1(w1).astype(MXU_DTYPE),
      w2.astype(MXU_DTYPE))


def _split3(x):
    hi = x.astype(MXU_DTYPE)
    r1 = x - _f32(hi)
    mid = r1.astype(MXU_DTYPE)
    lo = (r1 - _f32(mid)).astype(MXU_DTYPE)
    return hi, mid, lo


def _col_max(s):
    return jnp.max(s, axis=0, keepdims=True)


def _tile_lanes(x, n):
    return jnp.concatenate([x] * n, axis=1)


def _pad_head_rows(x, h, fill):
    other = jnp.full(x.shape, fill, x.dtype)
    return jnp.concatenate([x, other] if h == 0 else [other, x], axis=0)


def _values_with_ones(vt):
    return jnp.concatenate([vt, jnp.ones((2 * SUBLANES, vt.shape[1]), vt.dtype)], axis=0)


def _store_transposed_pairs(o_ref, heads_t, row0=0):
    for p in range(len(heads_t) // 2):
        pair = jnp.concatenate([heads_t[2 * p], heads_t[2 * p + 1]], axis=0)
        o_ref[row0:row0 + QBLK, p * LANES:(p + 1) * LANES] = pair.T.astype(o_ref.dtype)


def _nsa_kernel(qt_ref, k_ref, vt_ref, kc_ref, vct_ref, g_ref, c2s_ref, o_ref,
                selb_ref, sca_ref, scb_ref, cmp_ref, win_ref, *, seq):
    i = pl.program_id(1)
    q0 = i * QBLK
    n_cmp = kc_ref.shape[0]
    n_sel = seq // SEL_LEN
    kc_sz = min(SEL_CHUNK, seq)
    span = min(A_WINDOW + QBLK, seq)
    blocks_per_chunk = kc_sz // SEL_LEN

    gates_t = g_ref[...].T
    qp_lane = q0 + lax.broadcasted_iota(jnp.int32, (1, QBLK), 1)

    cmp_end = lax.broadcasted_iota(jnp.int32, (n_cmp, 1), 0) * CMP_STRIDE + (CMP_LEN - 1)
    bias_c = _tile_lanes(jnp.where(cmp_end <= qp_lane, 0.0, _NEG_INF), A_GROUP)

    w0 = pl.multiple_of(jnp.maximum(q0 + QBLK - span, 0), QBLK)
    diff = qp_lane - (w0 + lax.broadcasted_iota(jnp.int32, (span, 1), 0))
    bias_w = _tile_lanes(jnp.where(diff >= 0, jnp.where(diff < A_WINDOW, 0.0, _NEG_INF), _NEG_INF), A_GROUP)

    w_qs = []
    for h in range(A_KV_HEADS):
        q_t = jnp.concatenate([qt_ref[(h * A_GROUP + g) * HEAD_DIM:(h * A_GROUP + g + 1) * HEAD_DIM, :]
                               for g in range(A_GROUP)], axis=1)
        w_qs.append(_pad_head_rows(q_t, h, 0.0))

    n_chunks = (q0 + QBLK + kc_sz - 1) // kc_sz
    last = n_chunks - 1

    def issue_scores(c, dst_ref):
        k0 = pl.multiple_of(jnp.minimum(c, last) * kc_sz, kc_sz)
        keys = k_ref[pl.ds(k0, kc_sz), :LANES]
        for h in range(A_KV_HEADS):
            dst_ref[h] = _mm(keys, w_qs[h])

    for h in range(A_KV_HEADS):
        cmp_ref[h] = _mm(kc_ref[...], w_qs[h])
    for h in range(A_KV_HEADS):
        win_ref[h] = _mm(k_ref[pl.ds(w0, span), LANES:], w_qs[h])
    issue_scores(0, sca_ref)

    o_cs, o_ws = [], []
    for h in range(A_KV_HEADS):
        s = cmp_ref[h] + bias_c
        m = _col_max(s)
        m = jnp.where(m == _NEG_INF, 0.0, m)
        e = jnp.exp2(s - m)
        p = e * (1.0 / jnp.maximum(jnp.sum(e, axis=0, keepdims=True), 1e-30))
        o_cs.append(_mm(vct_ref[...], p)[h * HEAD_DIM:(h + 1) * HEAD_DIM])

        p_sum = p[:, 0:QBLK]
        for g in range(1, A_GROUP):
            p_sum = p_sum + p[:, g * QBLK:(g + 1) * QBLK]
        imp_t = sum(_mm(c2s_ref[...], part) for part in _split3(p_sum))
        blk = lax.broadcasted_iota(jnp.int32, (n_sel, QBLK), 0)
        cur = jnp.right_shift(qp_lane, 6)
        forced = (blk == 0) | (blk == cur) | (blk == cur - 1)
        score = jnp.where(forced, FORCED_SCORE, imp_t)
        score = jnp.where(blk <= cur, score, _NEG_INF)
        n_grp = n_sel // SUBLANES
        sub = lax.broadcasted_iota(jnp.int32, (SUBLANES, QBLK), 0)
        score_grp = [score[SUBLANES * v:SUBLANES * (v + 1), :] for v in range(n_grp)]
        rank_grp = [jnp.zeros((SUBLANES, QBLK), jnp.float32) for _ in range(n_grp)]
        for j in range(n_sel):
            row = jnp.broadcast_to(score[j:j + 1, :], (SUBLANES, QBLK))
            vj, rj = divmod(j, SUBLANES)
            for v in range(n_grp):
                if v > vj:
                    inc = jnp.where(row >= score_grp[v], 1.0, 0.0)
                elif v < vj:
                    inc = jnp.where(row > score_grp[v], 1.0, 0.0)
                else:
                    inc = jnp.where(sub > rj, jnp.where(row >= score_grp[v], 1.0, 0.0),
                                    jnp.where(row > score_grp[v], 1.0, 0.0))
                rank_grp[v] = rank_grp[v] + inc
        rank = jnp.concatenate(rank_grp, axis=0)
        picked = jnp.where(rank < float(min(SEL_TOPK, n_sel)), 0.0, _NEG_INF)
        selb_ref[h] = jnp.where(blk <= cur, picked, _NEG_INF)

    for h in range(A_KV_HEADS):
        sw = win_ref[h] + bias_w
        pw = jnp.exp2(sw - _col_max(sw))
        acc_w = _mm(_values_with_ones(vt_ref[LANES + h * HEAD_DIM:LANES + (h + 1) * HEAD_DIM, pl.ds(w0, span)]), pw)
        o_ws.append(acc_w[:HEAD_DIM] / acc_w[HEAD_DIM:HEAD_DIM + 1])

    def consume(c, src_ref, states):
        cc = jnp.minimum(c, last)
        k0 = pl.multiple_of(cc * kc_sz, kc_sz)
        causal = (c * kc_sz + lax.broadcasted_iota(jnp.int32, (kc_sz, 1), 0)) <= qp_lane
        out = []
        for h in range(A_KV_HEADS):
            m_run, acc = states[h]
            slabs = [jnp.broadcast_to(selb_ref[h, pl.ds(cc * blocks_per_chunk + jj, 1), :], (SEL_LEN, QBLK))
                     for jj in range(blocks_per_chunk)]
            bias = jnp.where(causal, jnp.concatenate(slabs, axis=0), _NEG_INF)
            sc = src_ref[h] + _tile_lanes(bias, A_GROUP)
            m_new = jnp.maximum(m_run, _col_max(sc))
            pr = jnp.exp2(sc - m_new)
            vt = _values_with_ones(vt_ref[h * HEAD_DIM:(h + 1) * HEAD_DIM, pl.ds(k0, kc_sz)])
            out.append((m_new, jnp.exp2(m_run - m_new) * acc + _mm(vt, pr)))
        return tuple(out)

    def sel_step(t, states):
        issue_scores(2 * t + 1, scb_ref)
        states = consume(2 * t, sca_ref, states)
        issue_scores(2 * t + 2, sca_ref)
        return consume(2 * t + 1, scb_ref, states)

    init = tuple((jnp.full((1, A_GROUP * QBLK), _M_INIT, jnp.float32),
                  jnp.zeros((HEAD_DIM + 2 * SUBLANES, A_GROUP * QBLK), jnp.float32)) for _ in range(A_KV_HEADS))
    sel_out = lax.fori_loop(0, (n_chunks + 1) // 2, sel_step, init)

    heads_t = []
    for h in range(A_KV_HEADS):
        acc_s = sel_out[h][1]
        o_s = acc_s[:HEAD_DIM] / jnp.maximum(acc_s[HEAD_DIM:HEAD_DIM + 1], 1e-30)
        for g in range(A_GROUP):
            hd = h * A_GROUP + g
            cols = slice(g * QBLK, (g + 1) * QBLK)
            heads_t.append(gates_t[hd:hd + 1, :] * o_cs[h][:, cols]
                           + gates_t[A_HEADS + hd:A_HEADS + hd + 1, :] * o_s[:, cols]
                           + gates_t[2 * A_HEADS + hd:2 * A_HEADS + hd + 1, :] * o_ws[h][:, cols])
    _store_transposed_pairs(o_ref, heads_t)


def _nsa_attention(qat, ka, vat, kc, vct, ga, bsz, seq):
    nblk = seq // QBLK
    n_seg = seq // CMP_STRIDE
    n_sel = seq // SEL_LEN
    assert seq >= A_WINDOW + QBLK and seq % SEL_CHUNK == 0
    c_start = jnp.arange(n_seg) * CMP_STRIDE
    s_start = jnp.arange(n_sel) * SEL_LEN
    overlap = (jnp.minimum(c_start[None, :] + CMP_LEN, s_start[:, None] + SEL_LEN)
               - jnp.maximum(c_start[None, :], s_start[:, None]))
    c2s = (jnp.clip(overlap, 0, None).astype(jnp.float32) / CMP_LEN).astype(MXU_DTYPE)
    return pl.pallas_call(
        functools.partial(_nsa_kernel, seq=seq),
        grid=(bsz, nblk),
        in_specs=[
            pl.BlockSpec((None, A_WIDTH, QBLK), lambda b, i: (b, 0, i)),
            pl.BlockSpec((seq, 2 * LANES), lambda b, i: (b, 0)),
            pl.BlockSpec((None, 2 * LANES, seq), lambda b, i: (b, 0, 0)),
            pl.BlockSpec((None, n_seg, LANES), lambda b, i: (b, 0, 0)),
            pl.BlockSpec((None, LANES, n_seg), lambda b, i: (b, 0, 0)),
            pl.BlockSpec((QBLK, LANES), lambda b, i: (b * nblk + i, 0)),
            _resident((n_sel, n_seg)),
        ],
        out_specs=pl.BlockSpec((QBLK, A_WIDTH), lambda b, i: (b * nblk + i, 0)),
        out_shape=jax.ShapeDtypeStruct((bsz * seq, A_WIDTH), MXU_DTYPE),
        scratch_shapes=[pltpu.VMEM((A_KV_HEADS, n_sel, QBLK), jnp.float32)]
        + [pltpu.VMEM((A_KV_HEADS, min(SEL_CHUNK, seq), A_GROUP * QBLK), jnp.float32)] * 2
        + [pltpu.VMEM((A_KV_HEADS, n_seg, A_GROUP * QBLK), jnp.float32),
           pltpu.VMEM((A_KV_HEADS, min(A_WINDOW + QBLK, seq), A_GROUP * QBLK), jnp.float32)],
        compiler_params=_params(("parallel", "arbitrary")),
        name="nsa_attention",
    )(qat, ka, vat, kc, vct, ga, c2s)


def _swa_kernel(sink_ref, qt_ref, kv_ref, vt_ref, o_ref, sc_ref, *, seq):
    i = pl.program_id(1)
    span = min(B_WINDOW + QBLK, seq)
    sink = jnp.concatenate([jnp.full((1, QBLK), sink_ref[hd] * LOG2E, jnp.float32) for hd in range(B_HEADS)],
                           axis=1)
    starts = []
    for sb in range(SWA_QSUB):
        q0 = (i * SWA_QSUB + sb) * QBLK
        w0 = pl.multiple_of(jnp.maximum(q0 + QBLK - span, 0), QBLK)
        starts.append((q0, w0))
        q_t = jnp.concatenate([qt_ref[hd * HEAD_DIM:(hd + 1) * HEAD_DIM, sb * QBLK:(sb + 1) * QBLK]
                               for hd in range(B_HEADS)], axis=1)
        sc_ref[sb] = _mm(kv_ref[pl.ds(w0, span), :], _pad_head_rows(q_t, 0, 0.0))
    for sb, (q0, w0) in enumerate(starts):
        qp_lane = q0 + lax.broadcasted_iota(jnp.int32, (1, QBLK), 1)
        diff = qp_lane - (w0 + lax.broadcasted_iota(jnp.int32, (span, 1), 0))
        bias = _tile_lanes(jnp.where(diff >= 0, jnp.where(diff < B_WINDOW, 0.0, _NEG_INF), _NEG_INF), B_HEADS)
        s = sc_ref[sb] + bias
        m = jnp.maximum(_col_max(s), sink)
        acc = _mm(_values_with_ones(vt_ref[:, pl.ds(w0, span)]), jnp.exp2(s - m))
        o_t = acc[:HEAD_DIM] / (acc[HEAD_DIM:HEAD_DIM + 1] + jnp.exp2(sink - m))
        _store_transposed_pairs(o_ref, [o_t[:, hd * QBLK:(hd + 1) * QBLK] for hd in range(B_HEADS)], sb * QBLK)


def _swa_attention(qbt, kvb, vbt, sinks, bsz, seq):
    rows = SWA_QSUB * QBLK
    nstep = seq // rows
    span = min(B_WINDOW + QBLK, seq)
    assert seq >= B_WINDOW + QBLK and seq % rows == 0
    return pl.pallas_call(
        functools.partial(_swa_kernel, seq=seq),
        grid=(bsz, nstep),
        in_specs=[
            pl.BlockSpec(memory_space=pltpu.SMEM),
            pl.BlockSpec((None, B_WIDTH, rows), lambda b, i: (b, 0, i)),
            pl.BlockSpec((seq, LANES), lambda b, i: (b, 0)),
            pl.BlockSpec((None, HEAD_DIM, seq), lambda b, i: (b, 0, 0)),
        ],
        out_specs=pl.BlockSpec((rows, B_WIDTH), lambda b, i: (b * nstep + i, 0)),
        out_shape=jax.ShapeDtypeStruct((bsz * seq, B_WIDTH), MXU_DTYPE),
        scratch_shapes=[pltpu.VMEM((SWA_QSUB, span, B_HEADS * QBLK), jnp.float32)],
        compiler_params=_params(("parallel", "arbitrary")),
        name="swa_attention",
    )(sinks, qbt, kvb, vbt)


def _s5_discretize_kernel(are_ref, aim_ref, logdt_ref, bre_ref, bim_ref,
                          abr_ref, abi_ref, bbr_ref, bbi_ref):
    ar = are_ref[...]
    ai = aim_ref[...]
    dt = jnp.exp(logdt_ref[...])
    mag = jnp.exp(dt * ar)
    abr = mag * jnp.cos(dt * ai)
    abi = mag * jnp.sin(dt * ai)
    den = ar * ar + ai * ai
    nr = abr - 1.0
    coef_r = (nr * ar + abi * ai) / den
    coef_i = (abi * ar - nr * ai) / den
    abr_ref[...] = abr
    abi_ref[...] = abi
    br = bre_ref[...]
    bi = bim_ref[...]
    bbr_ref[...] = coef_r[:, None, :] * br - coef_i[:, None, :] * bi
    bbi_ref[...] = coef_r[:, None, :] * bi + coef_i[:, None, :] * br


def _s5_discretize(a_re, a_im, log_dt, b_re, b_im):
    lg = a_re.shape[0] * a_re.shape[1]
    a2 = lambda a: a.reshape(lg, C_STATE)
    b3 = lambda b: jnp.swapaxes(b.reshape(lg, C_STATE, C_GROUP_CH), 1, 2)
    sa = jax.ShapeDtypeStruct((lg, C_STATE), jnp.float32)
    sb = jax.ShapeDtypeStruct((lg, C_GROUP_CH, C_STATE), jnp.float32)
    return pl.pallas_call(_s5_discretize_kernel, out_shape=[sa, sa, sb, sb], name="s5_discretize")(
        a2(a_re), a2(a_im), log_dt.reshape(lg, 1), b3(b_re), b3(b_im))


def _s5_kernel(u_ref, bmat_ref, abar_ref, cmat_ref, d_ref, gw_ref, gb_ref, y_ref, state_ref, ut_ref, xs_ref):
    bsz, t_steps, _ = u_ref.shape

    @pl.when(pl.program_id(0) == 0)
    def _():
        state_ref[...] = jnp.zeros_like(state_ref)

    def gather(t, carry):
        ut_ref[pl.ds(pl.multiple_of(t * bsz, bsz), bsz), :] = u_ref[:, t, :]
        return carry

    lax.fori_loop(0, t_steps, gather, 0, unroll=8)
    u = ut_ref[...]
    for hf in range(2):
        bu = _mm(u[:, hf * _HALF_CH:(hf + 1) * _HALF_CH], bmat_ref[hf])
        xs_ref[:, hf * _HALF_ST:(hf + 1) * _HALF_ST] = bu[:, :_HALF_ST]
        xs_ref[:, C_STATES + hf * _HALF_ST:C_STATES + (hf + 1) * _HALF_ST] = bu[:, _HALF_ST:]

    for cg in range(C_STATES // S5_LANES):
        re = slice(cg * S5_LANES, (cg + 1) * S5_LANES)
        im = slice(C_STATES + cg * S5_LANES, C_STATES + (cg + 1) * S5_LANES)
        ar = jnp.broadcast_to(abar_ref[0:1, re], (bsz, S5_LANES))
        ai = jnp.broadcast_to(abar_ref[0:1, im], (bsz, S5_LANES))

        def step(t, carry):
            xr, xi = carry
            r0 = pl.multiple_of(t * bsz, bsz)
            nr = ar * xr - ai * xi + xs_ref[pl.ds(r0, bsz), re]
            ni = ar * xi + ai * xr + xs_ref[pl.ds(r0, bsz), im]
            xs_ref[pl.ds(r0, bsz), re] = nr
            xs_ref[pl.ds(r0, bsz), im] = ni
            return nr, ni

        xr, xi = lax.fori_loop(0, t_steps, step, (state_ref[:, re], state_ref[:, im]), unroll=8)
        state_ref[:, re] = xr
        state_ref[:, im] = xi

    ys = []
    for hf in range(2):
        xr = xs_ref[:, hf * _HALF_ST:(hf + 1) * _HALF_ST]
        xi = xs_ref[:, C_STATES + hf * _HALF_ST:C_STATES + (hf + 1) * _HALF_ST]
        ys.append(_mm(xr, cmat_ref[hf, :_HALF_ST, :]) + _mm(xi, cmat_ref[hf, _HALF_ST:, :]))
    y = jnp.concatenate(ys, axis=1) + d_ref[...] * u
    z = jax.nn.gelu(y)
    out = z * jax.nn.sigmoid(_mm(z, gw_ref[...]) + gb_ref[...])
    y_ref[...] = out.reshape(t_steps, bsz, C_WIDTH).astype(y_ref.dtype)


def _s5(u, bmat, abar, cmat, d, glu_w, glu_b):
    bsz, seq, _ = u.shape
    t_steps = min(S5_CHUNK, seq)
    return pl.pallas_call(
        _s5_kernel,
        grid=(seq // t_steps,),
        in_specs=[
            pl.BlockSpec((bsz, t_steps, C_WIDTH), lambda i: (0, i, 0)),
            _resident((2, _HALF_CH, 2 * _HALF_ST)), _resident((1, 2 * C_STATES)), _resident((2, 2 * _HALF_ST, _HALF_CH)),
            _resident((1, C_WIDTH)), _resident((C_WIDTH, C_WIDTH)), _resident((1, C_WIDTH)),
        ],
        out_specs=pl.BlockSpec((t_steps, bsz, C_WIDTH), lambda i: (i, 0, 0)),
        out_shape=jax.ShapeDtypeStruct((seq, bsz, C_WIDTH), MXU_DTYPE),
        scratch_shapes=[pltpu.VMEM((bsz, 2 * C_STATES), jnp.float32),
                        pltpu.VMEM((t_steps * bsz, C_WIDTH), jnp.float32),
                        pltpu.VMEM((t_steps * bsz, 2 * C_STATES), jnp.float32)],
        compiler_params=_params(("arbitrary",)),
        name="s5_scan",
    )(u, bmat, abar, cmat, d.reshape(1, C_WIDTH), glu_w.astype(MXU_DTYPE), glu_b.reshape(1, C_WIDTH))


def _block_diag(per_group):
    g, r, c = per_group.shape
    eye = jnp.eye(g, dtype=per_group.dtype)
    return jnp.einsum("grc,gk->grkc", per_group, eye).reshape(g * r, g * c)


def _merge_kernel(x_ref, ya_ref, yb_ref, yc_ref, gain_ref, wg_ref, pa_ref, pb_ref, pc_ref, wo_ref, o_ref):
    x = x_ref[...]
    h = _rmsnorm(x, gain_ref[...]).astype(MXU_DTYPE)
    merged = None
    for j, (y_ref, p_ref) in enumerate(((ya_ref, pa_ref), (yb_ref, pb_ref), (yc_ref, pc_ref))):
        gate = jax.nn.sigmoid(jnp.dot(h, wg_ref[:, j * D_MODEL:(j + 1) * D_MODEL],
                                      preferred_element_type=jnp.float32))
        term = gate * jnp.dot(y_ref[...], p_ref[...], preferred_element_type=jnp.float32)
        merged = term if merged is None else merged + term
    o_ref[...] = x + _mm(merged, wo_ref[...])


def _merge(x2, ya, yb, yc, gain, wg, pa, pb, pc, wo):
    n = x2.shape[0]
    tm = min(ROW_TILE, n)

    def rows(width):
        return pl.BlockSpec((tm, width), lambda i: (i, 0))

    return pl.pallas_call(
        _merge_kernel,
        grid=(n // tm,),
        in_specs=[rows(D_MODEL), rows(A_WIDTH), rows(B_WIDTH), rows(C_WIDTH), _resident((1, D_MODEL)),
                  _resident((D_MODEL, N_BRANCH * D_MODEL)), _resident((A_WIDTH, D_MODEL)),
                  _resident((B_WIDTH, D_MODEL)), _resident((C_WIDTH, D_MODEL)), _resident((D_MODEL, D_MODEL))],
        out_specs=rows(D_MODEL),
        out_shape=jax.ShapeDtypeStruct((n, D_MODEL), jnp.float32),
        compiler_params=_params(("parallel",)),
        name="merge",
    )(x2, ya, yb, yc, gain.reshape(1, D_MODEL), wg, pa, pb, pc, wo)


def _mlp_kernel(x_ref, gain_ref, wu_ref, wd_ref, fgain_ref, o_ref, *, final_norm):
    x = x_ref[...]
    h = _rmsnorm(x, gain_ref[...]).astype(MXU_DTYPE)
    acc = x
    for c in range(D_FF // FF_CHUNK):
        cols = slice(c * FF_CHUNK, (c + 1) * FF_CHUNK)
        up = jnp.dot(h, wu_ref[:, cols], preferred_element_type=jnp.float32)
        acc = acc + _mm(jnp.square(jnp.maximum(up, 0.0)), wd_ref[cols, :])
    o_ref[...] = _rmsnorm(acc, fgain_ref[...]) if final_norm else acc


def _mlp(x2, gain, wu, wd, fgain, final_norm):
    n = x2.shape[0]
    tm = min(ROW_TILE, n)
    rows = pl.BlockSpec((tm, D_MODEL), lambda i: (i, 0))
    return pl.pallas_call(
        functools.partial(_mlp_kernel, final_norm=final_norm),
        grid=(n // tm,),
        in_specs=[rows, _resident((1, D_MODEL)), _resident((D_MODEL, D_FF)), _resident((D_FF, D_MODEL)),
                  _resident((1, D_MODEL))],
        out_specs=rows,
        out_shape=jax.ShapeDtypeStruct((n, D_MODEL), jnp.float32),
        compiler_params=_params(("parallel",)),
        name="mlp",
    )(x2, gain.reshape(1, D_MODEL), wu, wd, fgain.reshape(1, D_MODEL))


def kernel(x, positions, norm_mix, w_in, nsa_cmp_pos, nsa_cmp_w1, nsa_cmp_w2, swa_sinks, s5_a_re, s5_a_im, s5_log_dt, s5_b_re, s5_b_im, s5_c_re, s5_c_im, s5_d, s5_glu_w, s5_glu_b, w_branch_a, w_branch_b, w_branch_c, w_out, norm_mlp, w_mlp_up, w_mlp_down, norm_final):
    bsz, seq, _ = x.shape
    depth = w_in.shape[0]
    n = bsz * seq
    gate_col = sum(IN_WIDTHS[:-1])
    bf = lambda w: w.astype(MXU_DTYPE)

    tables = _rope_tables(positions)
    abr, abi, bbr, bbi = _s5_discretize(s5_a_re, s5_a_im, s5_log_dt, s5_b_re, s5_b_im)
    abr = abr.reshape(depth, 1, C_STATES)
    abi = abi.reshape(depth, 1, C_STATES)
    bbr = bbr.reshape(depth, C_GROUPS, C_GROUP_CH, C_STATE)
    bbi = bbi.reshape(depth, C_GROUPS, C_GROUP_CH, C_STATE)

    x2 = x.reshape(n, D_MODEL)
    for l in range(depth):
        qat, qbt, vat, vbt, ka, kvb, kseg, vseg, ga, uc = _in_projection(
            x2, norm_mix[l], *_pack_w_in(w_in[l]), tables, bsz, seq)

        kc, vct = _compress(kseg, vseg, nsa_cmp_pos[l].reshape(2, 1, CMP_LEN * HEAD_DIM), nsa_cmp_w1[l],
                            nsa_cmp_w2[l], bsz)
        ya = _nsa_attention(qat, ka, vat, kc, vct, ga, bsz, seq)

        yb = _swa_attention(qbt, kvb, vbt, swa_sinks[l], bsz, seq)

        cre_t = jnp.swapaxes(s5_c_re[l], 1, 2)
        cim_t = jnp.swapaxes(s5_c_im[l], 1, 2)
        halves = (slice(0, C_GROUPS // 2), slice(C_GROUPS // 2, C_GROUPS))
        bmat = jnp.stack([jnp.concatenate([_block_diag(bbr[l][g]), _block_diag(bbi[l][g])], axis=1)
                          for g in halves])
        cmat = jnp.stack([jnp.concatenate([_block_diag(cre_t[g]), -_block_diag(cim_t[g])], axis=0)
                          for g in halves])
        abar = jnp.concatenate([abr[l], abi[l]], axis=1)
        yc_tm = _s5(uc.reshape(bsz, seq, C_WIDTH), bf(bmat), abar, bf(cmat), s5_d[l], s5_glu_w[l], s5_glu_b[l])
        yc = jnp.swapaxes(yc_tm, 0, 1).reshape(n, C_WIDTH)

        x2 = _merge(x2, ya, yb, yc, norm_mix[l], bf(w_in[l][:, gate_col:]), bf(w_branch_a[l]),
                    bf(w_branch_b[l]), bf(w_branch_c[l]), bf(w_out[l]))
        x2 = _mlp(x2, norm_mlp[l], bf(w_mlp_up[l]), bf(w_mlp_down[l]), norm_final, l == depth - 1)
    return x2.reshape(bsz, seq, D_MODEL)
```

```python
import functools

import jax
import jax.numpy as jnp
from jax import lax
from jax.experimental import pallas as pl
from jax.experimental.pallas import tpu as pltpu

D_MODEL = 1024
HEAD_DIM = 64
ROPE_THETA = 10000.0
NORM_EPS = 1e-6
QBLK = 128

A_HEADS = 8
A_KV_HEADS = 2
A_GROUP = A_HEADS // A_KV_HEADS
A_WIDTH = A_HEADS * HEAD_DIM
NSA_BRANCHES = 3
CMP_LEN = 32
CMP_STRIDE = 16
CMP_HIDDEN = 2 * HEAD_DIM
SEL_LEN = 64
SEL_TOPK = 16
A_WINDOW = 512
FORCED_SCORE = 1e4

B_HEADS = 8
B_WIDTH = B_HEADS * HEAD_DIM
B_WINDOW = 128

C_WIDTH = 512
C_GROUP_CH = 16
C_GROUPS = C_WIDTH // C_GROUP_CH
C_STATE = 64
C_STATES = C_GROUPS * C_STATE
_HALF_CH = C_WIDTH // 2
_HALF_ST = C_STATES // 2

D_FF = 4 * D_MODEL
N_BRANCH = 3

IN_WIDTHS = (A_WIDTH, NSA_BRANCHES * 2 * A_KV_HEADS * HEAD_DIM, NSA_BRANCHES * A_HEADS,
             B_WIDTH, 2 * HEAD_DIM, C_WIDTH, N_BRANCH * D_MODEL)

LANES = 128
SUBLANES = 8
VMEM_LIMIT = 56 * 1024 * 1024

MXU_DTYPE = jnp.bfloat16
ROW_TILE = 512
S5_CHUNK = 64
S5_LANES = 1024
SEL_CHUNK = 256
SWA_QSUB = 2
FF_CHUNK = 1024

LOG2E = 1.4426950408889634
_NEG_INF = float("-inf")


def _f32(x):
    return x.astype(jnp.float32)


def _mm(a, b):
    return jnp.dot(a.astype(MXU_DTYPE), b.astype(MXU_DTYPE), preferred_element_type=jnp.float32)


def _mm_nt(a, b):
    return lax.dot_general(a.astype(MXU_DTYPE), b.astype(MXU_DTYPE), (((1,), (1,)), ((), ())),
                           preferred_element_type=jnp.float32)


def _rmsnorm(x, gain):
    return x * lax.rsqrt(jnp.mean(x * x, axis=-1, keepdims=True) + NORM_EPS) * gain


def _resident(shape):
    zeros = (0,) * len(shape)
    return pl.BlockSpec(shape, lambda *_: zeros, pipeline_mode=pl.Buffered(1))


def _params(sem):
    return pltpu.CompilerParams(dimension_semantics=sem, vmem_limit_bytes=VMEM_LIMIT)


def _rope_table_kernel(pos_ref, invf_ref, sign_ref, cos_ref, sin_ref):
    ang = _f32(pos_ref[...]) * invf_ref[...]
    cos_ref[...] = jnp.cos(ang)
    sin_ref[...] = jnp.sin(ang) * sign_ref[...]


def _rope_tables(positions):
    n = positions.size
    half = HEAD_DIM // 2
    inv_freq = ROPE_THETA ** (-jnp.arange(half, dtype=jnp.float32) / half)
    invf = jnp.tile(inv_freq, LANES // half)
    sign = jnp.tile(jnp.concatenate([-jnp.ones(half, jnp.float32), jnp.ones(half, jnp.float32)]),
                    LANES // HEAD_DIM)
    tm = min(1024, n)
    natural = pl.pallas_call(
        _rope_table_kernel,
        grid=(n // tm,),
        in_specs=[pl.BlockSpec((tm, 1), lambda i: (i, 0)), _resident((1, LANES)), _resident((1, LANES))],
        out_specs=[pl.BlockSpec((tm, LANES), lambda i: (i, 0))] * 2,
        out_shape=[jax.ShapeDtypeStruct((n, LANES), jnp.float32)] * 2,
        compiler_params=_params(("parallel",)),
        name="rope_tables",
    )(positions.reshape(n, 1), invf.reshape(1, LANES), sign.reshape(1, LANES))
    transposed = pl.pallas_call(
        _rope_table_kernel,
        grid=(n // tm,),
        in_specs=[pl.BlockSpec((1, tm), lambda i: (0, i)), _resident((HEAD_DIM, 1)), _resident((HEAD_DIM, 1))],
        out_specs=[pl.BlockSpec((HEAD_DIM, tm), lambda i: (0, i))] * 2,
        out_shape=[jax.ShapeDtypeStruct((HEAD_DIM, n), jnp.float32)] * 2,
        compiler_params=_params(("parallel",)),
        name="rope_tables_t",
    )(positions.reshape(1, n), invf[:HEAD_DIM].reshape(HEAD_DIM, 1), sign[:HEAD_DIM].reshape(HEAD_DIM, 1))
    return natural + transposed


def _swap_halves(x):
    lane = lax.broadcasted_iota(jnp.int32, x.shape, 1)
    first = (lane & (HEAD_DIM - 1)) < HEAD_DIM // 2
    return jnp.where(first, pltpu.roll(x, LANES - HEAD_DIM // 2, 1), pltpu.roll(x, HEAD_DIM // 2, 1))


def _proj_kernel(x_ref, gain_ref, wn_ref, wt_ref, cos_ref, sin_ref, cost_ref, sint_ref,
                 qat_ref, qbt_ref, vat_ref, vbt_ref, ka_ref, kvb_ref, kseg_ref, vseg_ref, ga_ref, uc_ref,
                 seg_ref):
    h = _rmsnorm(x_ref[...], gain_ref[...]).astype(MXU_DTYPE)
    nat = jnp.dot(h, wn_ref[...], preferred_element_type=jnp.float32)
    tr = _mm_nt(wt_ref[...], h)

    cos = cos_ref[...]
    sin = sin_ref[...]

    def roped(j):
        xj = nat[:, j * LANES:(j + 1) * LANES]
        return xj, xj * cos + _swap_halves(xj) * sin

    for j in range(1, NSA_BRANCHES):
        ka_ref[:, (j - 1) * LANES:j * LANES] = roped(j)[1].astype(ka_ref.dtype)
    raw, rot = roped(3)
    lane = lax.broadcasted_iota(jnp.int32, raw.shape, 1)
    kvb_ref[...] = jnp.where(lane < HEAD_DIM, rot, raw).astype(kvb_ref.dtype)
    n_seg_tile = nat.shape[0] // CMP_STRIDE
    for j, (src, dst_ref) in enumerate(((roped(0)[1], kseg_ref), (nat[:, 4 * LANES:5 * LANES], vseg_ref))):
        seg_ref[j] = src
        for tok in range(CMP_STRIDE):
            dst_ref[:, tok * LANES:(tok + 1) * LANES] = (
                seg_ref[j, pl.ds(tok, n_seg_tile, stride=CMP_STRIDE), :].astype(dst_ref.dtype))
    ga_ref[...] = jax.nn.sigmoid(nat[:, 5 * LANES:6 * LANES])
    uc_ref[...] = nat[:, 6 * LANES:]

    cos_t = cost_ref[...]
    sin_t = sint_ref[...]
    half = HEAD_DIM // 2

    def roped_t(r0):
        xh = tr[r0:r0 + HEAD_DIM]
        return xh * cos_t + jnp.concatenate([xh[half:], xh[:half]], axis=0) * sin_t

    for hd in range(A_HEADS):
        qat_ref[hd * HEAD_DIM:(hd + 1) * HEAD_DIM, :] = roped_t(hd * HEAD_DIM).astype(qat_ref.dtype)
    for hd in range(B_HEADS):
        qbt_ref[hd * HEAD_DIM:(hd + 1) * HEAD_DIM, :] = roped_t(A_WIDTH + hd * HEAD_DIM).astype(qbt_ref.dtype)
    v0 = A_WIDTH + B_WIDTH
    vat_ref[...] = tr[v0:v0 + 2 * LANES].astype(vat_ref.dtype)
    vbt_ref[...] = tr[v0 + 2 * LANES:].astype(vbt_ref.dtype)


_NAT_COLS = 10 * LANES
_TR_ROWS = A_WIDTH + B_WIDTH + 2 * LANES + HEAD_DIM


def _pack_w_in(w_in):
    o = [0]
    for w in IN_WIDTHS:
        o.append(o[-1] + w)
    scale = HEAD_DIM ** -0.5 * LOG2E
    q_a = w_in[:, o[0]:o[1]] * scale
    kv_cols = [w_in[:, o[1] + j * LANES:o[1] + (j + 1) * LANES] for j in range(2 * NSA_BRANCHES)]
    k_a = jnp.concatenate(kv_cols[0::2], axis=1)
    v_a = jnp.concatenate(kv_cols[1::2], axis=1)
    g_a = jnp.pad(w_in[:, o[2]:o[3]], ((0, 0), (0, LANES - NSA_BRANCHES * A_HEADS)))
    q_b = w_in[:, o[3]:o[4]] * scale
    kv_b = w_in[:, o[4]:o[5]]
    u_c = w_in[:, o[5]:o[6]]
    w_nat = jnp.concatenate([k_a, kv_b, v_a[:, :LANES], g_a, u_c], axis=1)
    w_tr = jnp.concatenate([q_a, q_b, v_a[:, LANES:], kv_b[:, HEAD_DIM:]], axis=1).T
    return w_nat.astype(MXU_DTYPE), w_tr.astype(MXU_DTYPE)


def _in_projection(x2, gain, w_nat, w_tr, tables, bsz, seq):
    n = x2.shape[0]
    tm = min(ROW_TILE, seq)
    tiles = seq // tm

    def rows(width):
        return pl.BlockSpec((tm, width), lambda i: (i, 0))

    def lanes(height):
        return pl.BlockSpec((None, height, tm), lambda i: (i // tiles, 0, i % tiles))

    table_t = pl.BlockSpec((HEAD_DIM, tm), lambda i: (0, i))
    t_heights = (A_WIDTH, B_WIDTH, 2 * LANES, HEAD_DIM)
    widths = (2 * LANES, LANES, LANES, C_WIDTH)
    dtypes = (MXU_DTYPE, MXU_DTYPE, jnp.float32, jnp.float32)
    seg_rows = tm // CMP_STRIDE
    seg_spec = pl.BlockSpec((seg_rows, CMP_STRIDE * LANES), lambda i: (i, 0))
    seg_shape = jax.ShapeDtypeStruct((n // CMP_STRIDE, CMP_STRIDE * LANES), MXU_DTYPE)
    return pl.pallas_call(
        _proj_kernel,
        grid=(n // tm,),
        in_specs=[rows(D_MODEL), _resident((1, D_MODEL)), _resident((D_MODEL, _NAT_COLS)),
                  _resident((_TR_ROWS, D_MODEL)), rows(LANES), rows(LANES), table_t, table_t],
        out_specs=[lanes(hh) for hh in t_heights] + [rows(w) for w in widths[:2]] + [seg_spec, seg_spec]
        + [rows(w) for w in widths[2:]],
        out_shape=[jax.ShapeDtypeStruct((bsz, hh, seq), MXU_DTYPE) for hh in t_heights]
        + [jax.ShapeDtypeStruct((n, w), dt) for w, dt in zip(widths[:2], dtypes[:2])] + [seg_shape, seg_shape]
        + [jax.ShapeDtypeStruct((n, w), dt) for w, dt in zip(widths[2:], dtypes[2:])],
        scratch_shapes=[pltpu.VMEM((2, tm, LANES), jnp.float32)],
        compiler_params=_params(("parallel",)),
        name="in_projection",
    )(x2, gain.reshape(1, D_MODEL), w_nat, w_tr, *tables)


def _compress_kernel(kseg_ref, vseg_ref, pos_ref, w1_ref, w1x_ref, w2_ref, kc_ref, vct_ref):
    nseg = kseg_ref.shape[0]
    both = []
    for j, seg_ref in enumerate((kseg_ref, vseg_ref)):
        seg = seg_ref[...]
        pos_term = _mm(pos_ref[j], w1_ref[j])
        heads = []
        for h in range(A_KV_HEADS):
            hidden = (_mm(seg, w1x_ref[j, 0, h]) + pltpu.roll(_mm(seg, w1x_ref[j, 1, h]), nseg - 1, 0)
                      + pos_term)
            heads.append(_mm(jax.nn.gelu(hidden), w2_ref[j]))
        both.append(jnp.concatenate(heads, axis=1))
    kc_ref[...] = both[0].astype(kc_ref.dtype)
    vct_ref[...] = both[1].T.astype(vct_ref.dtype)


def _expand_w1(w1):
    w = w1.reshape(2, 2, CMP_STRIDE, HEAD_DIM, CMP_HIDDEN)
    eye = jnp.eye(A_KV_HEADS, dtype=w1.dtype)
    return jnp.einsum("jatdc,hg->jahtgdc", w, eye).reshape(2, 2, A_KV_HEADS, CMP_STRIDE * LANES, CMP_HIDDEN)


def _compress(kseg, vseg, pos_flat, w1, w2, bsz):
    nseg = kseg.shape[0] // bsz
    segw = kseg.shape[1]
    seg_spec = pl.BlockSpec((nseg, segw), lambda b: (b, 0))
    return pl.pallas_call(
        _compress_kernel,
        grid=(bsz,),
        in_specs=[seg_spec, seg_spec, _resident((2, 1, CMP_LEN * HEAD_DIM)),
                  _resident((2, CMP_LEN * HEAD_DIM, CMP_HIDDEN)),
                  _resident((2, 2, A_KV_HEADS, segw, CMP_HIDDEN)), _resident((2, CMP_HIDDEN, HEAD_DIM))],
        out_specs=[pl.BlockSpec((None, nseg, LANES), lambda b: (b, 0, 0)),
                   pl.BlockSpec((None, LANES, nseg), lambda b: (b, 0, 0))],
        out_shape=[jax.ShapeDtypeStruct((bsz, nseg, LANES), MXU_DTYPE),
                   jax.ShapeDtypeStruct((bsz, LANES, nseg), MXU_DTYPE)],
        compiler_params=_params(("parallel",)),
        name="nsa_compress",
    )(kseg, vseg, pos_flat.astype(MXU_DTYPE), w1.astype(MXU_DTYPE), _expand_w1(w1).astype(MXU_DTYPE),
      w2.astype(MXU_DTYPE))


def _split3(x):
    hi = x.astype(MXU_DTYPE)
    r1 = x - _f32(hi)
    mid = r1.astype(MXU_DTYPE)
    lo = (r1 - _f32(mid)).astype(MXU_DTYPE)
    return hi, mid, lo


def _col_max(s):
    return jnp.max(s, axis=0, keepdims=True)


def _tile_lanes(x, n):
    return jnp.concatenate([x] * n, axis=1)


def _pad_head_rows(x, h, fill):
    other = jnp.full(x.shape, fill, x.dtype)
    return jnp.concatenate([x, other] if h == 0 else [other, x], axis=0)


def _values_with_ones(vt):
    return jnp.concatenate([vt, jnp.ones((2 * SUBLANES, vt.shape[1]), vt.dtype)], axis=0)


def _store_transposed_pairs(o_ref, heads_t, row0=0):
    for p in range(len(heads_t) // 2):
        pair = jnp.concatenate([heads_t[2 * p], heads_t[2 * p + 1]], axis=0)
        o_ref[row0:row0 + QBLK, p * LANES:(p + 1) * LANES] = pair.T.astype(o_ref.dtype)


def _nsa_kernel(qt_ref, k_ref, vt_ref, kc_ref, vct_ref, g_ref, c2s_ref, o_ref,
                selb_ref, sca_ref, scb_ref, cmp_ref, win_ref, diag_ref, *, seq):
    i = pl.program_id(1)
    q0 = i * QBLK
    n_cmp = kc_ref.shape[0]
    n_sel = seq // SEL_LEN
    kc_sz = min(SEL_CHUNK, seq)
    span = min(A_WINDOW + QBLK, seq)
    blocks_per_chunk = kc_sz // SEL_LEN

    gates_t = g_ref[...].T
    qp_lane = q0 + lax.broadcasted_iota(jnp.int32, (1, QBLK), 1)

    cmp_end = lax.broadcasted_iota(jnp.int32, (n_cmp, 1), 0) * CMP_STRIDE + (CMP_LEN - 1)
    bias_c = _tile_lanes(jnp.where(cmp_end <= qp_lane, 0.0, _NEG_INF), A_GROUP)

    w0 = pl.multiple_of(jnp.maximum(q0 + QBLK - span, 0), QBLK)
    diff = qp_lane - (w0 + lax.broadcasted_iota(jnp.int32, (span, 1), 0))
    bias_w = _tile_lanes(jnp.where(diff >= 0, jnp.where(diff < A_WINDOW, 0.0, _NEG_INF), _NEG_INF), A_GROUP)

    w_qs = []
    for h in range(A_KV_HEADS):
        q_t = jnp.concatenate([qt_ref[(h * A_GROUP + g) * HEAD_DIM:(h * A_GROUP + g + 1) * HEAD_DIM, :]
                               for g in range(A_GROUP)], axis=1)
        w_qs.append(_pad_head_rows(q_t, h, 0.0))

    n_chunks = (q0 + QBLK + kc_sz - 1) // kc_sz
    last = n_chunks - 1

    def issue_scores(c, dst_ref):
        k0 = pl.multiple_of(jnp.minimum(c, last) * kc_sz, kc_sz)
        keys = k_ref[pl.ds(k0, kc_sz), :LANES]
        for h in range(A_KV_HEADS):
            dst_ref[h] = _mm(keys, w_qs[h])

    for h in range(A_KV_HEADS):
        cmp_ref[h] = _mm(kc_ref[...], w_qs[h])
    for h in range(A_KV_HEADS):
        win_ref[h] = _mm(k_ref[pl.ds(w0, span), LANES:], w_qs[h])
    issue_scores(0, sca_ref)
    for h in range(A_KV_HEADS):
        diag_ref[h] = _mm(k_ref[pl.ds(pl.multiple_of(q0, QBLK), QBLK), :LANES], w_qs[h])

    o_cs, o_ws = [], []
    for h in range(A_KV_HEADS):
        s = cmp_ref[h] + bias_c
        m = _col_max(s)
        m = jnp.where(m == _NEG_INF, 0.0, m)
        e = jnp.exp2(s - m)
        p = e * (1.0 / jnp.maximum(jnp.sum(e, axis=0, keepdims=True), 1e-30))
        o_cs.append(_mm(vct_ref[...], p)[h * HEAD_DIM:(h + 1) * HEAD_DIM])

        p_sum = p[:, 0:QBLK]
        for g in range(1, A_GROUP):
            p_sum = p_sum + p[:, g * QBLK:(g + 1) * QBLK]
        imp_t = sum(_mm(c2s_ref[...], part) for part in _split3(p_sum))
        blk = lax.broadcasted_iota(jnp.int32, (n_sel, QBLK), 0)
        cur = jnp.right_shift(qp_lane, 6)
        forced = (blk == 0) | (blk == cur) | (blk == cur - 1)
        score = jnp.where(forced, FORCED_SCORE, imp_t)
        score = jnp.where(blk <= cur, score, _NEG_INF)
        n_grp = n_sel // SUBLANES
        sub = lax.broadcasted_iota(jnp.int32, (SUBLANES, QBLK), 0)
        score_grp = [score[SUBLANES * v:SUBLANES * (v + 1), :] for v in range(n_grp)]
        rank_grp = [jnp.zeros((SUBLANES, QBLK), jnp.float32) for _ in range(n_grp)]
        for j in range(n_sel):
            row = jnp.broadcast_to(score[j:j + 1, :], (SUBLANES, QBLK))
            vj, rj = divmod(j, SUBLANES)
            for v in range(n_grp):
                if v > vj:
                    inc = jnp.where(row >= score_grp[v], 1.0, 0.0)
                elif v < vj:
                    inc = jnp.where(row > score_grp[v], 1.0, 0.0)
                else:
                    inc = jnp.where(sub > rj, jnp.where(row >= score_grp[v], 1.0, 0.0),
                                    jnp.where(row > score_grp[v], 1.0, 0.0))
                rank_grp[v] = rank_grp[v] + inc
        rank = jnp.concatenate(rank_grp, axis=0)
        picked = jnp.where(rank < float(min(SEL_TOPK, n_sel)), 0.0, _NEG_INF)
        selb_ref[h] = jnp.where(blk < cur, picked, _NEG_INF)

    for h in range(A_KV_HEADS):
        sw = win_ref[h] + bias_w
        pw = jnp.exp2(sw - _col_max(sw))
        acc_w = _mm(_values_with_ones(vt_ref[LANES + h * HEAD_DIM:LANES + (h + 1) * HEAD_DIM, pl.ds(w0, span)]), pw)
        o_ws.append(acc_w[:HEAD_DIM] / acc_w[HEAD_DIM:HEAD_DIM + 1])

    def consume(c, src_ref, states):
        cc = jnp.minimum(c, last)
        k0 = pl.multiple_of(cc * kc_sz, kc_sz)
        out = []
        for h in range(A_KV_HEADS):
            m_run, acc = states[h]
            rows = [_tile_lanes(jnp.where(c <= last, selb_ref[h, pl.ds(cc * blocks_per_chunk + jj, 1), :], _NEG_INF),
                                A_GROUP) for jj in range(blocks_per_chunk)]
            blocks = [src_ref[h, jj * SEL_LEN:(jj + 1) * SEL_LEN, :] for jj in range(blocks_per_chunk)]
            part = None
            for sc_blk, row in zip(blocks, rows):
                blk_max = jnp.max(sc_blk.reshape(SEL_LEN // SUBLANES, SUBLANES, A_GROUP * QBLK), axis=0) + row
                part = blk_max if part is None else jnp.maximum(part, blk_max)
            m_new = jnp.maximum(m_run, _col_max(part))
            pr = jnp.concatenate([jnp.exp2(sc_blk + (row - m_new)) for sc_blk, row in zip(blocks, rows)], axis=0)
            vt = _values_with_ones(vt_ref[h * HEAD_DIM:(h + 1) * HEAD_DIM, pl.ds(k0, kc_sz)])
            out.append((m_new, jnp.exp2(m_run - m_new) * acc + _mm(vt, pr)))
        return tuple(out)

    def sel_step(t, states):
        issue_scores(2 * t + 1, scb_ref)
        states = consume(2 * t, sca_ref, states)
        issue_scores(2 * t + 2, sca_ref)
        return consume(2 * t + 1, scb_ref, states)

    kp_diag = q0 + lax.broadcasted_iota(jnp.int32, (QBLK, 1), 0)
    same_blk = jnp.right_shift(kp_diag, 6) == jnp.right_shift(qp_lane, 6)
    bias_d = _tile_lanes(jnp.where(kp_diag <= qp_lane, jnp.where(same_blk, 0.0, _NEG_INF), _NEG_INF), A_GROUP)
    q0_al = pl.multiple_of(q0, QBLK)
    init = []
    for h in range(A_KV_HEADS):
        sd = diag_ref[h] + bias_d
        m0 = _col_max(sd)
        vt = _values_with_ones(vt_ref[h * HEAD_DIM:(h + 1) * HEAD_DIM, pl.ds(q0_al, QBLK)])
        init.append((m0, _mm(vt, jnp.exp2(sd - m0))))
    init = tuple(init)
    sel_out = lax.fori_loop(0, (n_chunks + 1) // 2, sel_step, init)

    heads_t = []
    for h in range(A_KV_HEADS):
        acc_s = sel_out[h][1]
        o_s = acc_s[:HEAD_DIM] / jnp.maximum(acc_s[HEAD_DIM:HEAD_DIM + 1], 1e-30)
        for g in range(A_GROUP):
            hd = h * A_GROUP + g
            cols = slice(g * QBLK, (g + 1) * QBLK)
            heads_t.append(gates_t[hd:hd + 1, :] * o_cs[h][:, cols]
                           + gates_t[A_HEADS + hd:A_HEADS + hd + 1, :] * o_s[:, cols]
                           + gates_t[2 * A_HEADS + hd:2 * A_HEADS + hd + 1, :] * o_ws[h][:, cols])
    _store_transposed_pairs(o_ref, heads_t)


def _nsa_attention(qat, ka, vat, kc, vct, ga, bsz, seq):
    nblk = seq // QBLK
    n_seg = seq // CMP_STRIDE
    n_sel = seq // SEL_LEN
    assert seq >= A_WINDOW + QBLK and seq % SEL_CHUNK == 0
    c_start = jnp.arange(n_seg) * CMP_STRIDE
    s_start = jnp.arange(n_sel) * SEL_LEN
    overlap = (jnp.minimum(c_start[None, :] + CMP_LEN, s_start[:, None] + SEL_LEN)
               - jnp.maximum(c_start[None, :], s_start[:, None]))
    c2s = (jnp.clip(overlap, 0, None).astype(jnp.float32) / CMP_LEN).astype(MXU_DTYPE)
    return pl.pallas_call(
        functools.partial(_nsa_kernel, seq=seq),
        grid=(bsz, nblk),
        in_specs=[
            pl.BlockSpec((None, A_WIDTH, QBLK), lambda b, i: (b, 0, i)),
            pl.BlockSpec((seq, 2 * LANES), lambda b, i: (b, 0)),
            pl.BlockSpec((None, 2 * LANES, seq), lambda b, i: (b, 0, 0)),
            pl.BlockSpec((None, n_seg, LANES), lambda b, i: (b, 0, 0)),
            pl.BlockSpec((None, LANES, n_seg), lambda b, i: (b, 0, 0)),
            pl.BlockSpec((QBLK, LANES), lambda b, i: (b * nblk + i, 0)),
            _resident((n_sel, n_seg)),
        ],
        out_specs=pl.BlockSpec((QBLK, A_WIDTH), lambda b, i: (b * nblk + i, 0)),
        out_shape=jax.ShapeDtypeStruct((bsz * seq, A_WIDTH), MXU_DTYPE),
        scratch_shapes=[pltpu.VMEM((A_KV_HEADS, n_sel, QBLK), jnp.float32)]
        + [pltpu.VMEM((A_KV_HEADS, min(SEL_CHUNK, seq), A_GROUP * QBLK), jnp.float32)] * 2
        + [pltpu.VMEM((A_KV_HEADS, n_seg, A_GROUP * QBLK), jnp.float32),
           pltpu.VMEM((A_KV_HEADS, min(A_WINDOW + QBLK, seq), A_GROUP * QBLK), jnp.float32),
           pltpu.VMEM((A_KV_HEADS, QBLK, A_GROUP * QBLK), jnp.float32)],
        compiler_params=_params(("parallel", "arbitrary")),
        name="nsa_attention",
    )(qat, ka, vat, kc, vct, ga, c2s)


def _swa_kernel(sink_ref, qt_ref, kv_ref, vt_ref, o_ref, sc_ref, *, seq):
    i = pl.program_id(1)
    span = min(B_WINDOW + QBLK, seq)
    sink = jnp.concatenate([jnp.full((1, QBLK), sink_ref[hd] * LOG2E, jnp.float32) for hd in range(B_HEADS)],
                           axis=1)
    starts = []
    for sb in range(SWA_QSUB):
        q0 = (i * SWA_QSUB + sb) * QBLK
        w0 = pl.multiple_of(jnp.maximum(q0 + QBLK - span, 0), QBLK)
        starts.append((q0, w0))
        q_t = jnp.concatenate([qt_ref[hd * HEAD_DIM:(hd + 1) * HEAD_DIM, sb * QBLK:(sb + 1) * QBLK]
                               for hd in range(B_HEADS)], axis=1)
        sc_ref[sb] = _mm(kv_ref[pl.ds(w0, span), :], _pad_head_rows(q_t, 0, 0.0))
    for sb, (q0, w0) in enumerate(starts):
        qp_lane = q0 + lax.broadcasted_iota(jnp.int32, (1, QBLK), 1)
        diff = qp_lane - (w0 + lax.broadcasted_iota(jnp.int32, (span, 1), 0))
        bias = _tile_lanes(jnp.where(diff >= 0, jnp.where(diff < B_WINDOW, 0.0, _NEG_INF), _NEG_INF), B_HEADS)
        s = sc_ref[sb] + bias
        m = jnp.maximum(_col_max(s), sink)
        acc = _mm(_values_with_ones(vt_ref[:, pl.ds(w0, span)]), jnp.exp2(s - m))
        o_t = acc[:HEAD_DIM] / (acc[HEAD_DIM:HEAD_DIM + 1] + jnp.exp2(sink - m))
        _store_transposed_pairs(o_ref, [o_t[:, hd * QBLK:(hd + 1) * QBLK] for hd in range(B_HEADS)], sb * QBLK)


def _swa_attention(qbt, kvb, vbt, sinks, bsz, seq):
    rows = SWA_QSUB * QBLK
    nstep = seq // rows
    span = min(B_WINDOW + QBLK, seq)
    assert seq >= B_WINDOW + QBLK and seq % rows == 0
    return pl.pallas_call(
        functools.partial(_swa_kernel, seq=seq),
        grid=(bsz, nstep),
        in_specs=[
            pl.BlockSpec(memory_space=pltpu.SMEM),
            pl.BlockSpec((None, B_WIDTH, rows), lambda b, i: (b, 0, i)),
            pl.BlockSpec((seq, LANES), lambda b, i: (b, 0)),
            pl.BlockSpec((None, HEAD_DIM, seq), lambda b, i: (b, 0, 0)),
        ],
        out_specs=pl.BlockSpec((rows, B_WIDTH), lambda b, i: (b * nstep + i, 0)),
        out_shape=jax.ShapeDtypeStruct((bsz * seq, B_WIDTH), MXU_DTYPE),
        scratch_shapes=[pltpu.VMEM((SWA_QSUB, span, B_HEADS * QBLK), jnp.float32)],
        compiler_params=_params(("parallel", "arbitrary")),
        name="swa_attention",
    )(sinks, qbt, kvb, vbt)


def _s5_discretize_kernel(are_ref, aim_ref, logdt_ref, bre_ref, bim_ref,
                          abr_ref, abi_ref, bbr_ref, bbi_ref):
    ar = are_ref[...]
    ai = aim_ref[...]
    dt = jnp.exp(logdt_ref[...])
    mag = jnp.exp(dt * ar)
    abr = mag * jnp.cos(dt * ai)
    abi = mag * jnp.sin(dt * ai)
    den = ar * ar + ai * ai
    nr = abr - 1.0
    coef_r = (nr * ar + abi * ai) / den
    coef_i = (abi * ar - nr * ai) / den
    abr_ref[...] = abr
    abi_ref[...] = abi
    br = bre_ref[...]
    bi = bim_ref[...]
    bbr_ref[...] = coef_r[:, None, :] * br - coef_i[:, None, :] * bi
    bbi_ref[...] = coef_r[:, None, :] * bi + coef_i[:, None, :] * br


def _s5_discretize(a_re, a_im, log_dt, b_re, b_im):
    lg = a_re.shape[0] * a_re.shape[1]
    a2 = lambda a: a.reshape(lg, C_STATE)
    b3 = lambda b: jnp.swapaxes(b.reshape(lg, C_STATE, C_GROUP_CH), 1, 2)
    sa = jax.ShapeDtypeStruct((lg, C_STATE), jnp.float32)
    sb = jax.ShapeDtypeStruct((lg, C_GROUP_CH, C_STATE), jnp.float32)
    return pl.pallas_call(_s5_discretize_kernel, out_shape=[sa, sa, sb, sb], name="s5_discretize")(
        a2(a_re), a2(a_im), log_dt.reshape(lg, 1), b3(b_re), b3(b_im))


def _s5_kernel(u_ref, bmat_ref, abar_ref, cmat_ref, d_ref, gw_ref, gb_ref, y_ref, state_ref, ut_ref, xs_ref):
    bsz, t_steps, _ = u_ref.shape

    @pl.when(pl.program_id(0) == 0)
    def _():
        state_ref[...] = jnp.zeros_like(state_ref)

    def gather(t, carry):
        ut_ref[pl.ds(pl.multiple_of(t * bsz, bsz), bsz), :] = u_ref[:, t, :]
        return carry

    lax.fori_loop(0, t_steps, gather, 0, unroll=8)
    u = ut_ref[...]
    for hf in range(2):
        bu = _mm(u[:, hf * _HALF_CH:(hf + 1) * _HALF_CH], bmat_ref[hf])
        xs_ref[:, hf * _HALF_ST:(hf + 1) * _HALF_ST] = bu[:, :_HALF_ST]
        xs_ref[:, C_STATES + hf * _HALF_ST:C_STATES + (hf + 1) * _HALF_ST] = bu[:, _HALF_ST:]

    for cg in range(C_STATES // S5_LANES):
        re = slice(cg * S5_LANES, (cg + 1) * S5_LANES)
        im = slice(C_STATES + cg * S5_LANES, C_STATES + (cg + 1) * S5_LANES)
        ar = jnp.broadcast_to(abar_ref[0:1, re], (bsz, S5_LANES))
        ai = jnp.broadcast_to(abar_ref[0:1, im], (bsz, S5_LANES))

        def step(t, carry):
            xr, xi = carry
            r0 = pl.multiple_of(t * bsz, bsz)
            nr = ar * xr - ai * xi + xs_ref[pl.ds(r0, bsz), re]
            ni = ar * xi + ai * xr + xs_ref[pl.ds(r0, bsz), im]
            xs_ref[pl.ds(r0, bsz), re] = nr
            xs_ref[pl.ds(r0, bsz), im] = ni
            return nr, ni

        xr, xi = lax.fori_loop(0, t_steps, step, (state_ref[:, re], state_ref[:, im]), unroll=8)
        state_ref[:, re] = xr
        state_ref[:, im] = xi

    ys = []
    for hf in range(2):
        xr = xs_ref[:, hf * _HALF_ST:(hf + 1) * _HALF_ST]
        xi = xs_ref[:, C_STATES + hf * _HALF_ST:C_STATES + (hf + 1) * _HALF_ST]
        ys.append(_mm(xr, cmat_ref[hf, :_HALF_ST, :]) + _mm(xi, cmat_ref[hf, _HALF_ST:, :]))
    y = jnp.concatenate(ys, axis=1) + d_ref[...] * u
    z = jax.nn.gelu(y)
    out = z * jax.nn.sigmoid(_mm(z, gw_ref[...]) + gb_ref[...])
    y_ref[...] = out.reshape(t_steps, bsz, C_WIDTH).astype(y_ref.dtype)


def _s5(u, bmat, abar, cmat, d, glu_w, glu_b):
    bsz, seq, _ = u.shape
    t_steps = min(S5_CHUNK, seq)
    return pl.pallas_call(
        _s5_kernel,
        grid=(seq // t_steps,),
        in_specs=[
            pl.BlockSpec((bsz, t_steps, C_WIDTH), lambda i: (0, i, 0)),
            _resident((2, _HALF_CH, 2 * _HALF_ST)), _resident((1, 2 * C_STATES)), _resident((2, 2 * _HALF_ST, _HALF_CH)),
            _resident((1, C_WIDTH)), _resident((C_WIDTH, C_WIDTH)), _resident((1, C_WIDTH)),
        ],
        out_specs=pl.BlockSpec((t_steps, bsz, C_WIDTH), lambda i: (i, 0, 0)),
        out_shape=jax.ShapeDtypeStruct((seq, bsz, C_WIDTH), MXU_DTYPE),
        scratch_shapes=[pltpu.VMEM((bsz, 2 * C_STATES), jnp.float32),
                        pltpu.VMEM((t_steps * bsz, C_WIDTH), jnp.float32),
                        pltpu.VMEM((t_steps * bsz, 2 * C_STATES), jnp.float32)],
        compiler_params=_params(("arbitrary",)),
        name="s5_scan",
    )(u, bmat, abar, cmat, d.reshape(1, C_WIDTH), glu_w.astype(MXU_DTYPE), glu_b.reshape(1, C_WIDTH))


def _block_diag(per_group):
    g, r, c = per_group.shape
    eye = jnp.eye(g, dtype=per_group.dtype)
    return jnp.einsum("grc,gk->grkc", per_group, eye).reshape(g * r, g * c)


def _merge_kernel(x_ref, ya_ref, yb_ref, yc_ref, gain_ref, wg_ref, pa_ref, pb_ref, pc_ref, wo_ref, o_ref):
    x = x_ref[...]
    h = _rmsnorm(x, gain_ref[...]).astype(MXU_DTYPE)
    merged = None
    for j, (y_ref, p_ref) in enumerate(((ya_ref, pa_ref), (yb_ref, pb_ref), (yc_ref, pc_ref))):
        gate = jax.nn.sigmoid(jnp.dot(h, wg_ref[:, j * D_MODEL:(j + 1) * D_MODEL],
                                      preferred_element_type=jnp.float32))
        term = gate * jnp.dot(y_ref[...], p_ref[...], preferred_element_type=jnp.float32)
        merged = term if merged is None else merged + term
    o_ref[...] = x + _mm(merged, wo_ref[...])


def _merge(x2, ya, yb, yc, gain, wg, pa, pb, pc, wo):
    n = x2.shape[0]
    tm = min(ROW_TILE, n)

    def rows(width):
        return pl.BlockSpec((tm, width), lambda i: (i, 0))

    return pl.pallas_call(
        _merge_kernel,
        grid=(n // tm,),
        in_specs=[rows(D_MODEL), rows(A_WIDTH), rows(B_WIDTH), rows(C_WIDTH), _resident((1, D_MODEL)),
                  _resident((D_MODEL, N_BRANCH * D_MODEL)), _resident((A_WIDTH, D_MODEL)),
                  _resident((B_WIDTH, D_MODEL)), _resident((C_WIDTH, D_MODEL)), _resident((D_MODEL, D_MODEL))],
        out_specs=rows(D_MODEL),
        out_shape=jax.ShapeDtypeStruct((n, D_MODEL), jnp.float32),
        compiler_params=_params(("parallel",)),
        name="merge",
    )(x2, ya, yb, yc, gain.reshape(1, D_MODEL), wg, pa, pb, pc, wo)


def _mlp_kernel(x_ref, gain_ref, wu_ref, wd_ref, fgain_ref, o_ref, *, final_norm):
    x = x_ref[...]
    h = _rmsnorm(x, gain_ref[...]).astype(MXU_DTYPE)
    acc = x
    for c in range(D_FF // FF_CHUNK):
        cols = slice(c * FF_CHUNK, (c + 1) * FF_CHUNK)
        up = jnp.dot(h, wu_ref[:, cols], preferred_element_type=jnp.float32)
        acc = acc + _mm(jnp.square(jnp.maximum(up, 0.0)), wd_ref[cols, :])
    o_ref[...] = _rmsnorm(acc, fgain_ref[...]) if final_norm else acc


def _mlp(x2, gain, wu, wd, fgain, final_norm):
    n = x2.shape[0]
    tm = min(ROW_TILE, n)
    rows = pl.BlockSpec((tm, D_MODEL), lambda i: (i, 0))
    return pl.pallas_call(
        functools.partial(_mlp_kernel, final_norm=final_norm),
        grid=(n // tm,),
        in_specs=[rows, _resident((1, D_MODEL)), _resident((D_MODEL, D_FF)), _resident((D_FF, D_MODEL)),
                  _resident((1, D_MODEL))],
        out_specs=rows,
        out_shape=jax.ShapeDtypeStruct((n, D_MODEL), jnp.float32),
        compiler_params=_params(("parallel",)),
        name="mlp",
    )(x2, gain.reshape(1, D_MODEL), wu, wd, fgain.reshape(1, D_MODEL))


def kernel(x, positions, norm_mix, w_in, nsa_cmp_pos, nsa_cmp_w1, nsa_cmp_w2, swa_sinks, s5_a_re, s5_a_im, s5_log_dt, s5_b_re, s5_b_im, s5_c_re, s5_c_im, s5_d, s5_glu_w, s5_glu_b, w_branch_a, w_branch_b, w_branch_c, w_out, norm_mlp, w_mlp_up, w_mlp_down, norm_final):
    bsz, seq, _ = x.shape
    depth = w_in.shape[0]
    n = bsz * seq
    gate_col = sum(IN_WIDTHS[:-1])
    bf = lambda w: w.astype(MXU_DTYPE)

    tables = _rope_tables(positions)
    abr, abi, bbr, bbi = _s5_discretize(s5_a_re, s5_a_im, s5_log_dt, s5_b_re, s5_b_im)
    abr = abr.reshape(depth, 1, C_STATES)
    abi = abi.reshape(depth, 1, C_STATES)
    bbr = bbr.reshape(depth, C_GROUPS, C_GROUP_CH, C_STATE)
    bbi = bbi.reshape(depth, C_GROUPS, C_GROUP_CH, C_STATE)

    x2 = x.reshape(n, D_MODEL)
    for l in range(depth):
        qat, qbt, vat, vbt, ka, kvb, kseg, vseg, ga, uc = _in_projection(
            x2, norm_mix[l], *_pack_w_in(w_in[l]), tables, bsz, seq)

        kc, vct = _compress(kseg, vseg, nsa_cmp_pos[l].reshape(2, 1, CMP_LEN * HEAD_DIM), nsa_cmp_w1[l],
                            nsa_cmp_w2[l], bsz)
        ya = _nsa_attention(qat, ka, vat, kc, vct, ga, bsz, seq)

        yb = _swa_attention(qbt, kvb, vbt, swa_sinks[l], bsz, seq)

        cre_t = jnp.swapaxes(s5_c_re[l], 1, 2)
        cim_t = jnp.swapaxes(s5_c_im[l], 1, 2)
        halves = (slice(0, C_GROUPS // 2), slice(C_GROUPS // 2, C_GROUPS))
        bmat = jnp.stack([jnp.concatenate([_block_diag(bbr[l][g]), _block_diag(bbi[l][g])], axis=1)
                          for g in halves])
        cmat = jnp.stack([jnp.concatenate([_block_diag(cre_t[g]), -_block_diag(cim_t[g])], axis=0)
                          for g in halves])
        abar = jnp.concatenate([abr[l], abi[l]], axis=1)
        yc_tm = _s5(uc.reshape(bsz, seq, C_WIDTH), bf(bmat), abar, bf(cmat), s5_d[l], s5_glu_w[l], s5_glu_b[l])
        yc = jnp.swapaxes(yc_tm, 0, 1).reshape(n, C_WIDTH)

        x2 = _merge(x2, ya, yb, yc, norm_mix[l], bf(w_in[l][:, gate_col:]), bf(w_branch_a[l]),
                    bf(w_branch_b[l]), bf(w_branch_c[l]), bf(w_out[l]))
        x2 = _mlp(x2, norm_mlp[l], bf(w_mlp_up[l]), bf(w_mlp_down[l]), norm_final, l == depth - 1)
    return x2.reshape(bsz, seq, D_MODEL)
```

```python
import functools

import jax
import jax.numpy as jnp
from jax import lax
from jax.experimental import pallas as pl
from jax.experimental.pallas import tpu as pltpu

D_MODEL = 1024
HEAD_DIM = 64
ROPE_THETA = 10000.0
NORM_EPS = 1e-6
QBLK = 128

A_HEADS = 8
A_KV_HEADS = 2
A_GROUP = A_HEADS // A_KV_HEADS
A_WIDTH = A_HEADS * HEAD_DIM
NSA_BRANCHES = 3
CMP_LEN = 32
CMP_STRIDE = 16
CMP_HIDDEN = 2 * HEAD_DIM
SEL_LEN = 64
SEL_TOPK = 16
A_WINDOW = 512
FORCED_SCORE = 1e4

B_HEADS = 8
B_WIDTH = B_HEADS * HEAD_DIM
B_WINDOW = 128

C_WIDTH = 512
C_GROUP_CH = 16
C_GROUPS = C_WIDTH // C_GROUP_CH
C_STATE = 64
C_STATES = C_GROUPS * C_STATE
_HALF_CH = C_WIDTH // 2
_HALF_ST = C_STATES // 2

D_FF = 4 * D_MODEL
N_BRANCH = 3

IN_WIDTHS = (A_WIDTH, NSA_BRANCHES * 2 * A_KV_HEADS * HEAD_DIM, NSA_BRANCHES * A_HEADS,
             B_WIDTH, 2 * HEAD_DIM, C_WIDTH, N_BRANCH * D_MODEL)

LANES = 128
SUBLANES = 8
VMEM_LIMIT = 56 * 1024 * 1024

MXU_DTYPE = jnp.bfloat16
ROW_TILE = 512
S5_CHUNK = 64
S5_LANES = 1024
SEL_CHUNK = 256
SWA_QSUB = 4
FF_CHUNK = 1024

LOG2E = 1.4426950408889634
_NEG_INF = float("-inf")


def _f32(x):
    return x.astype(jnp.float32)


def _mm(a, b):
    return jnp.dot(a.astype(MXU_DTYPE), b.astype(MXU_DTYPE), preferred_element_type=jnp.float32)


def _mm_nt(a, b):
    return lax.dot_general(a.astype(MXU_DTYPE), b.astype(MXU_DTYPE), (((1,), (1,)), ((), ())),
                           preferred_element_type=jnp.float32)


def _rmsnorm(x, gain):
    return x * lax.rsqrt(jnp.mean(x * x, axis=-1, keepdims=True) + NORM_EPS) * gain


def _resident(shape):
    zeros = (0,) * len(shape)
    return pl.BlockSpec(shape, lambda *_: zeros, pipeline_mode=pl.Buffered(1))


def _params(sem):
    return pltpu.CompilerParams(dimension_semantics=sem, vmem_limit_bytes=VMEM_LIMIT)


def _rope_table_kernel(pos_ref, invf_ref, sign_ref, cos_ref, sin_ref):
    ang = _f32(pos_ref[...]) * invf_ref[...]
    cos_ref[...] = jnp.cos(ang)
    sin_ref[...] = jnp.sin(ang) * sign_ref[...]


def _rope_tables(positions):
    n = positions.size
    half = HEAD_DIM // 2
    inv_freq = ROPE_THETA ** (-jnp.arange(half, dtype=jnp.float32) / half)
    invf = jnp.tile(inv_freq, LANES // half)
    sign = jnp.tile(jnp.concatenate([-jnp.ones(half, jnp.float32), jnp.ones(half, jnp.float32)]),
                    LANES // HEAD_DIM)
    tm = min(1024, n)
    natural = pl.pallas_call(
        _rope_table_kernel,
        grid=(n // tm,),
        in_specs=[pl.BlockSpec((tm, 1), lambda i: (i, 0)), _resident((1, LANES)), _resident((1, LANES))],
        out_specs=[pl.BlockSpec((tm, LANES), lambda i: (i, 0))] * 2,
        out_shape=[jax.ShapeDtypeStruct((n, LANES), jnp.float32)] * 2,
        compiler_params=_params(("parallel",)),
        name="rope_tables",
    )(positions.reshape(n, 1), invf.reshape(1, LANES), sign.reshape(1, LANES))
    transposed = pl.pallas_call(
        _rope_table_kernel,
        grid=(n // tm,),
        in_specs=[pl.BlockSpec((1, tm), lambda i: (0, i)), _resident((HEAD_DIM, 1)), _resident((HEAD_DIM, 1))],
        out_specs=[pl.BlockSpec((HEAD_DIM, tm), lambda i: (0, i))] * 2,
        out_shape=[jax.ShapeDtypeStruct((HEAD_DIM, n), jnp.float32)] * 2,
        compiler_params=_params(("parallel",)),
        name="rope_tables_t",
    )(positions.reshape(1, n), invf[:HEAD_DIM].reshape(HEAD_DIM, 1), sign[:HEAD_DIM].reshape(HEAD_DIM, 1))
    return natural + transposed


def _swap_halves(x):
    lane = lax.broadcasted_iota(jnp.int32, x.shape, 1)
    first = (lane & (HEAD_DIM - 1)) < HEAD_DIM // 2
    return jnp.where(first, pltpu.roll(x, LANES - HEAD_DIM // 2, 1), pltpu.roll(x, HEAD_DIM // 2, 1))


def _proj_kernel(x_ref, gain_ref, wn_ref, wt_ref, cos_ref, sin_ref, cost_ref, sint_ref,
                 qat_ref, qbt_ref, vat_ref, vbt_ref, ka_ref, kvb_ref, kseg_ref, vseg_ref, ga_ref, uc_ref,
                 seg_ref):
    h = _rmsnorm(x_ref[...], gain_ref[...]).astype(MXU_DTYPE)
    nat = jnp.dot(h, wn_ref[...], preferred_element_type=jnp.float32)
    tr = _mm_nt(wt_ref[...], h)

    cos = cos_ref[...]
    sin = sin_ref[...]

    def roped(j):
        xj = nat[:, j * LANES:(j + 1) * LANES]
        return xj, xj * cos + _swap_halves(xj) * sin

    for j in range(1, NSA_BRANCHES):
        ka_ref[:, (j - 1) * LANES:j * LANES] = roped(j)[1].astype(ka_ref.dtype)
    raw, rot = roped(3)
    lane = lax.broadcasted_iota(jnp.int32, raw.shape, 1)
    kvb_ref[...] = jnp.where(lane < HEAD_DIM, rot, raw).astype(kvb_ref.dtype)
    n_seg_tile = nat.shape[0] // CMP_STRIDE
    for j, (src, dst_ref) in enumerate(((roped(0)[1], kseg_ref), (nat[:, 4 * LANES:5 * LANES], vseg_ref))):
        seg_ref[j] = src
        for tok in range(CMP_STRIDE):
            dst_ref[:, tok * LANES:(tok + 1) * LANES] = (
                seg_ref[j, pl.ds(tok, n_seg_tile, stride=CMP_STRIDE), :].astype(dst_ref.dtype))
    ga_ref[...] = jax.nn.sigmoid(nat[:, 5 * LANES:6 * LANES])
    uc_ref[...] = nat[:, 6 * LANES:]

    cos_t = cost_ref[...]
    sin_t = sint_ref[...]
    half = HEAD_DIM // 2

    def roped_t(r0):
        xh = tr[r0:r0 + HEAD_DIM]
        return xh * cos_t + jnp.concatenate([xh[half:], xh[:half]], axis=0) * sin_t

    for hd in range(A_HEADS):
        qat_ref[hd * HEAD_DIM:(hd + 1) * HEAD_DIM, :] = roped_t(hd * HEAD_DIM).astype(qat_ref.dtype)
    for hd in range(B_HEADS):
        qbt_ref[hd * HEAD_DIM:(hd + 1) * HEAD_DIM, :] = roped_t(A_WIDTH + hd * HEAD_DIM).astype(qbt_ref.dtype)
    v0 = A_WIDTH + B_WIDTH
    vat_ref[...] = tr[v0:v0 + 2 * LANES].astype(vat_ref.dtype)
    vbt_ref[...] = tr[v0 + 2 * LANES:].astype(vbt_ref.dtype)


_NAT_COLS = 10 * LANES
_TR_ROWS = A_WIDTH + B_WIDTH + 2 * LANES + HEAD_DIM


def _pack_w_in(w_in):
    o = [0]
    for w in IN_WIDTHS:
        o.append(o[-1] + w)
    scale = HEAD_DIM ** -0.5 * LOG2E
    q_a = w_in[:, o[0]:o[1]] * scale
    kv_cols = [w_in[:, o[1] + j * LANES:o[1] + (j + 1) * LANES] for j in range(2 * NSA_BRANCHES)]
    k_a = jnp.concatenate(kv_cols[0::2], axis=1)
    v_a = jnp.concatenate(kv_cols[1::2], axis=1)
    g_a = jnp.pad(w_in[:, o[2]:o[3]], ((0, 0), (0, LANES - NSA_BRANCHES * A_HEADS)))
    q_b = w_in[:, o[3]:o[4]] * scale
    kv_b = w_in[:, o[4]:o[5]]
    u_c = w_in[:, o[5]:o[6]]
    w_nat = jnp.concatenate([k_a, kv_b, v_a[:, :LANES], g_a, u_c], axis=1)
    w_tr = jnp.concatenate([q_a, q_b, v_a[:, LANES:], kv_b[:, HEAD_DIM:]], axis=1).T
    return w_nat.astype(MXU_DTYPE), w_tr.astype(MXU_DTYPE)


def _in_projection(x2, gain, w_nat, w_tr, tables, bsz, seq):
    n = x2.shape[0]
    tm = min(ROW_TILE, seq)
    tiles = seq // tm

    def rows(width):
        return pl.BlockSpec((tm, width), lambda i: (i, 0))

    def lanes(height):
        return pl.BlockSpec((None, height, tm), lambda i: (i // tiles, 0, i % tiles))

    table_t = pl.BlockSpec((HEAD_DIM, tm), lambda i: (0, i))
    t_heights = (A_WIDTH, B_WIDTH, 2 * LANES, HEAD_DIM)
    widths = (2 * LANES, LANES, LANES, C_WIDTH)
    dtypes = (MXU_DTYPE, MXU_DTYPE, jnp.float32, jnp.float32)
    seg_rows = tm // CMP_STRIDE
    seg_spec = pl.BlockSpec((seg_rows, CMP_STRIDE * LANES), lambda i: (i, 0))
    seg_shape = jax.ShapeDtypeStruct((n // CMP_STRIDE, CMP_STRIDE * LANES), MXU_DTYPE)
    return pl.pallas_call(
        _proj_kernel,
        grid=(n // tm,),
        in_specs=[rows(D_MODEL), _resident((1, D_MODEL)), _resident((D_MODEL, _NAT_COLS)),
                  _resident((_TR_ROWS, D_MODEL)), rows(LANES), rows(LANES), table_t, table_t],
        out_specs=[lanes(hh) for hh in t_heights] + [rows(w) for w in widths[:2]] + [seg_spec, seg_spec]
        + [rows(w) for w in widths[2:]],
        out_shape=[jax.ShapeDtypeStruct((bsz, hh, seq), MXU_DTYPE) for hh in t_heights]
        + [jax.ShapeDtypeStruct((n, w), dt) for w, dt in zip(widths[:2], dtypes[:2])] + [seg_shape, seg_shape]
        + [jax.ShapeDtypeStruct((n, w), dt) for w, dt in zip(widths[2:], dtypes[2:])],
        scratch_shapes=[pltpu.VMEM((2, tm, LANES), jnp.float32)],
        compiler_params=_params(("parallel",)),
        name="in_projection",
    )(x2, gain.reshape(1, D_MODEL), w_nat, w_tr, *tables)


def _compress_kernel(kseg_ref, vseg_ref, pos_ref, w1_ref, w1x_ref, w2_ref, kc_ref, vct_ref):
    nseg = kseg_ref.shape[0]
    both = []
    for j, seg_ref in enumerate((kseg_ref, vseg_ref)):
        seg = seg_ref[...]
        pos_term = _mm(pos_ref[j], w1_ref[j])
        heads = []
        for h in range(A_KV_HEADS):
            hidden = (_mm(seg, w1x_ref[j, 0, h]) + pltpu.roll(_mm(seg, w1x_ref[j, 1, h]), nseg - 1, 0)
                      + pos_term)
            heads.append(_mm(jax.nn.gelu(hidden), w2_ref[j]))
        both.append(jnp.concatenate(heads, axis=1))
    kc_ref[...] = both[0].astype(kc_ref.dtype)
    vct_ref[...] = both[1].T.astype(vct_ref.dtype)


def _expand_w1(w1):
    w = w1.reshape(2, 2, CMP_STRIDE, HEAD_DIM, CMP_HIDDEN)
    eye = jnp.eye(A_KV_HEADS, dtype=w1.dtype)
    return jnp.einsum("jatdc,hg->jahtgdc", w, eye).reshape(2, 2, A_KV_HEADS, CMP_STRIDE * LANES, CMP_HIDDEN)


def _compress(kseg, vseg, pos_flat, w1, w2, bsz):
    nseg = kseg.shape[0] // bsz
    segw = kseg.shape[1]
    seg_spec = pl.BlockSpec((nseg, segw), lambda b: (b, 0))
    return pl.pallas_call(
        _compress_kernel,
        grid=(bsz,),
        in_specs=[seg_spec, seg_spec, _resident((2, 1, CMP_LEN * HEAD_DIM)),
                  _resident((2, CMP_LEN * HEAD_DIM, CMP_HIDDEN)),
                  _resident((2, 2, A_KV_HEADS, segw, CMP_HIDDEN)), _resident((2, CMP_HIDDEN, HEAD_DIM))],
        out_specs=[pl.BlockSpec((None, nseg, LANES), lambda b: (b, 0, 0)),
                   pl.BlockSpec((None, LANES, nseg), lambda b: (b, 0, 0))],
        out_shape=[jax.ShapeDtypeStruct((bsz, nseg, LANES), MXU_DTYPE),
                   jax.ShapeDtypeStruct((bsz, LANES, nseg), MXU_DTYPE)],
        compiler_params=_params(("parallel",)),
        name="nsa_compress",
    )(kseg, vseg, pos_flat.astype(MXU_DTYPE), w1.astype(MXU_DTYPE), _expand_w1(w1).astype(MXU_DTYPE),
      w2.astype(MXU_DTYPE))


def _split3(x):
    hi = x.astype(MXU_DTYPE)
    r1 = x - _f32(hi)
    mid = r1.astype(MXU_DTYPE)
    lo = (r1 - _f32(mid)).astype(MXU_DTYPE)
    return hi, mid, lo


def _col_max(s):
    return jnp.max(s, axis=0, keepdims=True)


def _tile_lanes(x, n):
    return jnp.concatenate([x] * n, axis=1)


def _pad_head_rows(x, h, fill):
    other = jnp.full(x.shape, fill, x.dtype)
    return jnp.concatenate([x, other] if h == 0 else [other, x], axis=0)


def _values_with_ones(vt):
    return jnp.concatenate([vt, jnp.ones((2 * SUBLANES, vt.shape[1]), vt.dtype)], axis=0)


def _store_transposed_pairs(o_ref, heads_t, row0=0):
    for p in range(len(heads_t) // 2):
        pair = jnp.concatenate([heads_t[2 * p], heads_t[2 * p + 1]], axis=0)
        o_ref[row0:row0 + QBLK, p * LANES:(p + 1) * LANES] = pair.T.astype(o_ref.dtype)


def _nsa_kernel(qt_ref, k_ref, vt_ref, kc_ref, vct_ref, g_ref, c2s_ref, o_ref,
                selb_ref, sca_ref, scb_ref, cmp_ref, win_ref, diag_ref, *, seq):
    i = pl.program_id(1)
    q0 = i * QBLK
    n_cmp = kc_ref.shape[0]
    n_sel = seq // SEL_LEN
    kc_sz = min(SEL_CHUNK, seq)
    span = min(A_WINDOW + QBLK, seq)
    blocks_per_chunk = kc_sz // SEL_LEN

    gates_t = g_ref[...].T
    qp_lane = q0 + lax.broadcasted_iota(jnp.int32, (1, QBLK), 1)

    cmp_end = lax.broadcasted_iota(jnp.int32, (n_cmp, 1), 0) * CMP_STRIDE + (CMP_LEN - 1)
    bias_c = _tile_lanes(jnp.where(cmp_end <= qp_lane, 0.0, _NEG_INF), A_GROUP)

    w0 = pl.multiple_of(jnp.maximum(q0 + QBLK - span, 0), QBLK)
    diff = qp_lane - (w0 + lax.broadcasted_iota(jnp.int32, (span, 1), 0))
    bias_w = _tile_lanes(jnp.where(diff >= 0, jnp.where(diff < A_WINDOW, 0.0, _NEG_INF), _NEG_INF), A_GROUP)

    w_qs = []
    for h in range(A_KV_HEADS):
        q_t = jnp.concatenate([qt_ref[(h * A_GROUP + g) * HEAD_DIM:(h * A_GROUP + g + 1) * HEAD_DIM, :]
                               for g in range(A_GROUP)], axis=1)
        w_qs.append(_pad_head_rows(q_t, h, 0.0))

    n_chunks = (q0 + kc_sz - 1) // kc_sz
    last = jnp.maximum(n_chunks - 1, 0)

    def issue_scores(c, dst_ref):
        k0 = pl.multiple_of(jnp.minimum(c, last) * kc_sz, kc_sz)
        keys = k_ref[pl.ds(k0, kc_sz), :LANES]
        for h in range(A_KV_HEADS):
            dst_ref[h] = _mm(keys, w_qs[h])

    for h in range(A_KV_HEADS):
        cmp_ref[h] = _mm(kc_ref[...], w_qs[h])
    for h in range(A_KV_HEADS):
        win_ref[h] = _mm(k_ref[pl.ds(w0, span), LANES:], w_qs[h])
    issue_scores(0, sca_ref)
    for h in range(A_KV_HEADS):
        diag_ref[h] = _mm(k_ref[pl.ds(pl.multiple_of(q0, QBLK), QBLK), :LANES], w_qs[h])

    o_cs, o_ws = [], []
    for h in range(A_KV_HEADS):
        s = cmp_ref[h] + bias_c
        m = _col_max(s)
        m = jnp.where(m == _NEG_INF, 0.0, m)
        e = jnp.exp2(s - m)
        p = e * (1.0 / jnp.maximum(jnp.sum(e, axis=0, keepdims=True), 1e-30))
        o_cs.append(_mm(vct_ref[...], p)[h * HEAD_DIM:(h + 1) * HEAD_DIM])

        p_sum = p[:, 0:QBLK]
        for g in range(1, A_GROUP):
            p_sum = p_sum + p[:, g * QBLK:(g + 1) * QBLK]
        imp_t = sum(_mm(c2s_ref[...], part) for part in _split3(p_sum))
        blk = lax.broadcasted_iota(jnp.int32, (n_sel, QBLK), 0)
        cur = jnp.right_shift(qp_lane, 6)
        forced = (blk == 0) | (blk == cur) | (blk == cur - 1)
        score = jnp.where(forced, FORCED_SCORE, imp_t)
        score = jnp.where(blk <= cur, score, _NEG_INF)
        n_grp = n_sel // SUBLANES
        sub = lax.broadcasted_iota(jnp.int32, (SUBLANES, QBLK), 0)
        score_grp = [score[SUBLANES * v:SUBLANES * (v + 1), :] for v in range(n_grp)]
        rank_grp = [jnp.zeros((SUBLANES, QBLK), jnp.float32) for _ in range(n_grp)]
        for j in range(n_sel):
            row = jnp.broadcast_to(score[j:j + 1, :], (SUBLANES, QBLK))
            vj, rj = divmod(j, SUBLANES)
            for v in range(n_grp):
                if v > vj:
                    inc = jnp.where(row >= score_grp[v], 1.0, 0.0)
                elif v < vj:
                    inc = jnp.where(row > score_grp[v], 1.0, 0.0)
                else:
                    inc = jnp.where(sub > rj, jnp.where(row >= score_grp[v], 1.0, 0.0),
                                    jnp.where(row > score_grp[v], 1.0, 0.0))
                rank_grp[v] = rank_grp[v] + inc
        rank = jnp.concatenate(rank_grp, axis=0)
        picked = jnp.where(rank < float(min(SEL_TOPK, n_sel)), 0.0, _NEG_INF)
        selb_ref[h] = jnp.where(blk < jnp.right_shift(q0, 6), picked, _NEG_INF)

    for h in range(A_KV_HEADS):
        sw = win_ref[h] + bias_w
        pw = jnp.exp2(sw - _col_max(sw))
        acc_w = _mm(_values_with_ones(vt_ref[LANES + h * HEAD_DIM:LANES + (h + 1) * HEAD_DIM, pl.ds(w0, span)]), pw)
        o_ws.append(acc_w[:HEAD_DIM] / acc_w[HEAD_DIM:HEAD_DIM + 1])

    def consume(c, src_ref, states):
        cc = jnp.minimum(c, last)
        k0 = pl.multiple_of(cc * kc_sz, kc_sz)
        out = []
        for h in range(A_KV_HEADS):
            m_run, acc = states[h]
            rows = [_tile_lanes(jnp.where(c < n_chunks, selb_ref[h, pl.ds(cc * blocks_per_chunk + jj, 1), :], _NEG_INF),
                                A_GROUP) for jj in range(blocks_per_chunk)]
            blocks = [src_ref[h, jj * SEL_LEN:(jj + 1) * SEL_LEN, :] for jj in range(blocks_per_chunk)]
            part = None
            for sc_blk, row in zip(blocks, rows):
                blk_max = jnp.max(sc_blk.reshape(SEL_LEN // SUBLANES, SUBLANES, A_GROUP * QBLK), axis=0) + row
                part = blk_max if part is None else jnp.maximum(part, blk_max)
            m_new = jnp.maximum(m_run, _col_max(part))
            pr = jnp.concatenate([jnp.exp2(sc_blk + (row - m_new)) for sc_blk, row in zip(blocks, rows)], axis=0)
            vt = _values_with_ones(vt_ref[h * HEAD_DIM:(h + 1) * HEAD_DIM, pl.ds(k0, kc_sz)])
            out.append((m_new, jnp.exp2(m_run - m_new) * acc + _mm(vt, pr)))
        return tuple(out)

    def sel_step(t, states):
        issue_scores(2 * t + 1, scb_ref)
        states = consume(2 * t, sca_ref, states)
        issue_scores(2 * t + 2, sca_ref)
        return consume(2 * t + 1, scb_ref, states)

    kp_diag = q0 + lax.broadcasted_iota(jnp.int32, (QBLK, 1), 0)
    bias_d = _tile_lanes(jnp.where(kp_diag <= qp_lane, 0.0, _NEG_INF), A_GROUP)
    q0_al = pl.multiple_of(q0, QBLK)
    init = []
    for h in range(A_KV_HEADS):
        sd = diag_ref[h] + bias_d
        m0 = _col_max(sd)
        vt = _values_with_ones(vt_ref[h * HEAD_DIM:(h + 1) * HEAD_DIM, pl.ds(q0_al, QBLK)])
        init.append((m0, _mm(vt, jnp.exp2(sd - m0))))
    init = tuple(init)
    sel_out = lax.fori_loop(0, (n_chunks + 1) // 2, sel_step, init)

    heads_t = []
    for h in range(A_KV_HEADS):
        acc_s = sel_out[h][1]
        o_s = acc_s[:HEAD_DIM] / jnp.maximum(acc_s[HEAD_DIM:HEAD_DIM + 1], 1e-30)
        for g in range(A_GROUP):
            hd = h * A_GROUP + g
            cols = slice(g * QBLK, (g + 1) * QBLK)
            heads_t.append(gates_t[hd:hd + 1, :] * o_cs[h][:, cols]
                           + gates_t[A_HEADS + hd:A_HEADS + hd + 1, :] * o_s[:, cols]
                           + gates_t[2 * A_HEADS + hd:2 * A_HEADS + hd + 1, :] * o_ws[h][:, cols])
    _store_transposed_pairs(o_ref, heads_t)


def _nsa_attention(qat, ka, vat, kc, vct, ga, bsz, seq):
    nblk = seq // QBLK
    n_seg = seq // CMP_STRIDE
    n_sel = seq // SEL_LEN
    assert seq >= A_WINDOW + QBLK and seq % SEL_CHUNK == 0
    c_start = jnp.arange(n_seg) * CMP_STRIDE
    s_start = jnp.arange(n_sel) * SEL_LEN
    overlap = (jnp.minimum(c_start[None, :] + CMP_LEN, s_start[:, None] + SEL_LEN)
               - jnp.maximum(c_start[None, :], s_start[:, None]))
    c2s = (jnp.clip(overlap, 0, None).astype(jnp.float32) / CMP_LEN).astype(MXU_DTYPE)
    return pl.pallas_call(
        functools.partial(_nsa_kernel, seq=seq),
        grid=(bsz, nblk),
        in_specs=[
            pl.BlockSpec((None, A_WIDTH, QBLK), lambda b, i: (b, 0, i)),
            pl.BlockSpec((seq, 2 * LANES), lambda b, i: (b, 0)),
            pl.BlockSpec((None, 2 * LANES, seq), lambda b, i: (b, 0, 0)),
            pl.BlockSpec((None, n_seg, LANES), lambda b, i: (b, 0, 0)),
            pl.BlockSpec((None, LANES, n_seg), lambda b, i: (b, 0, 0)),
            pl.BlockSpec((QBLK, LANES), lambda b, i: (b * nblk + i, 0)),
            _resident((n_sel, n_seg)),
        ],
        out_specs=pl.BlockSpec((QBLK, A_WIDTH), lambda b, i: (b * nblk + i, 0)),
        out_shape=jax.ShapeDtypeStruct((bsz * seq, A_WIDTH), MXU_DTYPE),
        scratch_shapes=[pltpu.VMEM((A_KV_HEADS, n_sel, QBLK), jnp.float32)]
        + [pltpu.VMEM((A_KV_HEADS, min(SEL_CHUNK, seq), A_GROUP * QBLK), jnp.float32)] * 2
        + [pltpu.VMEM((A_KV_HEADS, n_seg, A_GROUP * QBLK), jnp.float32),
           pltpu.VMEM((A_KV_HEADS, min(A_WINDOW + QBLK, seq), A_GROUP * QBLK), jnp.float32),
           pltpu.VMEM((A_KV_HEADS, QBLK, A_GROUP * QBLK), jnp.float32)],
        compiler_params=_params(("parallel", "arbitrary")),
        name="nsa_attention",
    )(qat, ka, vat, kc, vct, ga, c2s)


def _swa_kernel(sink_ref, qt_ref, kv_ref, vt_ref, o_ref, sc_ref, *, seq):
    i = pl.program_id(1)
    span = min(B_WINDOW + QBLK, seq)
    sink = jnp.concatenate([jnp.full((1, QBLK), sink_ref[hd] * LOG2E, jnp.float32) for hd in range(B_HEADS)],
                           axis=1)
    starts = []
    for sb in range(SWA_QSUB):
        q0 = (i * SWA_QSUB + sb) * QBLK
        w0 = pl.multiple_of(jnp.maximum(q0 + QBLK - span, 0), QBLK)
        starts.append((q0, w0))
        q_t = jnp.concatenate([qt_ref[hd * HEAD_DIM:(hd + 1) * HEAD_DIM, sb * QBLK:(sb + 1) * QBLK]
                               for hd in range(B_HEADS)], axis=1)
        sc_ref[sb] = _mm(kv_ref[pl.ds(w0, span), :], _pad_head_rows(q_t, 0, 0.0))
    for sb, (q0, w0) in enumerate(starts):
        qp_lane = q0 + lax.broadcasted_iota(jnp.int32, (1, QBLK), 1)
        diff = qp_lane - (w0 + lax.broadcasted_iota(jnp.int32, (span, 1), 0))
        bias = _tile_lanes(jnp.where(diff >= 0, jnp.where(diff < B_WINDOW, 0.0, _NEG_INF), _NEG_INF), B_HEADS)
        s = sc_ref[sb] + bias
        m = jnp.maximum(_col_max(s), sink)
        acc = _mm(_values_with_ones(vt_ref[:, pl.ds(w0, span)]), jnp.exp2(s - m))
        o_t = acc[:HEAD_DIM] / (acc[HEAD_DIM:HEAD_DIM + 1] + jnp.exp2(sink - m))
        _store_transposed_pairs(o_ref, [o_t[:, hd * QBLK:(hd + 1) * QBLK] for hd in range(B_HEADS)], sb * QBLK)


def _swa_attention(qbt, kvb, vbt, sinks, bsz, seq):
    rows = SWA_QSUB * QBLK
    nstep = seq // rows
    span = min(B_WINDOW + QBLK, seq)
    assert seq >= B_WINDOW + QBLK and seq % rows == 0
    return pl.pallas_call(
        functools.partial(_swa_kernel, seq=seq),
        grid=(bsz, nstep),
        in_specs=[
            pl.BlockSpec(memory_space=pltpu.SMEM),
            pl.BlockSpec((None, B_WIDTH, rows), lambda b, i: (b, 0, i)),
            pl.BlockSpec((seq, LANES), lambda b, i: (b, 0)),
            pl.BlockSpec((None, HEAD_DIM, seq), lambda b, i: (b, 0, 0)),
        ],
        out_specs=pl.BlockSpec((rows, B_WIDTH), lambda b, i: (b * nstep + i, 0)),
        out_shape=jax.ShapeDtypeStruct((bsz * seq, B_WIDTH), MXU_DTYPE),
        scratch_shapes=[pltpu.VMEM((SWA_QSUB, span, B_HEADS * QBLK), jnp.float32)],
        compiler_params=_params(("parallel", "arbitrary")),
        name="swa_attention",
    )(sinks, qbt, kvb, vbt)


def _s5_discretize_kernel(are_ref, aim_ref, logdt_ref, bre_ref, bim_ref,
                          abr_ref, abi_ref, bbr_ref, bbi_ref):
    ar = are_ref[...]
    ai = aim_ref[...]
    dt = jnp.exp(logdt_ref[...])
    mag = jnp.exp(dt * ar)
    abr = mag * jnp.cos(dt * ai)
    abi = mag * jnp.sin(dt * ai)
    den = ar * ar + ai * ai
    nr = abr - 1.0
    coef_r = (nr * ar + abi * ai) / den
    coef_i = (abi * ar - nr * ai) / den
    abr_ref[...] = abr
    abi_ref[...] = abi
    br = bre_ref[...]
    bi = bim_ref[...]
    bbr_ref[...] = coef_r[:, None, :] * br - coef_i[:, None, :] * bi
    bbi_ref[...] = coef_r[:, None, :] * bi + coef_i[:, None, :] * br


def _s5_discretize(a_re, a_im, log_dt, b_re, b_im):
    lg = a_re.shape[0] * a_re.shape[1]
    a2 = lambda a: a.reshape(lg, C_STATE)
    b3 = lambda b: jnp.swapaxes(b.reshape(lg, C_STATE, C_GROUP_CH), 1, 2)
    sa = jax.ShapeDtypeStruct((lg, C_STATE), jnp.float32)
    sb = jax.ShapeDtypeStruct((lg, C_GROUP_CH, C_STATE), jnp.float32)
    return pl.pallas_call(_s5_discretize_kernel, out_shape=[sa, sa, sb, sb], name="s5_discretize")(
        a2(a_re), a2(a_im), log_dt.reshape(lg, 1), b3(b_re), b3(b_im))


def _s5_kernel(u_ref, bmat_ref, abar_ref, cmat_ref, d_ref, gw_ref, gb_ref, y_ref, state_ref, ut_ref, xs_ref):
    bsz, t_steps, _ = u_ref.shape

    @pl.when(pl.program_id(0) == 0)
    def _():
        state_ref[...] = jnp.zeros_like(state_ref)

    def gather(t, carry):
        ut_ref[pl.ds(pl.multiple_of(t * bsz, bsz), bsz), :] = u_ref[:, t, :]
        return carry

    lax.fori_loop(0, t_steps, gather, 0, unroll=8)
    u = ut_ref[...]
    for hf in range(2):
        bu = _mm(u[:, hf * _HALF_CH:(hf + 1) * _HALF_CH], bmat_ref[hf])
        xs_ref[:, hf * _HALF_ST:(hf + 1) * _HALF_ST] = bu[:, :_HALF_ST]
        xs_ref[:, C_STATES + hf * _HALF_ST:C_STATES + (hf + 1) * _HALF_ST] = bu[:, _HALF_ST:]

    for cg in range(C_STATES // S5_LANES):
        re = slice(cg * S5_LANES, (cg + 1) * S5_LANES)
        im = slice(C_STATES + cg * S5_LANES, C_STATES + (cg + 1) * S5_LANES)
        ar = jnp.broadcast_to(abar_ref[0:1, re], (bsz, S5_LANES))
        ai = jnp.broadcast_to(abar_ref[0:1, im], (bsz, S5_LANES))

        def step(t, carry):
            xr, xi = carry
            r0 = pl.multiple_of(t * bsz, bsz)
            nr = ar * xr - ai * xi + xs_ref[pl.ds(r0, bsz), re]
            ni = ar * xi + ai * xr + xs_ref[pl.ds(r0, bsz), im]
            xs_ref[pl.ds(r0, bsz), re] = nr
            xs_ref[pl.ds(r0, bsz), im] = ni
            return nr, ni

        xr, xi = lax.fori_loop(0, t_steps, step, (state_ref[:, re], state_ref[:, im]), unroll=8)
        state_ref[:, re] = xr
        state_ref[:, im] = xi

    ys = []
    for hf in range(2):
        xr = xs_ref[:, hf * _HALF_ST:(hf + 1) * _HALF_ST]
        xi = xs_ref[:, C_STATES + hf * _HALF_ST:C_STATES + (hf + 1) * _HALF_ST]
        ys.append(_mm(xr, cmat_ref[hf, :_HALF_ST, :]) + _mm(xi, cmat_ref[hf, _HALF_ST:, :]))
    y = jnp.concatenate(ys, axis=1) + d_ref[...] * u
    z = jax.nn.gelu(y)
    out = z * jax.nn.sigmoid(_mm(z, gw_ref[...]) + gb_ref[...])
    y_ref[...] = out.reshape(t_steps, bsz, C_WIDTH).astype(y_ref.dtype)


def _s5(u, bmat, abar, cmat, d, glu_w, glu_b):
    bsz, seq, _ = u.shape
    t_steps = min(S5_CHUNK, seq)
    return pl.pallas_call(
        _s5_kernel,
        grid=(seq // t_steps,),
        in_specs=[
            pl.BlockSpec((bsz, t_steps, C_WIDTH), lambda i: (0, i, 0)),
            _resident((2, _HALF_CH, 2 * _HALF_ST)), _resident((1, 2 * C_STATES)), _resident((2, 2 * _HALF_ST, _HALF_CH)),
            _resident((1, C_WIDTH)), _resident((C_WIDTH, C_WIDTH)), _resident((1, C_WIDTH)),
        ],
        out_specs=pl.BlockSpec((t_steps, bsz, C_WIDTH), lambda i: (i, 0, 0)),
        out_shape=jax.ShapeDtypeStruct((seq, bsz, C_WIDTH), MXU_DTYPE),
        scratch_shapes=[pltpu.VMEM((bsz, 2 * C_STATES), jnp.float32),
                        pltpu.VMEM((t_steps * bsz, C_WIDTH), jnp.float32),
                        pltpu.VMEM((t_steps * bsz, 2 * C_STATES), jnp.float32)],
        compiler_params=_params(("arbitrary",)),
        name="s5_scan",
    )(u, bmat, abar, cmat, d.reshape(1, C_WIDTH), glu_w.astype(MXU_DTYPE), glu_b.reshape(1, C_WIDTH))


def _block_diag(per_group):
    g, r, c = per_group.shape
    eye = jnp.eye(g, dtype=per_group.dtype)
    return jnp.einsum("grc,gk->grkc", per_group, eye).reshape(g * r, g * c)


def _merge_kernel(x_ref, ya_ref, yb_ref, yc_ref, gain_ref, wg_ref, pa_ref, pb_ref, pc_ref, wo_ref, o_ref):
    x = x_ref[...]
    h = _rmsnorm(x, gain_ref[...]).astype(MXU_DTYPE)
    merged = None
    for j, (y_ref, p_ref) in enumerate(((ya_ref, pa_ref), (yb_ref, pb_ref), (yc_ref, pc_ref))):
        gate = jax.nn.sigmoid(jnp.dot(h, wg_ref[:, j * D_MODEL:(j + 1) * D_MODEL],
                                      preferred_element_type=jnp.float32))
        term = gate * jnp.dot(y_ref[...], p_ref[...], preferred_element_type=jnp.float32)
        merged = term if merged is None else merged + term
    o_ref[...] = x + _mm(merged, wo_ref[...])


def _merge(x2, ya, yb, yc, gain, wg, pa, pb, pc, wo):
    n = x2.shape[0]
    tm = min(ROW_TILE, n)

    def rows(width):
        return pl.BlockSpec((tm, width), lambda i: (i, 0))

    return pl.pallas_call(
        _merge_kernel,
        grid=(n // tm,),
        in_specs=[rows(D_MODEL), rows(A_WIDTH), rows(B_WIDTH), rows(C_WIDTH), _resident((1, D_MODEL)),
                  _resident((D_MODEL, N_BRANCH * D_MODEL)), _resident((A_WIDTH, D_MODEL)),
                  _resident((B_WIDTH, D_MODEL)), _resident((C_WIDTH, D_MODEL)), _resident((D_MODEL, D_MODEL))],
        out_specs=rows(D_MODEL),
        out_shape=jax.ShapeDtypeStruct((n, D_MODEL), jnp.float32),
        compiler_params=_params(("parallel",)),
        name="merge",
    )(x2, ya, yb, yc, gain.reshape(1, D_MODEL), wg, pa, pb, pc, wo)


def _mlp_kernel(x_ref, gain_ref, wu_ref, wd_ref, fgain_ref, o_ref, *, final_norm):
    x = x_ref[...]
    h = _rmsnorm(x, gain_ref[...]).astype(MXU_DTYPE)
    acc = x
    for c in range(D_FF // FF_CHUNK):
        cols = slice(c * FF_CHUNK, (c + 1) * FF_CHUNK)
        up = jnp.dot(h, wu_ref[:, cols], preferred_element_type=jnp.float32)
        acc = acc + _mm(jnp.square(jnp.maximum(up, 0.0)), wd_ref[cols, :])
    o_ref[...] = _rmsnorm(acc, fgain_ref[...]) if final_norm else acc


def _mlp(x2, gain, wu, wd, fgain, final_norm):
    n = x2.shape[0]
    tm = min(ROW_TILE, n)
    rows = pl.BlockSpec((tm, D_MODEL), lambda i: (i, 0))
    return pl.pallas_call(
        functools.partial(_mlp_kernel, final_norm=final_norm),
        grid=(n // tm,),
        in_specs=[rows, _resident((1, D_MODEL)), _resident((D_MODEL, D_FF)), _resident((D_FF, D_MODEL)),
                  _resident((1, D_MODEL))],
        out_specs=rows,
        out_shape=jax.ShapeDtypeStruct((n, D_MODEL), jnp.float32),
        compiler_params=_params(("parallel",)),
        name="mlp",
    )(x2, gain.reshape(1, D_MODEL), wu, wd, fgain.reshape(1, D_MODEL))


def kernel(x, positions, norm_mix, w_in, nsa_cmp_pos, nsa_cmp_w1, nsa_cmp_w2, swa_sinks, s5_a_re, s5_a_im, s5_log_dt, s5_b_re, s5_b_im, s5_c_re, s5_c_im, s5_d, s5_glu_w, s5_glu_b, w_branch_a, w_branch_b, w_branch_c, w_out, norm_mlp, w_mlp_up, w_mlp_down, norm_final):
    bsz, seq, _ = x.shape
    depth = w_in.shape[0]
    n = bsz * seq
    gate_col = sum(IN_WIDTHS[:-1])
    bf = lambda w: w.astype(MXU_DTYPE)

    tables = _rope_tables(positions)
    abr, abi, bbr, bbi = _s5_discretize(s5_a_re, s5_a_im, s5_log_dt, s5_b_re, s5_b_im)
    abr = abr.reshape(depth, 1, C_STATES)
    abi = abi.reshape(depth, 1, C_STATES)
    bbr = bbr.reshape(depth, C_GROUPS, C_GROUP_CH, C_STATE)
    bbi = bbi.reshape(depth, C_GROUPS, C_GROUP_CH, C_STATE)

    x2 = x.reshape(n, D_MODEL)
    for l in range(depth):
        qat, qbt, vat, vbt, ka, kvb, kseg, vseg, ga, uc = _in_projection(
            x2, norm_mix[l], *_pack_w_in(w_in[l]), tables, bsz, seq)

        kc, vct = _compress(kseg, vseg, nsa_cmp_pos[l].reshape(2, 1, CMP_LEN * HEAD_DIM), nsa_cmp_w1[l],
                            nsa_cmp_w2[l], bsz)
        ya = _nsa_attention(qat, ka, vat, kc, vct, ga, bsz, seq)

        yb = _swa_attention(qbt, kvb, vbt, swa_sinks[l], bsz, seq)

        cre_t = jnp.swapaxes(s5_c_re[l], 1, 2)
        cim_t = jnp.swapaxes(s5_c_im[l], 1, 2)
        halves = (slice(0, C_GROUPS // 2), slice(C_GROUPS // 2, C_GROUPS))
        bmat = jnp.stack([jnp.concatenate([_block_diag(bbr[l][g]), _block_diag(bbi[l][g])], axis=1)
                          for g in halves])
        cmat = jnp.stack([jnp.concatenate([_block_diag(cre_t[g]), -_block_diag(cim_t[g])], axis=0)
                          for g in halves])
        abar = jnp.concatenate([abr[l], abi[l]], axis=1)
        yc_tm = _s5(uc.reshape(bsz, seq, C_WIDTH), bf(bmat), abar, bf(cmat), s5_d[l], s5_glu_w[l], s5_glu_b[l])
        yc = jnp.swapaxes(yc_tm, 0, 1).reshape(n, C_WIDTH)

        x2 = _merge(x2, ya, yb, yc, norm_mix[l], bf(w_in[l][:, gate_col:]), bf(w_branch_a[l]),
                    bf(w_branch_b[l]), bf(w_branch_c[l]), bf(w_out[l]))
        x2 = _mlp(x2, norm_mlp[l], bf(w_mlp_up[l]), bf(w_mlp_down[l]), norm_final, l == depth - 1)
    return x2.reshape(bsz, seq, D_MODEL)
```

```python
import functools

import jax
import jax.numpy as jnp
from jax import lax
from jax.experimental import pallas as pl
from jax.experimental.pallas import tpu as pltpu

D_MODEL = 1024
HEAD_DIM = 64
ROPE_THETA = 10000.0
NORM_EPS = 1e-6
QBLK = 128

A_HEADS = 8
A_KV_HEADS = 2
A_GROUP = A_HEADS // A_KV_HEADS
A_WIDTH = A_HEADS * HEAD_DIM
NSA_BRANCHES = 3
CMP_LEN = 32
CMP_STRIDE = 16
CMP_HIDDEN = 2 * HEAD_DIM
SEL_LEN = 64
SEL_TOPK = 16
A_WINDOW = 512
FORCED_SCORE = 1e4

B_HEADS = 8
B_WIDTH = B_HEADS * HEAD_DIM
B_WINDOW = 128

C_WIDTH = 512
C_GROUP_CH = 16
C_GROUPS = C_WIDTH // C_GROUP_CH
C_STATE = 64
C_STATES = C_GROUPS * C_STATE
_HALF_CH = C_WIDTH // 2
_HALF_ST = C_STATES // 2

D_FF = 4 * D_MODEL
N_BRANCH = 3

IN_WIDTHS = (A_WIDTH, NSA_BRANCHES * 2 * A_KV_HEADS * HEAD_DIM, NSA_BRANCHES * A_HEADS,
             B_WIDTH, 2 * HEAD_DIM, C_WIDTH, N_BRANCH * D_MODEL)

LANES = 128
SUBLANES = 8
VMEM_LIMIT = 56 * 1024 * 1024

MXU_DTYPE = jnp.bfloat16
ROW_TILE = 512
S5_CHUNK = 64
SEL_CHUNK = 256
SWA_QSUB = 4
FF_CHUNK = 1024

LOG2E = 1.4426950408889634
_NEG_INF = float("-inf")


def _f32(x):
    return x.astype(jnp.float32)


def _mm(a, b):
    return jnp.dot(a.astype(MXU_DTYPE), b.astype(MXU_DTYPE), preferred_element_type=jnp.float32)


def _mm_nt(a, b):
    return lax.dot_general(a.astype(MXU_DTYPE), b.astype(MXU_DTYPE), (((1,), (1,)), ((), ())),
                           preferred_element_type=jnp.float32)


def _rmsnorm(x, gain):
    return x * lax.rsqrt(jnp.mean(x * x, axis=-1, keepdims=True) + NORM_EPS) * gain


def _resident(shape):
    zeros = (0,) * len(shape)
    return pl.BlockSpec(shape, lambda *_: zeros, pipeline_mode=pl.Buffered(1))


def _resident_layer(shape, layer):
    zeros = (0,) * len(shape)
    return pl.BlockSpec((None,) + tuple(shape), lambda *_: (layer,) + zeros, pipeline_mode=pl.Buffered(1))


def _params(sem):
    return pltpu.CompilerParams(dimension_semantics=sem, vmem_limit_bytes=VMEM_LIMIT)


def _rope_table_kernel(pos_ref, invf_ref, sign_ref, cos_ref, sin_ref):
    ang = _f32(pos_ref[...]) * invf_ref[...]
    cos_ref[...] = jnp.cos(ang)
    sin_ref[...] = jnp.sin(ang) * sign_ref[...]


def _rope_tables(positions):
    n = positions.size
    half = HEAD_DIM // 2
    inv_freq = ROPE_THETA ** (-jnp.arange(half, dtype=jnp.float32) / half)
    invf = jnp.tile(inv_freq, LANES // half)
    sign = jnp.tile(jnp.concatenate([-jnp.ones(half, jnp.float32), jnp.ones(half, jnp.float32)]),
                    LANES // HEAD_DIM)
    tm = min(1024, n)
    natural = pl.pallas_call(
        _rope_table_kernel,
        grid=(n // tm,),
        in_specs=[pl.BlockSpec((tm, 1), lambda i: (i, 0)), _resident((1, LANES)), _resident((1, LANES))],
        out_specs=[pl.BlockSpec((tm, LANES), lambda i: (i, 0))] * 2,
        out_shape=[jax.ShapeDtypeStruct((n, LANES), jnp.float32)] * 2,
        compiler_params=_params(("parallel",)),
        name="rope_tables",
    )(positions.reshape(n, 1), invf.reshape(1, LANES), sign.reshape(1, LANES))
    transposed = pl.pallas_call(
        _rope_table_kernel,
        grid=(n // tm,),
        in_specs=[pl.BlockSpec((1, tm), lambda i: (0, i)), _resident((HEAD_DIM, 1)), _resident((HEAD_DIM, 1))],
        out_specs=[pl.BlockSpec((HEAD_DIM, tm), lambda i: (0, i))] * 2,
        out_shape=[jax.ShapeDtypeStruct((HEAD_DIM, n), jnp.float32)] * 2,
        compiler_params=_params(("parallel",)),
        name="rope_tables_t",
    )(positions.reshape(1, n), invf[:HEAD_DIM].reshape(HEAD_DIM, 1), sign[:HEAD_DIM].reshape(HEAD_DIM, 1))
    return natural + transposed


def _swap_halves(x):
    lane = lax.broadcasted_iota(jnp.int32, x.shape, 1)
    first = (lane & (HEAD_DIM - 1)) < HEAD_DIM // 2
    return jnp.where(first, pltpu.roll(x, LANES - HEAD_DIM // 2, 1), pltpu.roll(x, HEAD_DIM // 2, 1))


def _proj_kernel(x_ref, gain_ref, wn_ref, wt_ref, cos_ref, sin_ref, cost_ref, sint_ref,
                 qat_ref, qbt_ref, vat_ref, vbt_ref, ka_ref, kvb_ref, kseg_ref, vseg_ref, ga_ref, uc_ref,
                 seg_ref):
    h = _rmsnorm(x_ref[...], gain_ref[...]).astype(MXU_DTYPE)
    nat = jnp.dot(h, wn_ref[...], preferred_element_type=jnp.float32)
    tr = _mm_nt(wt_ref[...], h)

    cos = cos_ref[...]
    sin = sin_ref[...]

    def roped(j):
        xj = nat[:, j * LANES:(j + 1) * LANES]
        return xj, xj * cos + _swap_halves(xj) * sin

    for j in range(1, NSA_BRANCHES):
        ka_ref[:, (j - 1) * LANES:j * LANES] = roped(j)[1].astype(ka_ref.dtype)
    raw, rot = roped(3)
    lane = lax.broadcasted_iota(jnp.int32, raw.shape, 1)
    kvb_ref[...] = jnp.where(lane < HEAD_DIM, rot, raw).astype(kvb_ref.dtype)
    n_seg_tile = nat.shape[0] // CMP_STRIDE
    for j, (src, dst_ref) in enumerate(((roped(0)[1], kseg_ref), (nat[:, 4 * LANES:5 * LANES], vseg_ref))):
        seg_ref[j] = src
        for tok in range(CMP_STRIDE):
            dst_ref[:, tok * LANES:(tok + 1) * LANES] = (
                seg_ref[j, pl.ds(tok, n_seg_tile, stride=CMP_STRIDE), :].astype(dst_ref.dtype))
    ga_ref[...] = jax.nn.sigmoid(nat[:, 5 * LANES:6 * LANES])
    uc_ref[...] = nat[:, 6 * LANES:]

    cos_t = cost_ref[...]
    sin_t = sint_ref[...]
    half = HEAD_DIM // 2

    def roped_t(r0):
        xh = tr[r0:r0 + HEAD_DIM]
        return xh * cos_t + jnp.concatenate([xh[half:], xh[:half]], axis=0) * sin_t

    for hd in range(A_HEADS):
        qat_ref[hd * HEAD_DIM:(hd + 1) * HEAD_DIM, :] = roped_t(hd * HEAD_DIM).astype(qat_ref.dtype)
    for hd in range(B_HEADS):
        qbt_ref[hd * HEAD_DIM:(hd + 1) * HEAD_DIM, :] = roped_t(A_WIDTH + hd * HEAD_DIM).astype(qbt_ref.dtype)
    v0 = A_WIDTH + B_WIDTH
    vat_ref[...] = tr[v0:v0 + 2 * LANES].astype(vat_ref.dtype)
    vbt_ref[...] = tr[v0 + 2 * LANES:].astype(vbt_ref.dtype)


_NAT_COLS = 10 * LANES
_TR_ROWS = A_WIDTH + B_WIDTH + 2 * LANES + HEAD_DIM


def _pack_w_in(w_in):
    o = [0]
    for w in IN_WIDTHS:
        o.append(o[-1] + w)
    scale = HEAD_DIM ** -0.5 * LOG2E
    q_a = w_in[:, o[0]:o[1]] * scale
    kv_cols = [w_in[:, o[1] + j * LANES:o[1] + (j + 1) * LANES] for j in range(2 * NSA_BRANCHES)]
    k_a = jnp.concatenate(kv_cols[0::2], axis=1)
    v_a = jnp.concatenate(kv_cols[1::2], axis=1)
    g_a = jnp.pad(w_in[:, o[2]:o[3]], ((0, 0), (0, LANES - NSA_BRANCHES * A_HEADS)))
    q_b = w_in[:, o[3]:o[4]] * scale
    kv_b = w_in[:, o[4]:o[5]]
    u_c = w_in[:, o[5]:o[6]]
    w_nat = jnp.concatenate([k_a, kv_b, v_a[:, :LANES], g_a, u_c], axis=1)
    w_tr = jnp.concatenate([q_a, q_b, v_a[:, LANES:], kv_b[:, HEAD_DIM:]], axis=1).T
    return w_nat.astype(MXU_DTYPE), w_tr.astype(MXU_DTYPE)


def _in_projection(x2, gain, w_nat, w_tr, tables, bsz, seq):
    n = x2.shape[0]
    tm = min(ROW_TILE, seq)
    tiles = seq // tm

    def rows(width):
        return pl.BlockSpec((tm, width), lambda i: (i, 0))

    def lanes(height):
        return pl.BlockSpec((None, height, tm), lambda i: (i // tiles, 0, i % tiles))

    table_t = pl.BlockSpec((HEAD_DIM, tm), lambda i: (0, i))
    t_heights = (A_WIDTH, B_WIDTH, 2 * LANES, HEAD_DIM)
    widths = (2 * LANES, LANES, LANES, C_WIDTH)
    dtypes = (MXU_DTYPE, MXU_DTYPE, jnp.float32, jnp.float32)
    seg_rows = tm // CMP_STRIDE
    seg_spec = pl.BlockSpec((seg_rows, CMP_STRIDE * LANES), lambda i: (i, 0))
    seg_shape = jax.ShapeDtypeStruct((n // CMP_STRIDE, CMP_STRIDE * LANES), MXU_DTYPE)
    return pl.pallas_call(
        _proj_kernel,
        grid=(n // tm,),
        in_specs=[rows(D_MODEL), _resident((1, D_MODEL)), _resident((D_MODEL, _NAT_COLS)),
                  _resident((_TR_ROWS, D_MODEL)), rows(LANES), rows(LANES), table_t, table_t],
        out_specs=[lanes(hh) for hh in t_heights] + [rows(w) for w in widths[:2]] + [seg_spec, seg_spec]
        + [rows(w) for w in widths[2:]],
        out_shape=[jax.ShapeDtypeStruct((bsz, hh, seq), MXU_DTYPE) for hh in t_heights]
        + [jax.ShapeDtypeStruct((n, w), dt) for w, dt in zip(widths[:2], dtypes[:2])] + [seg_shape, seg_shape]
        + [jax.ShapeDtypeStruct((n, w), dt) for w, dt in zip(widths[2:], dtypes[2:])],
        scratch_shapes=[pltpu.VMEM((2, tm, LANES), jnp.float32)],
        compiler_params=_params(("parallel",)),
        name="in_projection",
    )(x2, gain.reshape(1, D_MODEL), w_nat, w_tr, *tables)


def _compress_kernel(kseg_ref, vseg_ref, pos_ref, w1_ref, w1x_ref, w2_ref, kc_ref, vct_ref):
    nseg = kseg_ref.shape[0]
    both = []
    for j, seg_ref in enumerate((kseg_ref, vseg_ref)):
        seg = seg_ref[...]
        pos_term = _mm(pos_ref[j], w1_ref[j])
        heads = []
        for h in range(A_KV_HEADS):
            hidden = (_mm(seg, w1x_ref[j, 0, h]) + pltpu.roll(_mm(seg, w1x_ref[j, 1, h]), nseg - 1, 0)
                      + pos_term)
            heads.append(_mm(jax.nn.gelu(hidden), w2_ref[j]))
        both.append(jnp.concatenate(heads, axis=1))
    kc_ref[...] = both[0].astype(kc_ref.dtype)
    vct_ref[...] = both[1].T.astype(vct_ref.dtype)


def _expand_w1(w1):
    w = w1.reshape(2, 2, CMP_STRIDE, HEAD_DIM, CMP_HIDDEN)
    eye = jnp.eye(A_KV_HEADS, dtype=w1.dtype)
    return jnp.einsum("jatdc,hg->jahtgdc", w, eye).reshape(2, 2, A_KV_HEADS, CMP_STRIDE * LANES, CMP_HIDDEN)


def _compress(kseg, vseg, pos_flat, w1, w2, bsz):
    nseg = kseg.shape[0] // bsz
    segw = kseg.shape[1]
    seg_spec = pl.BlockSpec((nseg, segw), lambda b: (b, 0))
    return pl.pallas_call(
        _compress_kernel,
        grid=(bsz,),
        in_specs=[seg_spec, seg_spec, _resident((2, 1, CMP_LEN * HEAD_DIM)),
                  _resident((2, CMP_LEN * HEAD_DIM, CMP_HIDDEN)),
                  _resident((2, 2, A_KV_HEADS, segw, CMP_HIDDEN)), _resident((2, CMP_HIDDEN, HEAD_DIM))],
        out_specs=[pl.BlockSpec((None, nseg, LANES), lambda b: (b, 0, 0)),
                   pl.BlockSpec((None, LANES, nseg), lambda b: (b, 0, 0))],
        out_shape=[jax.ShapeDtypeStruct((bsz, nseg, LANES), MXU_DTYPE),
                   jax.ShapeDtypeStruct((bsz, LANES, nseg), MXU_DTYPE)],
        compiler_params=_params(("parallel",)),
        name="nsa_compress",
    )(kseg, vseg, pos_flat.astype(MXU_DTYPE), w1.astype(MXU_DTYPE), _expand_w1(w1).astype(MXU_DTYPE),
      w2.astype(MXU_DTYPE))


def _split3(x):
    hi = x.astype(MXU_DTYPE)
    r1 = x - _f32(hi)
    mid = r1.astype(MXU_DTYPE)
    lo = (r1 - _f32(mid)).astype(MXU_DTYPE)
    return hi, mid, lo


def _col_max(s):
    return jnp.max(s, axis=0, keepdims=True)


def _tile_lanes(x, n):
    return jnp.concatenate([x] * n, axis=1)


def _pad_head_rows(x, h, fill):
    other = jnp.full(x.shape, fill, x.dtype)
    return jnp.concatenate([x, other] if h == 0 else [other, x], axis=0)


def _values_with_ones(vt):
    return jnp.concatenate([vt, jnp.ones((2 * SUBLANES, vt.shape[1]), vt.dtype)], axis=0)


def _store_transposed_pairs(o_ref, heads_t, row0=0):
    for p in range(len(heads_t) // 2):
        pair = jnp.concatenate([heads_t[2 * p], heads_t[2 * p + 1]], axis=0)
        o_ref[row0:row0 + QBLK, p * LANES:(p + 1) * LANES] = pair.T.astype(o_ref.dtype)


def _nsa_kernel(qt_ref, k_ref, vt_ref, kc_ref, vct_ref, g_ref, c2s_ref, o_ref,
                selb_ref, sca_ref, scb_ref, cmp_ref, win_ref, diag_ref, *, seq):
    i = pl.program_id(1)
    q0 = i * QBLK
    n_cmp = kc_ref.shape[0]
    n_sel = seq // SEL_LEN
    kc_sz = min(SEL_CHUNK, seq)
    span = min(A_WINDOW + QBLK, seq)
    blocks_per_chunk = kc_sz // SEL_LEN

    gates_t = g_ref[...].T
    qp_lane = q0 + lax.broadcasted_iota(jnp.int32, (1, QBLK), 1)

    cmp_end = lax.broadcasted_iota(jnp.int32, (n_cmp, 1), 0) * CMP_STRIDE + (CMP_LEN - 1)
    bias_c = _tile_lanes(jnp.where(cmp_end <= qp_lane, 0.0, _NEG_INF), A_GROUP)

    w0 = pl.multiple_of(jnp.maximum(q0 + QBLK - span, 0), QBLK)
    diff = qp_lane - (w0 + lax.broadcasted_iota(jnp.int32, (span, 1), 0))
    bias_w = _tile_lanes(jnp.where(diff >= 0, jnp.where(diff < A_WINDOW, 0.0, _NEG_INF), _NEG_INF), A_GROUP)

    w_qs = []
    for h in range(A_KV_HEADS):
        q_t = jnp.concatenate([qt_ref[(h * A_GROUP + g) * HEAD_DIM:(h * A_GROUP + g + 1) * HEAD_DIM, :]
                               for g in range(A_GROUP)], axis=1)
        w_qs.append(_pad_head_rows(q_t, h, 0.0))

    n_chunks = (q0 + kc_sz - 1) // kc_sz
    last = jnp.maximum(n_chunks - 1, 0)

    def issue_scores(c, dst_ref):
        k0 = pl.multiple_of(jnp.minimum(c, last) * kc_sz, kc_sz)
        keys = k_ref[pl.ds(k0, kc_sz), :LANES]
        for h in range(A_KV_HEADS):
            dst_ref[h] = _mm(keys, w_qs[h])

    for h in range(A_KV_HEADS):
        cmp_ref[h] = _mm(kc_ref[...], w_qs[h])
    for h in range(A_KV_HEADS):
        win_ref[h] = _mm(k_ref[pl.ds(w0, span), LANES:], w_qs[h])
    issue_scores(0, sca_ref)
    for h in range(A_KV_HEADS):
        diag_ref[h] = _mm(k_ref[pl.ds(pl.multiple_of(q0, QBLK), QBLK), :LANES], w_qs[h])

    o_cs, o_ws = [], []
    for h in range(A_KV_HEADS):
        s = cmp_ref[h] + bias_c
        m = _col_max(s)
        m = jnp.where(m == _NEG_INF, 0.0, m)
        e = jnp.exp2(s - m)
        p = e * (1.0 / jnp.maximum(jnp.sum(e, axis=0, keepdims=True), 1e-30))
        o_cs.append(_mm(vct_ref[...], p)[h * HEAD_DIM:(h + 1) * HEAD_DIM])

        p_sum = p[:, 0:QBLK]
        for g in range(1, A_GROUP):
            p_sum = p_sum + p[:, g * QBLK:(g + 1) * QBLK]
        imp_t = sum(_mm(c2s_ref[...], part) for part in _split3(p_sum))
        blk = lax.broadcasted_iota(jnp.int32, (n_sel, QBLK), 0)
        cur = jnp.right_shift(qp_lane, 6)
        forced = (blk == 0) | (blk == cur) | (blk == cur - 1)
        score = jnp.where(forced, FORCED_SCORE, imp_t)
        score = jnp.where(blk <= cur, score, _NEG_INF)
        n_grp = n_sel // SUBLANES
        sub = lax.broadcasted_iota(jnp.int32, (SUBLANES, QBLK), 0)
        score_grp = [score[SUBLANES * v:SUBLANES * (v + 1), :] for v in range(n_grp)]
        rank_grp = [jnp.zeros((SUBLANES, QBLK), jnp.float32) for _ in range(n_grp)]
        for j in range(n_sel):
            row = jnp.broadcast_to(score[j:j + 1, :], (SUBLANES, QBLK))
            vj, rj = divmod(j, SUBLANES)
            for v in range(n_grp):
                if v > vj:
                    inc = jnp.where(row >= score_grp[v], 1.0, 0.0)
                elif v < vj:
                    inc = jnp.where(row > score_grp[v], 1.0, 0.0)
                else:
                    inc = jnp.where(sub > rj, jnp.where(row >= score_grp[v], 1.0, 0.0),
                                    jnp.where(row > score_grp[v], 1.0, 0.0))
                rank_grp[v] = rank_grp[v] + inc
        rank = jnp.concatenate(rank_grp, axis=0)
        picked = jnp.where(rank < float(min(SEL_TOPK, n_sel)), 0.0, _NEG_INF)
        selb_ref[h] = jnp.where(blk < jnp.right_shift(q0, 6), picked, _NEG_INF)

    for h in range(A_KV_HEADS):
        sw = win_ref[h] + bias_w
        pw = jnp.exp2(sw - _col_max(sw))
        acc_w = _mm(_values_with_ones(vt_ref[LANES + h * HEAD_DIM:LANES + (h + 1) * HEAD_DIM, pl.ds(w0, span)]), pw)
        o_ws.append(acc_w[:HEAD_DIM] / acc_w[HEAD_DIM:HEAD_DIM + 1])

    def consume(c, src_ref, states):
        cc = jnp.minimum(c, last)
        k0 = pl.multiple_of(cc * kc_sz, kc_sz)
        out = []
        for h in range(A_KV_HEADS):
            m_run, acc = states[h]
            rows = [_tile_lanes(jnp.where(c < n_chunks, selb_ref[h, pl.ds(cc * blocks_per_chunk + jj, 1), :], _NEG_INF),
                                A_GROUP) for jj in range(blocks_per_chunk)]
            blocks = [src_ref[h, jj * SEL_LEN:(jj + 1) * SEL_LEN, :] for jj in range(blocks_per_chunk)]
            part = None
            for sc_blk, row in zip(blocks, rows):
                blk_max = jnp.max(sc_blk.reshape(SEL_LEN // SUBLANES, SUBLANES, A_GROUP * QBLK), axis=0) + row
                part = blk_max if part is None else jnp.maximum(part, blk_max)
            m_new = jnp.maximum(m_run, _col_max(part))
            pr = jnp.concatenate([jnp.exp2(sc_blk + (row - m_new)) for sc_blk, row in zip(blocks, rows)], axis=0)
            vt = _values_with_ones(vt_ref[h * HEAD_DIM:(h + 1) * HEAD_DIM, pl.ds(k0, kc_sz)])
            out.append((m_new, jnp.exp2(m_run - m_new) * acc + _mm(vt, pr)))
        return tuple(out)

    def sel_step(t, states):
        issue_scores(2 * t + 1, scb_ref)
        states = consume(2 * t, sca_ref, states)
        issue_scores(2 * t + 2, sca_ref)
        return consume(2 * t + 1, scb_ref, states)

    kp_diag = q0 + lax.broadcasted_iota(jnp.int32, (QBLK, 1), 0)
    bias_d = _tile_lanes(jnp.where(kp_diag <= qp_lane, 0.0, _NEG_INF), A_GROUP)
    q0_al = pl.multiple_of(q0, QBLK)
    init = []
    for h in range(A_KV_HEADS):
        sd = diag_ref[h] + bias_d
        m0 = _col_max(sd)
        vt = _values_with_ones(vt_ref[h * HEAD_DIM:(h + 1) * HEAD_DIM, pl.ds(q0_al, QBLK)])
        init.append((m0, _mm(vt, jnp.exp2(sd - m0))))
    init = tuple(init)
    sel_out = lax.fori_loop(0, (n_chunks + 1) // 2, sel_step, init)

    heads_t = []
    for h in range(A_KV_HEADS):
        acc_s = sel_out[h][1]
        o_s = acc_s[:HEAD_DIM] / jnp.maximum(acc_s[HEAD_DIM:HEAD_DIM + 1], 1e-30)
        for g in range(A_GROUP):
            hd = h * A_GROUP + g
            cols = slice(g * QBLK, (g + 1) * QBLK)
            heads_t.append(gates_t[hd:hd + 1, :] * o_cs[h][:, cols]
                           + gates_t[A_HEADS + hd:A_HEADS + hd + 1, :] * o_s[:, cols]
                           + gates_t[2 * A_HEADS + hd:2 * A_HEADS + hd + 1, :] * o_ws[h][:, cols])
    _store_transposed_pairs(o_ref, heads_t)


def _nsa_attention(qat, ka, vat, kc, vct, ga, bsz, seq):
    nblk = seq // QBLK
    n_seg = seq // CMP_STRIDE
    n_sel = seq // SEL_LEN
    assert seq >= A_WINDOW + QBLK and seq % SEL_CHUNK == 0
    c_start = jnp.arange(n_seg) * CMP_STRIDE
    s_start = jnp.arange(n_sel) * SEL_LEN
    overlap = (jnp.minimum(c_start[None, :] + CMP_LEN, s_start[:, None] + SEL_LEN)
               - jnp.maximum(c_start[None, :], s_start[:, None]))
    c2s = (jnp.clip(overlap, 0, None).astype(jnp.float32) / CMP_LEN).astype(MXU_DTYPE)
    return pl.pallas_call(
        functools.partial(_nsa_kernel, seq=seq),
        grid=(bsz, nblk),
        in_specs=[
            pl.BlockSpec((None, A_WIDTH, QBLK), lambda b, i: (b, 0, i)),
            pl.BlockSpec((seq, 2 * LANES), lambda b, i: (b, 0)),
            pl.BlockSpec((None, 2 * LANES, seq), lambda b, i: (b, 0, 0)),
            pl.BlockSpec((None, n_seg, LANES), lambda b, i: (b, 0, 0)),
            pl.BlockSpec((None, LANES, n_seg), lambda b, i: (b, 0, 0)),
            pl.BlockSpec((QBLK, LANES), lambda b, i: (b * nblk + i, 0)),
            _resident((n_sel, n_seg)),
        ],
        out_specs=pl.BlockSpec((QBLK, A_WIDTH), lambda b, i: (b * nblk + i, 0)),
        out_shape=jax.ShapeDtypeStruct((bsz * seq, A_WIDTH), MXU_DTYPE),
        scratch_shapes=[pltpu.VMEM((A_KV_HEADS, n_sel, QBLK), jnp.float32)]
        + [pltpu.VMEM((A_KV_HEADS, min(SEL_CHUNK, seq), A_GROUP * QBLK), jnp.float32)] * 2
        + [pltpu.VMEM((A_KV_HEADS, n_seg, A_GROUP * QBLK), jnp.float32),
           pltpu.VMEM((A_KV_HEADS, min(A_WINDOW + QBLK, seq), A_GROUP * QBLK), jnp.float32),
           pltpu.VMEM((A_KV_HEADS, QBLK, A_GROUP * QBLK), jnp.float32)],
        compiler_params=_params(("parallel", "arbitrary")),
        name="nsa_attention",
    )(qat, ka, vat, kc, vct, ga, c2s)


def _swa_kernel(sink_ref, qt_ref, kv_ref, vt_ref, o_ref, sc_ref, *, seq):
    i = pl.program_id(1)
    span = min(B_WINDOW + QBLK, seq)
    sink = jnp.concatenate([jnp.full((1, QBLK), sink_ref[hd] * LOG2E, jnp.float32) for hd in range(B_HEADS)],
                           axis=1)
    starts = []
    for sb in range(SWA_QSUB):
        q0 = (i * SWA_QSUB + sb) * QBLK
        w0 = pl.multiple_of(jnp.maximum(q0 + QBLK - span, 0), QBLK)
        starts.append((q0, w0))
        q_t = jnp.concatenate([qt_ref[hd * HEAD_DIM:(hd + 1) * HEAD_DIM, sb * QBLK:(sb + 1) * QBLK]
                               for hd in range(B_HEADS)], axis=1)
        sc_ref[sb] = _mm(kv_ref[pl.ds(w0, span), :], _pad_head_rows(q_t, 0, 0.0))
    for sb, (q0, w0) in enumerate(starts):
        qp_lane = q0 + lax.broadcasted_iota(jnp.int32, (1, QBLK), 1)
        diff = qp_lane - (w0 + lax.broadcasted_iota(jnp.int32, (span, 1), 0))
        bias = _tile_lanes(jnp.where(diff >= 0, jnp.where(diff < B_WINDOW, 0.0, _NEG_INF), _NEG_INF), B_HEADS)
        s = sc_ref[sb] + bias
        m = jnp.maximum(_col_max(s), sink)
        acc = _mm(_values_with_ones(vt_ref[:, pl.ds(w0, span)]), jnp.exp2(s - m))
        o_t = acc[:HEAD_DIM] / (acc[HEAD_DIM:HEAD_DIM + 1] + jnp.exp2(sink - m))
        _store_transposed_pairs(o_ref, [o_t[:, hd * QBLK:(hd + 1) * QBLK] for hd in range(B_HEADS)], sb * QBLK)


def _swa_attention(qbt, kvb, vbt, sinks, bsz, seq):
    rows = SWA_QSUB * QBLK
    nstep = seq // rows
    span = min(B_WINDOW + QBLK, seq)
    assert seq >= B_WINDOW + QBLK and seq % rows == 0
    return pl.pallas_call(
        functools.partial(_swa_kernel, seq=seq),
        grid=(bsz, nstep),
        in_specs=[
            pl.BlockSpec(memory_space=pltpu.SMEM),
            pl.BlockSpec((None, B_WIDTH, rows), lambda b, i: (b, 0, i)),
            pl.BlockSpec((seq, LANES), lambda b, i: (b, 0)),
            pl.BlockSpec((None, HEAD_DIM, seq), lambda b, i: (b, 0, 0)),
        ],
        out_specs=pl.BlockSpec((rows, B_WIDTH), lambda b, i: (b * nstep + i, 0)),
        out_shape=jax.ShapeDtypeStruct((bsz * seq, B_WIDTH), MXU_DTYPE),
        scratch_shapes=[pltpu.VMEM((SWA_QSUB, span, B_HEADS * QBLK), jnp.float32)],
        compiler_params=_params(("parallel", "arbitrary")),
        name="swa_attention",
    )(sinks, qbt, kvb, vbt)


def _s5_discretize_kernel(are_ref, aim_ref, logdt_ref, bre_ref, bim_ref,
                          abr_ref, abi_ref, bbr_ref, bbi_ref):
    ar = are_ref[...]
    ai = aim_ref[...]
    dt = jnp.exp(logdt_ref[...])
    mag = jnp.exp(dt * ar)
    abr = mag * jnp.cos(dt * ai)
    abi = mag * jnp.sin(dt * ai)
    den = ar * ar + ai * ai
    nr = abr - 1.0
    coef_r = (nr * ar + abi * ai) / den
    coef_i = (abi * ar - nr * ai) / den
    abr_ref[...] = abr
    abi_ref[...] = abi
    br = bre_ref[...]
    bi = bim_ref[...]
    bbr_ref[...] = coef_r[:, None, :] * br - coef_i[:, None, :] * bi
    bbi_ref[...] = coef_r[:, None, :] * bi + coef_i[:, None, :] * br


def _s5_discretize(a_re, a_im, log_dt, b_re, b_im):
    lg = a_re.shape[0] * a_re.shape[1]
    a2 = lambda a: a.reshape(lg, C_STATE)
    b3 = lambda b: jnp.swapaxes(b.reshape(lg, C_STATE, C_GROUP_CH), 1, 2)
    sa = jax.ShapeDtypeStruct((lg, C_STATE), jnp.float32)
    sb = jax.ShapeDtypeStruct((lg, C_GROUP_CH, C_STATE), jnp.float32)
    return pl.pallas_call(_s5_discretize_kernel, out_shape=[sa, sa, sb, sb], name="s5_discretize")(
        a2(a_re), a2(a_im), log_dt.reshape(lg, 1), b3(b_re), b3(b_im))


def _s5_kernel(u_ref, bmat_ref, abar_ref, cmat_ref, d_ref, gw_ref, gb_ref, y_ref, state_ref, ut_ref, xs_ref):
    bsz, t_steps, _ = u_ref.shape

    @pl.when(pl.program_id(0) == 0)
    def _():
        state_ref[...] = jnp.zeros_like(state_ref)

    def gather(t, carry):
        ut_ref[pl.ds(pl.multiple_of(t * bsz, bsz), bsz), :] = u_ref[:, t, :]
        return carry

    lax.fori_loop(0, t_steps, gather, 0, unroll=8)
    u = ut_ref[...]
    for hf in range(2):
        bu = _mm(u[:, hf * _HALF_CH:(hf + 1) * _HALF_CH], bmat_ref[hf])
        xs_ref[:, hf * _HALF_ST:(hf + 1) * _HALF_ST] = bu[:, :_HALF_ST]
        xs_ref[:, C_STATES + hf * _HALF_ST:C_STATES + (hf + 1) * _HALF_ST] = bu[:, _HALF_ST:]

    ys = []
    for hf in range(2):
        re = slice(hf * _HALF_ST, (hf + 1) * _HALF_ST)
        im = slice(C_STATES + hf * _HALF_ST, C_STATES + (hf + 1) * _HALF_ST)
        ar = jnp.broadcast_to(abar_ref[0:1, re], (bsz, _HALF_ST))
        ai = jnp.broadcast_to(abar_ref[0:1, im], (bsz, _HALF_ST))
        xr = state_ref[:, re]
        xi = state_ref[:, im]
        for t in range(t_steps):
            rows = slice(t * bsz, (t + 1) * bsz)
            xr, xi = (ar * xr - ai * xi + xs_ref[rows, re], ar * xi + ai * xr + xs_ref[rows, im])
            xs_ref[rows, re] = xr
            xs_ref[rows, im] = xi
        state_ref[:, re] = xr
        state_ref[:, im] = xi
        ys.append(_mm(xs_ref[:, re], cmat_ref[hf, :_HALF_ST, :]) + _mm(xs_ref[:, im], cmat_ref[hf, _HALF_ST:, :]))
    y = jnp.concatenate(ys, axis=1) + d_ref[...] * u
    z = jax.nn.gelu(y)
    out = z * jax.nn.sigmoid(_mm(z, gw_ref[...]) + gb_ref[...])
    y_ref[...] = out.reshape(t_steps, bsz, C_WIDTH).astype(y_ref.dtype)


def _s5(u, bmat, abar, cmat, d, glu_w, glu_b):
    bsz, seq, _ = u.shape
    t_steps = min(S5_CHUNK, seq)
    return pl.pallas_call(
        _s5_kernel,
        grid=(seq // t_steps,),
        in_specs=[
            pl.BlockSpec((bsz, t_steps, C_WIDTH), lambda i: (0, i, 0)),
            _resident((2, _HALF_CH, 2 * _HALF_ST)), _resident((1, 2 * C_STATES)), _resident((2, 2 * _HALF_ST, _HALF_CH)),
            _resident((1, C_WIDTH)), _resident((C_WIDTH, C_WIDTH)), _resident((1, C_WIDTH)),
        ],
        out_specs=pl.BlockSpec((t_steps, bsz, C_WIDTH), lambda i: (i, 0, 0)),
        out_shape=jax.ShapeDtypeStruct((seq, bsz, C_WIDTH), MXU_DTYPE),
        scratch_shapes=[pltpu.VMEM((bsz, 2 * C_STATES), jnp.float32),
                        pltpu.VMEM((t_steps * bsz, C_WIDTH), jnp.float32),
                        pltpu.VMEM((t_steps * bsz, 2 * C_STATES), jnp.float32)],
        compiler_params=_params(("arbitrary",)),
        name="s5_scan",
    )(u, bmat, abar, cmat, d.reshape(1, C_WIDTH), glu_w.astype(MXU_DTYPE), glu_b.reshape(1, C_WIDTH))


def _block_diag(per_group):
    g, r, c = per_group.shape
    eye = jnp.eye(g, dtype=per_group.dtype)
    return jnp.einsum("grc,gk->grkc", per_group, eye).reshape(g * r, g * c)


def _merge_kernel(x_ref, ya_ref, yb_ref, yc_ref, gain_ref, wg_ref, pa_ref, pb_ref, pc_ref, wo_ref, o_ref):
    x = x_ref[...]
    h = _rmsnorm(x, gain_ref[...]).astype(MXU_DTYPE)
    merged = None
    for j, (y_ref, p_ref) in enumerate(((ya_ref, pa_ref), (yb_ref, pb_ref), (yc_ref, pc_ref))):
        gate = jax.nn.sigmoid(jnp.dot(h, wg_ref[:, j * D_MODEL:(j + 1) * D_MODEL],
                                      preferred_element_type=jnp.float32))
        term = gate * jnp.dot(y_ref[...], p_ref[...], preferred_element_type=jnp.float32)
        merged = term if merged is None else merged + term
    o_ref[...] = x + _mm(merged, wo_ref[...])


def _merge(x2, ya, yb, yc, gain, wg, pa, pb, pc, wo, layer):
    n = x2.shape[0]
    tm = min(ROW_TILE, n)

    def rows(width):
        return pl.BlockSpec((tm, width), lambda i: (i, 0))

    return pl.pallas_call(
        _merge_kernel,
        grid=(n // tm,),
        in_specs=[rows(D_MODEL), rows(A_WIDTH), rows(B_WIDTH), rows(C_WIDTH), _resident((1, D_MODEL)),
                  _resident((D_MODEL, N_BRANCH * D_MODEL)), _resident_layer((A_WIDTH, D_MODEL), layer),
                  _resident_layer((B_WIDTH, D_MODEL), layer), _resident_layer((C_WIDTH, D_MODEL), layer),
                  _resident_layer((D_MODEL, D_MODEL), layer)],
        out_specs=rows(D_MODEL),
        out_shape=jax.ShapeDtypeStruct((n, D_MODEL), jnp.float32),
        compiler_params=_params(("parallel",)),
        name="merge",
    )(x2, ya, yb, yc, gain.reshape(1, D_MODEL), wg, pa, pb, pc, wo)


def _mlp_kernel(x_ref, gain_ref, wu_ref, wd_ref, fgain_ref, o_ref, *, final_norm):
    x = x_ref[...]
    h = _rmsnorm(x, gain_ref[...]).astype(MXU_DTYPE)
    acc = x
    for c in range(D_FF // FF_CHUNK):
        cols = slice(c * FF_CHUNK, (c + 1) * FF_CHUNK)
        up = jnp.dot(h, wu_ref[:, cols], preferred_element_type=jnp.float32)
        acc = acc + _mm(jnp.square(jnp.maximum(up, 0.0)), wd_ref[cols, :])
    o_ref[...] = _rmsnorm(acc, fgain_ref[...]) if final_norm else acc


def _mlp(x2, gain, wu, wd, fgain, layer, final_norm):
    n = x2.shape[0]
    tm = min(ROW_TILE, n)
    rows = pl.BlockSpec((tm, D_MODEL), lambda i: (i, 0))
    return pl.pallas_call(
        functools.partial(_mlp_kernel, final_norm=final_norm),
        grid=(n // tm,),
        in_specs=[rows, _resident((1, D_MODEL)), _resident_layer((D_MODEL, D_FF), layer),
                  _resident_layer((D_FF, D_MODEL), layer), _resident((1, D_MODEL))],
        out_specs=rows,
        out_shape=jax.ShapeDtypeStruct((n, D_MODEL), jnp.float32),
        compiler_params=_params(("parallel",)),
        name="mlp",
    )(x2, gain.reshape(1, D_MODEL), wu, wd, fgain.reshape(1, D_MODEL))


def kernel(x, positions, norm_mix, w_in, nsa_cmp_pos, nsa_cmp_w1, nsa_cmp_w2, swa_sinks, s5_a_re, s5_a_im, s5_log_dt, s5_b_re, s5_b_im, s5_c_re, s5_c_im, s5_d, s5_glu_w, s5_glu_b, w_branch_a, w_branch_b, w_branch_c, w_out, norm_mlp, w_mlp_up, w_mlp_down, norm_final):
    bsz, seq, _ = x.shape
    depth = w_in.shape[0]
    n = bsz * seq
    gate_col = sum(IN_WIDTHS[:-1])
    bf = lambda w: w.astype(MXU_DTYPE)

    tables = _rope_tables(positions)
    abr, abi, bbr, bbi = _s5_discretize(s5_a_re, s5_a_im, s5_log_dt, s5_b_re, s5_b_im)
    abr = abr.reshape(depth, 1, C_STATES)
    abi = abi.reshape(depth, 1, C_STATES)
    bbr = bbr.reshape(depth, C_GROUPS, C_GROUP_CH, C_STATE)
    bbi = bbi.reshape(depth, C_GROUPS, C_GROUP_CH, C_STATE)

    stacked = [bf(w) for w in (w_branch_a, w_branch_b, w_branch_c, w_out, w_mlp_up, w_mlp_down)]

    x2 = x.reshape(n, D_MODEL)
    for l in range(depth):
        qat, qbt, vat, vbt, ka, kvb, kseg, vseg, ga, uc = _in_projection(
            x2, norm_mix[l], *_pack_w_in(w_in[l]), tables, bsz, seq)

        kc, vct = _compress(kseg, vseg, nsa_cmp_pos[l].reshape(2, 1, CMP_LEN * HEAD_DIM), nsa_cmp_w1[l],
                            nsa_cmp_w2[l], bsz)
        ya = _nsa_attention(qat, ka, vat, kc, vct, ga, bsz, seq)

        yb = _swa_attention(qbt, kvb, vbt, swa_sinks[l], bsz, seq)

        cre_t = jnp.swapaxes(s5_c_re[l], 1, 2)
        cim_t = jnp.swapaxes(s5_c_im[l], 1, 2)
        halves = (slice(0, C_GROUPS // 2), slice(C_GROUPS // 2, C_GROUPS))
        bmat = jnp.stack([jnp.concatenate([_block_diag(bbr[l][g]), _block_diag(bbi[l][g])], axis=1)
                          for g in halves])
        cmat = jnp.stack([jnp.concatenate([_block_diag(cre_t[g]), -_block_diag(cim_t[g])], axis=0)
                          for g in halves])
        abar = jnp.concatenate([abr[l], abi[l]], axis=1)
        yc_tm = _s5(uc.reshape(bsz, seq, C_WIDTH), bf(bmat), abar, bf(cmat), s5_d[l], s5_glu_w[l], s5_glu_b[l])
        yc = jnp.swapaxes(yc_tm, 0, 1).reshape(n, C_WIDTH)

        x2 = _merge(x2, ya, yb, yc, norm_mix[l], bf(w_in[l][:, gate_col:]), *stacked[:4], l)
        x2 = _mlp(x2, norm_mlp[l], *stacked[4:], norm_final, l, l == depth - 1)
    return x2.reshape(bsz, seq, D_MODEL)
```

```python
import functools

import jax
import jax.numpy as jnp
from jax import lax
from jax.experimental import pallas as pl
from jax.experimental.pallas import tpu as pltpu

D_MODEL = 1024
HEAD_DIM = 64
ROPE_THETA = 10000.0
NORM_EPS = 1e-6
QBLK = 128

A_HEADS = 8
A_KV_HEADS = 2
A_GROUP = A_HEADS // A_KV_HEADS
A_WIDTH = A_HEADS * HEAD_DIM
NSA_BRANCHES = 3
CMP_LEN = 32
CMP_STRIDE = 16
CMP_HIDDEN = 2 * HEAD_DIM
SEL_LEN = 64
SEL_TOPK = 16
A_WINDOW = 512
FORCED_SCORE = 1e4

B_HEADS = 8
B_WIDTH = B_HEADS * HEAD_DIM
B_WINDOW = 128

C_WIDTH = 512
C_GROUP_CH = 16
C_GROUPS = C_WIDTH // C_GROUP_CH
C_STATE = 64
C_STATES = C_GROUPS * C_STATE
_HALF_CH = C_WIDTH // 2
_HALF_ST = C_STATES // 2

D_FF = 4 * D_MODEL
N_BRANCH = 3

IN_WIDTHS = (A_WIDTH, NSA_BRANCHES * 2 * A_KV_HEADS * HEAD_DIM, NSA_BRANCHES * A_HEADS,
             B_WIDTH, 2 * HEAD_DIM, C_WIDTH, N_BRANCH * D_MODEL)

LANES = 128
SUBLANES = 8
VMEM_LIMIT = 56 * 1024 * 1024

MXU_DTYPE = jnp.bfloat16
ROW_TILE = 1024
S5_CHUNK = 128
SEL_CHUNK = 256
SWA_QSUB = 4
FF_CHUNK = 1024

LOG2E = 1.4426950408889634
_NEG_INF = float("-inf")


def _f32(x):
    return x.astype(jnp.float32)


def _mm(a, b):
    return jnp.dot(a.astype(MXU_DTYPE), b.astype(MXU_DTYPE), preferred_element_type=jnp.float32)


def _mm_nt(a, b):
    return lax.dot_general(a.astype(MXU_DTYPE), b.astype(MXU_DTYPE), (((1,), (1,)), ((), ())),
                           preferred_element_type=jnp.float32)


def _rmsnorm(x, gain):
    return x * lax.rsqrt(jnp.mean(x * x, axis=-1, keepdims=True) + NORM_EPS) * gain


def _resident(shape):
    zeros = (0,) * len(shape)
    return pl.BlockSpec(shape, lambda *_: zeros, pipeline_mode=pl.Buffered(1))


def _resident_layer(shape, layer):
    zeros = (0,) * len(shape)
    return pl.BlockSpec((None,) + tuple(shape), lambda *_: (layer,) + zeros, pipeline_mode=pl.Buffered(1))


def _params(sem):
    return pltpu.CompilerParams(dimension_semantics=sem, vmem_limit_bytes=VMEM_LIMIT)


def _rope_table_kernel(pos_ref, invf_ref, sign_ref, cos_ref, sin_ref):
    ang = _f32(pos_ref[...]) * invf_ref[...]
    cos_ref[...] = jnp.cos(ang)
    sin_ref[...] = jnp.sin(ang) * sign_ref[...]


def _rope_tables(positions):
    n = positions.size
    half = HEAD_DIM // 2
    inv_freq = ROPE_THETA ** (-jnp.arange(half, dtype=jnp.float32) / half)
    invf = jnp.tile(inv_freq, LANES // half)
    sign = jnp.tile(jnp.concatenate([-jnp.ones(half, jnp.float32), jnp.ones(half, jnp.float32)]),
                    LANES // HEAD_DIM)
    tm = min(1024, n)
    natural = pl.pallas_call(
        _rope_table_kernel,
        grid=(n // tm,),
        in_specs=[pl.BlockSpec((tm, 1), lambda i: (i, 0)), _resident((1, LANES)), _resident((1, LANES))],
        out_specs=[pl.BlockSpec((tm, LANES), lambda i: (i, 0))] * 2,
        out_shape=[jax.ShapeDtypeStruct((n, LANES), jnp.float32)] * 2,
        compiler_params=_params(("parallel",)),
        name="rope_tables",
    )(positions.reshape(n, 1), invf.reshape(1, LANES), sign.reshape(1, LANES))
    transposed = pl.pallas_call(
        _rope_table_kernel,
        grid=(n // tm,),
        in_specs=[pl.BlockSpec((1, tm), lambda i: (0, i)), _resident((HEAD_DIM, 1)), _resident((HEAD_DIM, 1))],
        out_specs=[pl.BlockSpec((HEAD_DIM, tm), lambda i: (0, i))] * 2,
        out_shape=[jax.ShapeDtypeStruct((HEAD_DIM, n), jnp.float32)] * 2,
        compiler_params=_params(("parallel",)),
        name="rope_tables_t",
    )(positions.reshape(1, n), invf[:HEAD_DIM].reshape(HEAD_DIM, 1), sign[:HEAD_DIM].reshape(HEAD_DIM, 1))
    return natural + transposed


def _swap_halves(x):
    lane = lax.broadcasted_iota(jnp.int32, x.shape, 1)
    first = (lane & (HEAD_DIM - 1)) < HEAD_DIM // 2
    return jnp.where(first, pltpu.roll(x, LANES - HEAD_DIM // 2, 1), pltpu.roll(x, HEAD_DIM // 2, 1))


def _proj_kernel(x_ref, gain_ref, wn_ref, wt_ref, cos_ref, sin_ref, cost_ref, sint_ref,
                 qat_ref, qbt_ref, vat_ref, vbt_ref, ka_ref, kvb_ref, kseg_ref, vseg_ref, ga_ref, uc_ref,
                 seg_ref):
    h = _rmsnorm(x_ref[...], gain_ref[...]).astype(MXU_DTYPE)
    nat = jnp.dot(h, wn_ref[...], preferred_element_type=jnp.float32)
    tr = _mm_nt(wt_ref[...], h)

    cos = cos_ref[...]
    sin = sin_ref[...]

    def roped(j):
        xj = nat[:, j * LANES:(j + 1) * LANES]
        return xj, xj * cos + _swap_halves(xj) * sin

    for j in range(1, NSA_BRANCHES):
        ka_ref[:, (j - 1) * LANES:j * LANES] = roped(j)[1].astype(ka_ref.dtype)
    raw, rot = roped(3)
    lane = lax.broadcasted_iota(jnp.int32, raw.shape, 1)
    kvb_ref[...] = jnp.where(lane < HEAD_DIM, rot, raw).astype(kvb_ref.dtype)
    n_seg_tile = nat.shape[0] // CMP_STRIDE
    for j, (src, dst_ref) in enumerate(((roped(0)[1], kseg_ref), (nat[:, 4 * LANES:5 * LANES], vseg_ref))):
        seg_ref[j] = src
        for tok in range(CMP_STRIDE):
            dst_ref[:, tok * LANES:(tok + 1) * LANES] = (
                seg_ref[j, pl.ds(tok, n_seg_tile, stride=CMP_STRIDE), :].astype(dst_ref.dtype))
    ga_ref[...] = jax.nn.sigmoid(nat[:, 5 * LANES:6 * LANES])
    uc_ref[...] = nat[:, 6 * LANES:]

    cos_t = cost_ref[...]
    sin_t = sint_ref[...]
    half = HEAD_DIM // 2

    def roped_t(r0):
        xh = tr[r0:r0 + HEAD_DIM]
        return xh * cos_t + jnp.concatenate([xh[half:], xh[:half]], axis=0) * sin_t

    for hd in range(A_HEADS):
        qat_ref[hd * HEAD_DIM:(hd + 1) * HEAD_DIM, :] = roped_t(hd * HEAD_DIM).astype(qat_ref.dtype)
    for hd in range(B_HEADS):
        qbt_ref[hd * HEAD_DIM:(hd + 1) * HEAD_DIM, :] = roped_t(A_WIDTH + hd * HEAD_DIM).astype(qbt_ref.dtype)
    v0 = A_WIDTH + B_WIDTH
    vat_ref[...] = tr[v0:v0 + 2 * LANES].astype(vat_ref.dtype)
    vbt_ref[...] = tr[v0 + 2 * LANES:].astype(vbt_ref.dtype)


_NAT_COLS = 10 * LANES
_TR_ROWS = A_WIDTH + B_WIDTH + 2 * LANES + HEAD_DIM


def _pack_w_in(w_in):
    o = [0]
    for w in IN_WIDTHS:
        o.append(o[-1] + w)
    scale = HEAD_DIM ** -0.5 * LOG2E
    q_a = w_in[:, o[0]:o[1]] * scale
    kv_cols = [w_in[:, o[1] + j * LANES:o[1] + (j + 1) * LANES] for j in range(2 * NSA_BRANCHES)]
    k_a = jnp.concatenate(kv_cols[0::2], axis=1)
    v_a = jnp.concatenate(kv_cols[1::2], axis=1)
    g_a = jnp.pad(w_in[:, o[2]:o[3]], ((0, 0), (0, LANES - NSA_BRANCHES * A_HEADS)))
    q_b = w_in[:, o[3]:o[4]] * scale
    kv_b = w_in[:, o[4]:o[5]]
    u_c = w_in[:, o[5]:o[6]]
    w_nat = jnp.concatenate([k_a, kv_b, v_a[:, :LANES], g_a, u_c], axis=1)
    w_tr = jnp.concatenate([q_a, q_b, v_a[:, LANES:], kv_b[:, HEAD_DIM:]], axis=1).T
    return w_nat.astype(MXU_DTYPE), w_tr.astype(MXU_DTYPE)


def _in_projection(x2, gain, w_nat, w_tr, tables, bsz, seq):
    n = x2.shape[0]
    tm = min(ROW_TILE, seq)
    tiles = seq // tm

    def rows(width):
        return pl.BlockSpec((tm, width), lambda i: (i, 0))

    def lanes(height):
        return pl.BlockSpec((None, height, tm), lambda i: (i // tiles, 0, i % tiles))

    table_t = pl.BlockSpec((HEAD_DIM, tm), lambda i: (0, i))
    t_heights = (A_WIDTH, B_WIDTH, 2 * LANES, HEAD_DIM)
    widths = (2 * LANES, LANES, LANES, C_WIDTH)
    dtypes = (MXU_DTYPE, MXU_DTYPE, jnp.float32, jnp.float32)
    seg_rows = tm // CMP_STRIDE
    seg_spec = pl.BlockSpec((seg_rows, CMP_STRIDE * LANES), lambda i: (i, 0))
    seg_shape = jax.ShapeDtypeStruct((n // CMP_STRIDE, CMP_STRIDE * LANES), MXU_DTYPE)
    return pl.pallas_call(
        _proj_kernel,
        grid=(n // tm,),
        in_specs=[rows(D_MODEL), _resident((1, D_MODEL)), _resident((D_MODEL, _NAT_COLS)),
                  _resident((_TR_ROWS, D_MODEL)), rows(LANES), rows(LANES), table_t, table_t],
        out_specs=[lanes(hh) for hh in t_heights] + [rows(w) for w in widths[:2]] + [seg_spec, seg_spec]
        + [rows(w) for w in widths[2:]],
        out_shape=[jax.ShapeDtypeStruct((bsz, hh, seq), MXU_DTYPE) for hh in t_heights]
        + [jax.ShapeDtypeStruct((n, w), dt) for w, dt in zip(widths[:2], dtypes[:2])] + [seg_shape, seg_shape]
        + [jax.ShapeDtypeStruct((n, w), dt) for w, dt in zip(widths[2:], dtypes[2:])],
        scratch_shapes=[pltpu.VMEM((2, tm, LANES), jnp.float32)],
        compiler_params=_params(("parallel",)),
        name="in_projection",
    )(x2, gain.reshape(1, D_MODEL), w_nat, w_tr, *tables)


def _compress_kernel(kseg_ref, vseg_ref, pos_ref, w1_ref, w1x_ref, w2_ref, kc_ref, vct_ref):
    nseg = kseg_ref.shape[0]
    both = []
    for j, seg_ref in enumerate((kseg_ref, vseg_ref)):
        seg = seg_ref[...]
        pos_term = _mm(pos_ref[j], w1_ref[j])
        heads = []
        for h in range(A_KV_HEADS):
            hidden = (_mm(seg, w1x_ref[j, 0, h]) + pltpu.roll(_mm(seg, w1x_ref[j, 1, h]), nseg - 1, 0)
                      + pos_term)
            heads.append(_mm(jax.nn.gelu(hidden), w2_ref[j]))
        both.append(jnp.concatenate(heads, axis=1))
    kc_ref[...] = both[0].astype(kc_ref.dtype)
    vct_ref[...] = both[1].T.astype(vct_ref.dtype)


def _expand_w1(w1):
    w = w1.reshape(2, 2, CMP_STRIDE, HEAD_DIM, CMP_HIDDEN)
    eye = jnp.eye(A_KV_HEADS, dtype=w1.dtype)
    return jnp.einsum("jatdc,hg->jahtgdc", w, eye).reshape(2, 2, A_KV_HEADS, CMP_STRIDE * LANES, CMP_HIDDEN)


def _compress(kseg, vseg, pos_flat, w1, w2, bsz):
    nseg = kseg.shape[0] // bsz
    segw = kseg.shape[1]
    seg_spec = pl.BlockSpec((nseg, segw), lambda b: (b, 0))
    return pl.pallas_call(
        _compress_kernel,
        grid=(bsz,),
        in_specs=[seg_spec, seg_spec, _resident((2, 1, CMP_LEN * HEAD_DIM)),
                  _resident((2, CMP_LEN * HEAD_DIM, CMP_HIDDEN)),
                  _resident((2, 2, A_KV_HEADS, segw, CMP_HIDDEN)), _resident((2, CMP_HIDDEN, HEAD_DIM))],
        out_specs=[pl.BlockSpec((None, nseg, LANES), lambda b: (b, 0, 0)),
                   pl.BlockSpec((None, LANES, nseg), lambda b: (b, 0, 0))],
        out_shape=[jax.ShapeDtypeStruct((bsz, nseg, LANES), MXU_DTYPE),
                   jax.ShapeDtypeStruct((bsz, LANES, nseg), MXU_DTYPE)],
        compiler_params=_params(("parallel",)),
        name="nsa_compress",
    )(kseg, vseg, pos_flat.astype(MXU_DTYPE), w1.astype(MXU_DTYPE), _expand_w1(w1).astype(MXU_DTYPE),
      w2.astype(MXU_DTYPE))


def _split3(x):
    hi = x.astype(MXU_DTYPE)
    r1 = x - _f32(hi)
    mid = r1.astype(MXU_DTYPE)
    lo = (r1 - _f32(mid)).astype(MXU_DTYPE)
    return hi, mid, lo


def _col_max(s):
    return jnp.max(s, axis=0, keepdims=True)


def _tile_lanes(x, n):
    return jnp.concatenate([x] * n, axis=1)


def _pad_head_rows(x, h, fill):
    other = jnp.full(x.shape, fill, x.dtype)
    return jnp.concatenate([x, other] if h == 0 else [other, x], axis=0)


def _values_with_ones(vt):
    return jnp.concatenate([vt, jnp.ones((2 * SUBLANES, vt.shape[1]), vt.dtype)], axis=0)


def _store_transposed_pairs(o_ref, heads_t, row0=0):
    for p in range(len(heads_t) // 2):
        pair = jnp.concatenate([heads_t[2 * p], heads_t[2 * p + 1]], axis=0)
        o_ref[row0:row0 + QBLK, p * LANES:(p + 1) * LANES] = pair.T.astype(o_ref.dtype)


def _nsa_kernel(qt_ref, k_ref, vt_ref, kc_ref, vct_ref, g_ref, c2s_ref, o_ref,
                selb_ref, sca_ref, scb_ref, cmp_ref, win_ref, diag_ref, *, seq):
    i = pl.program_id(1)
    q0 = i * QBLK
    n_cmp = kc_ref.shape[0]
    n_sel = seq // SEL_LEN
    kc_sz = min(SEL_CHUNK, seq)
    span = min(A_WINDOW + QBLK, seq)
    blocks_per_chunk = kc_sz // SEL_LEN

    gates_t = g_ref[...].T
    qp_lane = q0 + lax.broadcasted_iota(jnp.int32, (1, QBLK), 1)

    cmp_end = lax.broadcasted_iota(jnp.int32, (n_cmp, 1), 0) * CMP_STRIDE + (CMP_LEN - 1)
    bias_c = _tile_lanes(jnp.where(cmp_end <= qp_lane, 0.0, _NEG_INF), A_GROUP)

    w0 = pl.multiple_of(jnp.maximum(q0 + QBLK - span, 0), QBLK)
    diff = qp_lane - (w0 + lax.broadcasted_iota(jnp.int32, (span, 1), 0))
    bias_w = _tile_lanes(jnp.where(diff >= 0, jnp.where(diff < A_WINDOW, 0.0, _NEG_INF), _NEG_INF), A_GROUP)

    w_qs = []
    for h in range(A_KV_HEADS):
        q_t = jnp.concatenate([qt_ref[(h * A_GROUP + g) * HEAD_DIM:(h * A_GROUP + g + 1) * HEAD_DIM, :]
                               for g in range(A_GROUP)], axis=1)
        w_qs.append(_pad_head_rows(q_t, h, 0.0))

    n_chunks = (q0 + kc_sz - 1) // kc_sz
    last = jnp.maximum(n_chunks - 1, 0)

    def issue_scores(c, dst_ref):
        k0 = pl.multiple_of(jnp.minimum(c, last) * kc_sz, kc_sz)
        keys = k_ref[pl.ds(k0, kc_sz), :LANES]
        for h in range(A_KV_HEADS):
            dst_ref[h] = _mm(keys, w_qs[h])

    for h in range(A_KV_HEADS):
        cmp_ref[h] = _mm(kc_ref[...], w_qs[h])
    for h in range(A_KV_HEADS):
        win_ref[h] = _mm(k_ref[pl.ds(w0, span), LANES:], w_qs[h])
    issue_scores(0, sca_ref)
    for h in range(A_KV_HEADS):
        diag_ref[h] = _mm(k_ref[pl.ds(pl.multiple_of(q0, QBLK), QBLK), :LANES], w_qs[h])

    o_cs, o_ws = [], []
    for h in range(A_KV_HEADS):
        s = cmp_ref[h] + bias_c
        m = _col_max(s)
        m = jnp.where(m == _NEG_INF, 0.0, m)
        e = jnp.exp2(s - m)
        p = e * (1.0 / jnp.maximum(jnp.sum(e, axis=0, keepdims=True), 1e-30))
        o_cs.append(_mm(vct_ref[...], p)[h * HEAD_DIM:(h + 1) * HEAD_DIM])

        p_sum = p[:, 0:QBLK]
        for g in range(1, A_GROUP):
            p_sum = p_sum + p[:, g * QBLK:(g + 1) * QBLK]
        imp_t = sum(_mm(c2s_ref[...], part) for part in _split3(p_sum))
        blk = lax.broadcasted_iota(jnp.int32, (n_sel, QBLK), 0)
        cur = jnp.right_shift(qp_lane, 6)
        forced = (blk == 0) | (blk == cur) | (blk == cur - 1)
        score = jnp.where(forced, FORCED_SCORE, imp_t)
        score = jnp.where(blk <= cur, score, _NEG_INF)
        n_grp = n_sel // SUBLANES
        sub = lax.broadcasted_iota(jnp.int32, (SUBLANES, QBLK), 0)
        score_grp = [score[SUBLANES * v:SUBLANES * (v + 1), :] for v in range(n_grp)]
        rank_grp = [jnp.zeros((SUBLANES, QBLK), jnp.float32) for _ in range(n_grp)]
        for j in range(n_sel):
            row = jnp.broadcast_to(score[j:j + 1, :], (SUBLANES, QBLK))
            vj, rj = divmod(j, SUBLANES)
            for v in range(n_grp):
                if v > vj:
                    inc = jnp.where(row >= score_grp[v], 1.0, 0.0)
                elif v < vj:
                    inc = jnp.where(row > score_grp[v], 1.0, 0.0)
                else:
                    inc = jnp.where(sub > rj, jnp.where(row >= score_grp[v], 1.0, 0.0),
                                    jnp.where(row > score_grp[v], 1.0, 0.0))
                rank_grp[v] = rank_grp[v] + inc
        rank = jnp.concatenate(rank_grp, axis=0)
        picked = jnp.where(rank < float(min(SEL_TOPK, n_sel)), 0.0, _NEG_INF)
        selb_ref[h] = jnp.where(blk < jnp.right_shift(q0, 6), picked, _NEG_INF)

    for h in range(A_KV_HEADS):
        sw = win_ref[h] + bias_w
        pw = jnp.exp2(sw - _col_max(sw))
        acc_w = _mm(_values_with_ones(vt_ref[LANES + h * HEAD_DIM:LANES + (h + 1) * HEAD_DIM, pl.ds(w0, span)]), pw)
        o_ws.append(acc_w[:HEAD_DIM] / acc_w[HEAD_DIM:HEAD_DIM + 1])

    def consume(c, src_ref, states):
        cc = jnp.minimum(c, last)
        k0 = pl.multiple_of(cc * kc_sz, kc_sz)
        out = []
        for h in range(A_KV_HEADS):
            m_run, acc = states[h]
            rows = [_tile_lanes(jnp.where(c < n_chunks, selb_ref[h, pl.ds(cc * blocks_per_chunk + jj, 1), :], _NEG_INF),
                                A_GROUP) for jj in range(blocks_per_chunk)]
            blocks = [src_ref[h, jj * SEL_LEN:(jj + 1) * SEL_LEN, :] for jj in range(blocks_per_chunk)]
            part = None
            for sc_blk, row in zip(blocks, rows):
                blk_max = jnp.max(sc_blk.reshape(SEL_LEN // SUBLANES, SUBLANES, A_GROUP * QBLK), axis=0) + row
                part = blk_max if part is None else jnp.maximum(part, blk_max)
            m_new = jnp.maximum(m_run, _col_max(part))
            pr = jnp.concatenate([jnp.exp2(sc_blk + (row - m_new)) for sc_blk, row in zip(blocks, rows)], axis=0)
            vt = _values_with_ones(vt_ref[h * HEAD_DIM:(h + 1) * HEAD_DIM, pl.ds(k0, kc_sz)])
            out.append((m_new, jnp.exp2(m_run - m_new) * acc + _mm(vt, pr)))
        return tuple(out)

    def sel_step(t, states):
        issue_scores(2 * t + 1, scb_ref)
        states = consume(2 * t, sca_ref, states)
        issue_scores(2 * t + 2, sca_ref)
        return consume(2 * t + 1, scb_ref, states)

    kp_diag = q0 + lax.broadcasted_iota(jnp.int32, (QBLK, 1), 0)
    bias_d = _tile_lanes(jnp.where(kp_diag <= qp_lane, 0.0, _NEG_INF), A_GROUP)
    q0_al = pl.multiple_of(q0, QBLK)
    init = []
    for h in range(A_KV_HEADS):
        sd = diag_ref[h] + bias_d
        m0 = _col_max(sd)
        vt = _values_with_ones(vt_ref[h * HEAD_DIM:(h + 1) * HEAD_DIM, pl.ds(q0_al, QBLK)])
        init.append((m0, _mm(vt, jnp.exp2(sd - m0))))
    init = tuple(init)
    sel_out = lax.fori_loop(0, (n_chunks + 1) // 2, sel_step, init)

    heads_t = []
    for h in range(A_KV_HEADS):
        acc_s = sel_out[h][1]
        o_s = acc_s[:HEAD_DIM] / jnp.maximum(acc_s[HEAD_DIM:HEAD_DIM + 1], 1e-30)
        for g in range(A_GROUP):
            hd = h * A_GROUP + g
            cols = slice(g * QBLK, (g + 1) * QBLK)
            heads_t.append(gates_t[hd:hd + 1, :] * o_cs[h][:, cols]
                           + gates_t[A_HEADS + hd:A_HEADS + hd + 1, :] * o_s[:, cols]
                           + gates_t[2 * A_HEADS + hd:2 * A_HEADS + hd + 1, :] * o_ws[h][:, cols])
    _store_transposed_pairs(o_ref, heads_t)


def _nsa_attention(qat, ka, vat, kc, vct, ga, bsz, seq):
    nblk = seq // QBLK
    n_seg = seq // CMP_STRIDE
    n_sel = seq // SEL_LEN
    assert seq >= A_WINDOW + QBLK and seq % SEL_CHUNK == 0
    c_start = jnp.arange(n_seg) * CMP_STRIDE
    s_start = jnp.arange(n_sel) * SEL_LEN
    overlap = (jnp.minimum(c_start[None, :] + CMP_LEN, s_start[:, None] + SEL_LEN)
               - jnp.maximum(c_start[None, :], s_start[:, None]))
    c2s = (jnp.clip(overlap, 0, None).astype(jnp.float32) / CMP_LEN).astype(MXU_DTYPE)
    return pl.pallas_call(
        functools.partial(_nsa_kernel, seq=seq),
        grid=(bsz, nblk),
        in_specs=[
            pl.BlockSpec((None, A_WIDTH, QBLK), lambda b, i: (b, 0, i)),
            pl.BlockSpec((seq, 2 * LANES), lambda b, i: (b, 0)),
            pl.BlockSpec((None, 2 * LANES, seq), lambda b, i: (b, 0, 0)),
            pl.BlockSpec((None, n_seg, LANES), lambda b, i: (b, 0, 0)),
            pl.BlockSpec((None, LANES, n_seg), lambda b, i: (b, 0, 0)),
            pl.BlockSpec((QBLK, LANES), lambda b, i: (b * nblk + i, 0)),
            _resident((n_sel, n_seg)),
        ],
        out_specs=pl.BlockSpec((QBLK, A_WIDTH), lambda b, i: (b * nblk + i, 0)),
        out_shape=jax.ShapeDtypeStruct((bsz * seq, A_WIDTH), MXU_DTYPE),
        scratch_shapes=[pltpu.VMEM((A_KV_HEADS, n_sel, QBLK), jnp.float32)]
        + [pltpu.VMEM((A_KV_HEADS, min(SEL_CHUNK, seq), A_GROUP * QBLK), jnp.float32)] * 2
        + [pltpu.VMEM((A_KV_HEADS, n_seg, A_GROUP * QBLK), jnp.float32),
           pltpu.VMEM((A_KV_HEADS, min(A_WINDOW + QBLK, seq), A_GROUP * QBLK), jnp.float32),
           pltpu.VMEM((A_KV_HEADS, QBLK, A_GROUP * QBLK), jnp.float32)],
        compiler_params=_params(("parallel", "arbitrary")),
        name="nsa_attention",
    )(qat, ka, vat, kc, vct, ga, c2s)


def _swa_kernel(sink_ref, qt_ref, kv_ref, vt_ref, o_ref, sc_ref, *, seq):
    i = pl.program_id(1)
    span = min(B_WINDOW + QBLK, seq)
    sink = jnp.concatenate([jnp.full((1, QBLK), sink_ref[hd] * LOG2E, jnp.float32) for hd in range(B_HEADS)],
                           axis=1)
    starts = []
    for sb in range(SWA_QSUB):
        q0 = (i * SWA_QSUB + sb) * QBLK
        w0 = pl.multiple_of(jnp.maximum(q0 + QBLK - span, 0), QBLK)
        starts.append((q0, w0))
        q_t = jnp.concatenate([qt_ref[hd * HEAD_DIM:(hd + 1) * HEAD_DIM, sb * QBLK:(sb + 1) * QBLK]
                               for hd in range(B_HEADS)], axis=1)
        sc_ref[sb] = _mm(kv_ref[pl.ds(w0, span), :], _pad_head_rows(q_t, 0, 0.0))
    for sb, (q0, w0) in enumerate(starts):
        qp_lane = q0 + lax.broadcasted_iota(jnp.int32, (1, QBLK), 1)
        diff = qp_lane - (w0 + lax.broadcasted_iota(jnp.int32, (span, 1), 0))
        bias = _tile_lanes(jnp.where(diff >= 0, jnp.where(diff < B_WINDOW, 0.0, _NEG_INF), _NEG_INF), B_HEADS)
        s = sc_ref[sb] + bias
        m = jnp.maximum(_col_max(s), sink)
        acc = _mm(_values_with_ones(vt_ref[:, pl.ds(w0, span)]), jnp.exp2(s - m))
        o_t = acc[:HEAD_DIM] / (acc[HEAD_DIM:HEAD_DIM + 1] + jnp.exp2(sink - m))
        _store_transposed_pairs(o_ref, [o_t[:, hd * QBLK:(hd + 1) * QBLK] for hd in range(B_HEADS)], sb * QBLK)


def _swa_attention(qbt, kvb, vbt, sinks, bsz, seq):
    rows = SWA_QSUB * QBLK
    nstep = seq // rows
    span = min(B_WINDOW + QBLK, seq)
    assert seq >= B_WINDOW + QBLK and seq % rows == 0
    return pl.pallas_call(
        functools.partial(_swa_kernel, seq=seq),
        grid=(bsz, nstep),
        in_specs=[
            pl.BlockSpec(memory_space=pltpu.SMEM),
            pl.BlockSpec((None, B_WIDTH, rows), lambda b, i: (b, 0, i)),
            pl.BlockSpec((seq, LANES), lambda b, i: (b, 0)),
            pl.BlockSpec((None, HEAD_DIM, seq), lambda b, i: (b, 0, 0)),
        ],
        out_specs=pl.BlockSpec((rows, B_WIDTH), lambda b, i: (b * nstep + i, 0)),
        out_shape=jax.ShapeDtypeStruct((bsz * seq, B_WIDTH), MXU_DTYPE),
        scratch_shapes=[pltpu.VMEM((SWA_QSUB, span, B_HEADS * QBLK), jnp.float32)],
        compiler_params=_params(("parallel", "arbitrary")),
        name="swa_attention",
    )(sinks, qbt, kvb, vbt)


def _s5_discretize_kernel(are_ref, aim_ref, logdt_ref, bre_ref, bim_ref,
                          abr_ref, abi_ref, bbr_ref, bbi_ref):
    ar = are_ref[...]
    ai = aim_ref[...]
    dt = jnp.exp(logdt_ref[...])
    mag = jnp.exp(dt * ar)
    abr = mag * jnp.cos(dt * ai)
    abi = mag * jnp.sin(dt * ai)
    den = ar * ar + ai * ai
    nr = abr - 1.0
    coef_r = (nr * ar + abi * ai) / den
    coef_i = (abi * ar - nr * ai) / den
    abr_ref[...] = abr
    abi_ref[...] = abi
    br = bre_ref[...]
    bi = bim_ref[...]
    bbr_ref[...] = coef_r[:, None, :] * br - coef_i[:, None, :] * bi
    bbi_ref[...] = coef_r[:, None, :] * bi + coef_i[:, None, :] * br


def _s5_discretize(a_re, a_im, log_dt, b_re, b_im):
    lg = a_re.shape[0] * a_re.shape[1]
    a2 = lambda a: a.reshape(lg, C_STATE)
    b3 = lambda b: jnp.swapaxes(b.reshape(lg, C_STATE, C_GROUP_CH), 1, 2)
    sa = jax.ShapeDtypeStruct((lg, C_STATE), jnp.float32)
    sb = jax.ShapeDtypeStruct((lg, C_GROUP_CH, C_STATE), jnp.float32)
    return pl.pallas_call(_s5_discretize_kernel, out_shape=[sa, sa, sb, sb], name="s5_discretize")(
        a2(a_re), a2(a_im), log_dt.reshape(lg, 1), b3(b_re), b3(b_im))


def _s5_kernel(u_ref, bmat_ref, abar_ref, cmat_ref, d_ref, gw_ref, gb_ref, y_ref, state_ref, ut_ref, xs_ref):
    bsz, t_steps, _ = u_ref.shape

    @pl.when(pl.program_id(0) == 0)
    def _():
        state_ref[...] = jnp.zeros_like(state_ref)

    def gather(t, carry):
        ut_ref[pl.ds(pl.multiple_of(t * bsz, bsz), bsz), :] = u_ref[:, t, :]
        return carry

    lax.fori_loop(0, t_steps, gather, 0, unroll=8)
    u = ut_ref[...]
    for hf in range(2):
        bu = _mm(u[:, hf * _HALF_CH:(hf + 1) * _HALF_CH], bmat_ref[hf])
        xs_ref[:, hf * _HALF_ST:(hf + 1) * _HALF_ST] = bu[:, :_HALF_ST]
        xs_ref[:, C_STATES + hf * _HALF_ST:C_STATES + (hf + 1) * _HALF_ST] = bu[:, _HALF_ST:]

    ys = []
    for hf in range(2):
        re = slice(hf * _HALF_ST, (hf + 1) * _HALF_ST)
        im = slice(C_STATES + hf * _HALF_ST, C_STATES + (hf + 1) * _HALF_ST)
        ar = jnp.broadcast_to(abar_ref[0:1, re], (bsz, _HALF_ST))
        ai = jnp.broadcast_to(abar_ref[0:1, im], (bsz, _HALF_ST))
        xr = state_ref[:, re]
        xi = state_ref[:, im]
        for t in range(t_steps):
            rows = slice(t * bsz, (t + 1) * bsz)
            xr, xi = (ar * xr - ai * xi + xs_ref[rows, re], ar * xi + ai * xr + xs_ref[rows, im])
            xs_ref[rows, re] = xr
            xs_ref[rows, im] = xi
        state_ref[:, re] = xr
        state_ref[:, im] = xi
        ys.append(_mm(xs_ref[:, re], cmat_ref[hf, :_HALF_ST, :]) + _mm(xs_ref[:, im], cmat_ref[hf, _HALF_ST:, :]))
    y = jnp.concatenate(ys, axis=1) + d_ref[...] * u
    z = jax.nn.gelu(y)
    out = z * jax.nn.sigmoid(_mm(z, gw_ref[...]) + gb_ref[...])
    y_ref[...] = out.reshape(t_steps, bsz, C_WIDTH).astype(y_ref.dtype)


def _s5(u, bmat, abar, cmat, d, glu_w, glu_b):
    bsz, seq, _ = u.shape
    t_steps = min(S5_CHUNK, seq)
    return pl.pallas_call(
        _s5_kernel,
        grid=(seq // t_steps,),
        in_specs=[
            pl.BlockSpec((bsz, t_steps, C_WIDTH), lambda i: (0, i, 0)),
            _resident((2, _HALF_CH, 2 * _HALF_ST)), _resident((1, 2 * C_STATES)), _resident((2, 2 * _HALF_ST, _HALF_CH)),
            _resident((1, C_WIDTH)), _resident((C_WIDTH, C_WIDTH)), _resident((1, C_WIDTH)),
        ],
        out_specs=pl.BlockSpec((t_steps, bsz, C_WIDTH), lambda i: (i, 0, 0)),
        out_shape=jax.ShapeDtypeStruct((seq, bsz, C_WIDTH), MXU_DTYPE),
        scratch_shapes=[pltpu.VMEM((bsz, 2 * C_STATES), jnp.float32),
                        pltpu.VMEM((t_steps * bsz, C_WIDTH), jnp.float32),
                        pltpu.VMEM((t_steps * bsz, 2 * C_STATES), jnp.float32)],
        compiler_params=_params(("arbitrary",)),
        name="s5_scan",
    )(u, bmat, abar, cmat, d.reshape(1, C_WIDTH), glu_w.astype(MXU_DTYPE), glu_b.reshape(1, C_WIDTH))


def _block_diag(per_group):
    g, r, c = per_group.shape
    eye = jnp.eye(g, dtype=per_group.dtype)
    return jnp.einsum("grc,gk->grkc", per_group, eye).reshape(g * r, g * c)


def _merge_kernel(x_ref, ya_ref, yb_ref, yc_ref, gain_ref, wg_ref, pa_ref, pb_ref, pc_ref, wo_ref, o_ref):
    x = x_ref[...]
    h = _rmsnorm(x, gain_ref[...]).astype(MXU_DTYPE)
    merged = None
    for j, (y_ref, p_ref) in enumerate(((ya_ref, pa_ref), (yb_ref, pb_ref), (yc_ref, pc_ref))):
        gate = jax.nn.sigmoid(jnp.dot(h, wg_ref[:, j * D_MODEL:(j + 1) * D_MODEL],
                                      preferred_element_type=jnp.float32))
        term = gate * jnp.dot(y_ref[...], p_ref[...], preferred_element_type=jnp.float32)
        merged = term if merged is None else merged + term
    o_ref[...] = x + _mm(merged, wo_ref[...])


def _merge(x2, ya, yb, yc, gain, wg, pa, pb, pc, wo, layer):
    n = x2.shape[0]
    tm = min(ROW_TILE, n)

    def rows(width):
        return pl.BlockSpec((tm, width), lambda i: (i, 0))

    return pl.pallas_call(
        _merge_kernel,
        grid=(n // tm,),
        in_specs=[rows(D_MODEL), rows(A_WIDTH), rows(B_WIDTH), rows(C_WIDTH), _resident((1, D_MODEL)),
                  _resident((D_MODEL, N_BRANCH * D_MODEL)), _resident_layer((A_WIDTH, D_MODEL), layer),
                  _resident_layer((B_WIDTH, D_MODEL), layer), _resident_layer((C_WIDTH, D_MODEL), layer),
                  _resident_layer((D_MODEL, D_MODEL), layer)],
        out_specs=rows(D_MODEL),
        out_shape=jax.ShapeDtypeStruct((n, D_MODEL), jnp.float32),
        compiler_params=_params(("parallel",)),
        name="merge",
    )(x2, ya, yb, yc, gain.reshape(1, D_MODEL), wg, pa, pb, pc, wo)


def _mlp_kernel(x_ref, gain_ref, wu_ref, wd_ref, fgain_ref, o_ref, *, final_norm):
    x = x_ref[...]
    h = _rmsnorm(x, gain_ref[...]).astype(MXU_DTYPE)
    acc = x
    for c in range(D_FF // FF_CHUNK):
        cols = slice(c * FF_CHUNK, (c + 1) * FF_CHUNK)
        up = jnp.dot(h, wu_ref[:, cols], preferred_element_type=jnp.float32)
        acc = acc + _mm(jnp.square(jnp.maximum(up, 0.0)), wd_ref[cols, :])
    o_ref[...] = _rmsnorm(acc, fgain_ref[...]) if final_norm else acc


def _mlp(x2, gain, wu, wd, fgain, layer, final_norm):
    n = x2.shape[0]
    tm = min(ROW_TILE, n)
    rows = pl.BlockSpec((tm, D_MODEL), lambda i: (i, 0))
    return pl.pallas_call(
        functools.partial(_mlp_kernel, final_norm=final_norm),
        grid=(n // tm,),
        in_specs=[rows, _resident((1, D_MODEL)), _resident_layer((D_MODEL, D_FF), layer),
                  _resident_layer((D_FF, D_MODEL), layer), _resident((1, D_MODEL))],
        out_specs=rows,
        out_shape=jax.ShapeDtypeStruct((n, D_MODEL), jnp.float32),
        compiler_params=_params(("parallel",)),
        name="mlp",
    )(x2, gain.reshape(1, D_MODEL), wu, wd, fgain.reshape(1, D_MODEL))


def kernel(x, positions, norm_mix, w_in, nsa_cmp_pos, nsa_cmp_w1, nsa_cmp_w2, swa_sinks, s5_a_re, s5_a_im, s5_log_dt, s5_b_re, s5_b_im, s5_c_re, s5_c_im, s5_d, s5_glu_w, s5_glu_b, w_branch_a, w_branch_b, w_branch_c, w_out, norm_mlp, w_mlp_up, w_mlp_down, norm_final):
    bsz, seq, _ = x.shape
    depth = w_in.shape[0]
    n = bsz * seq
    gate_col = sum(IN_WIDTHS[:-1])
    bf = lambda w: w.astype(MXU_DTYPE)

    tables = _rope_tables(positions)
    abr, abi, bbr, bbi = _s5_discretize(s5_a_re, s5_a_im, s5_log_dt, s5_b_re, s5_b_im)
    abr = abr.reshape(depth, 1, C_STATES)
    abi = abi.reshape(depth, 1, C_STATES)
    bbr = bbr.reshape(depth, C_GROUPS, C_GROUP_CH, C_STATE)
    bbi = bbi.reshape(depth, C_GROUPS, C_GROUP_CH, C_STATE)

    stacked = [bf(w) for w in (w_branch_a, w_branch_b, w_branch_c, w_out, w_mlp_up, w_mlp_down)]

    x2 = x.reshape(n, D_MODEL)
    for l in range(depth):
        qat, qbt, vat, vbt, ka, kvb, kseg, vseg, ga, uc = _in_projection(
            x2, norm_mix[l], *_pack_w_in(w_in[l]), tables, bsz, seq)

        kc, vct = _compress(kseg, vseg, nsa_cmp_pos[l].reshape(2, 1, CMP_LEN * HEAD_DIM), nsa_cmp_w1[l],
                            nsa_cmp_w2[l], bsz)
        ya = _nsa_attention(qat, ka, vat, kc, vct, ga, bsz, seq)

        yb = _swa_attention(qbt, kvb, vbt, swa_sinks[l], bsz, seq)

        cre_t = jnp.swapaxes(s5_c_re[l], 1, 2)
        cim_t = jnp.swapaxes(s5_c_im[l], 1, 2)
        halves = (slice(0, C_GROUPS // 2), slice(C_GROUPS // 2, C_GROUPS))
        bmat = jnp.stack([jnp.concatenate([_block_diag(bbr[l][g]), _block_diag(bbi[l][g])], axis=1)
                          for g in halves])
        cmat = jnp.stack([jnp.concatenate([_block_diag(cre_t[g]), -_block_diag(cim_t[g])], axis=0)
                          for g in halves])
        abar = jnp.concatenate([abr[l], abi[l]], axis=1)
        yc_tm = _s5(uc.reshape(bsz, seq, C_WIDTH), bf(bmat), abar, bf(cmat), s5_d[l], s5_glu_w[l], s5_glu_b[l])
        yc = jnp.swapaxes(yc_tm, 0, 1).reshape(n, C_WIDTH)

        x2 = _merge(x2, ya, yb, yc, norm_mix[l], bf(w_in[l][:, gate_col:]), *stacked[:4], l)
        x2 = _mlp(x2, norm_mlp[l], *stacked[4:], norm_final, l, l == depth - 1)
    return x2.reshape(bsz, seq, D_MODEL)
```

```python
import functools

import jax
import jax.numpy as jnp
from jax import lax
from jax.experimental import pallas as pl
from jax.experimental.pallas import tpu as pltpu

D_MODEL = 1024
HEAD_DIM = 64
ROPE_THETA = 10000.0
NORM_EPS = 1e-6
QBLK = 128

A_HEADS = 8
A_KV_HEADS = 2
A_GROUP = A_HEADS // A_KV_HEADS
A_WIDTH = A_HEADS * HEAD_DIM
NSA_BRANCHES = 3
CMP_LEN = 32
CMP_STRIDE = 16
CMP_HIDDEN = 2 * HEAD_DIM
SEL_LEN = 64
SEL_TOPK = 16
A_WINDOW = 512
FORCED_SCORE = 1e4

B_HEADS = 8
B_WIDTH = B_HEADS * HEAD_DIM
B_WINDOW = 128

C_WIDTH = 512
C_GROUP_CH = 16
C_GROUPS = C_WIDTH // C_GROUP_CH
C_STATE = 64
C_STATES = C_GROUPS * C_STATE
_HALF_CH = C_WIDTH // 2
_HALF_ST = C_STATES // 2

D_FF = 4 * D_MODEL
N_BRANCH = 3

IN_WIDTHS = (A_WIDTH, NSA_BRANCHES * 2 * A_KV_HEADS * HEAD_DIM, NSA_BRANCHES * A_HEADS,
             B_WIDTH, 2 * HEAD_DIM, C_WIDTH, N_BRANCH * D_MODEL)

LANES = 128
SUBLANES = 8
VMEM_LIMIT = 56 * 1024 * 1024

MXU_DTYPE = jnp.bfloat16
ROW_TILE = 1024
S5_CHUNK = 128
SEL_CHUNK = 256
SWA_QSUB = 4
NSA_QSUB = 2
FF_CHUNK = 1024

LOG2E = 1.4426950408889634
_NEG_INF = float("-inf")


def _f32(x):
    return x.astype(jnp.float32)


def _mm(a, b):
    return jnp.dot(a.astype(MXU_DTYPE), b.astype(MXU_DTYPE), preferred_element_type=jnp.float32)


def _mm_nt(a, b):
    return lax.dot_general(a.astype(MXU_DTYPE), b.astype(MXU_DTYPE), (((1,), (1,)), ((), ())),
                           preferred_element_type=jnp.float32)


def _rmsnorm(x, gain):
    return x * lax.rsqrt(jnp.mean(x * x, axis=-1, keepdims=True) + NORM_EPS) * gain


def _resident(shape):
    zeros = (0,) * len(shape)
    return pl.BlockSpec(shape, lambda *_: zeros, pipeline_mode=pl.Buffered(1))


def _resident_layer(shape, layer):
    zeros = (0,) * len(shape)
    return pl.BlockSpec((None,) + tuple(shape), lambda *_: (layer,) + zeros, pipeline_mode=pl.Buffered(1))


def _params(sem):
    return pltpu.CompilerParams(dimension_semantics=sem, vmem_limit_bytes=VMEM_LIMIT)


def _rope_table_kernel(pos_ref, invf_ref, sign_ref, cos_ref, sin_ref):
    ang = _f32(pos_ref[...]) * invf_ref[...]
    cos_ref[...] = jnp.cos(ang)
    sin_ref[...] = jnp.sin(ang) * sign_ref[...]


def _rope_tables(positions):
    n = positions.size
    half = HEAD_DIM // 2
    inv_freq = ROPE_THETA ** (-jnp.arange(half, dtype=jnp.float32) / half)
    invf = jnp.tile(inv_freq, LANES // half)
    sign = jnp.tile(jnp.concatenate([-jnp.ones(half, jnp.float32), jnp.ones(half, jnp.float32)]),
                    LANES // HEAD_DIM)
    tm = min(1024, n)
    natural = pl.pallas_call(
        _rope_table_kernel,
        grid=(n // tm,),
        in_specs=[pl.BlockSpec((tm, 1), lambda i: (i, 0)), _resident((1, LANES)), _resident((1, LANES))],
        out_specs=[pl.BlockSpec((tm, LANES), lambda i: (i, 0))] * 2,
        out_shape=[jax.ShapeDtypeStruct((n, LANES), jnp.float32)] * 2,
        compiler_params=_params(("parallel",)),
        name="rope_tables",
    )(positions.reshape(n, 1), invf.reshape(1, LANES), sign.reshape(1, LANES))
    transposed = pl.pallas_call(
        _rope_table_kernel,
        grid=(n // tm,),
        in_specs=[pl.BlockSpec((1, tm), lambda i: (0, i)), _resident((HEAD_DIM, 1)), _resident((HEAD_DIM, 1))],
        out_specs=[pl.BlockSpec((HEAD_DIM, tm), lambda i: (0, i))] * 2,
        out_shape=[jax.ShapeDtypeStruct((HEAD_DIM, n), jnp.float32)] * 2,
        compiler_params=_params(("parallel",)),
        name="rope_tables_t",
    )(positions.reshape(1, n), invf[:HEAD_DIM].reshape(HEAD_DIM, 1), sign[:HEAD_DIM].reshape(HEAD_DIM, 1))
    return natural + transposed


def _swap_halves(x):
    lane = lax.broadcasted_iota(jnp.int32, x.shape, 1)
    first = (lane & (HEAD_DIM - 1)) < HEAD_DIM // 2
    return jnp.where(first, pltpu.roll(x, LANES - HEAD_DIM // 2, 1), pltpu.roll(x, HEAD_DIM // 2, 1))


def _proj_kernel(x_ref, gain_ref, wn_ref, wt_ref, cos_ref, sin_ref, cost_ref, sint_ref,
                 qat_ref, qbt_ref, vat_ref, vbt_ref, ka_ref, kvb_ref, kseg_ref, vseg_ref, ga_ref, uc_ref,
                 seg_ref):
    h = _rmsnorm(x_ref[...], gain_ref[...]).astype(MXU_DTYPE)
    nat = jnp.dot(h, wn_ref[...], preferred_element_type=jnp.float32)
    tr = _mm_nt(wt_ref[...], h)

    cos = cos_ref[...]
    sin = sin_ref[...]

    def roped(j):
        xj = nat[:, j * LANES:(j + 1) * LANES]
        return xj, xj * cos + _swap_halves(xj) * sin

    for j in range(1, NSA_BRANCHES):
        ka_ref[:, (j - 1) * LANES:j * LANES] = roped(j)[1].astype(ka_ref.dtype)
    raw, rot = roped(3)
    lane = lax.broadcasted_iota(jnp.int32, raw.shape, 1)
    kvb_ref[...] = jnp.where(lane < HEAD_DIM, rot, raw).astype(kvb_ref.dtype)
    n_seg_tile = nat.shape[0] // CMP_STRIDE
    for j, (src, dst_ref) in enumerate(((roped(0)[1], kseg_ref), (nat[:, 4 * LANES:5 * LANES], vseg_ref))):
        seg_ref[j] = src
        for tok in range(CMP_STRIDE):
            dst_ref[:, tok * LANES:(tok + 1) * LANES] = (
                seg_ref[j, pl.ds(tok, n_seg_tile, stride=CMP_STRIDE), :].astype(dst_ref.dtype))
    ga_ref[...] = jax.nn.sigmoid(nat[:, 5 * LANES:6 * LANES])
    uc_ref[...] = nat[:, 6 * LANES:]

    cos_t = cost_ref[...]
    sin_t = sint_ref[...]
    half = HEAD_DIM // 2

    def roped_t(r0):
        xh = tr[r0:r0 + HEAD_DIM]
        return xh * cos_t + jnp.concatenate([xh[half:], xh[:half]], axis=0) * sin_t

    for hd in range(A_HEADS):
        qat_ref[hd * HEAD_DIM:(hd + 1) * HEAD_DIM, :] = roped_t(hd * HEAD_DIM).astype(qat_ref.dtype)
    for hd in range(B_HEADS):
        qbt_ref[hd * HEAD_DIM:(hd + 1) * HEAD_DIM, :] = roped_t(A_WIDTH + hd * HEAD_DIM).astype(qbt_ref.dtype)
    v0 = A_WIDTH + B_WIDTH
    vat_ref[...] = tr[v0:v0 + 2 * LANES].astype(vat_ref.dtype)
    vbt_ref[...] = tr[v0 + 2 * LANES:].astype(vbt_ref.dtype)


_NAT_COLS = 10 * LANES
_TR_ROWS = A_WIDTH + B_WIDTH + 2 * LANES + HEAD_DIM


def _pack_w_in(w_in):
    o = [0]
    for w in IN_WIDTHS:
        o.append(o[-1] + w)
    scale = HEAD_DIM ** -0.5 * LOG2E
    q_a = w_in[:, o[0]:o[1]] * scale
    kv_cols = [w_in[:, o[1] + j * LANES:o[1] + (j + 1) * LANES] for j in range(2 * NSA_BRANCHES)]
    k_a = jnp.concatenate(kv_cols[0::2], axis=1)
    v_a = jnp.concatenate(kv_cols[1::2], axis=1)
    g_a = jnp.pad(w_in[:, o[2]:o[3]], ((0, 0), (0, LANES - NSA_BRANCHES * A_HEADS)))
    q_b = w_in[:, o[3]:o[4]] * scale
    kv_b = w_in[:, o[4]:o[5]]
    u_c = w_in[:, o[5]:o[6]]
    w_nat = jnp.concatenate([k_a, kv_b, v_a[:, :LANES], g_a, u_c], axis=1)
    w_tr = jnp.concatenate([q_a, q_b, v_a[:, LANES:], kv_b[:, HEAD_DIM:]], axis=1).T
    return w_nat.astype(MXU_DTYPE), w_tr.astype(MXU_DTYPE)


def _in_projection(x2, gain, w_nat, w_tr, tables, bsz, seq):
    n = x2.shape[0]
    tm = min(ROW_TILE, seq)
    tiles = seq // tm

    def rows(width):
        return pl.BlockSpec((tm, width), lambda i: (i, 0))

    def lanes(height):
        return pl.BlockSpec((None, height, tm), lambda i: (i // tiles, 0, i % tiles))

    table_t = pl.BlockSpec((HEAD_DIM, tm), lambda i: (0, i))
    t_heights = (A_WIDTH, B_WIDTH, 2 * LANES, HEAD_DIM)
    widths = (2 * LANES, LANES, LANES, C_WIDTH)
    dtypes = (MXU_DTYPE, MXU_DTYPE, jnp.float32, jnp.float32)
    seg_rows = tm // CMP_STRIDE
    seg_spec = pl.BlockSpec((seg_rows, CMP_STRIDE * LANES), lambda i: (i, 0))
    seg_shape = jax.ShapeDtypeStruct((n // CMP_STRIDE, CMP_STRIDE * LANES), MXU_DTYPE)
    return pl.pallas_call(
        _proj_kernel,
        grid=(n // tm,),
        in_specs=[rows(D_MODEL), _resident((1, D_MODEL)), _resident((D_MODEL, _NAT_COLS)),
                  _resident((_TR_ROWS, D_MODEL)), rows(LANES), rows(LANES), table_t, table_t],
        out_specs=[lanes(hh) for hh in t_heights] + [rows(w) for w in widths[:2]] + [seg_spec, seg_spec]
        + [rows(w) for w in widths[2:]],
        out_shape=[jax.ShapeDtypeStruct((bsz, hh, seq), MXU_DTYPE) for hh in t_heights]
        + [jax.ShapeDtypeStruct((n, w), dt) for w, dt in zip(widths[:2], dtypes[:2])] + [seg_shape, seg_shape]
        + [jax.ShapeDtypeStruct((n, w), dt) for w, dt in zip(widths[2:], dtypes[2:])],
        scratch_shapes=[pltpu.VMEM((2, tm, LANES), jnp.float32)],
        compiler_params=_params(("parallel",)),
        name="in_projection",
    )(x2, gain.reshape(1, D_MODEL), w_nat, w_tr, *tables)


def _compress_kernel(kseg_ref, vseg_ref, pos_ref, w1_ref, w1x_ref, w2_ref, kc_ref, vct_ref):
    nseg = kseg_ref.shape[0]
    both = []
    for j, seg_ref in enumerate((kseg_ref, vseg_ref)):
        seg = seg_ref[...]
        pos_term = _mm(pos_ref[j], w1_ref[j])
        heads = []
        for h in range(A_KV_HEADS):
            hidden = (_mm(seg, w1x_ref[j, 0, h]) + pltpu.roll(_mm(seg, w1x_ref[j, 1, h]), nseg - 1, 0)
                      + pos_term)
            heads.append(_mm(jax.nn.gelu(hidden), w2_ref[j]))
        both.append(jnp.concatenate(heads, axis=1))
    kc_ref[...] = both[0].astype(kc_ref.dtype)
    vct_ref[...] = both[1].T.astype(vct_ref.dtype)


def _expand_w1(w1):
    w = w1.reshape(2, 2, CMP_STRIDE, HEAD_DIM, CMP_HIDDEN)
    eye = jnp.eye(A_KV_HEADS, dtype=w1.dtype)
    return jnp.einsum("jatdc,hg->jahtgdc", w, eye).reshape(2, 2, A_KV_HEADS, CMP_STRIDE * LANES, CMP_HIDDEN)


def _compress(kseg, vseg, pos_flat, w1, w2, bsz):
    nseg = kseg.shape[0] // bsz
    segw = kseg.shape[1]
    seg_spec = pl.BlockSpec((nseg, segw), lambda b: (b, 0))
    return pl.pallas_call(
        _compress_kernel,
        grid=(bsz,),
        in_specs=[seg_spec, seg_spec, _resident((2, 1, CMP_LEN * HEAD_DIM)),
                  _resident((2, CMP_LEN * HEAD_DIM, CMP_HIDDEN)),
                  _resident((2, 2, A_KV_HEADS, segw, CMP_HIDDEN)), _resident((2, CMP_HIDDEN, HEAD_DIM))],
        out_specs=[pl.BlockSpec((None, nseg, LANES), lambda b: (b, 0, 0)),
                   pl.BlockSpec((None, LANES, nseg), lambda b: (b, 0, 0))],
        out_shape=[jax.ShapeDtypeStruct((bsz, nseg, LANES), MXU_DTYPE),
                   jax.ShapeDtypeStruct((bsz, LANES, nseg), MXU_DTYPE)],
        compiler_params=_params(("parallel",)),
        name="nsa_compress",
    )(kseg, vseg, pos_flat.astype(MXU_DTYPE), w1.astype(MXU_DTYPE), _expand_w1(w1).astype(MXU_DTYPE),
      w2.astype(MXU_DTYPE))


def _split3(x):
    hi = x.astype(MXU_DTYPE)
    r1 = x - _f32(hi)
    mid = r1.astype(MXU_DTYPE)
    lo = (r1 - _f32(mid)).astype(MXU_DTYPE)
    return hi, mid, lo


def _col_max(s):
    return jnp.max(s, axis=0, keepdims=True)


def _tile_lanes(x, n):
    return jnp.concatenate([x] * n, axis=1)


def _pad_head_rows(x, h, fill):
    other = jnp.full(x.shape, fill, x.dtype)
    return jnp.concatenate([x, other] if h == 0 else [other, x], axis=0)


def _values_with_ones(vt):
    return jnp.concatenate([vt, jnp.ones((2 * SUBLANES, vt.shape[1]), vt.dtype)], axis=0)


def _store_transposed_pairs(o_ref, heads_t, row0=0):
    for p in range(len(heads_t) // 2):
        pair = jnp.concatenate([heads_t[2 * p], heads_t[2 * p + 1]], axis=0)
        o_ref[row0:row0 + QBLK, p * LANES:(p + 1) * LANES] = pair.T.astype(o_ref.dtype)


def _nsa_kernel(*refs, seq):
    for sb in range(NSA_QSUB):
        _nsa_block(sb, *refs, seq=seq)


def _nsa_block(sb, qt_ref, k_ref, vt_ref, kc_ref, vct_ref, g_ref, c2s_ref, o_ref,
               selb_ref, sca_ref, scb_ref, cmp_ref, win_ref, diag_ref, *, seq):
    q0 = (pl.program_id(1) * NSA_QSUB + sb) * QBLK
    q_cols = slice(sb * QBLK, (sb + 1) * QBLK)
    n_cmp = kc_ref.shape[0]
    n_sel = seq // SEL_LEN
    kc_sz = min(SEL_CHUNK, seq)
    span = min(A_WINDOW + QBLK, seq)
    blocks_per_chunk = kc_sz // SEL_LEN

    gates_t = g_ref[q_cols, :].T
    qp_lane = q0 + lax.broadcasted_iota(jnp.int32, (1, QBLK), 1)

    cmp_end = lax.broadcasted_iota(jnp.int32, (n_cmp, 1), 0) * CMP_STRIDE + (CMP_LEN - 1)
    bias_c = _tile_lanes(jnp.where(cmp_end <= qp_lane, 0.0, _NEG_INF), A_GROUP)

    w0 = pl.multiple_of(jnp.maximum(q0 + QBLK - span, 0), QBLK)
    diff = qp_lane - (w0 + lax.broadcasted_iota(jnp.int32, (span, 1), 0))
    bias_w = _tile_lanes(jnp.where(diff >= 0, jnp.where(diff < A_WINDOW, 0.0, _NEG_INF), _NEG_INF), A_GROUP)

    w_qs = []
    for h in range(A_KV_HEADS):
        q_t = jnp.concatenate([qt_ref[(h * A_GROUP + g) * HEAD_DIM:(h * A_GROUP + g + 1) * HEAD_DIM, q_cols]
                               for g in range(A_GROUP)], axis=1)
        w_qs.append(_pad_head_rows(q_t, h, 0.0))

    n_chunks = (q0 + kc_sz - 1) // kc_sz
    last = jnp.maximum(n_chunks - 1, 0)

    def issue_scores(c, dst_ref):
        k0 = pl.multiple_of(jnp.minimum(c, last) * kc_sz, kc_sz)
        keys = k_ref[pl.ds(k0, kc_sz), :LANES]
        for h in range(A_KV_HEADS):
            dst_ref[h] = _mm(keys, w_qs[h])

    for h in range(A_KV_HEADS):
        cmp_ref[h] = _mm(kc_ref[...], w_qs[h])
    for h in range(A_KV_HEADS):
        win_ref[h] = _mm(k_ref[pl.ds(w0, span), LANES:], w_qs[h])
    issue_scores(0, sca_ref)
    for h in range(A_KV_HEADS):
        diag_ref[h] = _mm(k_ref[pl.ds(pl.multiple_of(q0, QBLK), QBLK), :LANES], w_qs[h])

    o_cs, o_ws = [], []
    for h in range(A_KV_HEADS):
        s = cmp_ref[h] + bias_c
        m = _col_max(s)
        m = jnp.where(m == _NEG_INF, 0.0, m)
        e = jnp.exp2(s - m)
        p = e * (1.0 / jnp.maximum(jnp.sum(e, axis=0, keepdims=True), 1e-30))
        o_cs.append(_mm(vct_ref[...], p)[h * HEAD_DIM:(h + 1) * HEAD_DIM])

        p_sum = p[:, 0:QBLK]
        for g in range(1, A_GROUP):
            p_sum = p_sum + p[:, g * QBLK:(g + 1) * QBLK]
        imp_t = sum(_mm(c2s_ref[...], part) for part in _split3(p_sum))
        blk = lax.broadcasted_iota(jnp.int32, (n_sel, QBLK), 0)
        cur = jnp.right_shift(qp_lane, 6)
        forced = (blk == 0) | (blk == cur) | (blk == cur - 1)
        score = jnp.where(forced, FORCED_SCORE, imp_t)
        score = jnp.where(blk <= cur, score, _NEG_INF)
        n_grp = n_sel // SUBLANES
        sub = lax.broadcasted_iota(jnp.int32, (SUBLANES, QBLK), 0)
        score_grp = [score[SUBLANES * v:SUBLANES * (v + 1), :] for v in range(n_grp)]
        rank_grp = [jnp.zeros((SUBLANES, QBLK), jnp.float32) for _ in range(n_grp)]
        for j in range(n_sel):
            row = jnp.broadcast_to(score[j:j + 1, :], (SUBLANES, QBLK))
            vj, rj = divmod(j, SUBLANES)
            for v in range(n_grp):
                if v > vj:
                    inc = jnp.where(row >= score_grp[v], 1.0, 0.0)
                elif v < vj:
                    inc = jnp.where(row > score_grp[v], 1.0, 0.0)
                else:
                    inc = jnp.where(sub > rj, jnp.where(row >= score_grp[v], 1.0, 0.0),
                                    jnp.where(row > score_grp[v], 1.0, 0.0))
                rank_grp[v] = rank_grp[v] + inc
        rank = jnp.concatenate(rank_grp, axis=0)
        picked = jnp.where(rank < float(min(SEL_TOPK, n_sel)), 0.0, _NEG_INF)
        selb_ref[h] = jnp.where(blk < jnp.right_shift(q0, 6), picked, _NEG_INF)

    for h in range(A_KV_HEADS):
        sw = win_ref[h] + bias_w
        pw = jnp.exp2(sw - _col_max(sw))
        acc_w = _mm(_values_with_ones(vt_ref[LANES + h * HEAD_DIM:LANES + (h + 1) * HEAD_DIM, pl.ds(w0, span)]), pw)
        o_ws.append(acc_w[:HEAD_DIM] / acc_w[HEAD_DIM:HEAD_DIM + 1])

    def consume(c, src_ref, states):
        cc = jnp.minimum(c, last)
        k0 = pl.multiple_of(cc * kc_sz, kc_sz)
        out = []
        for h in range(A_KV_HEADS):
            m_run, acc = states[h]
            rows = [_tile_lanes(jnp.where(c < n_chunks, selb_ref[h, pl.ds(cc * blocks_per_chunk + jj, 1), :], _NEG_INF),
                                A_GROUP) for jj in range(blocks_per_chunk)]
            blocks = [src_ref[h, jj * SEL_LEN:(jj + 1) * SEL_LEN, :] for jj in range(blocks_per_chunk)]
            part = None
            for sc_blk, row in zip(blocks, rows):
                blk_max = jnp.max(sc_blk.reshape(SEL_LEN // SUBLANES, SUBLANES, A_GROUP * QBLK), axis=0) + row
                part = blk_max if part is None else jnp.maximum(part, blk_max)
            m_new = jnp.maximum(m_run, _col_max(part))
            pr = jnp.concatenate([jnp.exp2(sc_blk + (row - m_new)) for sc_blk, row in zip(blocks, rows)], axis=0)
            vt = _values_with_ones(vt_ref[h * HEAD_DIM:(h + 1) * HEAD_DIM, pl.ds(k0, kc_sz)])
            out.append((m_new, jnp.exp2(m_run - m_new) * acc + _mm(vt, pr)))
        return tuple(out)

    def sel_step(t, states):
        issue_scores(2 * t + 1, scb_ref)
        states = consume(2 * t, sca_ref, states)
        issue_scores(2 * t + 2, sca_ref)
        return consume(2 * t + 1, scb_ref, states)

    kp_diag = q0 + lax.broadcasted_iota(jnp.int32, (QBLK, 1), 0)
    bias_d = _tile_lanes(jnp.where(kp_diag <= qp_lane, 0.0, _NEG_INF), A_GROUP)
    q0_al = pl.multiple_of(q0, QBLK)
    init = []
    for h in range(A_KV_HEADS):
        sd = diag_ref[h] + bias_d
        m0 = _col_max(sd)
        vt = _values_with_ones(vt_ref[h * HEAD_DIM:(h + 1) * HEAD_DIM, pl.ds(q0_al, QBLK)])
        init.append((m0, _mm(vt, jnp.exp2(sd - m0))))
    init = tuple(init)
    sel_out = lax.fori_loop(0, (n_chunks + 1) // 2, sel_step, init)

    heads_t = []
    for h in range(A_KV_HEADS):
        acc_s = sel_out[h][1]
        o_s = acc_s[:HEAD_DIM] / jnp.maximum(acc_s[HEAD_DIM:HEAD_DIM + 1], 1e-30)
        for g in range(A_GROUP):
            hd = h * A_GROUP + g
            cols = slice(g * QBLK, (g + 1) * QBLK)
            heads_t.append(gates_t[hd:hd + 1, :] * o_cs[h][:, cols]
                           + gates_t[A_HEADS + hd:A_HEADS + hd + 1, :] * o_s[:, cols]
                           + gates_t[2 * A_HEADS + hd:2 * A_HEADS + hd + 1, :] * o_ws[h][:, cols])
    _store_transposed_pairs(o_ref, heads_t, sb * QBLK)


def _nsa_attention(qat, ka, vat, kc, vct, ga, bsz, seq):
    rows = NSA_QSUB * QBLK
    nstep = seq // rows
    n_seg = seq // CMP_STRIDE
    n_sel = seq // SEL_LEN
    assert seq >= A_WINDOW + QBLK and seq % SEL_CHUNK == 0 and seq % rows == 0
    c_start = jnp.arange(n_seg) * CMP_STRIDE
    s_start = jnp.arange(n_sel) * SEL_LEN
    overlap = (jnp.minimum(c_start[None, :] + CMP_LEN, s_start[:, None] + SEL_LEN)
               - jnp.maximum(c_start[None, :], s_start[:, None]))
    c2s = (jnp.clip(overlap, 0, None).astype(jnp.float32) / CMP_LEN).astype(MXU_DTYPE)
    return pl.pallas_call(
        functools.partial(_nsa_kernel, seq=seq),
        grid=(bsz, nstep),
        in_specs=[
            pl.BlockSpec((None, A_WIDTH, rows), lambda b, i: (b, 0, i)),
            pl.BlockSpec((seq, 2 * LANES), lambda b, i: (b, 0)),
            pl.BlockSpec((None, 2 * LANES, seq), lambda b, i: (b, 0, 0)),
            pl.BlockSpec((None, n_seg, LANES), lambda b, i: (b, 0, 0)),
            pl.BlockSpec((None, LANES, n_seg), lambda b, i: (b, 0, 0)),
            pl.BlockSpec((rows, LANES), lambda b, i: (b * nstep + i, 0)),
            _resident((n_sel, n_seg)),
        ],
        out_specs=pl.BlockSpec((rows, A_WIDTH), lambda b, i: (b * nstep + i, 0)),
        out_shape=jax.ShapeDtypeStruct((bsz * seq, A_WIDTH), MXU_DTYPE),
        scratch_shapes=[pltpu.VMEM((A_KV_HEADS, n_sel, QBLK), jnp.float32)]
        + [pltpu.VMEM((A_KV_HEADS, min(SEL_CHUNK, seq), A_GROUP * QBLK), jnp.float32)] * 2
        + [pltpu.VMEM((A_KV_HEADS, n_seg, A_GROUP * QBLK), jnp.float32),
           pltpu.VMEM((A_KV_HEADS, min(A_WINDOW + QBLK, seq), A_GROUP * QBLK), jnp.float32),
           pltpu.VMEM((A_KV_HEADS, QBLK, A_GROUP * QBLK), jnp.float32)],
        compiler_params=_params(("parallel", "arbitrary")),
        name="nsa_attention",
    )(qat, ka, vat, kc, vct, ga, c2s)


def _swa_kernel(sink_ref, qt_ref, kv_ref, vt_ref, o_ref, sc_ref, *, seq):
    i = pl.program_id(1)
    span = min(B_WINDOW + QBLK, seq)
    sink = jnp.concatenate([jnp.full((1, QBLK), sink_ref[hd] * LOG2E, jnp.float32) for hd in range(B_HEADS)],
                           axis=1)
    starts = []
    for sb in range(SWA_QSUB):
        q0 = (i * SWA_QSUB + sb) * QBLK
        w0 = pl.multiple_of(jnp.maximum(q0 + QBLK - span, 0), QBLK)
        starts.append((q0, w0))
        q_t = jnp.concatenate([qt_ref[hd * HEAD_DIM:(hd + 1) * HEAD_DIM, sb * QBLK:(sb + 1) * QBLK]
                               for hd in range(B_HEADS)], axis=1)
        sc_ref[sb] = _mm(kv_ref[pl.ds(w0, span), :], _pad_head_rows(q_t, 0, 0.0))
    for sb, (q0, w0) in enumerate(starts):
        qp_lane = q0 + lax.broadcasted_iota(jnp.int32, (1, QBLK), 1)
        diff = qp_lane - (w0 + lax.broadcasted_iota(jnp.int32, (span, 1), 0))
        bias = _tile_lanes(jnp.where(diff >= 0, jnp.where(diff < B_WINDOW, 0.0, _NEG_INF), _NEG_INF), B_HEADS)
        s = sc_ref[sb] + bias
        m = jnp.maximum(_col_max(s), sink)
        acc = _mm(_values_with_ones(vt_ref[:, pl.ds(w0, span)]), jnp.exp2(s - m))
        o_t = acc[:HEAD_DIM] / (acc[HEAD_DIM:HEAD_DIM + 1] + jnp.exp2(sink - m))
        _store_transposed_pairs(o_ref, [o_t[:, hd * QBLK:(hd + 1) * QBLK] for hd in range(B_HEADS)], sb * QBLK)


def _swa_attention(qbt, kvb, vbt, sinks, bsz, seq):
    rows = SWA_QSUB * QBLK
    nstep = seq // rows
    span = min(B_WINDOW + QBLK, seq)
    assert seq >= B_WINDOW + QBLK and seq % rows == 0
    return pl.pallas_call(
        functools.partial(_swa_kernel, seq=seq),
        grid=(bsz, nstep),
        in_specs=[
            pl.BlockSpec(memory_space=pltpu.SMEM),
            pl.BlockSpec((None, B_WIDTH, rows), lambda b, i: (b, 0, i)),
            pl.BlockSpec((seq, LANES), lambda b, i: (b, 0)),
            pl.BlockSpec((None, HEAD_DIM, seq), lambda b, i: (b, 0, 0)),
        ],
        out_specs=pl.BlockSpec((rows, B_WIDTH), lambda b, i: (b * nstep + i, 0)),
        out_shape=jax.ShapeDtypeStruct((bsz * seq, B_WIDTH), MXU_DTYPE),
        scratch_shapes=[pltpu.VMEM((SWA_QSUB, span, B_HEADS * QBLK), jnp.float32)],
        compiler_params=_params(("parallel", "arbitrary")),
        name="swa_attention",
    )(sinks, qbt, kvb, vbt)


def _s5_discretize_kernel(are_ref, aim_ref, logdt_ref, bre_ref, bim_ref,
                          abr_ref, abi_ref, bbr_ref, bbi_ref):
    ar = are_ref[...]
    ai = aim_ref[...]
    dt = jnp.exp(logdt_ref[...])
    mag = jnp.exp(dt * ar)
    abr = mag * jnp.cos(dt * ai)
    abi = mag * jnp.sin(dt * ai)
    den = ar * ar + ai * ai
    nr = abr - 1.0
    coef_r = (nr * ar + abi * ai) / den
    coef_i = (abi * ar - nr * ai) / den
    abr_ref[...] = abr
    abi_ref[...] = abi
    br = bre_ref[...]
    bi = bim_ref[...]
    bbr_ref[...] = coef_r[:, None, :] * br - coef_i[:, None, :] * bi
    bbi_ref[...] = coef_r[:, None, :] * bi + coef_i[:, None, :] * br


def _s5_discretize(a_re, a_im, log_dt, b_re, b_im):
    lg = a_re.shape[0] * a_re.shape[1]
    a2 = lambda a: a.reshape(lg, C_STATE)
    b3 = lambda b: jnp.swapaxes(b.reshape(lg, C_STATE, C_GROUP_CH), 1, 2)
    sa = jax.ShapeDtypeStruct((lg, C_STATE), jnp.float32)
    sb = jax.ShapeDtypeStruct((lg, C_GROUP_CH, C_STATE), jnp.float32)
    return pl.pallas_call(_s5_discretize_kernel, out_shape=[sa, sa, sb, sb], name="s5_discretize")(
        a2(a_re), a2(a_im), log_dt.reshape(lg, 1), b3(b_re), b3(b_im))


def _s5_kernel(u_ref, bmat_ref, abar_ref, cmat_ref, d_ref, gw_ref, gb_ref, y_ref, state_ref, ut_ref, xs_ref):
    bsz, t_steps, _ = u_ref.shape

    @pl.when(pl.program_id(0) == 0)
    def _():
        state_ref[...] = jnp.zeros_like(state_ref)

    def gather(t, carry):
        ut_ref[pl.ds(pl.multiple_of(t * bsz, bsz), bsz), :] = u_ref[:, t, :]
        return carry

    lax.fori_loop(0, t_steps, gather, 0, unroll=8)
    u = ut_ref[...]
    for hf in range(2):
        bu = _mm(u[:, hf * _HALF_CH:(hf + 1) * _HALF_CH], bmat_ref[hf])
        xs_ref[:, hf * _HALF_ST:(hf + 1) * _HALF_ST] = bu[:, :_HALF_ST]
        xs_ref[:, C_STATES + hf * _HALF_ST:C_STATES + (hf + 1) * _HALF_ST] = bu[:, _HALF_ST:]

    ys = []
    for hf in range(2):
        re = slice(hf * _HALF_ST, (hf + 1) * _HALF_ST)
        im = slice(C_STATES + hf * _HALF_ST, C_STATES + (hf + 1) * _HALF_ST)
        ar = jnp.broadcast_to(abar_ref[0:1, re], (bsz, _HALF_ST))
        ai = jnp.broadcast_to(abar_ref[0:1, im], (bsz, _HALF_ST))
        xr = state_ref[:, re]
        xi = state_ref[:, im]
        for t in range(t_steps):
            rows = slice(t * bsz, (t + 1) * bsz)
            xr, xi = (ar * xr - ai * xi + xs_ref[rows, re], ar * xi + ai * xr + xs_ref[rows, im])
            xs_ref[rows, re] = xr
            xs_ref[rows, im] = xi
        state_ref[:, re] = xr
        state_ref[:, im] = xi
        ys.append(_mm(xs_ref[:, re], cmat_ref[hf, :_HALF_ST, :]) + _mm(xs_ref[:, im], cmat_ref[hf, _HALF_ST:, :]))
    y = jnp.concatenate(ys, axis=1) + d_ref[...] * u
    z = jax.nn.gelu(y)
    out = z * jax.nn.sigmoid(_mm(z, gw_ref[...]) + gb_ref[...])
    y_ref[...] = out.reshape(t_steps, bsz, C_WIDTH).astype(y_ref.dtype)


def _s5(u, bmat, abar, cmat, d, glu_w, glu_b):
    bsz, seq, _ = u.shape
    t_steps = min(S5_CHUNK, seq)
    return pl.pallas_call(
        _s5_kernel,
        grid=(seq // t_steps,),
        in_specs=[
            pl.BlockSpec((bsz, t_steps, C_WIDTH), lambda i: (0, i, 0)),
            _resident((2, _HALF_CH, 2 * _HALF_ST)), _resident((1, 2 * C_STATES)), _resident((2, 2 * _HALF_ST, _HALF_CH)),
            _resident((1, C_WIDTH)), _resident((C_WIDTH, C_WIDTH)), _resident((1, C_WIDTH)),
        ],
        out_specs=pl.BlockSpec((t_steps, bsz, C_WIDTH), lambda i: (i, 0, 0)),
        out_shape=jax.ShapeDtypeStruct((seq, bsz, C_WIDTH), MXU_DTYPE),
        scratch_shapes=[pltpu.VMEM((bsz, 2 * C_STATES), jnp.float32),
                        pltpu.VMEM((t_steps * bsz, C_WIDTH), jnp.float32),
                        pltpu.VMEM((t_steps * bsz, 2 * C_STATES), jnp.float32)],
        compiler_params=_params(("arbitrary",)),
        name="s5_scan",
    )(u, bmat, abar, cmat, d.reshape(1, C_WIDTH), glu_w.astype(MXU_DTYPE), glu_b.reshape(1, C_WIDTH))


def _block_diag(per_group):
    g, r, c = per_group.shape
    eye = jnp.eye(g, dtype=per_group.dtype)
    return jnp.einsum("grc,gk->grkc", per_group, eye).reshape(g * r, g * c)


def _merge_kernel(x_ref, ya_ref, yb_ref, yc_ref, gain_ref, wg_ref, pa_ref, pb_ref, pc_ref, wo_ref, o_ref):
    x = x_ref[...]
    h = _rmsnorm(x, gain_ref[...]).astype(MXU_DTYPE)
    merged = None
    for j, (y_ref, p_ref) in enumerate(((ya_ref, pa_ref), (yb_ref, pb_ref), (yc_ref, pc_ref))):
        gate = jax.nn.sigmoid(jnp.dot(h, wg_ref[:, j * D_MODEL:(j + 1) * D_MODEL],
                                      preferred_element_type=jnp.float32))
        term = gate * jnp.dot(y_ref[...], p_ref[...], preferred_element_type=jnp.float32)
        merged = term if merged is None else merged + term
    o_ref[...] = x + _mm(merged, wo_ref[...])


def _merge(x2, ya, yb, yc, gain, wg, pa, pb, pc, wo, layer):
    n = x2.shape[0]
    tm = min(ROW_TILE, n)

    def rows(width):
        return pl.BlockSpec((tm, width), lambda i: (i, 0))

    return pl.pallas_call(
        _merge_kernel,
        grid=(n // tm,),
        in_specs=[rows(D_MODEL), rows(A_WIDTH), rows(B_WIDTH), rows(C_WIDTH), _resident((1, D_MODEL)),
                  _resident((D_MODEL, N_BRANCH * D_MODEL)), _resident_layer((A_WIDTH, D_MODEL), layer),
                  _resident_layer((B_WIDTH, D_MODEL), layer), _resident_layer((C_WIDTH, D_MODEL), layer),
                  _resident_layer((D_MODEL, D_MODEL), layer)],
        out_specs=rows(D_MODEL),
        out_shape=jax.ShapeDtypeStruct((n, D_MODEL), jnp.float32),
        compiler_params=_params(("parallel",)),
        name="merge",
    )(x2, ya, yb, yc, gain.reshape(1, D_MODEL), wg, pa, pb, pc, wo)


def _mlp_kernel(x_ref, gain_ref, wu_ref, wd_ref, fgain_ref, o_ref, *, final_norm):
    x = x_ref[...]
    h = _rmsnorm(x, gain_ref[...]).astype(MXU_DTYPE)
    acc = x
    for c in range(D_FF // FF_CHUNK):
        cols = slice(c * FF_CHUNK, (c + 1) * FF_CHUNK)
        up = jnp.dot(h, wu_ref[:, cols], preferred_element_type=jnp.float32)
        acc = acc + _mm(jnp.square(jnp.maximum(up, 0.0)), wd_ref[cols, :])
    o_ref[...] = _rmsnorm(acc, fgain_ref[...]) if final_norm else acc


def _mlp(x2, gain, wu, wd, fgain, layer, final_norm):
    n = x2.shape[0]
    tm = min(ROW_TILE, n)
    rows = pl.BlockSpec((tm, D_MODEL), lambda i: (i, 0))
    return pl.pallas_call(
        functools.partial(_mlp_kernel, final_norm=final_norm),
        grid=(n // tm,),
        in_specs=[rows, _resident((1, D_MODEL)), _resident_layer((D_MODEL, D_FF), layer),
                  _resident_layer((D_FF, D_MODEL), layer), _resident((1, D_MODEL))],
        out_specs=rows,
        out_shape=jax.ShapeDtypeStruct((n, D_MODEL), jnp.float32),
        compiler_params=_params(("parallel",)),
        name="mlp",
    )(x2, gain.reshape(1, D_MODEL), wu, wd, fgain.reshape(1, D_MODEL))


def kernel(x, positions, norm_mix, w_in, nsa_cmp_pos, nsa_cmp_w1, nsa_cmp_w2, swa_sinks, s5_a_re, s5_a_im, s5_log_dt, s5_b_re, s5_b_im, s5_c_re, s5_c_im, s5_d, s5_glu_w, s5_glu_b, w_branch_a, w_branch_b, w_branch_c, w_out, norm_mlp, w_mlp_up, w_mlp_down, norm_final):
    bsz, seq, _ = x.shape
    depth = w_in.shape[0]
    n = bsz * seq
    gate_col = sum(IN_WIDTHS[:-1])
    bf = lambda w: w.astype(MXU_DTYPE)

    tables = _rope_tables(positions)
    abr, abi, bbr, bbi = _s5_discretize(s5_a_re, s5_a_im, s5_log_dt, s5_b_re, s5_b_im)
    abr = abr.reshape(depth, 1, C_STATES)
    abi = abi.reshape(depth, 1, C_STATES)
    bbr = bbr.reshape(depth, C_GROUPS, C_GROUP_CH, C_STATE)
    bbi = bbi.reshape(depth, C_GROUPS, C_GROUP_CH, C_STATE)

    stacked = [bf(w) for w in (w_branch_a, w_branch_b, w_branch_c, w_out, w_mlp_up, w_mlp_down)]

    x2 = x.reshape(n, D_MODEL)
    for l in range(depth):
        qat, qbt, vat, vbt, ka, kvb, kseg, vseg, ga, uc = _in_projection(
            x2, norm_mix[l], *_pack_w_in(w_in[l]), tables, bsz, seq)

        kc, vct = _compress(kseg, vseg, nsa_cmp_pos[l].reshape(2, 1, CMP_LEN * HEAD_DIM), nsa_cmp_w1[l],
                            nsa_cmp_w2[l], bsz)
        ya = _nsa_attention(qat, ka, vat, kc, vct, ga, bsz, seq)

        yb = _swa_attention(qbt, kvb, vbt, swa_sinks[l], bsz, seq)

        cre_t = jnp.swapaxes(s5_c_re[l], 1, 2)
        cim_t = jnp.swapaxes(s5_c_im[l], 1, 2)
        halves = (slice(0, C_GROUPS // 2), slice(C_GROUPS // 2, C_GROUPS))
        bmat = jnp.stack([jnp.concatenate([_block_diag(bbr[l][g]), _block_diag(bbi[l][g])], axis=1)
                          for g in halves])
        cmat = jnp.stack([jnp.concatenate([_block_diag(cre_t[g]), -_block_diag(cim_t[g])], axis=0)
                          for g in halves])
        abar = jnp.concatenate([abr[l], abi[l]], axis=1)
        yc_tm = _s5(uc.reshape(bsz, seq, C_WIDTH), bf(bmat), abar, bf(cmat), s5_d[l], s5_glu_w[l], s5_glu_b[l])
        yc = jnp.swapaxes(yc_tm, 0, 1).reshape(n, C_WIDTH)

        x2 = _merge(x2, ya, yb, yc, norm_mix[l], bf(w_in[l][:, gate_col:]), *stacked[:4], l)
        x2 = _mlp(x2, norm_mlp[l], *stacked[4:], norm_final, l, l == depth - 1)
    return x2.reshape(bsz, seq, D_MODEL)
```

```python
import functools

import jax
import jax.numpy as jnp
from jax import lax
from jax.experimental import pallas as pl
from jax.experimental.pallas import tpu as pltpu

D_MODEL = 1024
HEAD_DIM = 64
ROPE_THETA = 10000.0
NORM_EPS = 1e-6
QBLK = 128

A_HEADS = 8
A_KV_HEADS = 2
A_GROUP = A_HEADS // A_KV_HEADS
A_WIDTH = A_HEADS * HEAD_DIM
NSA_BRANCHES = 3
CMP_LEN = 32
CMP_STRIDE = 16
CMP_HIDDEN = 2 * HEAD_DIM
SEL_LEN = 64
SEL_TOPK = 16
A_WINDOW = 512
FORCED_SCORE = 1e4

B_HEADS = 8
B_WIDTH = B_HEADS * HEAD_DIM
B_WINDOW = 128

C_WIDTH = 512
C_GROUP_CH = 16
C_GROUPS = C_WIDTH // C_GROUP_CH
C_STATE = 64
C_STATES = C_GROUPS * C_STATE
_HALF_CH = C_WIDTH // 2
_HALF_ST = C_STATES // 2

D_FF = 4 * D_MODEL
N_BRANCH = 3

IN_WIDTHS = (A_WIDTH, NSA_BRANCHES * 2 * A_KV_HEADS * HEAD_DIM, NSA_BRANCHES * A_HEADS,
             B_WIDTH, 2 * HEAD_DIM, C_WIDTH, N_BRANCH * D_MODEL)

LANES = 128
SUBLANES = 8
VMEM_LIMIT = 56 * 1024 * 1024

MXU_DTYPE = jnp.bfloat16
ROW_TILE = 1024
S5_CHUNK = 128
SEL_CHUNK = 256
SWA_QSUB = 4
NSA_QSUB = 4
FF_CHUNK = 1024

LOG2E = 1.4426950408889634
_NEG_INF = float("-inf")


def _f32(x):
    return x.astype(jnp.float32)


def _mm(a, b):
    return jnp.dot(a.astype(MXU_DTYPE), b.astype(MXU_DTYPE), preferred_element_type=jnp.float32)


def _mm_nt(a, b):
    return lax.dot_general(a.astype(MXU_DTYPE), b.astype(MXU_DTYPE), (((1,), (1,)), ((), ())),
                           preferred_element_type=jnp.float32)


def _rmsnorm(x, gain):
    return x * lax.rsqrt(jnp.mean(x * x, axis=-1, keepdims=True) + NORM_EPS) * gain


def _resident(shape):
    zeros = (0,) * len(shape)
    return pl.BlockSpec(shape, lambda *_: zeros, pipeline_mode=pl.Buffered(1))


def _resident_layer(shape, layer):
    zeros = (0,) * len(shape)
    return pl.BlockSpec((None,) + tuple(shape), lambda *_: (layer,) + zeros, pipeline_mode=pl.Buffered(1))


def _params(sem):
    return pltpu.CompilerParams(dimension_semantics=sem, vmem_limit_bytes=VMEM_LIMIT)


def _rope_table_kernel(pos_ref, invf_ref, sign_ref, cos_ref, sin_ref):
    ang = _f32(pos_ref[...]) * invf_ref[...]
    cos_ref[...] = jnp.cos(ang)
    sin_ref[...] = jnp.sin(ang) * sign_ref[...]


def _rope_tables(positions):
    n = positions.size
    half = HEAD_DIM // 2
    inv_freq = ROPE_THETA ** (-jnp.arange(half, dtype=jnp.float32) / half)
    invf = jnp.tile(inv_freq, LANES // half)
    sign = jnp.tile(jnp.concatenate([-jnp.ones(half, jnp.float32), jnp.ones(half, jnp.float32)]),
                    LANES // HEAD_DIM)
    tm = min(1024, n)
    natural = pl.pallas_call(
        _rope_table_kernel,
        grid=(n // tm,),
        in_specs=[pl.BlockSpec((tm, 1), lambda i: (i, 0)), _resident((1, LANES)), _resident((1, LANES))],
        out_specs=[pl.BlockSpec((tm, LANES), lambda i: (i, 0))] * 2,
        out_shape=[jax.ShapeDtypeStruct((n, LANES), jnp.float32)] * 2,
        compiler_params=_params(("parallel",)),
        name="rope_tables",
    )(positions.reshape(n, 1), invf.reshape(1, LANES), sign.reshape(1, LANES))
    transposed = pl.pallas_call(
        _rope_table_kernel,
        grid=(n // tm,),
        in_specs=[pl.BlockSpec((1, tm), lambda i: (0, i)), _resident((HEAD_DIM, 1)), _resident((HEAD_DIM, 1))],
        out_specs=[pl.BlockSpec((HEAD_DIM, tm), lambda i: (0, i))] * 2,
        out_shape=[jax.ShapeDtypeStruct((HEAD_DIM, n), jnp.float32)] * 2,
        compiler_params=_params(("parallel",)),
        name="rope_tables_t",
    )(positions.reshape(1, n), invf[:HEAD_DIM].reshape(HEAD_DIM, 1), sign[:HEAD_DIM].reshape(HEAD_DIM, 1))
    return natural + transposed


def _swap_halves(x):
    lane = lax.broadcasted_iota(jnp.int32, x.shape, 1)
    first = (lane & (HEAD_DIM - 1)) < HEAD_DIM // 2
    return jnp.where(first, pltpu.roll(x, LANES - HEAD_DIM // 2, 1), pltpu.roll(x, HEAD_DIM // 2, 1))


def _proj_kernel(x_ref, gain_ref, wn_ref, wt_ref, cos_ref, sin_ref, cost_ref, sint_ref,
                 qat_ref, qbt_ref, vat_ref, vbt_ref, ka_ref, kvb_ref, kseg_ref, vseg_ref, ga_ref, uc_ref,
                 seg_ref):
    h = _rmsnorm(x_ref[...], gain_ref[...]).astype(MXU_DTYPE)
    nat = jnp.dot(h, wn_ref[...], preferred_element_type=jnp.float32)
    tr = _mm_nt(wt_ref[...], h)

    cos = cos_ref[...]
    sin = sin_ref[...]

    def roped(j):
        xj = nat[:, j * LANES:(j + 1) * LANES]
        return xj, xj * cos + _swap_halves(xj) * sin

    for j in range(1, NSA_BRANCHES):
        ka_ref[:, (j - 1) * LANES:j * LANES] = roped(j)[1].astype(ka_ref.dtype)
    raw, rot = roped(3)
    lane = lax.broadcasted_iota(jnp.int32, raw.shape, 1)
    kvb_ref[...] = jnp.where(lane < HEAD_DIM, rot, raw).astype(kvb_ref.dtype)
    n_seg_tile = nat.shape[0] // CMP_STRIDE
    for j, (src, dst_ref) in enumerate(((roped(0)[1], kseg_ref), (nat[:, 4 * LANES:5 * LANES], vseg_ref))):
        seg_ref[j] = src
        for tok in range(CMP_STRIDE):
            dst_ref[:, tok * LANES:(tok + 1) * LANES] = (
                seg_ref[j, pl.ds(tok, n_seg_tile, stride=CMP_STRIDE), :].astype(dst_ref.dtype))
    ga_ref[...] = jax.nn.sigmoid(nat[:, 5 * LANES:6 * LANES])
    uc_ref[...] = nat[:, 6 * LANES:]

    cos_t = cost_ref[...]
    sin_t = sint_ref[...]
    half = HEAD_DIM // 2

    def roped_t(r0):
        xh = tr[r0:r0 + HEAD_DIM]
        return xh * cos_t + jnp.concatenate([xh[half:], xh[:half]], axis=0) * sin_t

    for hd in range(A_HEADS):
        qat_ref[hd * HEAD_DIM:(hd + 1) * HEAD_DIM, :] = roped_t(hd * HEAD_DIM).astype(qat_ref.dtype)
    for hd in range(B_HEADS):
        qbt_ref[hd * HEAD_DIM:(hd + 1) * HEAD_DIM, :] = roped_t(A_WIDTH + hd * HEAD_DIM).astype(qbt_ref.dtype)
    v0 = A_WIDTH + B_WIDTH
    vat_ref[...] = tr[v0:v0 + 2 * LANES].astype(vat_ref.dtype)
    vbt_ref[...] = tr[v0 + 2 * LANES:].astype(vbt_ref.dtype)


_NAT_COLS = 10 * LANES
_TR_ROWS = A_WIDTH + B_WIDTH + 2 * LANES + HEAD_DIM


def _pack_w_in(w_in):
    o = [0]
    for w in IN_WIDTHS:
        o.append(o[-1] + w)
    scale = HEAD_DIM ** -0.5 * LOG2E
    q_a = w_in[:, o[0]:o[1]] * scale
    kv_cols = [w_in[:, o[1] + j * LANES:o[1] + (j + 1) * LANES] for j in range(2 * NSA_BRANCHES)]
    k_a = jnp.concatenate(kv_cols[0::2], axis=1)
    v_a = jnp.concatenate(kv_cols[1::2], axis=1)
    g_a = jnp.pad(w_in[:, o[2]:o[3]], ((0, 0), (0, LANES - NSA_BRANCHES * A_HEADS)))
    q_b = w_in[:, o[3]:o[4]] * scale
    kv_b = w_in[:, o[4]:o[5]]
    u_c = w_in[:, o[5]:o[6]]
    w_nat = jnp.concatenate([k_a, kv_b, v_a[:, :LANES], g_a, u_c], axis=1)
    w_tr = jnp.concatenate([q_a, q_b, v_a[:, LANES:], kv_b[:, HEAD_DIM:]], axis=1).T
    return w_nat.astype(MXU_DTYPE), w_tr.astype(MXU_DTYPE)


def _in_projection(x2, gain, w_nat, w_tr, tables, bsz, seq):
    n = x2.shape[0]
    tm = min(ROW_TILE, seq)
    tiles = seq // tm

    def rows(width):
        return pl.BlockSpec((tm, width), lambda i: (i, 0))

    def lanes(height):
        return pl.BlockSpec((None, height, tm), lambda i: (i // tiles, 0, i % tiles))

    table_t = pl.BlockSpec((HEAD_DIM, tm), lambda i: (0, i))
    t_heights = (A_WIDTH, B_WIDTH, 2 * LANES, HEAD_DIM)
    widths = (2 * LANES, LANES, LANES, C_WIDTH)
    dtypes = (MXU_DTYPE, MXU_DTYPE, jnp.float32, jnp.float32)
    seg_rows = tm // CMP_STRIDE
    seg_spec = pl.BlockSpec((seg_rows, CMP_STRIDE * LANES), lambda i: (i, 0))
    seg_shape = jax.ShapeDtypeStruct((n // CMP_STRIDE, CMP_STRIDE * LANES), MXU_DTYPE)
    return pl.pallas_call(
        _proj_kernel,
        grid=(n // tm,),
        in_specs=[rows(D_MODEL), _resident((1, D_MODEL)), _resident((D_MODEL, _NAT_COLS)),
                  _resident((_TR_ROWS, D_MODEL)), rows(LANES), rows(LANES), table_t, table_t],
        out_specs=[lanes(hh) for hh in t_heights] + [rows(w) for w in widths[:2]] + [seg_spec, seg_spec]
        + [rows(w) for w in widths[2:]],
        out_shape=[jax.ShapeDtypeStruct((bsz, hh, seq), MXU_DTYPE) for hh in t_heights]
        + [jax.ShapeDtypeStruct((n, w), dt) for w, dt in zip(widths[:2], dtypes[:2])] + [seg_shape, seg_shape]
        + [jax.ShapeDtypeStruct((n, w), dt) for w, dt in zip(widths[2:], dtypes[2:])],
        scratch_shapes=[pltpu.VMEM((2, tm, LANES), jnp.float32)],
        compiler_params=_params(("parallel",)),
        name="in_projection",
    )(x2, gain.reshape(1, D_MODEL), w_nat, w_tr, *tables)


def _compress_kernel(kseg_ref, vseg_ref, pos_ref, w1_ref, w1x_ref, w2_ref, kc_ref, vct_ref):
    nseg = kseg_ref.shape[0]
    both = []
    for j, seg_ref in enumerate((kseg_ref, vseg_ref)):
        seg = seg_ref[...]
        pos_term = _mm(pos_ref[j], w1_ref[j])
        heads = []
        for h in range(A_KV_HEADS):
            hidden = (_mm(seg, w1x_ref[j, 0, h]) + pltpu.roll(_mm(seg, w1x_ref[j, 1, h]), nseg - 1, 0)
                      + pos_term)
            heads.append(_mm(jax.nn.gelu(hidden), w2_ref[j]))
        both.append(jnp.concatenate(heads, axis=1))
    kc_ref[...] = both[0].astype(kc_ref.dtype)
    vct_ref[...] = both[1].T.astype(vct_ref.dtype)


def _expand_w1(w1):
    w = w1.reshape(2, 2, CMP_STRIDE, HEAD_DIM, CMP_HIDDEN)
    eye = jnp.eye(A_KV_HEADS, dtype=w1.dtype)
    return jnp.einsum("jatdc,hg->jahtgdc", w, eye).reshape(2, 2, A_KV_HEADS, CMP_STRIDE * LANES, CMP_HIDDEN)


def _compress(kseg, vseg, pos_flat, w1, w2, bsz):
    nseg = kseg.shape[0] // bsz
    segw = kseg.shape[1]
    seg_spec = pl.BlockSpec((nseg, segw), lambda b: (b, 0))
    return pl.pallas_call(
        _compress_kernel,
        grid=(bsz,),
        in_specs=[seg_spec, seg_spec, _resident((2, 1, CMP_LEN * HEAD_DIM)),
                  _resident((2, CMP_LEN * HEAD_DIM, CMP_HIDDEN)),
                  _resident((2, 2, A_KV_HEADS, segw, CMP_HIDDEN)), _resident((2, CMP_HIDDEN, HEAD_DIM))],
        out_specs=[pl.BlockSpec((None, nseg, LANES), lambda b: (b, 0, 0)),
                   pl.BlockSpec((None, LANES, nseg), lambda b: (b, 0, 0))],
        out_shape=[jax.ShapeDtypeStruct((bsz, nseg, LANES), MXU_DTYPE),
                   jax.ShapeDtypeStruct((bsz, LANES, nseg), MXU_DTYPE)],
        compiler_params=_params(("parallel",)),
        name="nsa_compress",
    )(kseg, vseg, pos_flat.astype(MXU_DTYPE), w1.astype(MXU_DTYPE), _expand_w1(w1).astype(MXU_DTYPE),
      w2.astype(MXU_DTYPE))


def _split3(x):
    hi = x.astype(MXU_DTYPE)
    r1 = x - _f32(hi)
    mid = r1.astype(MXU_DTYPE)
    lo = (r1 - _f32(mid)).astype(MXU_DTYPE)
    return hi, mid, lo


def _col_max(s):
    return jnp.max(s, axis=0, keepdims=True)


def _tile_lanes(x, n):
    return jnp.concatenate([x] * n, axis=1)


def _pad_head_rows(x, h, fill):
    other = jnp.full(x.shape, fill, x.dtype)
    return jnp.concatenate([x, other] if h == 0 else [other, x], axis=0)


def _values_with_ones(vt):
    return jnp.concatenate([vt, jnp.ones((2 * SUBLANES, vt.shape[1]), vt.dtype)], axis=0)


def _store_transposed_pairs(o_ref, heads_t, row0=0):
    for p in range(len(heads_t) // 2):
        pair = jnp.concatenate([heads_t[2 * p], heads_t[2 * p + 1]], axis=0)
        o_ref[row0:row0 + QBLK, p * LANES:(p + 1) * LANES] = pair.T.astype(o_ref.dtype)


def _rank_blocks(selb_ref, p_sum, c2s_ref, blk, before, qp_lane, n_sel):
    imp_t = sum(_mm(c2s_ref[...], part) for part in _split3(p_sum))
    cur = jnp.right_shift(qp_lane, 6)
    forced = (blk == 0) | (blk == cur) | (blk == cur - 1)
    score = jnp.where(forced, FORCED_SCORE, imp_t)
    score = jnp.where(blk <= cur, score, _NEG_INF)
    n_grp = n_sel // SUBLANES
    sub = lax.broadcasted_iota(jnp.int32, (SUBLANES, QBLK), 0)
    score_grp = [score[SUBLANES * v:SUBLANES * (v + 1), :] for v in range(n_grp)]
    rank_grp = [jnp.zeros((SUBLANES, QBLK), jnp.float32) for _ in range(n_grp)]
    for j in range(n_sel):
        row = jnp.broadcast_to(score[j:j + 1, :], (SUBLANES, QBLK))
        vj, rj = divmod(j, SUBLANES)
        for v in range(n_grp):
            if v > vj:
                inc = jnp.where(row >= score_grp[v], 1.0, 0.0)
            elif v < vj:
                inc = jnp.where(row > score_grp[v], 1.0, 0.0)
            else:
                inc = jnp.where(sub > rj, jnp.where(row >= score_grp[v], 1.0, 0.0),
                                jnp.where(row > score_grp[v], 1.0, 0.0))
            rank_grp[v] = rank_grp[v] + inc
    rank = jnp.concatenate(rank_grp, axis=0)
    picked = jnp.where(rank < float(min(SEL_TOPK, n_sel)), 0.0, _NEG_INF)
    selb_ref[...] = jnp.where(before, picked, _NEG_INF)


def _nsa_kernel(*refs, seq):
    for sb in range(NSA_QSUB):
        _nsa_block(sb, *refs, seq=seq)


def _nsa_block(sb, qt_ref, k_ref, vt_ref, kc_ref, vct_ref, g_ref, c2s_ref, o_ref,
               selb_ref, sca_ref, scb_ref, cmp_ref, win_ref, diag_ref, *, seq):
    q0 = (pl.program_id(1) * NSA_QSUB + sb) * QBLK
    q_cols = slice(sb * QBLK, (sb + 1) * QBLK)
    n_cmp = kc_ref.shape[0]
    n_sel = seq // SEL_LEN
    kc_sz = min(SEL_CHUNK, seq)
    span = min(A_WINDOW + QBLK, seq)
    blocks_per_chunk = kc_sz // SEL_LEN

    gates_t = g_ref[q_cols, :].T
    qp_lane = q0 + lax.broadcasted_iota(jnp.int32, (1, QBLK), 1)

    cmp_end = lax.broadcasted_iota(jnp.int32, (n_cmp, 1), 0) * CMP_STRIDE + (CMP_LEN - 1)
    bias_c = _tile_lanes(jnp.where(cmp_end <= qp_lane, 0.0, _NEG_INF), A_GROUP)

    w0 = pl.multiple_of(jnp.maximum(q0 + QBLK - span, 0), QBLK)
    diff = qp_lane - (w0 + lax.broadcasted_iota(jnp.int32, (span, 1), 0))
    bias_w = _tile_lanes(jnp.where(diff >= 0, jnp.where(diff < A_WINDOW, 0.0, _NEG_INF), _NEG_INF), A_GROUP)

    w_qs = []
    for h in range(A_KV_HEADS):
        q_t = jnp.concatenate([qt_ref[(h * A_GROUP + g) * HEAD_DIM:(h * A_GROUP + g + 1) * HEAD_DIM, q_cols]
                               for g in range(A_GROUP)], axis=1)
        w_qs.append(_pad_head_rows(q_t, h, 0.0))

    n_chunks = (q0 + kc_sz - 1) // kc_sz
    last = jnp.maximum(n_chunks - 1, 0)

    def issue_scores(c, dst_ref):
        k0 = pl.multiple_of(jnp.minimum(c, last) * kc_sz, kc_sz)
        keys = k_ref[pl.ds(k0, kc_sz), :LANES]
        for h in range(A_KV_HEADS):
            dst_ref[h] = _mm(keys, w_qs[h])

    for h in range(A_KV_HEADS):
        cmp_ref[h] = _mm(kc_ref[...], w_qs[h])
    for h in range(A_KV_HEADS):
        win_ref[h] = _mm(k_ref[pl.ds(w0, span), LANES:], w_qs[h])
    issue_scores(0, sca_ref)
    for h in range(A_KV_HEADS):
        diag_ref[h] = _mm(k_ref[pl.ds(pl.multiple_of(q0, QBLK), QBLK), :LANES], w_qs[h])

    o_cs, o_ws = [], []
    for h in range(A_KV_HEADS):
        s = cmp_ref[h] + bias_c
        m = _col_max(s)
        m = jnp.where(m == _NEG_INF, 0.0, m)
        e = jnp.exp2(s - m)
        p = e * (1.0 / jnp.maximum(jnp.sum(e, axis=0, keepdims=True), 1e-30))
        o_cs.append(_mm(vct_ref[...], p)[h * HEAD_DIM:(h + 1) * HEAD_DIM])

        p_sum = p[:, 0:QBLK]
        for g in range(1, A_GROUP):
            p_sum = p_sum + p[:, g * QBLK:(g + 1) * QBLK]
        blk = lax.broadcasted_iota(jnp.int32, (n_sel, QBLK), 0)
        before = blk < jnp.right_shift(q0, 6)
        _rank_blocks(selb_ref.at[h], p_sum, c2s_ref, blk, before, qp_lane, n_sel)

    for h in range(A_KV_HEADS):
        sw = win_ref[h] + bias_w
        pw = jnp.exp2(sw - _col_max(sw))
        acc_w = _mm(_values_with_ones(vt_ref[LANES + h * HEAD_DIM:LANES + (h + 1) * HEAD_DIM, pl.ds(w0, span)]), pw)
        o_ws.append(acc_w[:HEAD_DIM] / acc_w[HEAD_DIM:HEAD_DIM + 1])

    def consume(c, src_ref, states):
        cc = jnp.minimum(c, last)
        k0 = pl.multiple_of(cc * kc_sz, kc_sz)
        out = []
        for h in range(A_KV_HEADS):
            m_run, acc = states[h]
            rows = [_tile_lanes(jnp.where(c < n_chunks, selb_ref[h, pl.ds(cc * blocks_per_chunk + jj, 1), :], _NEG_INF),
                                A_GROUP) for jj in range(blocks_per_chunk)]
            blocks = [src_ref[h, jj * SEL_LEN:(jj + 1) * SEL_LEN, :] for jj in range(blocks_per_chunk)]
            part = None
            for sc_blk, row in zip(blocks, rows):
                blk_max = jnp.max(sc_blk.reshape(SEL_LEN // SUBLANES, SUBLANES, A_GROUP * QBLK), axis=0) + row
                part = blk_max if part is None else jnp.maximum(part, blk_max)
            m_new = jnp.maximum(m_run, _col_max(part))
            pr = jnp.concatenate([jnp.exp2(sc_blk + (row - m_new)) for sc_blk, row in zip(blocks, rows)], axis=0)
            vt = _values_with_ones(vt_ref[h * HEAD_DIM:(h + 1) * HEAD_DIM, pl.ds(k0, kc_sz)])
            out.append((m_new, jnp.exp2(m_run - m_new) * acc + _mm(vt, pr)))
        return tuple(out)

    def sel_step(t, states):
        issue_scores(2 * t + 1, scb_ref)
        states = consume(2 * t, sca_ref, states)
        issue_scores(2 * t + 2, sca_ref)
        return consume(2 * t + 1, scb_ref, states)

    kp_diag = q0 + lax.broadcasted_iota(jnp.int32, (QBLK, 1), 0)
    bias_d = _tile_lanes(jnp.where(kp_diag <= qp_lane, 0.0, _NEG_INF), A_GROUP)
    q0_al = pl.multiple_of(q0, QBLK)
    init = []
    for h in range(A_KV_HEADS):
        sd = diag_ref[h] + bias_d
        m0 = _col_max(sd)
        vt = _values_with_ones(vt_ref[h * HEAD_DIM:(h + 1) * HEAD_DIM, pl.ds(q0_al, QBLK)])
        init.append((m0, _mm(vt, jnp.exp2(sd - m0))))
    init = tuple(init)
    sel_out = lax.fori_loop(0, (n_chunks + 1) // 2, sel_step, init)

    heads_t = []
    for h in range(A_KV_HEADS):
        acc_s = sel_out[h][1]
        o_s = acc_s[:HEAD_DIM] / jnp.maximum(acc_s[HEAD_DIM:HEAD_DIM + 1], 1e-30)
        for g in range(A_GROUP):
            hd = h * A_GROUP + g
            cols = slice(g * QBLK, (g + 1) * QBLK)
            heads_t.append(gates_t[hd:hd + 1, :] * o_cs[h][:, cols]
                           + gates_t[A_HEADS + hd:A_HEADS + hd + 1, :] * o_s[:, cols]
                           + gates_t[2 * A_HEADS + hd:2 * A_HEADS + hd + 1, :] * o_ws[h][:, cols])
    _store_transposed_pairs(o_ref, heads_t, sb * QBLK)


def _nsa_attention(qat, ka, vat, kc, vct, ga, bsz, seq):
    rows = NSA_QSUB * QBLK
    nstep = seq // rows
    n_seg = seq // CMP_STRIDE
    n_sel = seq // SEL_LEN
    assert seq >= A_WINDOW + QBLK and seq % SEL_CHUNK == 0 and seq % rows == 0
    c_start = jnp.arange(n_seg) * CMP_STRIDE
    s_start = jnp.arange(n_sel) * SEL_LEN
    overlap = (jnp.minimum(c_start[None, :] + CMP_LEN, s_start[:, None] + SEL_LEN)
               - jnp.maximum(c_start[None, :], s_start[:, None]))
    c2s = (jnp.clip(overlap, 0, None).astype(jnp.float32) / CMP_LEN).astype(MXU_DTYPE)
    return pl.pallas_call(
        functools.partial(_nsa_kernel, seq=seq),
        grid=(bsz, nstep),
        in_specs=[
            pl.BlockSpec((None, A_WIDTH, rows), lambda b, i: (b, 0, i)),
            pl.BlockSpec((seq, 2 * LANES), lambda b, i: (b, 0)),
            pl.BlockSpec((None, 2 * LANES, seq), lambda b, i: (b, 0, 0)),
            pl.BlockSpec((None, n_seg, LANES), lambda b, i: (b, 0, 0)),
            pl.BlockSpec((None, LANES, n_seg), lambda b, i: (b, 0, 0)),
            pl.BlockSpec((rows, LANES), lambda b, i: (b * nstep + i, 0)),
            _resident((n_sel, n_seg)),
        ],
        out_specs=pl.BlockSpec((rows, A_WIDTH), lambda b, i: (b * nstep + i, 0)),
        out_shape=jax.ShapeDtypeStruct((bsz * seq, A_WIDTH), MXU_DTYPE),
        scratch_shapes=[pltpu.VMEM((A_KV_HEADS, n_sel, QBLK), jnp.float32)]
        + [pltpu.VMEM((A_KV_HEADS, min(SEL_CHUNK, seq), A_GROUP * QBLK), jnp.float32)] * 2
        + [pltpu.VMEM((A_KV_HEADS, n_seg, A_GROUP * QBLK), jnp.float32),
           pltpu.VMEM((A_KV_HEADS, min(A_WINDOW + QBLK, seq), A_GROUP * QBLK), jnp.float32),
           pltpu.VMEM((A_KV_HEADS, QBLK, A_GROUP * QBLK), jnp.float32)],
        compiler_params=_params(("parallel", "arbitrary")),
        name="nsa_attention",
    )(qat, ka, vat, kc, vct, ga, c2s)


def _swa_kernel(sink_ref, qt_ref, kv_ref, vt_ref, o_ref, sc_ref, *, seq):
    i = pl.program_id(1)
    span = min(B_WINDOW + QBLK, seq)
    sink = jnp.concatenate([jnp.full((1, QBLK), sink_ref[hd] * LOG2E, jnp.float32) for hd in range(B_HEADS)],
                           axis=1)
    starts = []
    for sb in range(SWA_QSUB):
        q0 = (i * SWA_QSUB + sb) * QBLK
        w0 = pl.multiple_of(jnp.maximum(q0 + QBLK - span, 0), QBLK)
        starts.append((q0, w0))
        q_t = jnp.concatenate([qt_ref[hd * HEAD_DIM:(hd + 1) * HEAD_DIM, sb * QBLK:(sb + 1) * QBLK]
                               for hd in range(B_HEADS)], axis=1)
        sc_ref[sb] = _mm(kv_ref[pl.ds(w0, span), :], _pad_head_rows(q_t, 0, 0.0))
    for sb, (q0, w0) in enumerate(starts):
        qp_lane = q0 + lax.broadcasted_iota(jnp.int32, (1, QBLK), 1)
        diff = qp_lane - (w0 + lax.broadcasted_iota(jnp.int32, (span, 1), 0))
        bias = _tile_lanes(jnp.where(diff >= 0, jnp.where(diff < B_WINDOW, 0.0, _NEG_INF), _NEG_INF), B_HEADS)
        s = sc_ref[sb] + bias
        m = jnp.maximum(_col_max(s), sink)
        acc = _mm(_values_with_ones(vt_ref[:, pl.ds(w0, span)]), jnp.exp2(s - m))
        o_t = acc[:HEAD_DIM] / (acc[HEAD_DIM:HEAD_DIM + 1] + jnp.exp2(sink - m))
        _store_transposed_pairs(o_ref, [o_t[:, hd * QBLK:(hd + 1) * QBLK] for hd in range(B_HEADS)], sb * QBLK)


def _swa_attention(qbt, kvb, vbt, sinks, bsz, seq):
    rows = SWA_QSUB * QBLK
    nstep = seq // rows
    span = min(B_WINDOW + QBLK, seq)
    assert seq >= B_WINDOW + QBLK and seq % rows == 0
    return pl.pallas_call(
        functools.partial(_swa_kernel, seq=seq),
        grid=(bsz, nstep),
        in_specs=[
            pl.BlockSpec(memory_space=pltpu.SMEM),
            pl.BlockSpec((None, B_WIDTH, rows), lambda b, i: (b, 0, i)),
            pl.BlockSpec((seq, LANES), lambda b, i: (b, 0)),
            pl.BlockSpec((None, HEAD_DIM, seq), lambda b, i: (b, 0, 0)),
        ],
        out_specs=pl.BlockSpec((rows, B_WIDTH), lambda b, i: (b * nstep + i, 0)),
        out_shape=jax.ShapeDtypeStruct((bsz * seq, B_WIDTH), MXU_DTYPE),
        scratch_shapes=[pltpu.VMEM((SWA_QSUB, span, B_HEADS * QBLK), jnp.float32)],
        compiler_params=_params(("parallel", "arbitrary")),
        name="swa_attention",
    )(sinks, qbt, kvb, vbt)


def _s5_discretize_kernel(are_ref, aim_ref, logdt_ref, bre_ref, bim_ref,
                          abr_ref, abi_ref, bbr_ref, bbi_ref):
    ar = are_ref[...]
    ai = aim_ref[...]
    dt = jnp.exp(logdt_ref[...])
    mag = jnp.exp(dt * ar)
    abr = mag * jnp.cos(dt * ai)
    abi = mag * jnp.sin(dt * ai)
    den = ar * ar + ai * ai
    nr = abr - 1.0
    coef_r = (nr * ar + abi * ai) / den
    coef_i = (abi * ar - nr * ai) / den
    abr_ref[...] = abr
    abi_ref[...] = abi
    br = bre_ref[...]
    bi = bim_ref[...]
    bbr_ref[...] = coef_r[:, None, :] * br - coef_i[:, None, :] * bi
    bbi_ref[...] = coef_r[:, None, :] * bi + coef_i[:, None, :] * br


def _s5_discretize(a_re, a_im, log_dt, b_re, b_im):
    lg = a_re.shape[0] * a_re.shape[1]
    a2 = lambda a: a.reshape(lg, C_STATE)
    b3 = lambda b: jnp.swapaxes(b.reshape(lg, C_STATE, C_GROUP_CH), 1, 2)
    sa = jax.ShapeDtypeStruct((lg, C_STATE), jnp.float32)
    sb = jax.ShapeDtypeStruct((lg, C_GROUP_CH, C_STATE), jnp.float32)
    return pl.pallas_call(_s5_discretize_kernel, out_shape=[sa, sa, sb, sb], name="s5_discretize")(
        a2(a_re), a2(a_im), log_dt.reshape(lg, 1), b3(b_re), b3(b_im))


def _s5_kernel(u_ref, bmat_ref, abar_ref, cmat_ref, d_ref, gw_ref, gb_ref, y_ref, state_ref, ut_ref, xs_ref):
    bsz, t_steps, _ = u_ref.shape

    @pl.when(pl.program_id(0) == 0)
    def _():
        state_ref[...] = jnp.zeros_like(state_ref)

    def gather(t, carry):
        ut_ref[pl.ds(pl.multiple_of(t * bsz, bsz), bsz), :] = u_ref[:, t, :]
        return carry

    lax.fori_loop(0, t_steps, gather, 0, unroll=8)
    u = ut_ref[...]
    for hf in range(2):
        bu = _mm(u[:, hf * _HALF_CH:(hf + 1) * _HALF_CH], bmat_ref[hf])
        xs_ref[:, hf * _HALF_ST:(hf + 1) * _HALF_ST] = bu[:, :_HALF_ST]
        xs_ref[:, C_STATES + hf * _HALF_ST:C_STATES + (hf + 1) * _HALF_ST] = bu[:, _HALF_ST:]

    ys = []
    for hf in range(2):
        re = slice(hf * _HALF_ST, (hf + 1) * _HALF_ST)
        im = slice(C_STATES + hf * _HALF_ST, C_STATES + (hf + 1) * _HALF_ST)
        ar = jnp.broadcast_to(abar_ref[0:1, re], (bsz, _HALF_ST))
        ai = jnp.broadcast_to(abar_ref[0:1, im], (bsz, _HALF_ST))
        xr = state_ref[:, re]
        xi = state_ref[:, im]
        for t in range(t_steps):
            rows = slice(t * bsz, (t + 1) * bsz)
            xr, xi = (ar * xr - ai * xi + xs_ref[rows, re], ar * xi + ai * xr + xs_ref[rows, im])
            xs_ref[rows, re] = xr
            xs_ref[rows, im] = xi
        state_ref[:, re] = xr
        state_ref[:, im] = xi
        ys.append(_mm(xs_ref[:, re], cmat_ref[hf, :_HALF_ST, :]) + _mm(xs_ref[:, im], cmat_ref[hf, _HALF_ST:, :]))
    y = jnp.concatenate(ys, axis=1) + d_ref[...] * u
    z = jax.nn.gelu(y)
    out = z * jax.nn.sigmoid(_mm(z, gw_ref[...]) + gb_ref[...])
    y_ref[...] = out.reshape(t_steps, bsz, C_WIDTH).astype(y_ref.dtype)


def _s5(u, bmat, abar, cmat, d, glu_w, glu_b):
    bsz, seq, _ = u.shape
    t_steps = min(S5_CHUNK, seq)
    return pl.pallas_call(
        _s5_kernel,
        grid=(seq // t_steps,),
        in_specs=[
            pl.BlockSpec((bsz, t_steps, C_WIDTH), lambda i: (0, i, 0)),
            _resident((2, _HALF_CH, 2 * _HALF_ST)), _resident((1, 2 * C_STATES)), _resident((2, 2 * _HALF_ST, _HALF_CH)),
            _resident((1, C_WIDTH)), _resident((C_WIDTH, C_WIDTH)), _resident((1, C_WIDTH)),
        ],
        out_specs=pl.BlockSpec((t_steps, bsz, C_WIDTH), lambda i: (i, 0, 0)),
        out_shape=jax.ShapeDtypeStruct((seq, bsz, C_WIDTH), MXU_DTYPE),
        scratch_shapes=[pltpu.VMEM((bsz, 2 * C_STATES), jnp.float32),
                        pltpu.VMEM((t_steps * bsz, C_WIDTH), jnp.float32),
                        pltpu.VMEM((t_steps * bsz, 2 * C_STATES), jnp.float32)],
        compiler_params=_params(("arbitrary",)),
        name="s5_scan",
    )(u, bmat, abar, cmat, d.reshape(1, C_WIDTH), glu_w.astype(MXU_DTYPE), glu_b.reshape(1, C_WIDTH))


def _block_diag(per_group):
    g, r, c = per_group.shape
    eye = jnp.eye(g, dtype=per_group.dtype)
    return jnp.einsum("grc,gk->grkc", per_group, eye).reshape(g * r, g * c)


def _merge_kernel(x_ref, ya_ref, yb_ref, yc_ref, gain_ref, wg_ref, pa_ref, pb_ref, pc_ref, wo_ref, o_ref):
    x = x_ref[...]
    h = _rmsnorm(x, gain_ref[...]).astype(MXU_DTYPE)
    merged = None
    for j, (y_ref, p_ref) in enumerate(((ya_ref, pa_ref), (yb_ref, pb_ref), (yc_ref, pc_ref))):
        gate = jax.nn.sigmoid(jnp.dot(h, wg_ref[:, j * D_MODEL:(j + 1) * D_MODEL],
                                      preferred_element_type=jnp.float32))
        term = gate * jnp.dot(y_ref[...], p_ref[...], preferred_element_type=jnp.float32)
        merged = term if merged is None else merged + term
    o_ref[...] = x + _mm(merged, wo_ref[...])


def _merge(x2, ya, yb, yc, gain, wg, pa, pb, pc, wo, layer):
    n = x2.shape[0]
    tm = min(ROW_TILE, n)

    def rows(width):
        return pl.BlockSpec((tm, width), lambda i: (i, 0))

    return pl.pallas_call(
        _merge_kernel,
        grid=(n // tm,),
        in_specs=[rows(D_MODEL), rows(A_WIDTH), rows(B_WIDTH), rows(C_WIDTH), _resident((1, D_MODEL)),
                  _resident((D_MODEL, N_BRANCH * D_MODEL)), _resident_layer((A_WIDTH, D_MODEL), layer),
                  _resident_layer((B_WIDTH, D_MODEL), layer), _resident_layer((C_WIDTH, D_MODEL), layer),
                  _resident_layer((D_MODEL, D_MODEL), layer)],
        out_specs=rows(D_MODEL),
        out_shape=jax.ShapeDtypeStruct((n, D_MODEL), jnp.float32),
        compiler_params=_params(("parallel",)),
        name="merge",
    )(x2, ya, yb, yc, gain.reshape(1, D_MODEL), wg, pa, pb, pc, wo)


def _mlp_kernel(x_ref, gain_ref, wu_ref, wd_ref, fgain_ref, o_ref, *, final_norm):
    x = x_ref[...]
    h = _rmsnorm(x, gain_ref[...]).astype(MXU_DTYPE)
    acc = x
    for c in range(D_FF // FF_CHUNK):
        cols = slice(c * FF_CHUNK, (c + 1) * FF_CHUNK)
        up = jnp.dot(h, wu_ref[:, cols], preferred_element_type=jnp.float32)
        acc = acc + _mm(jnp.square(jnp.maximum(up, 0.0)), wd_ref[cols, :])
    o_ref[...] = _rmsnorm(acc, fgain_ref[...]) if final_norm else acc


def _mlp(x2, gain, wu, wd, fgain, layer, final_norm):
    n = x2.shape[0]
    tm = min(ROW_TILE, n)
    rows = pl.BlockSpec((tm, D_MODEL), lambda i: (i, 0))
    return pl.pallas_call(
        functools.partial(_mlp_kernel, final_norm=final_norm),
        grid=(n // tm,),
        in_specs=[rows, _resident((1, D_MODEL)), _resident_layer((D_MODEL, D_FF), layer),
                  _resident_layer((D_FF, D_MODEL), layer), _resident((1, D_MODEL))],
        out_specs=rows,
        out_shape=jax.ShapeDtypeStruct((n, D_MODEL), jnp.float32),
        compiler_params=_params(("parallel",)),
        name="mlp",
    )(x2, gain.reshape(1, D_MODEL), wu, wd, fgain.reshape(1, D_MODEL))


def kernel(x, positions, norm_mix, w_in, nsa_cmp_pos, nsa_cmp_w1, nsa_cmp_w2, swa_sinks, s5_a_re, s5_a_im, s5_log_dt, s5_b_re, s5_b_im, s5_c_re, s5_c_im, s5_d, s5_glu_w, s5_glu_b, w_branch_a, w_branch_b, w_branch_c, w_out, norm_mlp, w_mlp_up, w_mlp_down, norm_final):
    bsz, seq, _ = x.shape
    depth = w_in.shape[0]
    n = bsz * seq
    gate_col = sum(IN_WIDTHS[:-1])
    bf = lambda w: w.astype(MXU_DTYPE)

    tables = _rope_tables(positions)
    abr, abi, bbr, bbi = _s5_discretize(s5_a_re, s5_a_im, s5_log_dt, s5_b_re, s5_b_im)
    abr = abr.reshape(depth, 1, C_STATES)
    abi = abi.reshape(depth, 1, C_STATES)
    bbr = bbr.reshape(depth, C_GROUPS, C_GROUP_CH, C_STATE)
    bbi = bbi.reshape(depth, C_GROUPS, C_GROUP_CH, C_STATE)

    stacked = [bf(w) for w in (w_branch_a, w_branch_b, w_branch_c, w_out, w_mlp_up, w_mlp_down)]

    x2 = x.reshape(n, D_MODEL)
    for l in range(depth):
        qat, qbt, vat, vbt, ka, kvb, kseg, vseg, ga, uc = _in_projection(
            x2, norm_mix[l], *_pack_w_in(w_in[l]), tables, bsz, seq)

        kc, vct = _compress(kseg, vseg, nsa_cmp_pos[l].reshape(2, 1, CMP_LEN * HEAD_DIM), nsa_cmp_w1[l],
                            nsa_cmp_w2[l], bsz)
        ya = _nsa_attention(qat, ka, vat, kc, vct, ga, bsz, seq)

        yb = _swa_attention(qbt, kvb, vbt, swa_sinks[l], bsz, seq)

        cre_t = jnp.swapaxes(s5_c_re[l], 1, 2)
        cim_t = jnp.swapaxes(s5_c_im[l], 1, 2)
        halves = (slice(0, C_GROUPS // 2), slice(C_GROUPS // 2, C_GROUPS))
        bmat = jnp.stack([jnp.concatenate([_block_diag(bbr[l][g]), _block_diag(bbi[l][g])], axis=1)
                          for g in halves])
        cmat = jnp.stack([jnp.concatenate([_block_diag(cre_t[g]), -_block_diag(cim_t[g])], axis=0)
                          for g in halves])
        abar = jnp.concatenate([abr[l], abi[l]], axis=1)
        yc_tm = _s5(uc.reshape(bsz, seq, C_WIDTH), bf(bmat), abar, bf(cmat), s5_d[l], s5_glu_w[l], s5_glu_b[l])
        yc = jnp.swapaxes(yc_tm, 0, 1).reshape(n, C_WIDTH)

        x2 = _merge(x2, ya, yb, yc, norm_mix[l], bf(w_in[l][:, gate_col:]), *stacked[:4], l)
        x2 = _mlp(x2, norm_mlp[l], *stacked[4:], norm_final, l, l == depth - 1)
    return x2.reshape(bsz, seq, D_MODEL)
```

```python
import functools

import jax
import jax.numpy as jnp
from jax import lax
from jax.experimental import pallas as pl
from jax.experimental.pallas import tpu as pltpu

D_MODEL = 1024
HEAD_DIM = 64
ROPE_THETA = 10000.0
NORM_EPS = 1e-6
QBLK = 128

A_HEADS = 8
A_KV_HEADS = 2
A_GROUP = A_HEADS // A_KV_HEADS
A_WIDTH = A_HEADS * HEAD_DIM
NSA_BRANCHES = 3
CMP_LEN = 32
CMP_STRIDE = 16
CMP_HIDDEN = 2 * HEAD_DIM
SEL_LEN = 64
SEL_TOPK = 16
A_WINDOW = 512
FORCED_SCORE = 1e4

B_HEADS = 8
B_WIDTH = B_HEADS * HEAD_DIM
B_WINDOW = 128

C_WIDTH = 512
C_GROUP_CH = 16
C_GROUPS = C_WIDTH // C_GROUP_CH
C_STATE = 64
C_STATES = C_GROUPS * C_STATE
_HALF_CH = C_WIDTH // 2
_HALF_ST = C_STATES // 2

D_FF = 4 * D_MODEL
N_BRANCH = 3

IN_WIDTHS = (A_WIDTH, NSA_BRANCHES * 2 * A_KV_HEADS * HEAD_DIM, NSA_BRANCHES * A_HEADS,
             B_WIDTH, 2 * HEAD_DIM, C_WIDTH, N_BRANCH * D_MODEL)

LANES = 128
SUBLANES = 8
VMEM_LIMIT = 56 * 1024 * 1024

MXU_DTYPE = jnp.bfloat16
ROW_TILE = 1024
S5_CHUNK = 128
SEL_CHUNK = 256
SWA_QSUB = 8
NSA_QSUB = 2
FF_CHUNK = 1024

LOG2E = 1.4426950408889634
DEN_FLOOR = 1e-30
_NEG_INF = float("-inf")
_SEL_SHIFT = SEL_LEN.bit_length() - 1


def _f32(x):
    return x.astype(jnp.float32)


def _mm(a, b):
    return jnp.dot(a.astype(MXU_DTYPE), b.astype(MXU_DTYPE), preferred_element_type=jnp.float32)


def _mm_nt(a, b):
    return lax.dot_general(a.astype(MXU_DTYPE), b.astype(MXU_DTYPE), (((1,), (1,)), ((), ())),
                           preferred_element_type=jnp.float32)


def _rmsnorm(x, gain):
    return x * lax.rsqrt(jnp.mean(x * x, axis=-1, keepdims=True) + NORM_EPS) * gain


def _resident(shape):
    zeros = (0,) * len(shape)
    return pl.BlockSpec(shape, lambda *_: zeros, pipeline_mode=pl.Buffered(1))


def _resident_layer(shape, layer):
    zeros = (0,) * len(shape)
    return pl.BlockSpec((None,) + tuple(shape), lambda *_: (layer,) + zeros, pipeline_mode=pl.Buffered(1))


def _params(sem):
    return pltpu.CompilerParams(dimension_semantics=sem, vmem_limit_bytes=VMEM_LIMIT)


def _rope_table_kernel(pos_ref, invf_ref, sign_ref, cos_ref, sin_ref):
    ang = _f32(pos_ref[...]) * invf_ref[...]
    cos_ref[...] = jnp.cos(ang)
    sin_ref[...] = jnp.sin(ang) * sign_ref[...]


def _rope_tables(positions):
    n = positions.size
    half = HEAD_DIM // 2
    inv_freq = ROPE_THETA ** (-jnp.arange(half, dtype=jnp.float32) / half)
    invf = jnp.tile(inv_freq, LANES // half)
    sign = jnp.tile(jnp.concatenate([-jnp.ones(half, jnp.float32), jnp.ones(half, jnp.float32)]),
                    LANES // HEAD_DIM)
    tm = min(1024, n)
    natural = pl.pallas_call(
        _rope_table_kernel,
        grid=(n // tm,),
        in_specs=[pl.BlockSpec((tm, 1), lambda i: (i, 0)), _resident((1, LANES)), _resident((1, LANES))],
        out_specs=[pl.BlockSpec((tm, LANES), lambda i: (i, 0))] * 2,
        out_shape=[jax.ShapeDtypeStruct((n, LANES), jnp.float32)] * 2,
        compiler_params=_params(("parallel",)),
        name="rope_tables",
    )(positions.reshape(n, 1), invf.reshape(1, LANES), sign.reshape(1, LANES))
    transposed = pl.pallas_call(
        _rope_table_kernel,
        grid=(n // tm,),
        in_specs=[pl.BlockSpec((1, tm), lambda i: (0, i)), _resident((HEAD_DIM, 1)), _resident((HEAD_DIM, 1))],
        out_specs=[pl.BlockSpec((HEAD_DIM, tm), lambda i: (0, i))] * 2,
        out_shape=[jax.ShapeDtypeStruct((HEAD_DIM, n), jnp.float32)] * 2,
        compiler_params=_params(("parallel",)),
        name="rope_tables_t",
    )(positions.reshape(1, n), invf[:HEAD_DIM].reshape(HEAD_DIM, 1), sign[:HEAD_DIM].reshape(HEAD_DIM, 1))
    return natural + transposed


def _swap_halves(x):
    lane = lax.broadcasted_iota(jnp.int32, x.shape, 1)
    first = (lane & (HEAD_DIM - 1)) < HEAD_DIM // 2
    return jnp.where(first, pltpu.roll(x, LANES - HEAD_DIM // 2, 1), pltpu.roll(x, HEAD_DIM // 2, 1))


def _proj_kernel(x_ref, gain_ref, wn_ref, wt_ref, cos_ref, sin_ref, cost_ref, sint_ref,
                 qat_ref, qbt_ref, vat_ref, vbt_ref, ka_ref, kvb_ref, kseg_ref, vseg_ref, ga_ref, uc_ref,
                 seg_ref):
    h = _rmsnorm(x_ref[...], gain_ref[...]).astype(MXU_DTYPE)
    nat = jnp.dot(h, wn_ref[...], preferred_element_type=jnp.float32)
    tr = _mm_nt(wt_ref[...], h)

    cos = cos_ref[...]
    sin = sin_ref[...]

    def roped(j):
        xj = nat[:, j * LANES:(j + 1) * LANES]
        return xj, xj * cos + _swap_halves(xj) * sin

    for j in range(1, NSA_BRANCHES):
        ka_ref[:, (j - 1) * LANES:j * LANES] = roped(j)[1].astype(ka_ref.dtype)
    raw, rot = roped(3)
    lane = lax.broadcasted_iota(jnp.int32, raw.shape, 1)
    kvb_ref[...] = jnp.where(lane < HEAD_DIM, rot, raw).astype(kvb_ref.dtype)
    n_seg_tile = nat.shape[0] // CMP_STRIDE
    for j, (src, dst_ref) in enumerate(((roped(0)[1], kseg_ref), (nat[:, 4 * LANES:5 * LANES], vseg_ref))):
        seg_ref[j] = src
        for tok in range(CMP_STRIDE):
            dst_ref[:, tok * LANES:(tok + 1) * LANES] = (
                seg_ref[j, pl.ds(tok, n_seg_tile, stride=CMP_STRIDE), :].astype(dst_ref.dtype))
    ga_ref[...] = jax.nn.sigmoid(nat[:, 5 * LANES:6 * LANES])
    uc_ref[...] = nat[:, 6 * LANES:]

    cos_t = cost_ref[...]
    sin_t = sint_ref[...]
    half = HEAD_DIM // 2

    def roped_t(r0):
        xh = tr[r0:r0 + HEAD_DIM]
        return xh * cos_t + jnp.concatenate([xh[half:], xh[:half]], axis=0) * sin_t

    for hd in range(A_HEADS):
        qat_ref[hd * HEAD_DIM:(hd + 1) * HEAD_DIM, :] = roped_t(hd * HEAD_DIM).astype(qat_ref.dtype)
    for hd in range(B_HEADS):
        qbt_ref[hd * HEAD_DIM:(hd + 1) * HEAD_DIM, :] = roped_t(A_WIDTH + hd * HEAD_DIM).astype(qbt_ref.dtype)
    v0 = A_WIDTH + B_WIDTH
    vat_ref[...] = tr[v0:v0 + 2 * LANES].astype(vat_ref.dtype)
    vbt_ref[...] = tr[v0 + 2 * LANES:].astype(vbt_ref.dtype)


_NAT_COLS = 10 * LANES
_TR_ROWS = A_WIDTH + B_WIDTH + 2 * LANES + HEAD_DIM


def _pack_w_in(w_in):
    o = [0]
    for w in IN_WIDTHS:
        o.append(o[-1] + w)
    scale = HEAD_DIM ** -0.5 * LOG2E
    q_a = w_in[..., o[0]:o[1]] * scale
    kv_cols = [w_in[..., o[1] + j * LANES:o[1] + (j + 1) * LANES] for j in range(2 * NSA_BRANCHES)]
    k_a = jnp.concatenate(kv_cols[0::2], axis=-1)
    v_a = jnp.concatenate(kv_cols[1::2], axis=-1)
    g_a = w_in[..., o[2]:o[3]]
    g_a = jnp.concatenate([g_a, jnp.zeros(g_a.shape[:-1] + (LANES - NSA_BRANCHES * A_HEADS,), g_a.dtype)], axis=-1)
    q_b = w_in[..., o[3]:o[4]] * scale
    kv_b = w_in[..., o[4]:o[5]]
    u_c = w_in[..., o[5]:o[6]]
    w_nat = jnp.concatenate([k_a, kv_b, v_a[..., :LANES], g_a, u_c], axis=-1)
    w_tr = jnp.swapaxes(jnp.concatenate([q_a, q_b, v_a[..., LANES:], kv_b[..., HEAD_DIM:]], axis=-1), -1, -2)
    return w_nat.astype(MXU_DTYPE), w_tr.astype(MXU_DTYPE)


def _in_projection(x2, gain, w_nat, w_tr, tables, layer, bsz, seq):
    n = x2.shape[0]
    tm = min(ROW_TILE, seq)
    tiles = seq // tm

    def rows(width):
        return pl.BlockSpec((tm, width), lambda i: (i, 0))

    def lanes(height):
        return pl.BlockSpec((None, height, tm), lambda i: (i // tiles, 0, i % tiles))

    table_t = pl.BlockSpec((HEAD_DIM, tm), lambda i: (0, i))
    t_heights = (A_WIDTH, B_WIDTH, 2 * LANES, HEAD_DIM)
    widths = (2 * LANES, LANES, LANES, C_WIDTH)
    dtypes = (MXU_DTYPE, MXU_DTYPE, jnp.float32, jnp.float32)
    seg_rows = tm // CMP_STRIDE
    seg_spec = pl.BlockSpec((seg_rows, CMP_STRIDE * LANES), lambda i: (i, 0))
    seg_shape = jax.ShapeDtypeStruct((n // CMP_STRIDE, CMP_STRIDE * LANES), MXU_DTYPE)
    return pl.pallas_call(
        _proj_kernel,
        grid=(n // tm,),
        in_specs=[rows(D_MODEL), _resident_layer((1, D_MODEL), layer), _resident_layer((D_MODEL, _NAT_COLS), layer),
                  _resident_layer((_TR_ROWS, D_MODEL), layer), rows(LANES), rows(LANES), table_t, table_t],
        out_specs=[lanes(hh) for hh in t_heights] + [rows(w) for w in widths[:2]] + [seg_spec, seg_spec]
        + [rows(w) for w in widths[2:]],
        out_shape=[jax.ShapeDtypeStruct((bsz, hh, seq), MXU_DTYPE) for hh in t_heights]
        + [jax.ShapeDtypeStruct((n, w), dt) for w, dt in zip(widths[:2], dtypes[:2])] + [seg_shape, seg_shape]
        + [jax.ShapeDtypeStruct((n, w), dt) for w, dt in zip(widths[2:], dtypes[2:])],
        scratch_shapes=[pltpu.VMEM((2, tm, LANES), jnp.float32)],
        compiler_params=_params(("parallel",)),
        name="in_projection",
    )(x2, gain, w_nat, w_tr, *tables)


def _compress_kernel(kseg_ref, vseg_ref, pos_ref, w1_ref, w1x_ref, w2_ref, kc_ref, vct_ref):
    nseg = kseg_ref.shape[0]
    both = []
    for j, seg_ref in enumerate((kseg_ref, vseg_ref)):
        seg = seg_ref[...]
        pos_term = _mm(pos_ref[j], w1_ref[j])
        heads = []
        for h in range(A_KV_HEADS):
            hidden = (_mm(seg, w1x_ref[j, 0, h]) + pltpu.roll(_mm(seg, w1x_ref[j, 1, h]), nseg - 1, 0)
                      + pos_term)
            heads.append(_mm(jax.nn.gelu(hidden), w2_ref[j]))
        both.append(jnp.concatenate(heads, axis=1))
    kc_ref[...] = both[0].astype(kc_ref.dtype)
    vct_ref[...] = both[1].T.astype(vct_ref.dtype)


def _expand_w1(w1):
    depth = w1.shape[0]
    w = w1.reshape(depth, 2, 2, CMP_STRIDE, HEAD_DIM, CMP_HIDDEN)
    eye = jnp.eye(A_KV_HEADS, dtype=w1.dtype)
    return jnp.einsum("ljatdc,hg->ljahtgdc", w, eye).reshape(depth, 2, 2, A_KV_HEADS, CMP_STRIDE * LANES,
                                                              CMP_HIDDEN)


def _compress(kseg, vseg, pos_flat, w1, w1x, w2, layer, bsz):
    nseg = kseg.shape[0] // bsz
    segw = kseg.shape[1]
    seg_spec = pl.BlockSpec((nseg, segw), lambda b: (b, 0))
    return pl.pallas_call(
        _compress_kernel,
        grid=(bsz,),
        in_specs=[seg_spec, seg_spec, _resident_layer((2, 1, CMP_LEN * HEAD_DIM), layer),
                  _resident_layer((2, CMP_LEN * HEAD_DIM, CMP_HIDDEN), layer),
                  _resident_layer((2, 2, A_KV_HEADS, segw, CMP_HIDDEN), layer),
                  _resident_layer((2, CMP_HIDDEN, HEAD_DIM), layer)],
        out_specs=[pl.BlockSpec((None, nseg, LANES), lambda b: (b, 0, 0)),
                   pl.BlockSpec((None, LANES, nseg), lambda b: (b, 0, 0))],
        out_shape=[jax.ShapeDtypeStruct((bsz, nseg, LANES), MXU_DTYPE),
                   jax.ShapeDtypeStruct((bsz, LANES, nseg), MXU_DTYPE)],
        compiler_params=_params(("parallel",)),
        name="nsa_compress",
    )(kseg, vseg, pos_flat, w1, w1x, w2)


def _split3(x):
    hi = x.astype(MXU_DTYPE)
    r1 = x - _f32(hi)
    mid = r1.astype(MXU_DTYPE)
    lo = (r1 - _f32(mid)).astype(MXU_DTYPE)
    return hi, mid, lo


def _col_max(s):
    return jnp.max(s, axis=0, keepdims=True)


def _tile_lanes(x, n):
    return jnp.concatenate([x] * n, axis=1)


def _pad_head_rows(x, h, fill):
    other = jnp.full(x.shape, fill, x.dtype)
    return jnp.concatenate([x, other] if h == 0 else [other, x], axis=0)


def _values_with_ones(vt):
    return jnp.concatenate([vt, jnp.ones((2 * SUBLANES, vt.shape[1]), vt.dtype)], axis=0)


def _store_transposed_pairs(o_ref, heads_t, row0=0):
    for p in range(len(heads_t) // 2):
        pair = jnp.concatenate([heads_t[2 * p], heads_t[2 * p + 1]], axis=0)
        o_ref[row0:row0 + QBLK, p * LANES:(p + 1) * LANES] = pair.T.astype(o_ref.dtype)


def _rank_blocks(selb_ref, p_sum, c2s_ref, blk, before, qp_lane, n_sel):
    imp_t = sum(_mm(c2s_ref[...], part) for part in _split3(p_sum))
    cur = jnp.right_shift(qp_lane, _SEL_SHIFT)
    forced = (blk == 0) | (blk == cur) | (blk == cur - 1)
    score = jnp.where(forced, FORCED_SCORE, imp_t)
    score = jnp.where(blk <= cur, score, _NEG_INF)
    n_grp = n_sel // SUBLANES
    sub = lax.broadcasted_iota(jnp.int32, (SUBLANES, QBLK), 0)
    score_grp = [score[SUBLANES * v:SUBLANES * (v + 1), :] for v in range(n_grp)]
    rank_grp = [jnp.zeros((SUBLANES, QBLK), jnp.float32) for _ in range(n_grp)]
    for j in range(n_sel):
        row = jnp.broadcast_to(score[j:j + 1, :], (SUBLANES, QBLK))
        vj, rj = divmod(j, SUBLANES)
        for v in range(n_grp):
            if v > vj:
                inc = jnp.where(row >= score_grp[v], 1.0, 0.0)
            elif v < vj:
                inc = jnp.where(row > score_grp[v], 1.0, 0.0)
            else:
                inc = jnp.where(sub > rj, jnp.where(row >= score_grp[v], 1.0, 0.0),
                                jnp.where(row > score_grp[v], 1.0, 0.0))
            rank_grp[v] = rank_grp[v] + inc
    rank = jnp.concatenate(rank_grp, axis=0)
    picked = jnp.where(rank < float(min(SEL_TOPK, n_sel)), 0.0, _NEG_INF)
    selb_ref[...] = jnp.where(before, picked, _NEG_INF)


def _nsa_kernel(*refs, seq):
    for sb in range(NSA_QSUB):
        _nsa_block(sb, *refs, seq=seq)


def _nsa_block(sb, qt_ref, k_ref, vt_ref, kc_ref, vct_ref, g_ref, c2s_ref, o_ref,
               selb_ref, sca_ref, scb_ref, cmp_ref, win_ref, diag_ref, *, seq):
    q0 = (pl.program_id(1) * NSA_QSUB + sb) * QBLK
    q_cols = slice(sb * QBLK, (sb + 1) * QBLK)
    n_cmp = kc_ref.shape[0]
    n_sel = seq // SEL_LEN
    kc_sz = min(SEL_CHUNK, seq)
    span = min(A_WINDOW + QBLK, seq)
    blocks_per_chunk = kc_sz // SEL_LEN

    gates_t = g_ref[q_cols, :].T
    qp_lane = q0 + lax.broadcasted_iota(jnp.int32, (1, QBLK), 1)

    cmp_end = lax.broadcasted_iota(jnp.int32, (n_cmp, 1), 0) * CMP_STRIDE + (CMP_LEN - 1)
    bias_c = _tile_lanes(jnp.where(cmp_end <= qp_lane, 0.0, _NEG_INF), A_GROUP)

    w0 = pl.multiple_of(jnp.maximum(q0 + QBLK - span, 0), QBLK)
    diff = qp_lane - (w0 + lax.broadcasted_iota(jnp.int32, (span, 1), 0))
    bias_w = _tile_lanes(jnp.where(diff >= 0, jnp.where(diff < A_WINDOW, 0.0, _NEG_INF), _NEG_INF), A_GROUP)

    w_qs = []
    for h in range(A_KV_HEADS):
        q_t = jnp.concatenate([qt_ref[(h * A_GROUP + g) * HEAD_DIM:(h * A_GROUP + g + 1) * HEAD_DIM, q_cols]
                               for g in range(A_GROUP)], axis=1)
        w_qs.append(_pad_head_rows(q_t, h, 0.0))

    n_chunks = (q0 + kc_sz - 1) // kc_sz
    last = jnp.maximum(n_chunks - 1, 0)

    def issue_scores(c, dst_ref):
        k0 = pl.multiple_of(jnp.minimum(c, last) * kc_sz, kc_sz)
        keys = k_ref[pl.ds(k0, kc_sz), :LANES]
        for h in range(A_KV_HEADS):
            dst_ref[h] = _mm(keys, w_qs[h])

    for h in range(A_KV_HEADS):
        cmp_ref[h] = _mm(kc_ref[...], w_qs[h])
    for h in range(A_KV_HEADS):
        win_ref[h] = _mm(k_ref[pl.ds(w0, span), LANES:], w_qs[h])
    issue_scores(0, sca_ref)
    for h in range(A_KV_HEADS):
        diag_ref[h] = _mm(k_ref[pl.ds(pl.multiple_of(q0, QBLK), QBLK), :LANES], w_qs[h])

    o_cs, o_ws = [], []
    for h in range(A_KV_HEADS):
        s = cmp_ref[h] + bias_c
        m = _col_max(s)
        m = jnp.where(m == _NEG_INF, 0.0, m)
        e = jnp.exp2(s - m)
        p = e * (1.0 / jnp.maximum(jnp.sum(e, axis=0, keepdims=True), DEN_FLOOR))
        o_cs.append(_mm(vct_ref[...], p)[h * HEAD_DIM:(h + 1) * HEAD_DIM])

        p_sum = p[:, 0:QBLK]
        for g in range(1, A_GROUP):
            p_sum = p_sum + p[:, g * QBLK:(g + 1) * QBLK]
        blk = lax.broadcasted_iota(jnp.int32, (n_sel, QBLK), 0)
        before = blk < jnp.right_shift(q0, _SEL_SHIFT)
        _rank_blocks(selb_ref.at[h], p_sum, c2s_ref, blk, before, qp_lane, n_sel)

    for h in range(A_KV_HEADS):
        sw = win_ref[h] + bias_w
        pw = jnp.exp2(sw - _col_max(sw))
        acc_w = _mm(_values_with_ones(vt_ref[LANES + h * HEAD_DIM:LANES + (h + 1) * HEAD_DIM, pl.ds(w0, span)]), pw)
        o_ws.append(acc_w[:HEAD_DIM] / acc_w[HEAD_DIM:HEAD_DIM + 1])

    def consume(c, src_ref, states):
        cc = jnp.minimum(c, last)
        k0 = pl.multiple_of(cc * kc_sz, kc_sz)
        out = []
        for h in range(A_KV_HEADS):
            m_run, acc = states[h]
            rows = [_tile_lanes(jnp.where(c < n_chunks, selb_ref[h, pl.ds(cc * blocks_per_chunk + jj, 1), :], _NEG_INF),
                                A_GROUP) for jj in range(blocks_per_chunk)]
            blocks = [src_ref[h, jj * SEL_LEN:(jj + 1) * SEL_LEN, :] for jj in range(blocks_per_chunk)]
            part = None
            for sc_blk, row in zip(blocks, rows):
                blk_max = jnp.max(sc_blk.reshape(SEL_LEN // SUBLANES, SUBLANES, A_GROUP * QBLK), axis=0) + row
                part = blk_max if part is None else jnp.maximum(part, blk_max)
            m_new = jnp.maximum(m_run, _col_max(part))
            pr = jnp.concatenate([jnp.exp2(sc_blk + (row - m_new)) for sc_blk, row in zip(blocks, rows)], axis=0)
            vt = _values_with_ones(vt_ref[h * HEAD_DIM:(h + 1) * HEAD_DIM, pl.ds(k0, kc_sz)])
            out.append((m_new, jnp.exp2(m_run - m_new) * acc + _mm(vt, pr)))
        return tuple(out)

    def sel_step(t, states):
        issue_scores(2 * t + 1, scb_ref)
        states = consume(2 * t, sca_ref, states)
        issue_scores(2 * t + 2, sca_ref)
        return consume(2 * t + 1, scb_ref, states)

    kp_diag = q0 + lax.broadcasted_iota(jnp.int32, (QBLK, 1), 0)
    bias_d = _tile_lanes(jnp.where(kp_diag <= qp_lane, 0.0, _NEG_INF), A_GROUP)
    q0_al = pl.multiple_of(q0, QBLK)
    init = []
    for h in range(A_KV_HEADS):
        sd = diag_ref[h] + bias_d
        m0 = _col_max(sd)
        vt = _values_with_ones(vt_ref[h * HEAD_DIM:(h + 1) * HEAD_DIM, pl.ds(q0_al, QBLK)])
        init.append((m0, _mm(vt, jnp.exp2(sd - m0))))
    init = tuple(init)
    sel_out = lax.fori_loop(0, (n_chunks + 1) // 2, sel_step, init)

    heads_t = []
    for h in range(A_KV_HEADS):
        acc_s = sel_out[h][1]
        o_s = acc_s[:HEAD_DIM] / jnp.maximum(acc_s[HEAD_DIM:HEAD_DIM + 1], DEN_FLOOR)
        for g in range(A_GROUP):
            hd = h * A_GROUP + g
            cols = slice(g * QBLK, (g + 1) * QBLK)
            heads_t.append(gates_t[hd:hd + 1, :] * o_cs[h][:, cols]
                           + gates_t[A_HEADS + hd:A_HEADS + hd + 1, :] * o_s[:, cols]
                           + gates_t[2 * A_HEADS + hd:2 * A_HEADS + hd + 1, :] * o_ws[h][:, cols])
    _store_transposed_pairs(o_ref, heads_t, sb * QBLK)


def _nsa_attention(qat, ka, vat, kc, vct, ga, bsz, seq):
    rows = NSA_QSUB * QBLK
    nstep = seq // rows
    n_seg = seq // CMP_STRIDE
    n_sel = seq // SEL_LEN
    assert seq >= A_WINDOW + QBLK and seq % SEL_CHUNK == 0 and seq % rows == 0
    c_start = jnp.arange(n_seg) * CMP_STRIDE
    s_start = jnp.arange(n_sel) * SEL_LEN
    overlap = (jnp.minimum(c_start[None, :] + CMP_LEN, s_start[:, None] + SEL_LEN)
               - jnp.maximum(c_start[None, :], s_start[:, None]))
    c2s = (jnp.clip(overlap, 0, None).astype(jnp.float32) / CMP_LEN).astype(MXU_DTYPE)
    return pl.pallas_call(
        functools.partial(_nsa_kernel, seq=seq),
        grid=(bsz, nstep),
        in_specs=[
            pl.BlockSpec((None, A_WIDTH, rows), lambda b, i: (b, 0, i)),
            pl.BlockSpec((seq, 2 * LANES), lambda b, i: (b, 0)),
            pl.BlockSpec((None, 2 * LANES, seq), lambda b, i: (b, 0, 0)),
            pl.BlockSpec((None, n_seg, LANES), lambda b, i: (b, 0, 0)),
            pl.BlockSpec((None, LANES, n_seg), lambda b, i: (b, 0, 0)),
            pl.BlockSpec((rows, LANES), lambda b, i: (b * nstep + i, 0)),
            _resident((n_sel, n_seg)),
        ],
        out_specs=pl.BlockSpec((rows, A_WIDTH), lambda b, i: (b * nstep + i, 0)),
        out_shape=jax.ShapeDtypeStruct((bsz * seq, A_WIDTH), MXU_DTYPE),
        scratch_shapes=[pltpu.VMEM((A_KV_HEADS, n_sel, QBLK), jnp.float32)]
        + [pltpu.VMEM((A_KV_HEADS, min(SEL_CHUNK, seq), A_GROUP * QBLK), jnp.float32)] * 2
        + [pltpu.VMEM((A_KV_HEADS, n_seg, A_GROUP * QBLK), jnp.float32),
           pltpu.VMEM((A_KV_HEADS, min(A_WINDOW + QBLK, seq), A_GROUP * QBLK), jnp.float32),
           pltpu.VMEM((A_KV_HEADS, QBLK, A_GROUP * QBLK), jnp.float32)],
        compiler_params=_params(("parallel", "arbitrary")),
        name="nsa_attention",
    )(qat, ka, vat, kc, vct, ga, c2s)


def _swa_kernel(sink_ref, qt_ref, kv_ref, vt_ref, o_ref, sc_ref, *, seq):
    i = pl.program_id(1)
    span = min(B_WINDOW + QBLK, seq)
    sink = jnp.concatenate([jnp.full((1, QBLK), sink_ref[hd] * LOG2E, jnp.float32) for hd in range(B_HEADS)],
                           axis=1)
    starts = []
    for sb in range(SWA_QSUB):
        q0 = (i * SWA_QSUB + sb) * QBLK
        w0 = pl.multiple_of(jnp.maximum(q0 + QBLK - span, 0), QBLK)
        starts.append((q0, w0))
        q_t = jnp.concatenate([qt_ref[hd * HEAD_DIM:(hd + 1) * HEAD_DIM, sb * QBLK:(sb + 1) * QBLK]
                               for hd in range(B_HEADS)], axis=1)
        sc_ref[sb] = _mm(kv_ref[pl.ds(w0, span), :], _pad_head_rows(q_t, 0, 0.0))
    for sb, (q0, w0) in enumerate(starts):
        qp_lane = q0 + lax.broadcasted_iota(jnp.int32, (1, QBLK), 1)
        diff = qp_lane - (w0 + lax.broadcasted_iota(jnp.int32, (span, 1), 0))
        bias = _tile_lanes(jnp.where(diff >= 0, jnp.where(diff < B_WINDOW, 0.0, _NEG_INF), _NEG_INF), B_HEADS)
        s = sc_ref[sb] + bias
        m = jnp.maximum(_col_max(s), sink)
        acc = _mm(_values_with_ones(vt_ref[:, pl.ds(w0, span)]), jnp.exp2(s - m))
        o_t = acc[:HEAD_DIM] / (acc[HEAD_DIM:HEAD_DIM + 1] + jnp.exp2(sink - m))
        _store_transposed_pairs(o_ref, [o_t[:, hd * QBLK:(hd + 1) * QBLK] for hd in range(B_HEADS)], sb * QBLK)


def _swa_attention(qbt, kvb, vbt, sinks, bsz, seq):
    rows = SWA_QSUB * QBLK
    nstep = seq // rows
    span = min(B_WINDOW + QBLK, seq)
    assert seq >= B_WINDOW + QBLK and seq % rows == 0
    return pl.pallas_call(
        functools.partial(_swa_kernel, seq=seq),
        grid=(bsz, nstep),
        in_specs=[
            pl.BlockSpec(memory_space=pltpu.SMEM),
            pl.BlockSpec((None, B_WIDTH, rows), lambda b, i: (b, 0, i)),
            pl.BlockSpec((seq, LANES), lambda b, i: (b, 0)),
            pl.BlockSpec((None, HEAD_DIM, seq), lambda b, i: (b, 0, 0)),
        ],
        out_specs=pl.BlockSpec((rows, B_WIDTH), lambda b, i: (b * nstep + i, 0)),
        out_shape=jax.ShapeDtypeStruct((bsz * seq, B_WIDTH), MXU_DTYPE),
        scratch_shapes=[pltpu.VMEM((SWA_QSUB, span, B_HEADS * QBLK), jnp.float32)],
        compiler_params=_params(("parallel", "arbitrary")),
        name="swa_attention",
    )(sinks, qbt, kvb, vbt)


def _s5_discretize_kernel(are_ref, aim_ref, logdt_ref, bre_ref, bim_ref,
                          abr_ref, abi_ref, bbr_ref, bbi_ref):
    ar = are_ref[...]
    ai = aim_ref[...]
    dt = jnp.exp(logdt_ref[...])
    mag = jnp.exp(dt * ar)
    abr = mag * jnp.cos(dt * ai)
    abi = mag * jnp.sin(dt * ai)
    den = ar * ar + ai * ai
    nr = abr - 1.0
    coef_r = (nr * ar + abi * ai) / den
    coef_i = (abi * ar - nr * ai) / den
    abr_ref[...] = abr
    abi_ref[...] = abi
    br = bre_ref[...]
    bi = bim_ref[...]
    bbr_ref[...] = coef_r[:, None, :] * br - coef_i[:, None, :] * bi
    bbi_ref[...] = coef_r[:, None, :] * bi + coef_i[:, None, :] * br


def _s5_discretize(a_re, a_im, log_dt, b_re, b_im):
    lg = a_re.shape[0] * a_re.shape[1]
    a2 = lambda a: a.reshape(lg, C_STATE)
    b3 = lambda b: jnp.swapaxes(b.reshape(lg, C_STATE, C_GROUP_CH), 1, 2)
    sa = jax.ShapeDtypeStruct((lg, C_STATE), jnp.float32)
    sb = jax.ShapeDtypeStruct((lg, C_GROUP_CH, C_STATE), jnp.float32)
    return pl.pallas_call(_s5_discretize_kernel, out_shape=[sa, sa, sb, sb], name="s5_discretize")(
        a2(a_re), a2(a_im), log_dt.reshape(lg, 1), b3(b_re), b3(b_im))


def _s5_kernel(u_ref, bmat_ref, abar_ref, cmat_ref, d_ref, gw_ref, gb_ref, y_ref, state_ref, ut_ref, xs_ref):
    bsz, t_steps, _ = u_ref.shape

    @pl.when(pl.program_id(0) == 0)
    def _():
        state_ref[...] = jnp.zeros_like(state_ref)

    def gather(t, carry):
        ut_ref[pl.ds(pl.multiple_of(t * bsz, bsz), bsz), :] = u_ref[:, t, :]
        return carry

    lax.fori_loop(0, t_steps, gather, 0, unroll=8)
    u = ut_ref[...]
    for hf in range(2):
        bu = _mm(u[:, hf * _HALF_CH:(hf + 1) * _HALF_CH], bmat_ref[hf])
        xs_ref[:, hf * _HALF_ST:(hf + 1) * _HALF_ST] = bu[:, :_HALF_ST]
        xs_ref[:, C_STATES + hf * _HALF_ST:C_STATES + (hf + 1) * _HALF_ST] = bu[:, _HALF_ST:]

    ys = []
    for hf in range(2):
        re = slice(hf * _HALF_ST, (hf + 1) * _HALF_ST)
        im = slice(C_STATES + hf * _HALF_ST, C_STATES + (hf + 1) * _HALF_ST)
        ar = jnp.broadcast_to(abar_ref[0:1, re], (bsz, _HALF_ST))
        ai = jnp.broadcast_to(abar_ref[0:1, im], (bsz, _HALF_ST))
        xr = state_ref[:, re]
        xi = state_ref[:, im]
        for t in range(t_steps):
            rows = slice(t * bsz, (t + 1) * bsz)
            xr, xi = (ar * xr - ai * xi + xs_ref[rows, re], ar * xi + ai * xr + xs_ref[rows, im])
            xs_ref[rows, re] = xr
            xs_ref[rows, im] = xi
        state_ref[:, re] = xr
        state_ref[:, im] = xi
        ys.append(_mm(xs_ref[:, re], cmat_ref[hf, :_HALF_ST, :]) + _mm(xs_ref[:, im], cmat_ref[hf, _HALF_ST:, :]))
    y = jnp.concatenate(ys, axis=1) + d_ref[...] * u
    z = jax.nn.gelu(y)
    out = z * jax.nn.sigmoid(_mm(z, gw_ref[...]) + gb_ref[...])
    y_ref[...] = out.reshape(t_steps, bsz, C_WIDTH).astype(y_ref.dtype)


def _s5(u, bmat, abar, cmat, d, glu_w, glu_b, layer):
    bsz, seq, _ = u.shape
    t_steps = min(S5_CHUNK, seq)
    return pl.pallas_call(
        _s5_kernel,
        grid=(seq // t_steps,),
        in_specs=[
            pl.BlockSpec((bsz, t_steps, C_WIDTH), lambda i: (0, i, 0)),
            _resident_layer((2, _HALF_CH, 2 * _HALF_ST), layer), _resident_layer((1, 2 * C_STATES), layer),
            _resident_layer((2, 2 * _HALF_ST, _HALF_CH), layer), _resident_layer((1, C_WIDTH), layer),
            _resident_layer((C_WIDTH, C_WIDTH), layer), _resident_layer((1, C_WIDTH), layer),
        ],
        out_specs=pl.BlockSpec((t_steps, bsz, C_WIDTH), lambda i: (i, 0, 0)),
        out_shape=jax.ShapeDtypeStruct((seq, bsz, C_WIDTH), MXU_DTYPE),
        scratch_shapes=[pltpu.VMEM((bsz, 2 * C_STATES), jnp.float32),
                        pltpu.VMEM((t_steps * bsz, C_WIDTH), jnp.float32),
                        pltpu.VMEM((t_steps * bsz, 2 * C_STATES), jnp.float32)],
        compiler_params=_params(("arbitrary",)),
        name="s5_scan",
    )(u, bmat, abar, cmat, d, glu_w, glu_b)


def _block_diag(per_group):
    depth, g, r, c = per_group.shape
    eye = jnp.eye(g, dtype=per_group.dtype)
    return jnp.einsum("lgrc,gk->lgrkc", per_group, eye).reshape(depth, g * r, g * c)


def _merge_kernel(x_ref, ya_ref, yb_ref, yc_ref, gain_ref, wg_ref, pa_ref, pb_ref, pc_ref, wo_ref, o_ref):
    x = x_ref[...]
    h = _rmsnorm(x, gain_ref[...]).astype(MXU_DTYPE)
    merged = None
    for j, (y_ref, p_ref) in enumerate(((ya_ref, pa_ref), (yb_ref, pb_ref), (yc_ref, pc_ref))):
        gate = jax.nn.sigmoid(jnp.dot(h, wg_ref[:, j * D_MODEL:(j + 1) * D_MODEL],
                                      preferred_element_type=jnp.float32))
        term = gate * jnp.dot(y_ref[...], p_ref[...], preferred_element_type=jnp.float32)
        merged = term if merged is None else merged + term
    o_ref[...] = x + _mm(merged, wo_ref[...])


def _merge(x2, ya, yb, yc, gain, wg, pa, pb, pc, wo, layer):
    n = x2.shape[0]
    tm = min(ROW_TILE, n)

    def rows(width):
        return pl.BlockSpec((tm, width), lambda i: (i, 0))

    return pl.pallas_call(
        _merge_kernel,
        grid=(n // tm,),
        in_specs=[rows(D_MODEL), rows(A_WIDTH), rows(B_WIDTH), rows(C_WIDTH), _resident_layer((1, D_MODEL), layer),
                  _resident_layer((D_MODEL, N_BRANCH * D_MODEL), layer), _resident_layer((A_WIDTH, D_MODEL), layer),
                  _resident_layer((B_WIDTH, D_MODEL), layer), _resident_layer((C_WIDTH, D_MODEL), layer),
                  _resident_layer((D_MODEL, D_MODEL), layer)],
        out_specs=rows(D_MODEL),
        out_shape=jax.ShapeDtypeStruct((n, D_MODEL), jnp.float32),
        compiler_params=_params(("parallel",)),
        name="merge",
    )(x2, ya, yb, yc, gain, wg, pa, pb, pc, wo)


def _mlp_kernel(x_ref, gain_ref, wu_ref, wd_ref, fgain_ref, o_ref, *, final_norm):
    x = x_ref[...]
    h = _rmsnorm(x, gain_ref[...]).astype(MXU_DTYPE)
    acc = x
    for c in range(D_FF // FF_CHUNK):
        cols = slice(c * FF_CHUNK, (c + 1) * FF_CHUNK)
        up = jnp.dot(h, wu_ref[:, cols], preferred_element_type=jnp.float32)
        acc = acc + _mm(jnp.square(jnp.maximum(up, 0.0)), wd_ref[cols, :])
    o_ref[...] = _rmsnorm(acc, fgain_ref[...]) if final_norm else acc


def _mlp(x2, gain, wu, wd, fgain, layer, final_norm):
    n = x2.shape[0]
    tm = min(ROW_TILE, n)
    rows = pl.BlockSpec((tm, D_MODEL), lambda i: (i, 0))
    return pl.pallas_call(
        functools.partial(_mlp_kernel, final_norm=final_norm),
        grid=(n // tm,),
        in_specs=[rows, _resident_layer((1, D_MODEL), layer), _resident_layer((D_MODEL, D_FF), layer),
                  _resident_layer((D_FF, D_MODEL), layer), _resident((1, D_MODEL))],
        out_specs=rows,
        out_shape=jax.ShapeDtypeStruct((n, D_MODEL), jnp.float32),
        compiler_params=_params(("parallel",)),
        name="mlp",
    )(x2, gain, wu, wd, fgain.reshape(1, D_MODEL))


def kernel(x, positions, norm_mix, w_in, nsa_cmp_pos, nsa_cmp_w1, nsa_cmp_w2, swa_sinks, s5_a_re, s5_a_im, s5_log_dt, s5_b_re, s5_b_im, s5_c_re, s5_c_im, s5_d, s5_glu_w, s5_glu_b, w_branch_a, w_branch_b, w_branch_c, w_out, norm_mlp, w_mlp_up, w_mlp_down, norm_final):
    bsz, seq, _ = x.shape
    depth = w_in.shape[0]
    n = bsz * seq
    gate_col = sum(IN_WIDTHS[:-1])
    bf = lambda w: w.astype(MXU_DTYPE)

    tables = _rope_tables(positions)
    abr, abi, bbr, bbi = _s5_discretize(s5_a_re, s5_a_im, s5_log_dt, s5_b_re, s5_b_im)
    abar = jnp.concatenate([abr.reshape(depth, 1, C_STATES), abi.reshape(depth, 1, C_STATES)], axis=2)
    bbr = bbr.reshape(depth, C_GROUPS, C_GROUP_CH, C_STATE)
    bbi = bbi.reshape(depth, C_GROUPS, C_GROUP_CH, C_STATE)
    cre_t = jnp.swapaxes(s5_c_re, 2, 3)
    cim_t = jnp.swapaxes(s5_c_im, 2, 3)
    halves = (slice(0, C_GROUPS // 2), slice(C_GROUPS // 2, C_GROUPS))
    bmat = bf(jnp.stack([jnp.concatenate([_block_diag(bbr[:, g]), _block_diag(bbi[:, g])], axis=2)
                         for g in halves], axis=1))
    cmat = bf(jnp.stack([jnp.concatenate([_block_diag(cre_t[:, g]), -_block_diag(cim_t[:, g])], axis=1)
                         for g in halves], axis=1))
    s5_params = (bmat, abar, cmat, s5_d.reshape(depth, 1, C_WIDTH), bf(s5_glu_w), s5_glu_b.reshape(depth, 1, C_WIDTH))
    w_nat, w_tr = _pack_w_in(w_in)
    w_gate = bf(w_in[..., gate_col:])
    cmp_params = (bf(nsa_cmp_pos.reshape(depth, 2, 1, CMP_LEN * HEAD_DIM)), bf(nsa_cmp_w1), bf(_expand_w1(nsa_cmp_w1)),
                  bf(nsa_cmp_w2))
    gain_mix = norm_mix.reshape(depth, 1, D_MODEL)
    gain_mlp = norm_mlp.reshape(depth, 1, D_MODEL)
    stacked = [bf(w) for w in (w_branch_a, w_branch_b, w_branch_c, w_out, w_mlp_up, w_mlp_down)]

    x2 = x.reshape(n, D_MODEL)
    for l in range(depth):
        qat, qbt, vat, vbt, ka, kvb, kseg, vseg, ga, uc = _in_projection(x2, gain_mix, w_nat, w_tr, tables, l, bsz, seq)

        kc, vct = _compress(kseg, vseg, *cmp_params, l, bsz)
        ya = _nsa_attention(qat, ka, vat, kc, vct, ga, bsz, seq)

        yb = _swa_attention(qbt, kvb, vbt, swa_sinks[l], bsz, seq)

        yc_tm = _s5(uc.reshape(bsz, seq, C_WIDTH), *s5_params, l)
        yc = jnp.swapaxes(yc_tm, 0, 1).reshape(n, C_WIDTH)

        x2 = _merge(x2, ya, yb, yc, gain_mix, w_gate, *stacked[:4], l)
        x2 = _mlp(x2, gain_mlp, *stacked[4:], norm_final, l, l == depth - 1)
    return x2.reshape(bsz, seq, D_MODEL)
```

```python
import functools

import jax
import jax.numpy as jnp
from jax import lax
from jax.experimental import pallas as pl
from jax.experimental.pallas import tpu as pltpu

D_MODEL = 1024
HEAD_DIM = 64
ROPE_THETA = 10000.0
NORM_EPS = 1e-6
QBLK = 128

A_HEADS = 8
A_KV_HEADS = 2
A_GROUP = A_HEADS // A_KV_HEADS
A_WIDTH = A_HEADS * HEAD_DIM
NSA_BRANCHES = 3
CMP_LEN = 32
CMP_STRIDE = 16
CMP_HIDDEN = 2 * HEAD_DIM
SEL_LEN = 64
SEL_TOPK = 16
A_WINDOW = 512
FORCED_SCORE = 1e4

B_HEADS = 8
B_WIDTH = B_HEADS * HEAD_DIM
B_WINDOW = 128

C_WIDTH = 512
C_GROUP_CH = 16
C_GROUPS = C_WIDTH // C_GROUP_CH
C_STATE = 64
C_STATES = C_GROUPS * C_STATE
_HALF_CH = C_WIDTH // 2
_HALF_ST = C_STATES // 2

D_FF = 4 * D_MODEL
N_BRANCH = 3

IN_WIDTHS = (A_WIDTH, NSA_BRANCHES * 2 * A_KV_HEADS * HEAD_DIM, NSA_BRANCHES * A_HEADS,
             B_WIDTH, 2 * HEAD_DIM, C_WIDTH, N_BRANCH * D_MODEL)

LANES = 128
SUBLANES = 8
VMEM_LIMIT = 56 * 1024 * 1024

MXU_DTYPE = jnp.bfloat16
ROW_TILE = 1024
S5_CHUNK = 128
SEL_CHUNK = 256
SWA_QSUB = 8
NSA_QSUB = 2
FF_CHUNK = 1024

LOG2E = 1.4426950408889634
DEN_FLOOR = 1e-30
_NEG_INF = float("-inf")
_SEL_SHIFT = SEL_LEN.bit_length() - 1


def _f32(x):
    return x.astype(jnp.float32)


def _mm(a, b):
    return jnp.dot(a.astype(MXU_DTYPE), b.astype(MXU_DTYPE), preferred_element_type=jnp.float32)


def _mm_nt(a, b):
    return lax.dot_general(a.astype(MXU_DTYPE), b.astype(MXU_DTYPE), (((1,), (1,)), ((), ())),
                           preferred_element_type=jnp.float32)


def _rmsnorm(x, gain):
    return x * lax.rsqrt(jnp.mean(x * x, axis=-1, keepdims=True) + NORM_EPS) * gain


def _resident(shape):
    zeros = (0,) * len(shape)
    return pl.BlockSpec(shape, lambda *_: zeros, pipeline_mode=pl.Buffered(1))


def _resident_layer(shape, layer):
    zeros = (0,) * len(shape)
    return pl.BlockSpec((None,) + tuple(shape), lambda *_: (layer,) + zeros, pipeline_mode=pl.Buffered(1))


def _params(sem):
    return pltpu.CompilerParams(dimension_semantics=sem, vmem_limit_bytes=VMEM_LIMIT)


def _rope_table_kernel(pos_ref, invf_ref, sign_ref, cos_ref, sin_ref):
    ang = _f32(pos_ref[...]) * invf_ref[...]
    cos_ref[...] = jnp.cos(ang)
    sin_ref[...] = jnp.sin(ang) * sign_ref[...]


def _rope_tables(positions):
    n = positions.size
    half = HEAD_DIM // 2
    inv_freq = ROPE_THETA ** (-jnp.arange(half, dtype=jnp.float32) / half)
    invf = jnp.tile(inv_freq, LANES // half)
    sign = jnp.tile(jnp.concatenate([-jnp.ones(half, jnp.float32), jnp.ones(half, jnp.float32)]),
                    LANES // HEAD_DIM)
    tm = min(1024, n)
    natural = pl.pallas_call(
        _rope_table_kernel,
        grid=(n // tm,),
        in_specs=[pl.BlockSpec((tm, 1), lambda i: (i, 0)), _resident((1, LANES)), _resident((1, LANES))],
        out_specs=[pl.BlockSpec((tm, LANES), lambda i: (i, 0))] * 2,
        out_shape=[jax.ShapeDtypeStruct((n, LANES), jnp.float32)] * 2,
        compiler_params=_params(("parallel",)),
        name="rope_tables",
    )(positions.reshape(n, 1), invf.reshape(1, LANES), sign.reshape(1, LANES))
    transposed = pl.pallas_call(
        _rope_table_kernel,
        grid=(n // tm,),
        in_specs=[pl.BlockSpec((1, tm), lambda i: (0, i)), _resident((HEAD_DIM, 1)), _resident((HEAD_DIM, 1))],
        out_specs=[pl.BlockSpec((HEAD_DIM, tm), lambda i: (0, i))] * 2,
        out_shape=[jax.ShapeDtypeStruct((HEAD_DIM, n), jnp.float32)] * 2,
        compiler_params=_params(("parallel",)),
        name="rope_tables_t",
    )(positions.reshape(1, n), invf[:HEAD_DIM].reshape(HEAD_DIM, 1), sign[:HEAD_DIM].reshape(HEAD_DIM, 1))
    return natural + transposed


def _swap_halves(x):
    lane = lax.broadcasted_iota(jnp.int32, x.shape, 1)
    first = (lane & (HEAD_DIM - 1)) < HEAD_DIM // 2
    return jnp.where(first, pltpu.roll(x, LANES - HEAD_DIM // 2, 1), pltpu.roll(x, HEAD_DIM // 2, 1))


def _proj_kernel(x_ref, gain_ref, wn_ref, wt_ref, cos_ref, sin_ref, cost_ref, sint_ref,
                 qat_ref, qbt_ref, vat_ref, vbt_ref, ka_ref, kvb_ref, kseg_ref, vseg_ref, ga_ref, uc_ref,
                 seg_ref):
    h = _rmsnorm(x_ref[...], gain_ref[...]).astype(MXU_DTYPE)
    nat = jnp.dot(h, wn_ref[...], preferred_element_type=jnp.float32)
    tr = _mm_nt(wt_ref[...], h)

    cos = cos_ref[...]
    sin = sin_ref[...]

    def roped(j):
        xj = nat[:, j * LANES:(j + 1) * LANES]
        return xj, xj * cos + _swap_halves(xj) * sin

    for j in range(1, NSA_BRANCHES):
        ka_ref[:, (j - 1) * LANES:j * LANES] = roped(j)[1].astype(ka_ref.dtype)
    raw, rot = roped(3)
    lane = lax.broadcasted_iota(jnp.int32, raw.shape, 1)
    kvb_ref[...] = jnp.where(lane < HEAD_DIM, rot, raw).astype(kvb_ref.dtype)
    n_seg_tile = nat.shape[0] // CMP_STRIDE
    for j, (src, dst_ref) in enumerate(((roped(0)[1], kseg_ref), (nat[:, 4 * LANES:5 * LANES], vseg_ref))):
        seg_ref[j] = src
        for tok in range(CMP_STRIDE):
            dst_ref[:, tok * LANES:(tok + 1) * LANES] = (
                seg_ref[j, pl.ds(tok, n_seg_tile, stride=CMP_STRIDE), :].astype(dst_ref.dtype))
    ga_ref[...] = jax.nn.sigmoid(nat[:, 5 * LANES:6 * LANES])
    uc_ref[...] = nat[:, 6 * LANES:]

    cos_t = cost_ref[...]
    sin_t = sint_ref[...]
    half = HEAD_DIM // 2

    def roped_t(r0):
        xh = tr[r0:r0 + HEAD_DIM]
        return xh * cos_t + jnp.concatenate([xh[half:], xh[:half]], axis=0) * sin_t

    for hd in range(A_HEADS):
        qat_ref[hd * HEAD_DIM:(hd + 1) * HEAD_DIM, :] = roped_t(hd * HEAD_DIM).astype(qat_ref.dtype)
    for hd in range(B_HEADS):
        qbt_ref[hd * HEAD_DIM:(hd + 1) * HEAD_DIM, :] = roped_t(A_WIDTH + hd * HEAD_DIM).astype(qbt_ref.dtype)
    v0 = A_WIDTH + B_WIDTH
    vat_ref[...] = tr[v0:v0 + 2 * LANES].astype(vat_ref.dtype)
    vbt_ref[...] = tr[v0 + 2 * LANES:].astype(vbt_ref.dtype)


_NAT_COLS = 10 * LANES
_TR_ROWS = A_WIDTH + B_WIDTH + 2 * LANES + HEAD_DIM


def _pack_w_in(w_in):
    o = [0]
    for w in IN_WIDTHS:
        o.append(o[-1] + w)
    scale = HEAD_DIM ** -0.5 * LOG2E
    q_a = w_in[..., o[0]:o[1]] * scale
    kv_cols = [w_in[..., o[1] + j * LANES:o[1] + (j + 1) * LANES] for j in range(2 * NSA_BRANCHES)]
    k_a = jnp.concatenate(kv_cols[0::2], axis=-1)
    v_a = jnp.concatenate(kv_cols[1::2], axis=-1)
    g_a = w_in[..., o[2]:o[3]]
    g_a = jnp.concatenate([g_a, jnp.zeros(g_a.shape[:-1] + (LANES - NSA_BRANCHES * A_HEADS,), g_a.dtype)], axis=-1)
    q_b = w_in[..., o[3]:o[4]] * scale
    kv_b = w_in[..., o[4]:o[5]]
    u_c = w_in[..., o[5]:o[6]]
    w_nat = jnp.concatenate([k_a, kv_b, v_a[..., :LANES], g_a, u_c], axis=-1)
    w_tr = jnp.concatenate([q_a, q_b, v_a[..., LANES:], kv_b[..., HEAD_DIM:]], axis=-1).astype(MXU_DTYPE)
    return w_nat.astype(MXU_DTYPE), jnp.swapaxes(w_tr, -1, -2)


def _in_projection(x2, gain, w_nat, w_tr, tables, layer, bsz, seq):
    n = x2.shape[0]
    tm = min(ROW_TILE, seq)
    tiles = seq // tm

    def rows(width):
        return pl.BlockSpec((tm, width), lambda i: (i, 0))

    def lanes(height):
        return pl.BlockSpec((None, height, tm), lambda i: (i // tiles, 0, i % tiles))

    table_t = pl.BlockSpec((HEAD_DIM, tm), lambda i: (0, i))
    t_heights = (A_WIDTH, B_WIDTH, 2 * LANES, HEAD_DIM)
    widths = (2 * LANES, LANES, LANES, C_WIDTH)
    dtypes = (MXU_DTYPE, MXU_DTYPE, jnp.float32, jnp.float32)
    seg_rows = tm // CMP_STRIDE
    seg_spec = pl.BlockSpec((seg_rows, CMP_STRIDE * LANES), lambda i: (i, 0))
    seg_shape = jax.ShapeDtypeStruct((n // CMP_STRIDE, CMP_STRIDE * LANES), MXU_DTYPE)
    return pl.pallas_call(
        _proj_kernel,
        grid=(n // tm,),
        in_specs=[rows(D_MODEL), _resident_layer((1, D_MODEL), layer), _resident_layer((D_MODEL, _NAT_COLS), layer),
                  _resident_layer((_TR_ROWS, D_MODEL), layer), rows(LANES), rows(LANES), table_t, table_t],
        out_specs=[lanes(hh) for hh in t_heights] + [rows(w) for w in widths[:2]] + [seg_spec, seg_spec]
        + [rows(w) for w in widths[2:]],
        out_shape=[jax.ShapeDtypeStruct((bsz, hh, seq), MXU_DTYPE) for hh in t_heights]
        + [jax.ShapeDtypeStruct((n, w), dt) for w, dt in zip(widths[:2], dtypes[:2])] + [seg_shape, seg_shape]
        + [jax.ShapeDtypeStruct((n, w), dt) for w, dt in zip(widths[2:], dtypes[2:])],
        scratch_shapes=[pltpu.VMEM((2, tm, LANES), jnp.float32)],
        compiler_params=_params(("parallel",)),
        name="in_projection",
    )(x2, gain, w_nat, w_tr, *tables)


def _compress_kernel(kseg_ref, vseg_ref, pos_ref, w1_ref, w1x_ref, w2_ref, kc_ref, vct_ref):
    nseg = kseg_ref.shape[0]
    both = []
    for j, seg_ref in enumerate((kseg_ref, vseg_ref)):
        seg = seg_ref[...]
        pos_term = _mm(pos_ref[j], w1_ref[j])
        heads = []
        for h in range(A_KV_HEADS):
            hidden = (_mm(seg, w1x_ref[j, 0, h]) + pltpu.roll(_mm(seg, w1x_ref[j, 1, h]), nseg - 1, 0)
                      + pos_term)
            heads.append(_mm(jax.nn.gelu(hidden), w2_ref[j]))
        both.append(jnp.concatenate(heads, axis=1))
    kc_ref[...] = both[0].astype(kc_ref.dtype)
    vct_ref[...] = both[1].T.astype(vct_ref.dtype)


def _expand_w1(w1):
    depth = w1.shape[0]
    w = w1.reshape(depth, 2, 2, CMP_STRIDE, HEAD_DIM, CMP_HIDDEN)
    eye = jnp.eye(A_KV_HEADS, dtype=w1.dtype)
    return jnp.einsum("ljatdc,hg->ljahtgdc", w, eye).reshape(depth, 2, 2, A_KV_HEADS, CMP_STRIDE * LANES,
                                                              CMP_HIDDEN)


def _compress(kseg, vseg, pos_flat, w1, w1x, w2, layer, bsz):
    nseg = kseg.shape[0] // bsz
    segw = kseg.shape[1]
    seg_spec = pl.BlockSpec((nseg, segw), lambda b: (b, 0))
    return pl.pallas_call(
        _compress_kernel,
        grid=(bsz,),
        in_specs=[seg_spec, seg_spec, _resident_layer((2, 1, CMP_LEN * HEAD_DIM), layer),
                  _resident_layer((2, CMP_LEN * HEAD_DIM, CMP_HIDDEN), layer),
                  _resident_layer((2, 2, A_KV_HEADS, segw, CMP_HIDDEN), layer),
                  _resident_layer((2, CMP_HIDDEN, HEAD_DIM), layer)],
        out_specs=[pl.BlockSpec((None, nseg, LANES), lambda b: (b, 0, 0)),
                   pl.BlockSpec((None, LANES, nseg), lambda b: (b, 0, 0))],
        out_shape=[jax.ShapeDtypeStruct((bsz, nseg, LANES), MXU_DTYPE),
                   jax.ShapeDtypeStruct((bsz, LANES, nseg), MXU_DTYPE)],
        compiler_params=_params(("parallel",)),
        name="nsa_compress",
    )(kseg, vseg, pos_flat, w1, w1x, w2)


def _split3(x):
    hi = x.astype(MXU_DTYPE)
    r1 = x - _f32(hi)
    mid = r1.astype(MXU_DTYPE)
    lo = (r1 - _f32(mid)).astype(MXU_DTYPE)
    return hi, mid, lo


def _col_max(s):
    return jnp.max(s, axis=0, keepdims=True)


def _tile_lanes(x, n):
    return jnp.concatenate([x] * n, axis=1)


def _pad_head_rows(x, h, fill):
    other = jnp.full(x.shape, fill, x.dtype)
    return jnp.concatenate([x, other] if h == 0 else [other, x], axis=0)


def _values_with_ones(vt):
    return jnp.concatenate([vt, jnp.ones((2 * SUBLANES, vt.shape[1]), vt.dtype)], axis=0)


def _store_transposed_pairs(o_ref, heads_t, row0=0):
    for p in range(len(heads_t) // 2):
        pair = jnp.concatenate([heads_t[2 * p], heads_t[2 * p + 1]], axis=0)
        o_ref[row0:row0 + QBLK, p * LANES:(p + 1) * LANES] = pair.T.astype(o_ref.dtype)


def _rank_blocks(selb_ref, p_sum, c2s_ref, blk, before, qp_lane, n_sel):
    imp_t = sum(_mm(c2s_ref[...], part) for part in _split3(p_sum))
    cur = jnp.right_shift(qp_lane, _SEL_SHIFT)
    forced = (blk == 0) | (blk == cur) | (blk == cur - 1)
    score = jnp.where(forced, FORCED_SCORE, imp_t)
    score = jnp.where(blk <= cur, score, _NEG_INF)
    n_grp = n_sel // SUBLANES
    sub = lax.broadcasted_iota(jnp.int32, (SUBLANES, QBLK), 0)
    score_grp = [score[SUBLANES * v:SUBLANES * (v + 1), :] for v in range(n_grp)]
    rank_grp = [jnp.zeros((SUBLANES, QBLK), jnp.float32) for _ in range(n_grp)]
    for j in range(n_sel):
        row = jnp.broadcast_to(score[j:j + 1, :], (SUBLANES, QBLK))
        vj, rj = divmod(j, SUBLANES)
        for v in range(n_grp):
            if v > vj:
                inc = jnp.where(row >= score_grp[v], 1.0, 0.0)
            elif v < vj:
                inc = jnp.where(row > score_grp[v], 1.0, 0.0)
            else:
                inc = jnp.where(sub > rj, jnp.where(row >= score_grp[v], 1.0, 0.0),
                                jnp.where(row > score_grp[v], 1.0, 0.0))
            rank_grp[v] = rank_grp[v] + inc
    rank = jnp.concatenate(rank_grp, axis=0)
    picked = jnp.where(rank < float(min(SEL_TOPK, n_sel)), 0.0, _NEG_INF)
    selb_ref[...] = jnp.where(before, picked, _NEG_INF)


def _nsa_kernel(*refs, seq):
    for sb in range(NSA_QSUB):
        _nsa_block(sb, *refs, seq=seq)


def _nsa_block(sb, qt_ref, k_ref, vt_ref, kc_ref, vct_ref, g_ref, c2s_ref, o_ref,
               selb_ref, sca_ref, scb_ref, cmp_ref, win_ref, diag_ref, *, seq):
    q0 = (pl.program_id(1) * NSA_QSUB + sb) * QBLK
    q_cols = slice(sb * QBLK, (sb + 1) * QBLK)
    n_cmp = kc_ref.shape[0]
    n_sel = seq // SEL_LEN
    kc_sz = min(SEL_CHUNK, seq)
    span = min(A_WINDOW + QBLK, seq)
    blocks_per_chunk = kc_sz // SEL_LEN

    gates_t = g_ref[q_cols, :].T
    qp_lane = q0 + lax.broadcasted_iota(jnp.int32, (1, QBLK), 1)

    cmp_end = lax.broadcasted_iota(jnp.int32, (n_cmp, 1), 0) * CMP_STRIDE + (CMP_LEN - 1)
    bias_c = _tile_lanes(jnp.where(cmp_end <= qp_lane, 0.0, _NEG_INF), A_GROUP)

    w0 = pl.multiple_of(jnp.maximum(q0 + QBLK - span, 0), QBLK)
    diff = qp_lane - (w0 + lax.broadcasted_iota(jnp.int32, (span, 1), 0))
    bias_w = _tile_lanes(jnp.where(diff >= 0, jnp.where(diff < A_WINDOW, 0.0, _NEG_INF), _NEG_INF), A_GROUP)

    w_qs = []
    for h in range(A_KV_HEADS):
        q_t = jnp.concatenate([qt_ref[(h * A_GROUP + g) * HEAD_DIM:(h * A_GROUP + g + 1) * HEAD_DIM, q_cols]
                               for g in range(A_GROUP)], axis=1)
        w_qs.append(_pad_head_rows(q_t, h, 0.0))

    n_chunks = (q0 + kc_sz - 1) // kc_sz
    last = jnp.maximum(n_chunks - 1, 0)

    def issue_scores(c, dst_ref):
        k0 = pl.multiple_of(jnp.minimum(c, last) * kc_sz, kc_sz)
        keys = k_ref[pl.ds(k0, kc_sz), :LANES]
        for h in range(A_KV_HEADS):
            dst_ref[h] = _mm(keys, w_qs[h])

    for h in range(A_KV_HEADS):
        cmp_ref[h] = _mm(kc_ref[...], w_qs[h])
    for h in range(A_KV_HEADS):
        win_ref[h] = _mm(k_ref[pl.ds(w0, span), LANES:], w_qs[h])
    issue_scores(0, sca_ref)
    for h in range(A_KV_HEADS):
        diag_ref[h] = _mm(k_ref[pl.ds(pl.multiple_of(q0, QBLK), QBLK), :LANES], w_qs[h])

    o_cs, o_ws = [], []
    for h in range(A_KV_HEADS):
        s = cmp_ref[h] + bias_c
        m = _col_max(s)
        m = jnp.where(m == _NEG_INF, 0.0, m)
        e = jnp.exp2(s - m)
        p = e * (1.0 / jnp.maximum(jnp.sum(e, axis=0, keepdims=True), DEN_FLOOR))
        o_cs.append(_mm(vct_ref[...], p)[h * HEAD_DIM:(h + 1) * HEAD_DIM])

        p_sum = p[:, 0:QBLK]
        for g in range(1, A_GROUP):
            p_sum = p_sum + p[:, g * QBLK:(g + 1) * QBLK]
        blk = lax.broadcasted_iota(jnp.int32, (n_sel, QBLK), 0)
        before = blk < jnp.right_shift(q0, _SEL_SHIFT)
        _rank_blocks(selb_ref.at[h], p_sum, c2s_ref, blk, before, qp_lane, n_sel)

    for h in range(A_KV_HEADS):
        sw = win_ref[h] + bias_w
        pw = jnp.exp2(sw - _col_max(sw))
        acc_w = _mm(_values_with_ones(vt_ref[LANES + h * HEAD_DIM:LANES + (h + 1) * HEAD_DIM, pl.ds(w0, span)]), pw)
        o_ws.append(acc_w[:HEAD_DIM] / acc_w[HEAD_DIM:HEAD_DIM + 1])

    def consume(c, src_ref, states):
        cc = jnp.minimum(c, last)
        k0 = pl.multiple_of(cc * kc_sz, kc_sz)
        out = []
        for h in range(A_KV_HEADS):
            m_run, acc = states[h]
            rows = [_tile_lanes(jnp.where(c < n_chunks, selb_ref[h, pl.ds(cc * blocks_per_chunk + jj, 1), :], _NEG_INF),
                                A_GROUP) for jj in range(blocks_per_chunk)]
            blocks = [src_ref[h, jj * SEL_LEN:(jj + 1) * SEL_LEN, :] for jj in range(blocks_per_chunk)]
            part = None
            for sc_blk, row in zip(blocks, rows):
                blk_max = jnp.max(sc_blk.reshape(SEL_LEN // SUBLANES, SUBLANES, A_GROUP * QBLK), axis=0) + row
                part = blk_max if part is None else jnp.maximum(part, blk_max)
            m_new = jnp.maximum(m_run, _col_max(part))
            pr = jnp.concatenate([jnp.exp2(sc_blk + (row - m_new)) for sc_blk, row in zip(blocks, rows)], axis=0)
            vt = _values_with_ones(vt_ref[h * HEAD_DIM:(h + 1) * HEAD_DIM, pl.ds(k0, kc_sz)])
            out.append((m_new, jnp.exp2(m_run - m_new) * acc + _mm(vt, pr)))
        return tuple(out)

    def sel_step(t, states):
        issue_scores(2 * t + 1, scb_ref)
        states = consume(2 * t, sca_ref, states)
        issue_scores(2 * t + 2, sca_ref)
        return consume(2 * t + 1, scb_ref, states)

    kp_diag = q0 + lax.broadcasted_iota(jnp.int32, (QBLK, 1), 0)
    bias_d = _tile_lanes(jnp.where(kp_diag <= qp_lane, 0.0, _NEG_INF), A_GROUP)
    q0_al = pl.multiple_of(q0, QBLK)
    init = []
    for h in range(A_KV_HEADS):
        sd = diag_ref[h] + bias_d
        m0 = _col_max(sd)
        vt = _values_with_ones(vt_ref[h * HEAD_DIM:(h + 1) * HEAD_DIM, pl.ds(q0_al, QBLK)])
        init.append((m0, _mm(vt, jnp.exp2(sd - m0))))
    init = tuple(init)
    sel_out = lax.fori_loop(0, (n_chunks + 1) // 2, sel_step, init)

    heads_t = []
    for h in range(A_KV_HEADS):
        acc_s = sel_out[h][1]
        o_s = acc_s[:HEAD_DIM] / jnp.maximum(acc_s[HEAD_DIM:HEAD_DIM + 1], DEN_FLOOR)
        for g in range(A_GROUP):
            hd = h * A_GROUP + g
            cols = slice(g * QBLK, (g + 1) * QBLK)
            heads_t.append(gates_t[hd:hd + 1, :] * o_cs[h][:, cols]
                           + gates_t[A_HEADS + hd:A_HEADS + hd + 1, :] * o_s[:, cols]
                           + gates_t[2 * A_HEADS + hd:2 * A_HEADS + hd + 1, :] * o_ws[h][:, cols])
    _store_transposed_pairs(o_ref, heads_t, sb * QBLK)


def _nsa_attention(qat, ka, vat, kc, vct, ga, bsz, seq):
    rows = NSA_QSUB * QBLK
    nstep = seq // rows
    n_seg = seq // CMP_STRIDE
    n_sel = seq // SEL_LEN
    assert seq >= A_WINDOW + QBLK and seq % SEL_CHUNK == 0 and seq % rows == 0
    c_start = jnp.arange(n_seg) * CMP_STRIDE
    s_start = jnp.arange(n_sel) * SEL_LEN
    overlap = (jnp.minimum(c_start[None, :] + CMP_LEN, s_start[:, None] + SEL_LEN)
               - jnp.maximum(c_start[None, :], s_start[:, None]))
    c2s = (jnp.clip(overlap, 0, None).astype(jnp.float32) / CMP_LEN).astype(MXU_DTYPE)
    return pl.pallas_call(
        functools.partial(_nsa_kernel, seq=seq),
        grid=(bsz, nstep),
        in_specs=[
            pl.BlockSpec((None, A_WIDTH, rows), lambda b, i: (b, 0, i)),
            pl.BlockSpec((seq, 2 * LANES), lambda b, i: (b, 0)),
            pl.BlockSpec((None, 2 * LANES, seq), lambda b, i: (b, 0, 0)),
            pl.BlockSpec((None, n_seg, LANES), lambda b, i: (b, 0, 0)),
            pl.BlockSpec((None, LANES, n_seg), lambda b, i: (b, 0, 0)),
            pl.BlockSpec((rows, LANES), lambda b, i: (b * nstep + i, 0)),
            _resident((n_sel, n_seg)),
        ],
        out_specs=pl.BlockSpec((rows, A_WIDTH), lambda b, i: (b * nstep + i, 0)),
        out_shape=jax.ShapeDtypeStruct((bsz * seq, A_WIDTH), MXU_DTYPE),
        scratch_shapes=[pltpu.VMEM((A_KV_HEADS, n_sel, QBLK), jnp.float32)]
        + [pltpu.VMEM((A_KV_HEADS, min(SEL_CHUNK, seq), A_GROUP * QBLK), jnp.float32)] * 2
        + [pltpu.VMEM((A_KV_HEADS, n_seg, A_GROUP * QBLK), jnp.float32),
           pltpu.VMEM((A_KV_HEADS, min(A_WINDOW + QBLK, seq), A_GROUP * QBLK), jnp.float32),
           pltpu.VMEM((A_KV_HEADS, QBLK, A_GROUP * QBLK), jnp.float32)],
        compiler_params=_params(("parallel", "arbitrary")),
        name="nsa_attention",
    )(qat, ka, vat, kc, vct, ga, c2s)


def _swa_kernel(sink_ref, qt_ref, kv_ref, vt_ref, o_ref, sc_ref, *, seq):
    i = pl.program_id(1)
    span = min(B_WINDOW + QBLK, seq)
    sink = jnp.concatenate([jnp.full((1, QBLK), sink_ref[hd] * LOG2E, jnp.float32) for hd in range(B_HEADS)],
                           axis=1)
    starts = []
    for sb in range(SWA_QSUB):
        q0 = (i * SWA_QSUB + sb) * QBLK
        w0 = pl.multiple_of(jnp.maximum(q0 + QBLK - span, 0), QBLK)
        starts.append((q0, w0))
        q_t = jnp.concatenate([qt_ref[hd * HEAD_DIM:(hd + 1) * HEAD_DIM, sb * QBLK:(sb + 1) * QBLK]
                               for hd in range(B_HEADS)], axis=1)
        sc_ref[sb] = _mm(kv_ref[pl.ds(w0, span), :], _pad_head_rows(q_t, 0, 0.0))
    for sb, (q0, w0) in enumerate(starts):
        qp_lane = q0 + lax.broadcasted_iota(jnp.int32, (1, QBLK), 1)
        diff = qp_lane - (w0 + lax.broadcasted_iota(jnp.int32, (span, 1), 0))
        bias = _tile_lanes(jnp.where(diff >= 0, jnp.where(diff < B_WINDOW, 0.0, _NEG_INF), _NEG_INF), B_HEADS)
        s = sc_ref[sb] + bias
        m = jnp.maximum(_col_max(s), sink)
        acc = _mm(_values_with_ones(vt_ref[:, pl.ds(w0, span)]), jnp.exp2(s - m))
        o_t = acc[:HEAD_DIM] / (acc[HEAD_DIM:HEAD_DIM + 1] + jnp.exp2(sink - m))
        _store_transposed_pairs(o_ref, [o_t[:, hd * QBLK:(hd + 1) * QBLK] for hd in range(B_HEADS)], sb * QBLK)


def _swa_attention(qbt, kvb, vbt, sinks, bsz, seq):
    rows = SWA_QSUB * QBLK
    nstep = seq // rows
    span = min(B_WINDOW + QBLK, seq)
    assert seq >= B_WINDOW + QBLK and seq % rows == 0
    return pl.pallas_call(
        functools.partial(_swa_kernel, seq=seq),
        grid=(bsz, nstep),
        in_specs=[
            pl.BlockSpec(memory_space=pltpu.SMEM),
            pl.BlockSpec((None, B_WIDTH, rows), lambda b, i: (b, 0, i)),
            pl.BlockSpec((seq, LANES), lambda b, i: (b, 0)),
            pl.BlockSpec((None, HEAD_DIM, seq), lambda b, i: (b, 0, 0)),
        ],
        out_specs=pl.BlockSpec((rows, B_WIDTH), lambda b, i: (b * nstep + i, 0)),
        out_shape=jax.ShapeDtypeStruct((bsz * seq, B_WIDTH), MXU_DTYPE),
        scratch_shapes=[pltpu.VMEM((SWA_QSUB, span, B_HEADS * QBLK), jnp.float32)],
        compiler_params=_params(("parallel", "arbitrary")),
        name="swa_attention",
    )(sinks, qbt, kvb, vbt)


def _s5_discretize_kernel(are_ref, aim_ref, logdt_ref, bre_ref, bim_ref,
                          abr_ref, abi_ref, bbr_ref, bbi_ref):
    ar = are_ref[...]
    ai = aim_ref[...]
    dt = jnp.exp(logdt_ref[...])
    mag = jnp.exp(dt * ar)
    abr = mag * jnp.cos(dt * ai)
    abi = mag * jnp.sin(dt * ai)
    den = ar * ar + ai * ai
    nr = abr - 1.0
    coef_r = (nr * ar + abi * ai) / den
    coef_i = (abi * ar - nr * ai) / den
    abr_ref[...] = abr
    abi_ref[...] = abi
    br = bre_ref[...]
    bi = bim_ref[...]
    bbr_ref[...] = coef_r[:, None, :] * br - coef_i[:, None, :] * bi
    bbi_ref[...] = coef_r[:, None, :] * bi + coef_i[:, None, :] * br


def _s5_discretize(a_re, a_im, log_dt, b_re, b_im):
    lg = a_re.shape[0] * a_re.shape[1]
    a2 = lambda a: a.reshape(lg, C_STATE)
    b3 = lambda b: jnp.swapaxes(b.reshape(lg, C_STATE, C_GROUP_CH), 1, 2)
    sa = jax.ShapeDtypeStruct((lg, C_STATE), jnp.float32)
    sb = jax.ShapeDtypeStruct((lg, C_GROUP_CH, C_STATE), jnp.float32)
    return pl.pallas_call(_s5_discretize_kernel, out_shape=[sa, sa, sb, sb], name="s5_discretize")(
        a2(a_re), a2(a_im), log_dt.reshape(lg, 1), b3(b_re), b3(b_im))


def _s5_kernel(u_ref, bmat_ref, abar_ref, cmat_ref, d_ref, gw_ref, gb_ref, y_ref, state_ref, ut_ref, xs_ref):
    bsz, t_steps, _ = u_ref.shape

    @pl.when(pl.program_id(0) == 0)
    def _():
        state_ref[...] = jnp.zeros_like(state_ref)

    def gather(t, carry):
        ut_ref[pl.ds(pl.multiple_of(t * bsz, bsz), bsz), :] = u_ref[:, t, :]
        return carry

    lax.fori_loop(0, t_steps, gather, 0, unroll=8)
    u = ut_ref[...]
    for hf in range(2):
        bu = _mm(u[:, hf * _HALF_CH:(hf + 1) * _HALF_CH], bmat_ref[hf])
        xs_ref[:, hf * _HALF_ST:(hf + 1) * _HALF_ST] = bu[:, :_HALF_ST]
        xs_ref[:, C_STATES + hf * _HALF_ST:C_STATES + (hf + 1) * _HALF_ST] = bu[:, _HALF_ST:]

    ys = []
    for hf in range(2):
        re = slice(hf * _HALF_ST, (hf + 1) * _HALF_ST)
        im = slice(C_STATES + hf * _HALF_ST, C_STATES + (hf + 1) * _HALF_ST)
        ar = jnp.broadcast_to(abar_ref[0:1, re], (bsz, _HALF_ST))
        ai = jnp.broadcast_to(abar_ref[0:1, im], (bsz, _HALF_ST))
        xr = state_ref[:, re]
        xi = state_ref[:, im]
        for t in range(t_steps):
            rows = slice(t * bsz, (t + 1) * bsz)
            xr, xi = (ar * xr - ai * xi + xs_ref[rows, re], ar * xi + ai * xr + xs_ref[rows, im])
            xs_ref[rows, re] = xr
            xs_ref[rows, im] = xi
        state_ref[:, re] = xr
        state_ref[:, im] = xi
        ys.append(_mm(xs_ref[:, re], cmat_ref[hf, :_HALF_ST, :]) + _mm(xs_ref[:, im], cmat_ref[hf, _HALF_ST:, :]))
    y = jnp.concatenate(ys, axis=1) + d_ref[...] * u
    z = jax.nn.gelu(y)
    out = z * jax.nn.sigmoid(_mm(z, gw_ref[...]) + gb_ref[...])
    y_ref[...] = out.reshape(t_steps, bsz, C_WIDTH).astype(y_ref.dtype)


def _s5(u, bmat, abar, cmat, d, glu_w, glu_b, layer):
    bsz, seq, _ = u.shape
    t_steps = min(S5_CHUNK, seq)
    return pl.pallas_call(
        _s5_kernel,
        grid=(seq // t_steps,),
        in_specs=[
            pl.BlockSpec((bsz, t_steps, C_WIDTH), lambda i: (0, i, 0)),
            _resident_layer((2, _HALF_CH, 2 * _HALF_ST), layer), _resident_layer((1, 2 * C_STATES), layer),
            _resident_layer((2, 2 * _HALF_ST, _HALF_CH), layer), _resident_layer((1, C_WIDTH), layer),
            _resident_layer((C_WIDTH, C_WIDTH), layer), _resident_layer((1, C_WIDTH), layer),
        ],
        out_specs=pl.BlockSpec((t_steps, bsz, C_WIDTH), lambda i: (i, 0, 0)),
        out_shape=jax.ShapeDtypeStruct((seq, bsz, C_WIDTH), MXU_DTYPE),
        scratch_shapes=[pltpu.VMEM((bsz, 2 * C_STATES), jnp.float32),
                        pltpu.VMEM((t_steps * bsz, C_WIDTH), jnp.float32),
                        pltpu.VMEM((t_steps * bsz, 2 * C_STATES), jnp.float32)],
        compiler_params=_params(("arbitrary",)),
        name="s5_scan",
    )(u, bmat, abar, cmat, d, glu_w, glu_b)


def _block_diag(per_group):
    depth, g, r, c = per_group.shape
    eye = jnp.eye(g, dtype=per_group.dtype)
    return jnp.einsum("lgrc,gk->lgrkc", per_group, eye).reshape(depth, g * r, g * c)


def _merge_kernel(x_ref, ya_ref, yb_ref, yc_ref, gain_ref, wg_ref, pa_ref, pb_ref, pc_ref, wo_ref, o_ref):
    x = x_ref[...]
    h = _rmsnorm(x, gain_ref[...]).astype(MXU_DTYPE)
    merged = None
    for j, (y_ref, p_ref) in enumerate(((ya_ref, pa_ref), (yb_ref, pb_ref), (yc_ref, pc_ref))):
        gate = jax.nn.sigmoid(jnp.dot(h, wg_ref[:, j * D_MODEL:(j + 1) * D_MODEL],
                                      preferred_element_type=jnp.float32))
        term = gate * jnp.dot(y_ref[...], p_ref[...], preferred_element_type=jnp.float32)
        merged = term if merged is None else merged + term
    o_ref[...] = x + _mm(merged, wo_ref[...])


def _merge(x2, ya, yb, yc, gain, wg, pa, pb, pc, wo, layer):
    n = x2.shape[0]
    tm = min(ROW_TILE, n)

    def rows(width):
        return pl.BlockSpec((tm, width), lambda i: (i, 0))

    return pl.pallas_call(
        _merge_kernel,
        grid=(n // tm,),
        in_specs=[rows(D_MODEL), rows(A_WIDTH), rows(B_WIDTH), rows(C_WIDTH), _resident_layer((1, D_MODEL), layer),
                  _resident_layer((D_MODEL, N_BRANCH * D_MODEL), layer), _resident_layer((A_WIDTH, D_MODEL), layer),
                  _resident_layer((B_WIDTH, D_MODEL), layer), _resident_layer((C_WIDTH, D_MODEL), layer),
                  _resident_layer((D_MODEL, D_MODEL), layer)],
        out_specs=rows(D_MODEL),
        out_shape=jax.ShapeDtypeStruct((n, D_MODEL), jnp.float32),
        compiler_params=_params(("parallel",)),
        name="merge",
    )(x2, ya, yb, yc, gain, wg, pa, pb, pc, wo)


def _mlp_kernel(x_ref, gain_ref, wu_ref, wd_ref, fgain_ref, o_ref, *, final_norm):
    x = x_ref[...]
    h = _rmsnorm(x, gain_ref[...]).astype(MXU_DTYPE)
    acc = x
    for c in range(D_FF // FF_CHUNK):
        cols = slice(c * FF_CHUNK, (c + 1) * FF_CHUNK)
        up = jnp.dot(h, wu_ref[:, cols], preferred_element_type=jnp.float32)
        acc = acc + _mm(jnp.square(jnp.maximum(up, 0.0)), wd_ref[cols, :])
    o_ref[...] = _rmsnorm(acc, fgain_ref[...]) if final_norm else acc


def _mlp(x2, gain, wu, wd, fgain, layer, final_norm):
    n = x2.shape[0]
    tm = min(ROW_TILE, n)
    rows = pl.BlockSpec((tm, D_MODEL), lambda i: (i, 0))
    return pl.pallas_call(
        functools.partial(_mlp_kernel, final_norm=final_norm),
        grid=(n // tm,),
        in_specs=[rows, _resident_layer((1, D_MODEL), layer), _resident_layer((D_MODEL, D_FF), layer),
                  _resident_layer((D_FF, D_MODEL), layer), _resident((1, D_MODEL))],
        out_specs=rows,
        out_shape=jax.ShapeDtypeStruct((n, D_MODEL), jnp.float32),
        compiler_params=_params(("parallel",)),
        name="mlp",
    )(x2, gain, wu, wd, fgain.reshape(1, D_MODEL))


def kernel(x, positions, norm_mix, w_in, nsa_cmp_pos, nsa_cmp_w1, nsa_cmp_w2, swa_sinks, s5_a_re, s5_a_im, s5_log_dt, s5_b_re, s5_b_im, s5_c_re, s5_c_im, s5_d, s5_glu_w, s5_glu_b, w_branch_a, w_branch_b, w_branch_c, w_out, norm_mlp, w_mlp_up, w_mlp_down, norm_final):
    bsz, seq, _ = x.shape
    depth = w_in.shape[0]
    n = bsz * seq
    gate_col = sum(IN_WIDTHS[:-1])
    bf = lambda w: w.astype(MXU_DTYPE)

    tables = _rope_tables(positions)
    abr, abi, bbr, bbi = _s5_discretize(s5_a_re, s5_a_im, s5_log_dt, s5_b_re, s5_b_im)
    abar = jnp.concatenate([abr.reshape(depth, 1, C_STATES), abi.reshape(depth, 1, C_STATES)], axis=2)
    bbr = bbr.reshape(depth, C_GROUPS, C_GROUP_CH, C_STATE)
    bbi = bbi.reshape(depth, C_GROUPS, C_GROUP_CH, C_STATE)
    cre_t = jnp.swapaxes(s5_c_re, 2, 3)
    cim_t = jnp.swapaxes(s5_c_im, 2, 3)
    halves = (slice(0, C_GROUPS // 2), slice(C_GROUPS // 2, C_GROUPS))
    bmat = bf(jnp.stack([jnp.concatenate([_block_diag(bbr[:, g]), _block_diag(bbi[:, g])], axis=2)
                         for g in halves], axis=1))
    cmat = bf(jnp.stack([jnp.concatenate([_block_diag(cre_t[:, g]), -_block_diag(cim_t[:, g])], axis=1)
                         for g in halves], axis=1))
    s5_params = (bmat, abar, cmat, s5_d.reshape(depth, 1, C_WIDTH), bf(s5_glu_w), s5_glu_b.reshape(depth, 1, C_WIDTH))
    w_nat, w_tr = _pack_w_in(w_in)
    w_gate = bf(w_in[..., gate_col:])
    cmp_params = (bf(nsa_cmp_pos.reshape(depth, 2, 1, CMP_LEN * HEAD_DIM)), bf(nsa_cmp_w1), bf(_expand_w1(nsa_cmp_w1)),
                  bf(nsa_cmp_w2))
    gain_mix = norm_mix.reshape(depth, 1, D_MODEL)
    gain_mlp = norm_mlp.reshape(depth, 1, D_MODEL)
    stacked = [bf(w) for w in (w_branch_a, w_branch_b, w_branch_c, w_out, w_mlp_up, w_mlp_down)]

    x2 = x.reshape(n, D_MODEL)
    for l in range(depth):
        qat, qbt, vat, vbt, ka, kvb, kseg, vseg, ga, uc = _in_projection(x2, gain_mix, w_nat, w_tr, tables, l, bsz, seq)

        kc, vct = _compress(kseg, vseg, *cmp_params, l, bsz)
        ya = _nsa_attention(qat, ka, vat, kc, vct, ga, bsz, seq)

        yb = _swa_attention(qbt, kvb, vbt, swa_sinks[l], bsz, seq)

        yc_tm = _s5(uc.reshape(bsz, seq, C_WIDTH), *s5_params, l)
        yc = jnp.swapaxes(yc_tm, 0, 1).reshape(n, C_WIDTH)

        x2 = _merge(x2, ya, yb, yc, gain_mix, w_gate, *stacked[:4], l)
        x2 = _mlp(x2, gain_mlp, *stacked[4:], norm_final, l, l == depth - 1)
    return x2.reshape(bsz, seq, D_MODEL)
```

```python
import functools

import jax
import jax.numpy as jnp
from jax import lax
from jax.experimental import pallas as pl
from jax.experimental.pallas import tpu as pltpu

D_MODEL = 1024
HEAD_DIM = 64
ROPE_THETA = 10000.0
NORM_EPS = 1e-6
QBLK = 128

A_HEADS = 8
A_KV_HEADS = 2
A_GROUP = A_HEADS // A_KV_HEADS
A_WIDTH = A_HEADS * HEAD_DIM
NSA_BRANCHES = 3
CMP_LEN = 32
CMP_STRIDE = 16
CMP_HIDDEN = 2 * HEAD_DIM
SEL_LEN = 64
SEL_TOPK = 16
A_WINDOW = 512
FORCED_SCORE = 1e4

B_HEADS = 8
B_WIDTH = B_HEADS * HEAD_DIM
B_WINDOW = 128

C_WIDTH = 512
C_GROUP_CH = 16
C_GROUPS = C_WIDTH // C_GROUP_CH
C_STATE = 64
C_STATES = C_GROUPS * C_STATE
_HALF_CH = C_WIDTH // 2
_HALF_ST = C_STATES // 2

D_FF = 4 * D_MODEL
N_BRANCH = 3

IN_WIDTHS = (A_WIDTH, NSA_BRANCHES * 2 * A_KV_HEADS * HEAD_DIM, NSA_BRANCHES * A_HEADS,
             B_WIDTH, 2 * HEAD_DIM, C_WIDTH, N_BRANCH * D_MODEL)

LANES = 128
SUBLANES = 8
VMEM_LIMIT = 56 * 1024 * 1024

MXU_DTYPE = jnp.bfloat16
ROW_TILE = 1024
S5_CHUNK = 128
SEL_CHUNK = 256
SWA_QSUB = 16
NSA_QSUB = 2
FF_CHUNK = 2048

LOG2E = 1.4426950408889634
DEN_FLOOR = 1e-30
_NEG_INF = float("-inf")
_SEL_SHIFT = SEL_LEN.bit_length() - 1


def _f32(x):
    return x.astype(jnp.float32)


def _mm(a, b):
    return jnp.dot(a.astype(MXU_DTYPE), b.astype(MXU_DTYPE), preferred_element_type=jnp.float32)


def _mm_nt(a, b):
    return lax.dot_general(a.astype(MXU_DTYPE), b.astype(MXU_DTYPE), (((1,), (1,)), ((), ())),
                           preferred_element_type=jnp.float32)


def _rmsnorm(x, gain):
    return x * lax.rsqrt(jnp.mean(x * x, axis=-1, keepdims=True) + NORM_EPS) * gain


def _resident(shape):
    zeros = (0,) * len(shape)
    return pl.BlockSpec(shape, lambda *_: zeros, pipeline_mode=pl.Buffered(1))


def _resident_layer(shape, layer):
    zeros = (0,) * len(shape)
    return pl.BlockSpec((None,) + tuple(shape), lambda *_: (layer,) + zeros, pipeline_mode=pl.Buffered(1))


def _params(sem):
    return pltpu.CompilerParams(dimension_semantics=sem, vmem_limit_bytes=VMEM_LIMIT)


def _rope_table_kernel(pos_ref, invf_ref, sign_ref, cos_ref, sin_ref):
    ang = _f32(pos_ref[...]) * invf_ref[...]
    cos_ref[...] = jnp.cos(ang)
    sin_ref[...] = jnp.sin(ang) * sign_ref[...]


def _rope_tables(positions):
    n = positions.size
    half = HEAD_DIM // 2
    inv_freq = ROPE_THETA ** (-jnp.arange(half, dtype=jnp.float32) / half)
    invf = jnp.tile(inv_freq, LANES // half)
    sign = jnp.tile(jnp.concatenate([-jnp.ones(half, jnp.float32), jnp.ones(half, jnp.float32)]),
                    LANES // HEAD_DIM)
    tm = min(1024, n)
    natural = pl.pallas_call(
        _rope_table_kernel,
        grid=(n // tm,),
        in_specs=[pl.BlockSpec((tm, 1), lambda i: (i, 0)), _resident((1, LANES)), _resident((1, LANES))],
        out_specs=[pl.BlockSpec((tm, LANES), lambda i: (i, 0))] * 2,
        out_shape=[jax.ShapeDtypeStruct((n, LANES), jnp.float32)] * 2,
        compiler_params=_params(("parallel",)),
        name="rope_tables",
    )(positions.reshape(n, 1), invf.reshape(1, LANES), sign.reshape(1, LANES))
    transposed = pl.pallas_call(
        _rope_table_kernel,
        grid=(n // tm,),
        in_specs=[pl.BlockSpec((1, tm), lambda i: (0, i)), _resident((HEAD_DIM, 1)), _resident((HEAD_DIM, 1))],
        out_specs=[pl.BlockSpec((HEAD_DIM, tm), lambda i: (0, i))] * 2,
        out_shape=[jax.ShapeDtypeStruct((HEAD_DIM, n), jnp.float32)] * 2,
        compiler_params=_params(("parallel",)),
        name="rope_tables_t",
    )(positions.reshape(1, n), invf[:HEAD_DIM].reshape(HEAD_DIM, 1), sign[:HEAD_DIM].reshape(HEAD_DIM, 1))
    return natural + transposed


def _swap_halves(x):
    lane = lax.broadcasted_iota(jnp.int32, x.shape, 1)
    first = (lane & (HEAD_DIM - 1)) < HEAD_DIM // 2
    return jnp.where(first, pltpu.roll(x, LANES - HEAD_DIM // 2, 1), pltpu.roll(x, HEAD_DIM // 2, 1))


def _proj_kernel(x_ref, gain_ref, wn_ref, wt_ref, cos_ref, sin_ref, cost_ref, sint_ref,
                 qat_ref, qbt_ref, vat_ref, vbt_ref, ka_ref, kvb_ref, kseg_ref, vseg_ref, ga_ref, uc_ref,
                 seg_ref):
    h = _rmsnorm(x_ref[...], gain_ref[...]).astype(MXU_DTYPE)
    nat = jnp.dot(h, wn_ref[...], preferred_element_type=jnp.float32)
    tr = _mm_nt(wt_ref[...], h)

    cos = cos_ref[...]
    sin = sin_ref[...]

    def roped(j):
        xj = nat[:, j * LANES:(j + 1) * LANES]
        return xj, xj * cos + _swap_halves(xj) * sin

    for j in range(1, NSA_BRANCHES):
        ka_ref[:, (j - 1) * LANES:j * LANES] = roped(j)[1].astype(ka_ref.dtype)
    raw, rot = roped(3)
    lane = lax.broadcasted_iota(jnp.int32, raw.shape, 1)
    kvb_ref[...] = jnp.where(lane < HEAD_DIM, rot, raw).astype(kvb_ref.dtype)
    n_seg_tile = nat.shape[0] // CMP_STRIDE
    for j, (src, dst_ref) in enumerate(((roped(0)[1], kseg_ref), (nat[:, 4 * LANES:5 * LANES], vseg_ref))):
        seg_ref[j] = src
        for tok in range(CMP_STRIDE):
            dst_ref[:, tok * LANES:(tok + 1) * LANES] = (
                seg_ref[j, pl.ds(tok, n_seg_tile, stride=CMP_STRIDE), :].astype(dst_ref.dtype))
    ga_ref[...] = jax.nn.sigmoid(nat[:, 5 * LANES:6 * LANES])
    uc_ref[...] = nat[:, 6 * LANES:]

    cos_t = cost_ref[...]
    sin_t = sint_ref[...]
    half = HEAD_DIM // 2

    def roped_t(r0):
        xh = tr[r0:r0 + HEAD_DIM]
        return xh * cos_t + jnp.concatenate([xh[half:], xh[:half]], axis=0) * sin_t

    for hd in range(A_HEADS):
        qat_ref[hd * HEAD_DIM:(hd + 1) * HEAD_DIM, :] = roped_t(hd * HEAD_DIM).astype(qat_ref.dtype)
    for hd in range(B_HEADS):
        qbt_ref[hd * HEAD_DIM:(hd + 1) * HEAD_DIM, :] = roped_t(A_WIDTH + hd * HEAD_DIM).astype(qbt_ref.dtype)
    v0 = A_WIDTH + B_WIDTH
    vat_ref[...] = tr[v0:v0 + 2 * LANES].astype(vat_ref.dtype)
    vbt_ref[...] = tr[v0 + 2 * LANES:].astype(vbt_ref.dtype)


_NAT_COLS = 10 * LANES
_TR_ROWS = A_WIDTH + B_WIDTH + 2 * LANES + HEAD_DIM


def _pack_w_in(w_in):
    o = [0]
    for w in IN_WIDTHS:
        o.append(o[-1] + w)
    scale = HEAD_DIM ** -0.5 * LOG2E
    q_a = w_in[..., o[0]:o[1]] * scale
    kv_cols = [w_in[..., o[1] + j * LANES:o[1] + (j + 1) * LANES] for j in range(2 * NSA_BRANCHES)]
    k_a = jnp.concatenate(kv_cols[0::2], axis=-1)
    v_a = jnp.concatenate(kv_cols[1::2], axis=-1)
    g_a = w_in[..., o[2]:o[3]]
    g_a = jnp.concatenate([g_a, jnp.zeros(g_a.shape[:-1] + (LANES - NSA_BRANCHES * A_HEADS,), g_a.dtype)], axis=-1)
    q_b = w_in[..., o[3]:o[4]] * scale
    kv_b = w_in[..., o[4]:o[5]]
    u_c = w_in[..., o[5]:o[6]]
    w_nat = jnp.concatenate([k_a, kv_b, v_a[..., :LANES], g_a, u_c], axis=-1)
    w_tr = jnp.swapaxes(jnp.concatenate([q_a, q_b, v_a[..., LANES:], kv_b[..., HEAD_DIM:]], axis=-1), -1, -2)
    return w_nat.astype(MXU_DTYPE), w_tr.astype(MXU_DTYPE)


def _in_projection(x2, gain, w_nat, w_tr, tables, layer, bsz, seq):
    n = x2.shape[0]
    tm = min(ROW_TILE, seq)
    tiles = seq // tm

    def rows(width):
        return pl.BlockSpec((tm, width), lambda i: (i, 0))

    def lanes(height):
        return pl.BlockSpec((None, height, tm), lambda i: (i // tiles, 0, i % tiles))

    table_t = pl.BlockSpec((HEAD_DIM, tm), lambda i: (0, i))
    t_heights = (A_WIDTH, B_WIDTH, 2 * LANES, HEAD_DIM)
    widths = (2 * LANES, LANES, LANES, C_WIDTH)
    dtypes = (MXU_DTYPE, MXU_DTYPE, jnp.float32, jnp.float32)
    seg_rows = tm // CMP_STRIDE
    seg_spec = pl.BlockSpec((seg_rows, CMP_STRIDE * LANES), lambda i: (i, 0))
    seg_shape = jax.ShapeDtypeStruct((n // CMP_STRIDE, CMP_STRIDE * LANES), MXU_DTYPE)
    return pl.pallas_call(
        _proj_kernel,
        grid=(n // tm,),
        in_specs=[rows(D_MODEL), _resident_layer((1, D_MODEL), layer), _resident_layer((D_MODEL, _NAT_COLS), layer),
                  _resident_layer((_TR_ROWS, D_MODEL), layer), rows(LANES), rows(LANES), table_t, table_t],
        out_specs=[lanes(hh) for hh in t_heights] + [rows(w) for w in widths[:2]] + [seg_spec, seg_spec]
        + [rows(w) for w in widths[2:]],
        out_shape=[jax.ShapeDtypeStruct((bsz, hh, seq), MXU_DTYPE) for hh in t_heights]
        + [jax.ShapeDtypeStruct((n, w), dt) for w, dt in zip(widths[:2], dtypes[:2])] + [seg_shape, seg_shape]
        + [jax.ShapeDtypeStruct((n, w), dt) for w, dt in zip(widths[2:], dtypes[2:])],
        scratch_shapes=[pltpu.VMEM((2, tm, LANES), jnp.float32)],
        compiler_params=_params(("parallel",)),
        name="in_projection",
    )(x2, gain, w_nat, w_tr, *tables)


def _compress_kernel(kseg_ref, vseg_ref, pos_ref, w1_ref, w1x_ref, w2_ref, kc_ref, vct_ref):
    nseg = kseg_ref.shape[0]
    both = []
    for j, seg_ref in enumerate((kseg_ref, vseg_ref)):
        seg = seg_ref[...]
        pos_term = _mm(pos_ref[j], w1_ref[j])
        heads = []
        for h in range(A_KV_HEADS):
            hidden = (_mm(seg, w1x_ref[j, 0, h]) + pltpu.roll(_mm(seg, w1x_ref[j, 1, h]), nseg - 1, 0)
                      + pos_term)
            heads.append(_mm(jax.nn.gelu(hidden), w2_ref[j]))
        both.append(jnp.concatenate(heads, axis=1))
    kc_ref[...] = both[0].astype(kc_ref.dtype)
    vct_ref[...] = both[1].T.astype(vct_ref.dtype)


def _expand_w1(w1):
    depth = w1.shape[0]
    w = w1.reshape(depth, 2, 2, CMP_STRIDE, HEAD_DIM, CMP_HIDDEN)
    eye = jnp.eye(A_KV_HEADS, dtype=w1.dtype)
    return jnp.einsum("ljatdc,hg->ljahtgdc", w, eye).reshape(depth, 2, 2, A_KV_HEADS, CMP_STRIDE * LANES,
                                                              CMP_HIDDEN)


def _compress(kseg, vseg, pos_flat, w1, w1x, w2, layer, bsz):
    nseg = kseg.shape[0] // bsz
    segw = kseg.shape[1]
    seg_spec = pl.BlockSpec((nseg, segw), lambda b: (b, 0))
    return pl.pallas_call(
        _compress_kernel,
        grid=(bsz,),
        in_specs=[seg_spec, seg_spec, _resident_layer((2, 1, CMP_LEN * HEAD_DIM), layer),
                  _resident_layer((2, CMP_LEN * HEAD_DIM, CMP_HIDDEN), layer),
                  _resident_layer((2, 2, A_KV_HEADS, segw, CMP_HIDDEN), layer),
                  _resident_layer((2, CMP_HIDDEN, HEAD_DIM), layer)],
        out_specs=[pl.BlockSpec((None, nseg, LANES), lambda b: (b, 0, 0)),
                   pl.BlockSpec((None, LANES, nseg), lambda b: (b, 0, 0))],
        out_shape=[jax.ShapeDtypeStruct((bsz, nseg, LANES), MXU_DTYPE),
                   jax.ShapeDtypeStruct((bsz, LANES, nseg), MXU_DTYPE)],
        compiler_params=_params(("parallel",)),
        name="nsa_compress",
    )(kseg, vseg, pos_flat, w1, w1x, w2)


def _split3(x):
    hi = x.astype(MXU_DTYPE)
    r1 = x - _f32(hi)
    mid = r1.astype(MXU_DTYPE)
    lo = (r1 - _f32(mid)).astype(MXU_DTYPE)
    return hi, mid, lo


def _col_max(s):
    return jnp.max(s, axis=0, keepdims=True)


def _tile_lanes(x, n):
    return jnp.concatenate([x] * n, axis=1)


def _pad_head_rows(x, h, fill):
    other = jnp.full(x.shape, fill, x.dtype)
    return jnp.concatenate([x, other] if h == 0 else [other, x], axis=0)


def _values_with_ones(vt):
    return jnp.concatenate([vt, jnp.ones((2 * SUBLANES, vt.shape[1]), vt.dtype)], axis=0)


def _store_transposed_pairs(o_ref, heads_t, row0=0):
    for p in range(len(heads_t) // 2):
        pair = jnp.concatenate([heads_t[2 * p], heads_t[2 * p + 1]], axis=0)
        o_ref[row0:row0 + QBLK, p * LANES:(p + 1) * LANES] = pair.T.astype(o_ref.dtype)


def _rank_blocks(selb_ref, p_sum, c2s_ref, blk, before, qp_lane, n_sel):
    imp_t = sum(_mm(c2s_ref[...], part) for part in _split3(p_sum))
    cur = jnp.right_shift(qp_lane, _SEL_SHIFT)
    forced = (blk == 0) | (blk == cur) | (blk == cur - 1)
    score = jnp.where(forced, FORCED_SCORE, imp_t)
    score = jnp.where(blk <= cur, score, _NEG_INF)
    n_grp = n_sel // SUBLANES
    sub = lax.broadcasted_iota(jnp.int32, (SUBLANES, QBLK), 0)
    score_grp = [score[SUBLANES * v:SUBLANES * (v + 1), :] for v in range(n_grp)]
    rank_grp = [jnp.zeros((SUBLANES, QBLK), jnp.float32) for _ in range(n_grp)]
    for j in range(n_sel):
        row = jnp.broadcast_to(score[j:j + 1, :], (SUBLANES, QBLK))
        vj, rj = divmod(j, SUBLANES)
        for v in range(n_grp):
            if v > vj:
                inc = jnp.where(row >= score_grp[v], 1.0, 0.0)
            elif v < vj:
                inc = jnp.where(row > score_grp[v], 1.0, 0.0)
            else:
                inc = jnp.where(sub > rj, jnp.where(row >= score_grp[v], 1.0, 0.0),
                                jnp.where(row > score_grp[v], 1.0, 0.0))
            rank_grp[v] = rank_grp[v] + inc
    rank = jnp.concatenate(rank_grp, axis=0)
    picked = jnp.where(rank < float(min(SEL_TOPK, n_sel)), 0.0, _NEG_INF)
    selb_ref[...] = jnp.where(before, picked, _NEG_INF)


def _nsa_kernel(*refs, seq):
    for sb in range(NSA_QSUB):
        _nsa_block(sb, *refs, seq=seq)


def _nsa_block(sb, qt_ref, k_ref, vt_ref, kc_ref, vct_ref, g_ref, c2s_ref, o_ref,
               selb_ref, sca_ref, scb_ref, cmp_ref, win_ref, diag_ref, *, seq):
    q0 = (pl.program_id(1) * NSA_QSUB + sb) * QBLK
    q_cols = slice(sb * QBLK, (sb + 1) * QBLK)
    n_cmp = kc_ref.shape[0]
    n_sel = seq // SEL_LEN
    kc_sz = min(SEL_CHUNK, seq)
    span = min(A_WINDOW + QBLK, seq)
    blocks_per_chunk = kc_sz // SEL_LEN

    gates_t = g_ref[q_cols, :].T
    qp_lane = q0 + lax.broadcasted_iota(jnp.int32, (1, QBLK), 1)

    cmp_end = lax.broadcasted_iota(jnp.int32, (n_cmp, 1), 0) * CMP_STRIDE + (CMP_LEN - 1)
    bias_c = _tile_lanes(jnp.where(cmp_end <= qp_lane, 0.0, _NEG_INF), A_GROUP)

    w0 = pl.multiple_of(jnp.maximum(q0 + QBLK - span, 0), QBLK)
    diff = qp_lane - (w0 + lax.broadcasted_iota(jnp.int32, (span, 1), 0))
    bias_w = _tile_lanes(jnp.where(diff >= 0, jnp.where(diff < A_WINDOW, 0.0, _NEG_INF), _NEG_INF), A_GROUP)

    w_qs = []
    for h in range(A_KV_HEADS):
        q_t = jnp.concatenate([qt_ref[(h * A_GROUP + g) * HEAD_DIM:(h * A_GROUP + g + 1) * HEAD_DIM, q_cols]
                               for g in range(A_GROUP)], axis=1)
        w_qs.append(_pad_head_rows(q_t, h, 0.0))

    n_chunks = (q0 + kc_sz - 1) // kc_sz
    last = jnp.maximum(n_chunks - 1, 0)

    def issue_scores(c, dst_ref):
        k0 = pl.multiple_of(jnp.minimum(c, last) * kc_sz, kc_sz)
        keys = k_ref[pl.ds(k0, kc_sz), :LANES]
        for h in range(A_KV_HEADS):
            dst_ref[h] = _mm(keys, w_qs[h])

    for h in range(A_KV_HEADS):
        cmp_ref[h] = _mm(kc_ref[...], w_qs[h])
    for h in range(A_KV_HEADS):
        win_ref[h] = _mm(k_ref[pl.ds(w0, span), LANES:], w_qs[h])
    issue_scores(0, sca_ref)
    for h in range(A_KV_HEADS):
        diag_ref[h] = _mm(k_ref[pl.ds(pl.multiple_of(q0, QBLK), QBLK), :LANES], w_qs[h])

    o_cs, o_ws = [], []
    for h in range(A_KV_HEADS):
        s = cmp_ref[h] + bias_c
        m = _col_max(s)
        m = jnp.where(m == _NEG_INF, 0.0, m)
        e = jnp.exp2(s - m)
        p = e * (1.0 / jnp.maximum(jnp.sum(e, axis=0, keepdims=True), DEN_FLOOR))
        o_cs.append(_mm(vct_ref[...], p)[h * HEAD_DIM:(h + 1) * HEAD_DIM])

        p_sum = p[:, 0:QBLK]
        for g in range(1, A_GROUP):
            p_sum = p_sum + p[:, g * QBLK:(g + 1) * QBLK]
        blk = lax.broadcasted_iota(jnp.int32, (n_sel, QBLK), 0)
        before = blk < jnp.right_shift(q0, _SEL_SHIFT)
        _rank_blocks(selb_ref.at[h], p_sum, c2s_ref, blk, before, qp_lane, n_sel)

    for h in range(A_KV_HEADS):
        sw = win_ref[h] + bias_w
        pw = jnp.exp2(sw - _col_max(sw))
        acc_w = _mm(_values_with_ones(vt_ref[LANES + h * HEAD_DIM:LANES + (h + 1) * HEAD_DIM, pl.ds(w0, span)]), pw)
        o_ws.append(acc_w[:HEAD_DIM] / acc_w[HEAD_DIM:HEAD_DIM + 1])

    def consume(c, src_ref, states):
        cc = jnp.minimum(c, last)
        k0 = pl.multiple_of(cc * kc_sz, kc_sz)
        out = []
        for h in range(A_KV_HEADS):
            m_run, acc = states[h]
            rows = [_tile_lanes(jnp.where(c < n_chunks, selb_ref[h, pl.ds(cc * blocks_per_chunk + jj, 1), :], _NEG_INF),
                                A_GROUP) for jj in range(blocks_per_chunk)]
            blocks = [src_ref[h, jj * SEL_LEN:(jj + 1) * SEL_LEN, :] for jj in range(blocks_per_chunk)]
            part = None
            for sc_blk, row in zip(blocks, rows):
                blk_max = jnp.max(sc_blk.reshape(SEL_LEN // SUBLANES, SUBLANES, A_GROUP * QBLK), axis=0) + row
                part = blk_max if part is None else jnp.maximum(part, blk_max)
            m_new = jnp.maximum(m_run, _col_max(part))
            pr = jnp.concatenate([jnp.exp2(sc_blk + (row - m_new)) for sc_blk, row in zip(blocks, rows)], axis=0)
            vt = _values_with_ones(vt_ref[h * HEAD_DIM:(h + 1) * HEAD_DIM, pl.ds(k0, kc_sz)])
            out.append((m_new, jnp.exp2(m_run - m_new) * acc + _mm(vt, pr)))
        return tuple(out)

    def sel_step(t, states):
        issue_scores(2 * t + 1, scb_ref)
        states = consume(2 * t, sca_ref, states)
        issue_scores(2 * t + 2, sca_ref)
        return consume(2 * t + 1, scb_ref, states)

    kp_diag = q0 + lax.broadcasted_iota(jnp.int32, (QBLK, 1), 0)
    bias_d = _tile_lanes(jnp.where(kp_diag <= qp_lane, 0.0, _NEG_INF), A_GROUP)
    q0_al = pl.multiple_of(q0, QBLK)
    init = []
    for h in range(A_KV_HEADS):
        sd = diag_ref[h] + bias_d
        m0 = _col_max(sd)
        vt = _values_with_ones(vt_ref[h * HEAD_DIM:(h + 1) * HEAD_DIM, pl.ds(q0_al, QBLK)])
        init.append((m0, _mm(vt, jnp.exp2(sd - m0))))
    init = tuple(init)
    sel_out = lax.fori_loop(0, (n_chunks + 1) // 2, sel_step, init)

    heads_t = []
    for h in range(A_KV_HEADS):
        acc_s = sel_out[h][1]
        o_s = acc_s[:HEAD_DIM] / jnp.maximum(acc_s[HEAD_DIM:HEAD_DIM + 1], DEN_FLOOR)
        for g in range(A_GROUP):
            hd = h * A_GROUP + g
            cols = slice(g * QBLK, (g + 1) * QBLK)
            heads_t.append(gates_t[hd:hd + 1, :] * o_cs[h][:, cols]
                           + gates_t[A_HEADS + hd:A_HEADS + hd + 1, :] * o_s[:, cols]
                           + gates_t[2 * A_HEADS + hd:2 * A_HEADS + hd + 1, :] * o_ws[h][:, cols])
    _store_transposed_pairs(o_ref, heads_t, sb * QBLK)


def _nsa_attention(qat, ka, vat, kc, vct, ga, bsz, seq):
    rows = NSA_QSUB * QBLK
    nstep = seq // rows
    n_seg = seq // CMP_STRIDE
    n_sel = seq // SEL_LEN
    assert seq >= A_WINDOW + QBLK and seq % SEL_CHUNK == 0 and seq % rows == 0
    c_start = jnp.arange(n_seg) * CMP_STRIDE
    s_start = jnp.arange(n_sel) * SEL_LEN
    overlap = (jnp.minimum(c_start[None, :] + CMP_LEN, s_start[:, None] + SEL_LEN)
               - jnp.maximum(c_start[None, :], s_start[:, None]))
    c2s = (jnp.clip(overlap, 0, None).astype(jnp.float32) / CMP_LEN).astype(MXU_DTYPE)
    return pl.pallas_call(
        functools.partial(_nsa_kernel, seq=seq),
        grid=(bsz, nstep),
        in_specs=[
            pl.BlockSpec((None, A_WIDTH, rows), lambda b, i: (b, 0, i)),
            pl.BlockSpec((seq, 2 * LANES), lambda b, i: (b, 0)),
            pl.BlockSpec((None, 2 * LANES, seq), lambda b, i: (b, 0, 0)),
            pl.BlockSpec((None, n_seg, LANES), lambda b, i: (b, 0, 0)),
            pl.BlockSpec((None, LANES, n_seg), lambda b, i: (b, 0, 0)),
            pl.BlockSpec((rows, LANES), lambda b, i: (b * nstep + i, 0)),
            _resident((n_sel, n_seg)),
        ],
        out_specs=pl.BlockSpec((rows, A_WIDTH), lambda b, i: (b * nstep + i, 0)),
        out_shape=jax.ShapeDtypeStruct((bsz * seq, A_WIDTH), MXU_DTYPE),
        scratch_shapes=[pltpu.VMEM((A_KV_HEADS, n_sel, QBLK), jnp.float32)]
        + [pltpu.VMEM((A_KV_HEADS, min(SEL_CHUNK, seq), A_GROUP * QBLK), jnp.float32)] * 2
        + [pltpu.VMEM((A_KV_HEADS, n_seg, A_GROUP * QBLK), jnp.float32),
           pltpu.VMEM((A_KV_HEADS, min(A_WINDOW + QBLK, seq), A_GROUP * QBLK), jnp.float32),
           pltpu.VMEM((A_KV_HEADS, QBLK, A_GROUP * QBLK), jnp.float32)],
        compiler_params=_params(("parallel", "arbitrary")),
        name="nsa_attention",
    )(qat, ka, vat, kc, vct, ga, c2s)


def _swa_kernel(sink_ref, qt_ref, kv_ref, vt_ref, o_ref, sc_ref, *, seq):
    i = pl.program_id(1)
    span = min(B_WINDOW + QBLK, seq)
    sink = jnp.concatenate([jnp.full((1, QBLK), sink_ref[hd] * LOG2E, jnp.float32) for hd in range(B_HEADS)],
                           axis=1)
    starts = []
    for sb in range(SWA_QSUB):
        q0 = (i * SWA_QSUB + sb) * QBLK
        w0 = pl.multiple_of(jnp.maximum(q0 + QBLK - span, 0), QBLK)
        starts.append((q0, w0))
        q_t = jnp.concatenate([qt_ref[hd * HEAD_DIM:(hd + 1) * HEAD_DIM, sb * QBLK:(sb + 1) * QBLK]
                               for hd in range(B_HEADS)], axis=1)
        sc_ref[sb] = _mm(kv_ref[pl.ds(w0, span), :], _pad_head_rows(q_t, 0, 0.0))
    for sb, (q0, w0) in enumerate(starts):
        qp_lane = q0 + lax.broadcasted_iota(jnp.int32, (1, QBLK), 1)
        diff = qp_lane - (w0 + lax.broadcasted_iota(jnp.int32, (span, 1), 0))
        bias = _tile_lanes(jnp.where(diff >= 0, jnp.where(diff < B_WINDOW, 0.0, _NEG_INF), _NEG_INF), B_HEADS)
        s = sc_ref[sb] + bias
        m = jnp.maximum(_col_max(s), sink)
        acc = _mm(_values_with_ones(vt_ref[:, pl.ds(w0, span)]), jnp.exp2(s - m))
        o_t = acc[:HEAD_DIM] / (acc[HEAD_DIM:HEAD_DIM + 1] + jnp.exp2(sink - m))
        _store_transposed_pairs(o_ref, [o_t[:, hd * QBLK:(hd + 1) * QBLK] for hd in range(B_HEADS)], sb * QBLK)


def _swa_attention(qbt, kvb, vbt, sinks, bsz, seq):
    rows = SWA_QSUB * QBLK
    nstep = seq // rows
    span = min(B_WINDOW + QBLK, seq)
    assert seq >= B_WINDOW + QBLK and seq % rows == 0
    return pl.pallas_call(
        functools.partial(_swa_kernel, seq=seq),
        grid=(bsz, nstep),
        in_specs=[
            pl.BlockSpec(memory_space=pltpu.SMEM),
            pl.BlockSpec((None, B_WIDTH, rows), lambda b, i: (b, 0, i)),
            pl.BlockSpec((seq, LANES), lambda b, i: (b, 0)),
            pl.BlockSpec((None, HEAD_DIM, seq), lambda b, i: (b, 0, 0)),
        ],
        out_specs=pl.BlockSpec((rows, B_WIDTH), lambda b, i: (b * nstep + i, 0)),
        out_shape=jax.ShapeDtypeStruct((bsz * seq, B_WIDTH), MXU_DTYPE),
        scratch_shapes=[pltpu.VMEM((SWA_QSUB, span, B_HEADS * QBLK), jnp.float32)],
        compiler_params=_params(("parallel", "arbitrary")),
        name="swa_attention",
    )(sinks, qbt, kvb, vbt)


def _s5_discretize_kernel(are_ref, aim_ref, logdt_ref, bre_ref, bim_ref,
                          abr_ref, abi_ref, bbr_ref, bbi_ref):
    ar = are_ref[...]
    ai = aim_ref[...]
    dt = jnp.exp(logdt_ref[...])
    mag = jnp.exp(dt * ar)
    abr = mag * jnp.cos(dt * ai)
    abi = mag * jnp.sin(dt * ai)
    den = ar * ar + ai * ai
    nr = abr - 1.0
    coef_r = (nr * ar + abi * ai) / den
    coef_i = (abi * ar - nr * ai) / den
    abr_ref[...] = abr
    abi_ref[...] = abi
    br = bre_ref[...]
    bi = bim_ref[...]
    bbr_ref[...] = coef_r[:, None, :] * br - coef_i[:, None, :] * bi
    bbi_ref[...] = coef_r[:, None, :] * bi + coef_i[:, None, :] * br


def _s5_discretize(a_re, a_im, log_dt, b_re, b_im):
    lg = a_re.shape[0] * a_re.shape[1]
    a2 = lambda a: a.reshape(lg, C_STATE)
    b3 = lambda b: jnp.swapaxes(b.reshape(lg, C_STATE, C_GROUP_CH), 1, 2)
    sa = jax.ShapeDtypeStruct((lg, C_STATE), jnp.float32)
    sb = jax.ShapeDtypeStruct((lg, C_GROUP_CH, C_STATE), jnp.float32)
    return pl.pallas_call(_s5_discretize_kernel, out_shape=[sa, sa, sb, sb], name="s5_discretize")(
        a2(a_re), a2(a_im), log_dt.reshape(lg, 1), b3(b_re), b3(b_im))


def _s5_kernel(u_ref, bmat_ref, abar_ref, cmat_ref, d_ref, gw_ref, gb_ref, y_ref, state_ref, ut_ref, xs_ref):
    bsz, t_steps, _ = u_ref.shape

    @pl.when(pl.program_id(0) == 0)
    def _():
        state_ref[...] = jnp.zeros_like(state_ref)

    def gather(t, carry):
        ut_ref[pl.ds(pl.multiple_of(t * bsz, bsz), bsz), :] = u_ref[:, t, :]
        return carry

    lax.fori_loop(0, t_steps, gather, 0, unroll=8)
    u = ut_ref[...]
    for hf in range(2):
        bu = _mm(u[:, hf * _HALF_CH:(hf + 1) * _HALF_CH], bmat_ref[hf])
        xs_ref[:, hf * _HALF_ST:(hf + 1) * _HALF_ST] = bu[:, :_HALF_ST]
        xs_ref[:, C_STATES + hf * _HALF_ST:C_STATES + (hf + 1) * _HALF_ST] = bu[:, _HALF_ST:]

    ys = []
    for hf in range(2):
        re = slice(hf * _HALF_ST, (hf + 1) * _HALF_ST)
        im = slice(C_STATES + hf * _HALF_ST, C_STATES + (hf + 1) * _HALF_ST)
        ar = jnp.broadcast_to(abar_ref[0:1, re], (bsz, _HALF_ST))
        ai = jnp.broadcast_to(abar_ref[0:1, im], (bsz, _HALF_ST))
        xr = state_ref[:, re]
        xi = state_ref[:, im]
        for t in range(t_steps):
            rows = slice(t * bsz, (t + 1) * bsz)
            xr, xi = (ar * xr - ai * xi + xs_ref[rows, re], ar * xi + ai * xr + xs_ref[rows, im])
            xs_ref[rows, re] = xr
            xs_ref[rows, im] = xi
        state_ref[:, re] = xr
        state_ref[:, im] = xi
        ys.append(_mm(xs_ref[:, re], cmat_ref[hf, :_HALF_ST, :]) + _mm(xs_ref[:, im], cmat_ref[hf, _HALF_ST:, :]))
    y = jnp.concatenate(ys, axis=1) + d_ref[...] * u
    z = jax.nn.gelu(y)
    out = z * jax.nn.sigmoid(_mm(z, gw_ref[...]) + gb_ref[...])
    y_ref[...] = out.reshape(t_steps, bsz, C_WIDTH).astype(y_ref.dtype)


def _s5(u, bmat, abar, cmat, d, glu_w, glu_b, layer):
    bsz, seq, _ = u.shape
    t_steps = min(S5_CHUNK, seq)
    return pl.pallas_call(
        _s5_kernel,
        grid=(seq // t_steps,),
        in_specs=[
            pl.BlockSpec((bsz, t_steps, C_WIDTH), lambda i: (0, i, 0)),
            _resident_layer((2, _HALF_CH, 2 * _HALF_ST), layer), _resident_layer((1, 2 * C_STATES), layer),
            _resident_layer((2, 2 * _HALF_ST, _HALF_CH), layer), _resident_layer((1, C_WIDTH), layer),
            _resident_layer((C_WIDTH, C_WIDTH), layer), _resident_layer((1, C_WIDTH), layer),
        ],
        out_specs=pl.BlockSpec((t_steps, bsz, C_WIDTH), lambda i: (i, 0, 0)),
        out_shape=jax.ShapeDtypeStruct((seq, bsz, C_WIDTH), MXU_DTYPE),
        scratch_shapes=[pltpu.VMEM((bsz, 2 * C_STATES), jnp.float32),
                        pltpu.VMEM((t_steps * bsz, C_WIDTH), jnp.float32),
                        pltpu.VMEM((t_steps * bsz, 2 * C_STATES), jnp.float32)],
        compiler_params=_params(("arbitrary",)),
        name="s5_scan",
    )(u, bmat, abar, cmat, d, glu_w, glu_b)


def _block_diag(per_group):
    depth, g, r, c = per_group.shape
    eye = jnp.eye(g, dtype=per_group.dtype)
    return jnp.einsum("lgrc,gk->lgrkc", per_group, eye).reshape(depth, g * r, g * c)


def _merge_kernel(x_ref, ya_ref, yb_ref, yc_ref, gain_ref, wg_ref, pa_ref, pb_ref, pc_ref, wo_ref, o_ref):
    x = x_ref[...]
    h = _rmsnorm(x, gain_ref[...]).astype(MXU_DTYPE)
    merged = None
    for j, (y_ref, p_ref) in enumerate(((ya_ref, pa_ref), (yb_ref, pb_ref), (yc_ref, pc_ref))):
        gate = jax.nn.sigmoid(jnp.dot(h, wg_ref[:, j * D_MODEL:(j + 1) * D_MODEL],
                                      preferred_element_type=jnp.float32))
        term = gate * jnp.dot(y_ref[...], p_ref[...], preferred_element_type=jnp.float32)
        merged = term if merged is None else merged + term
    o_ref[...] = x + _mm(merged, wo_ref[...])


def _merge(x2, ya, yb, yc, gain, wg, pa, pb, pc, wo, layer):
    n = x2.shape[0]
    tm = min(ROW_TILE, n)

    def rows(width):
        return pl.BlockSpec((tm, width), lambda i: (i, 0))

    return pl.pallas_call(
        _merge_kernel,
        grid=(n // tm,),
        in_specs=[rows(D_MODEL), rows(A_WIDTH), rows(B_WIDTH), rows(C_WIDTH), _resident_layer((1, D_MODEL), layer),
                  _resident_layer((D_MODEL, N_BRANCH * D_MODEL), layer), _resident_layer((A_WIDTH, D_MODEL), layer),
                  _resident_layer((B_WIDTH, D_MODEL), layer), _resident_layer((C_WIDTH, D_MODEL), layer),
                  _resident_layer((D_MODEL, D_MODEL), layer)],
        out_specs=rows(D_MODEL),
        out_shape=jax.ShapeDtypeStruct((n, D_MODEL), jnp.float32),
        compiler_params=_params(("parallel",)),
        name="merge",
    )(x2, ya, yb, yc, gain, wg, pa, pb, pc, wo)


def _mlp_kernel(x_ref, gain_ref, wu_ref, wd_ref, fgain_ref, o_ref, *, final_norm):
    x = x_ref[...]
    h = _rmsnorm(x, gain_ref[...]).astype(MXU_DTYPE)
    acc = x
    for c in range(D_FF // FF_CHUNK):
        cols = slice(c * FF_CHUNK, (c + 1) * FF_CHUNK)
        up = jnp.dot(h, wu_ref[:, cols], preferred_element_type=jnp.float32)
        acc = acc + _mm(jnp.square(jnp.maximum(up, 0.0)), wd_ref[cols, :])
    o_ref[...] = _rmsnorm(acc, fgain_ref[...]) if final_norm else acc


def _mlp(x2, gain, wu, wd, fgain, layer, final_norm):
    n = x2.shape[0]
    tm = min(ROW_TILE, n)
    rows = pl.BlockSpec((tm, D_MODEL), lambda i: (i, 0))
    return pl.pallas_call(
        functools.partial(_mlp_kernel, final_norm=final_norm),
        grid=(n // tm,),
        in_specs=[rows, _resident_layer((1, D_MODEL), layer), _resident_layer((D_MODEL, D_FF), layer),
                  _resident_layer((D_FF, D_MODEL), layer), _resident((1, D_MODEL))],
        out_specs=rows,
        out_shape=jax.ShapeDtypeStruct((n, D_MODEL), jnp.float32),
        compiler_params=_params(("parallel",)),
        name="mlp",
    )(x2, gain, wu, wd, fgain.reshape(1, D_MODEL))


def kernel(x, positions, norm_mix, w_in, nsa_cmp_pos, nsa_cmp_w1, nsa_cmp_w2, swa_sinks, s5_a_re, s5_a_im, s5_log_dt, s5_b_re, s5_b_im, s5_c_re, s5_c_im, s5_d, s5_glu_w, s5_glu_b, w_branch_a, w_branch_b, w_branch_c, w_out, norm_mlp, w_mlp_up, w_mlp_down, norm_final):
    bsz, seq, _ = x.shape
    depth = w_in.shape[0]
    n = bsz * seq
    gate_col = sum(IN_WIDTHS[:-1])
    bf = lambda w: w.astype(MXU_DTYPE)

    tables = _rope_tables(positions)
    abr, abi, bbr, bbi = _s5_discretize(s5_a_re, s5_a_im, s5_log_dt, s5_b_re, s5_b_im)
    abar = jnp.concatenate([abr.reshape(depth, 1, C_STATES), abi.reshape(depth, 1, C_STATES)], axis=2)
    bbr = bbr.reshape(depth, C_GROUPS, C_GROUP_CH, C_STATE)
    bbi = bbi.reshape(depth, C_GROUPS, C_GROUP_CH, C_STATE)
    cre_t = jnp.swapaxes(s5_c_re, 2, 3)
    cim_t = jnp.swapaxes(s5_c_im, 2, 3)
    halves = (slice(0, C_GROUPS // 2), slice(C_GROUPS // 2, C_GROUPS))
    bmat = bf(jnp.stack([jnp.concatenate([_block_diag(bbr[:, g]), _block_diag(bbi[:, g])], axis=2)
                         for g in halves], axis=1))
    cmat = bf(jnp.stack([jnp.concatenate([_block_diag(cre_t[:, g]), -_block_diag(cim_t[:, g])], axis=1)
                         for g in halves], axis=1))
    s5_params = (bmat, abar, cmat, s5_d.reshape(depth, 1, C_WIDTH), bf(s5_glu_w), s5_glu_b.reshape(depth, 1, C_WIDTH))
    w_nat, w_tr = _pack_w_in(w_in)
    w_gate = bf(w_in[..., gate_col:])
    cmp_params = (bf(nsa_cmp_pos.reshape(depth, 2, 1, CMP_LEN * HEAD_DIM)), bf(nsa_cmp_w1), bf(_expand_w1(nsa_cmp_w1)),
                  bf(nsa_cmp_w2))
    gain_mix = norm_mix.reshape(depth, 1, D_MODEL)
    gain_mlp = norm_mlp.reshape(depth, 1, D_MODEL)
    stacked = [bf(w) for w in (w_branch_a, w_branch_b, w_branch_c, w_out, w_mlp_up, w_mlp_down)]

    x2 = x.reshape(n, D_MODEL)
    for l in range(depth):
        qat, qbt, vat, vbt, ka, kvb, kseg, vseg, ga, uc = _in_projection(x2, gain_mix, w_nat, w_tr, tables, l, bsz, seq)

        kc, vct = _compress(kseg, vseg, *cmp_params, l, bsz)
        ya = _nsa_attention(qat, ka, vat, kc, vct, ga, bsz, seq)

        yb = _swa_attention(qbt, kvb, vbt, swa_sinks[l], bsz, seq)

        yc_tm = _s5(uc.reshape(bsz, seq, C_WIDTH), *s5_params, l)
        yc = jnp.swapaxes(yc_tm, 0, 1).reshape(n, C_WIDTH)

        x2 = _merge(x2, ya, yb, yc, gain_mix, w_gate, *stacked[:4], l)
        x2 = _mlp(x2, gain_mlp, *stacked[4:], norm_final, l, l == depth - 1)
    return x2.reshape(bsz, seq, D_MODEL)
```
